```python
import jax, jax.numpy as jnp
from jax import lax
import numpy as np

D_MODEL = 2048
BATCH = 1
SEQ = 8192
DEPTH = 1

CHUNK = 64
D_PLE = 256

FOX_HEAD_DIM = 64
FOX_WIDTH = D_MODEL // 2
FOX_HEADS = FOX_WIDTH // FOX_HEAD_DIM
Q_BLOCK = 128

SSD_HEAD_DIM = 64
SSD_WIDTH = D_MODEL // 2
SSD_HEADS = SSD_WIDTH // SSD_HEAD_DIM
SSD_GROUPS = 2
SSD_STATE = 128
SSD_CONV = 4
SSD_CONV_DIM = SSD_WIDTH + 2 * SSD_GROUPS * SSD_STATE

MIX_WIDTH = FOX_WIDTH + SSD_WIDTH

IN_SPLITS = [
    FOX_WIDTH,
    FOX_WIDTH,
    FOX_WIDTH,
    FOX_HEADS,
    SSD_WIDTH,
    SSD_GROUPS * SSD_STATE,
    SSD_GROUPS * SSD_STATE,
    SSD_WIDTH,
    SSD_HEADS,
]
IN_COLS = int(sum(IN_SPLITS))

D_FF = 5632
FFN_CONV = 3

EPS = 1e-6

kernel_name = "hybrid_fox_ssd_convffn_ple"


def rmsnorm(u, w):
    uf = u.astype(jnp.float32)
    y = uf * lax.rsqrt(jnp.mean(uf * uf, axis=-1, keepdims=True) + EPS)
    return (y * w.astype(jnp.float32)).astype(u.dtype)


def causal_dwconv(u, w, b):
    K = w.shape[0]
    S = u.shape[1]
    up = jnp.pad(u, ((0, 0), (K - 1, 0), (0, 0)))
    return sum(up[:, k:k + S] * w[k] for k in range(K)) + b


def fox_attention(q, k, v, f_logit, b_f):
    Bsz, S, _ = q.shape
    H, Dh = FOX_HEADS, FOX_HEAD_DIM
    scale = Dh ** -0.5
    qh = q.reshape(Bsz, S, H, Dh).transpose(0, 2, 1, 3).astype(jnp.float32)
    kh = k.reshape(Bsz, S, H, Dh).transpose(0, 2, 1, 3).astype(jnp.float32)
    vh = v.reshape(Bsz, S, H, Dh).transpose(0, 2, 1, 3).astype(jnp.float32)
    log_f = jax.nn.log_sigmoid(f_logit.astype(jnp.float32) + b_f.astype(jnp.float32))
    c = jnp.cumsum(log_f, axis=1).transpose(0, 2, 1)
    n_blk = S // Q_BLOCK
    q_blk = qh.reshape(Bsz, H, n_blk, Q_BLOCK, Dh).transpose(2, 0, 1, 3, 4)
    c_blk = c.reshape(Bsz, H, n_blk, Q_BLOCK).transpose(2, 0, 1, 3)
    k_pos = jnp.arange(S)

    def one_block(args):
        qb, cb, i = args
        s = jnp.einsum('bhqd,bhkd->bhqk', qb, kh) * scale
        s = s + cb[..., :, None] - c[..., None, :]
        q_pos = i * Q_BLOCK + jnp.arange(Q_BLOCK)
        mask = k_pos[None, :] <= q_pos[:, None]
        s = jnp.where(mask[None, None], s, -jnp.inf)
        pr = jax.nn.softmax(s, axis=-1)
        return jnp.einsum('bhqk,bhkd->bhqd', pr, vh)

    out = lax.map(one_block, (q_blk, c_blk, jnp.arange(n_blk)))
    out = out.transpose(1, 0, 3, 2, 4).reshape(Bsz, S, H * Dh)
    return out.astype(q.dtype)


def ssd_mixer(xs, Bm, Cm, z, dt_raw, conv_w, conv_b, dt_bias, A_log, D_skip, norm_w):
    Bsz, S, _ = xs.shape
    H, P, G, N, L = SSD_HEADS, SSD_HEAD_DIM, SSD_GROUPS, SSD_STATE, CHUNK
    nc = S // L
    xbc = jnp.concatenate([xs, Bm, Cm], axis=-1)
    xbc = jax.nn.silu(causal_dwconv(xbc, conv_w, conv_b))
    xs, Bm, Cm = jnp.split(xbc, [SSD_WIDTH, SSD_WIDTH + G * N], axis=-1)

    xc = xs.astype(jnp.float32).reshape(Bsz, nc, L, H, P)
    Bc = Bm.astype(jnp.float32).reshape(Bsz, nc, L, G, N)
    Cc = Cm.astype(jnp.float32).reshape(Bsz, nc, L, G, N)
    dt = jax.nn.softplus(dt_raw.astype(jnp.float32) + dt_bias.astype(jnp.float32))
    A = -jnp.exp(A_log.astype(jnp.float32))
    dtc = dt.reshape(Bsz, nc, L, H)
    a_cum = jnp.cumsum(dtc * A, axis=2)

    rep = H // G
    Bh = jnp.repeat(Bc, rep, axis=3)
    Ch = jnp.repeat(Cc, rep, axis=3)

    causal = jnp.tril(jnp.ones((L, L), dtype=bool))
    seg = a_cum[:, :, :, None, :] - a_cum[:, :, None, :, :]
    decay = jnp.exp(jnp.where(causal[None, None, :, :, None], seg, -jnp.inf))
    CB = jnp.einsum('bclgn,bcsgn->bcgls', Cc, Bc)
    CB = jnp.repeat(CB, rep, axis=2)
    y_diag = jnp.einsum('bchls,bclsh,bcsh,bcshp->bclhp', CB, decay, dtc, xc)

    decay_to_end = jnp.exp(a_cum[:, :, -1:, :] - a_cum)
    states = jnp.einsum('bcshn,bcsh,bcsh,bcshp->bchpn', Bh, decay_to_end, dtc, xc)
    chunk_decay = jnp.exp(a_cum[:, :, -1, :])

    def step(h, inp):
        st, dec = inp
        return h * dec[..., None, None] + st, h

    h0 = jnp.zeros((Bsz, H, P, N), jnp.float32)
    _, h_prev = lax.scan(step, h0, (states.transpose(1, 0, 2, 3, 4), chunk_decay.transpose(1, 0, 2)))
    h_prev = h_prev.transpose(1, 0, 2, 3, 4)
    y_off = jnp.einsum('bclhn,bchpn,bclh->bclhp', Ch, h_prev, jnp.exp(a_cum))

    y = y_diag + y_off + D_skip.astype(jnp.float32)[:, None] * xc
    y = y.reshape(Bsz, S, H * P)
    y = y * jax.nn.silu(z.astype(jnp.float32))
    return rmsnorm(y, norm_w).astype(xs.dtype)


def setup_inputs(seed: int = 0) -> dict:
    key = jax.random.key(seed)
    ks = jax.random.split(key, 24)
    f32 = jnp.float32

    def nrm(k, shape, scale):
        return jax.random.normal(k, shape, f32) * scale

    def gain(k, shape):
        return 1.0 + 0.05 * jax.random.normal(k, shape, f32)

    dt0 = jnp.exp(jax.random.uniform(ks[8], (DEPTH, SSD_HEADS), f32,
                                     np.log(1e-3).astype(np.float32), np.log(1e-1).astype(np.float32)))
    dt_bias = dt0 + jnp.log(-jnp.expm1(-dt0))
    return {
        "x": jax.random.normal(ks[0], (BATCH, SEQ, D_MODEL), f32),
        "p": jax.random.normal(ks[1], (DEPTH, BATCH, SEQ, D_PLE), f32),
        "mix_norm_w": gain(ks[2], (DEPTH, D_MODEL)),
        "w_in": nrm(ks[3], (DEPTH, D_MODEL, IN_COLS), D_MODEL ** -0.5),
        "fox_forget_bias": jax.random.uniform(ks[4], (DEPTH, FOX_HEADS), f32, 1.0, 4.0),
        "ssd_conv_w": nrm(ks[5], (DEPTH, SSD_CONV, SSD_CONV_DIM), SSD_CONV ** -0.5),
        "ssd_conv_b": nrm(ks[6], (DEPTH, SSD_CONV_DIM), 0.02),
        "ssd_dt_bias": dt_bias,
        "ssd_A_log": jnp.log(jax.random.uniform(ks[7], (DEPTH, SSD_HEADS), f32, 1.0, 16.0)),
        "ssd_D": gain(ks[9], (DEPTH, SSD_HEADS)),
        "ssd_norm_w": gain(ks[10], (DEPTH, SSD_WIDTH)),
        "w_out": nrm(ks[11], (DEPTH, MIX_WIDTH, D_MODEL), MIX_WIDTH ** -0.5),
        "ffn_norm_w": gain(ks[12], (DEPTH, D_MODEL)),
        "w_gate_up": nrm(ks[13], (DEPTH, D_MODEL, 2 * D_FF), D_MODEL ** -0.5),
        "ffn_conv_w": nrm(ks[14], (DEPTH, FFN_CONV, D_FF), FFN_CONV ** -0.5),
        "ffn_conv_b": nrm(ks[15], (DEPTH, D_FF), 0.02),
        "w_down": nrm(ks[16], (DEPTH, D_FF, D_MODEL), D_FF ** -0.5),
        "ple_norm_w": gain(ks[17], (DEPTH, D_MODEL)),
        "w_ple_gate": nrm(ks[18], (DEPTH, D_MODEL, D_MODEL), D_MODEL ** -0.5),
        "w_ple_proj": nrm(ks[19], (DEPTH, D_PLE, D_MODEL), D_PLE ** -0.5),
        "final_norm_w": gain(ks[20], (D_MODEL,)),
    }


def reference(x, p, mix_norm_w, w_in, fox_forget_bias, ssd_conv_w, ssd_conv_b, ssd_dt_bias,
              ssd_A_log, ssd_D, ssd_norm_w, w_out, ffn_norm_w, w_gate_up, ffn_conv_w,
              ffn_conv_b, w_down, ple_norm_w, w_ple_gate, w_ple_proj, final_norm_w):
    split_idx = [int(v) for v in np.cumsum(IN_SPLITS)[:-1]]
    for i in range(DEPTH):
        h = rmsnorm(x, mix_norm_w[i])
        u = h @ w_in[i]
        q, k, v, f_logit, xs, Bm, Cm, z, dt_raw = jnp.split(u, split_idx, axis=-1)
        y_fox = fox_attention(q, k, v, f_logit, fox_forget_bias[i])
        y_ssd = ssd_mixer(xs, Bm, Cm, z, dt_raw, ssd_conv_w[i], ssd_conv_b[i], ssd_dt_bias[i],
                          ssd_A_log[i], ssd_D[i], ssd_norm_w[i])
        x = x + jnp.concatenate([y_fox, y_ssd], axis=-1) @ w_out[i]

        h = rmsnorm(x, ffn_norm_w[i])
        g, up = jnp.split(h @ w_gate_up[i], [D_FF], axis=-1)
        g = causal_dwconv(g, ffn_conv_w[i], ffn_conv_b[i])
        x = x + (jax.nn.silu(g) * up) @ w_down[i]

        h = rmsnorm(x, ple_norm_w[i])
        gate = jax.nn.sigmoid(h @ w_ple_gate[i])
        x = x + gate * (p[i] @ w_ple_proj[i])
    return rmsnorm(x, final_norm_w)
```

```python
import functools

import jax
import jax.numpy as jnp
from jax import lax
from jax.experimental import pallas as pl
from jax.experimental.pallas import tpu as pltpu

F32 = jnp.float32
BF16 = jnp.bfloat16

D_MODEL = 2048
SEQ = 8192
D_PLE = 256
HEAD_DIM = 64
FOX_WIDTH = 1024
FOX_HEADS = 16
SSD_WIDTH = 1024
SSD_HEADS = 16
SSD_GROUPS = 2
SSD_STATE = 128
SSD_CONV = 4
D_FF = 5632
FFN_CONV = 3
EPS = 1e-6

_OFF_Q = 0
_OFF_F = 3 * FOX_WIDTH
_OFF_XS = _OFF_F + FOX_HEADS
_OFF_B = _OFF_XS + SSD_WIDTH
_OFF_C = _OFF_B + SSD_GROUPS * SSD_STATE
_OFF_Z = _OFF_C + SSD_GROUPS * SSD_STATE
_OFF_DT = _OFF_Z + SSD_WIDTH
_IN_COLS = _OFF_DT + SSD_HEADS

LANES = 128
SUBLANES = 8
VMEM_CAP = 60 * 1024 * 1024

_BC_WIDTH = 2 * SSD_GROUPS * SSD_STATE
_U2_COLS = 2 * SSD_WIDTH + _BC_WIDTH + LANES
_U2_SMALL = 2 * SSD_WIDTH + _BC_WIDTH


def _params(sem, vmem_bytes):
    return pltpu.CompilerParams(dimension_semantics=sem,
                                vmem_limit_bytes=int(min(VMEM_CAP, vmem_bytes)))


def _rms(xf, w):
    ms = jnp.mean(xf * xf, axis=-1, keepdims=True)
    return xf * lax.rsqrt(ms + EPS) * w


def _split3(a):
    hi = a.astype(BF16)
    r1 = a - hi.astype(F32)
    mid = r1.astype(BF16)
    lo = (r1 - mid.astype(F32)).astype(BF16)
    return hi, mid, lo


def _tri_left(tri, a):
    return sum(jnp.dot(tri, t, preferred_element_type=F32) for t in _split3(a))


def _tri_right(a, tri):
    return sum(jnp.dot(t, tri, preferred_element_type=F32) for t in _split3(a))


def _softplus(v):
    return jnp.maximum(v, 0.0) + jnp.log1p(jnp.exp(-jnp.abs(v)))


def _silu(v):
    return v * jax.nn.sigmoid(v)


def _norm_matmul_kernel(x_ref, nw_ref, w_ref, o_ref, h_sc):
    @pl.when(pl.program_id(1) == 0)
    def _():
        h_sc[...] = _rms(x_ref[...], nw_ref[...]).astype(BF16)

    o_ref[...] = jnp.dot(h_sc[...], w_ref[...],
                         preferred_element_type=F32).astype(o_ref.dtype)


def _norm_matmul(x, nw, w, out_dtype, tm, tn, name):
    m, k = x.shape
    n = w.shape[1]
    ob = jnp.dtype(out_dtype).itemsize
    vmem = 2 * (tm * k * 4 + k * tn * 2 + tm * tn * ob) + tm * k * 2 + 3 * tm * k * 4
    return pl.pallas_call(
        _norm_matmul_kernel,
        out_shape=jax.ShapeDtypeStruct((m, n), out_dtype),
        grid=(m // tm, n // tn),
        in_specs=[pl.BlockSpec((tm, k), lambda i, j: (i, 0)),
                  pl.BlockSpec((1, k), lambda i, j: (0, 0)),
                  pl.BlockSpec((k, tn), lambda i, j: (0, j))],
        out_specs=pl.BlockSpec((tm, tn), lambda i, j: (i, j)),
        scratch_shapes=[pltpu.VMEM((tm, k), BF16)],
        compiler_params=_params(("arbitrary", "arbitrary"), vmem),
        name=name,
    )(x, nw, w)


def _fox_c_kernel(f_ref, b_ref, c_ref, carry_sc):
    @pl.when(pl.program_id(0) == 0)
    def _():
        carry_sc[...] = jnp.zeros_like(carry_sc)

    z = f_ref[...] + b_ref[...]
    lf = jnp.minimum(z, 0.0) - jnp.log1p(jnp.exp(-jnp.abs(z)))
    t = lf.shape[0]
    row = lax.broadcasted_iota(jnp.int32, (t, t), 0)
    col = lax.broadcasted_iota(jnp.int32, (t, t), 1)
    tril = (col <= row).astype(BF16)
    cum = _tri_left(tril, lf) + carry_sc[...]
    c_ref[...] = cum
    carry_sc[...] = cum[t - 1:t, :]


def _fox_c(u2, b_pad, t=256):
    s = u2.shape[0]
    return pl.pallas_call(
        _fox_c_kernel,
        out_shape=jax.ShapeDtypeStruct((s, LANES), F32),
        grid=(s // t,),
        in_specs=[pl.BlockSpec((t, LANES), lambda i: (i, _U2_SMALL // LANES)),
                  pl.BlockSpec((1, LANES), lambda i: (0, 0))],
        out_specs=pl.BlockSpec((t, LANES), lambda i: (i, 0)),
        scratch_shapes=[pltpu.VMEM((1, LANES), F32)],
        compiler_params=_params(("arbitrary",), 8 * 1024 * 1024),
        name="fox_cumlogf",
    )(u2, b_pad)


def _fox_attn_kernel(q_ref, k_ref, v_ref, cq_ref, ck_ref, o_ref, m_sc, l_sc, acc_sc, *, t):
    qi = pl.program_id(1)
    row = lax.broadcasted_iota(jnp.int32, (t, t), 0)
    col = lax.broadcasted_iota(jnp.int32, (t, t), 1)
    causal = col <= row
    scale = HEAD_DIM ** -0.5

    for e in range(2):
        lanes = slice(e * HEAD_DIM, (e + 1) * HEAD_DIM)
        q = q_ref[:, lanes] * jnp.asarray(scale, BF16)
        cq = cq_ref[e]
        m_sc[...] = jnp.full_like(m_sc, -1e30)
        l_sc[...] = jnp.zeros_like(l_sc)
        acc_sc[...] = jnp.zeros_like(acc_sc)

        def block(kb, masked):
            start = pl.multiple_of(kb * t, t)
            k = k_ref[pl.ds(start, t), lanes]
            v = v_ref[pl.ds(start, t), lanes]
            ck = ck_ref[e, :, pl.ds(start, t)]
            s = lax.dot_general(q, k, (((1,), (1,)), ((), ())),
                                preferred_element_type=F32)
            s = s + (cq - ck)
            if masked:
                s = jnp.where(causal, s, -jnp.inf)
            m_prev = m_sc[...]
            m_new = jnp.maximum(m_prev, jnp.max(s, axis=1, keepdims=True))
            p = jnp.exp(s - m_new)
            alpha = jnp.exp(m_prev - m_new)
            l_sc[...] = alpha * l_sc[...] + jnp.sum(p, axis=1, keepdims=True)
            acc_sc[...] = alpha * acc_sc[...] + jnp.dot(
                p.astype(BF16), v, preferred_element_type=F32)
            m_sc[...] = m_new

        def body(kb, carry):
            block(kb, False)
            return carry

        lax.fori_loop(0, qi, body, 0)
        block(qi, True)
        o_ref[:, lanes] = (acc_sc[...] / l_sc[...]).astype(o_ref.dtype)


def _fox_attn(qkv, cq, ck, t=512):
    s = qkv.shape[0]
    n_pairs = FOX_HEADS // 2
    kcol = FOX_WIDTH // LANES
    vmem = 2 * (t * LANES * 2 * 2 + 2 * s * LANES * 2 + 2 * t * LANES * 4 + 2 * 8 * s * 4) \
        + 8 * t * t * 4
    return pl.pallas_call(
        functools.partial(_fox_attn_kernel, t=t),
        out_shape=jax.ShapeDtypeStruct((s, FOX_WIDTH), BF16),
        grid=(n_pairs, s // t),
        in_specs=[pl.BlockSpec((t, LANES), lambda pr, qi: (qi, pr)),
                  pl.BlockSpec((s, LANES), lambda pr, qi: (0, kcol + pr)),
                  pl.BlockSpec((s, LANES), lambda pr, qi: (0, 2 * kcol + pr)),
                  pl.BlockSpec((2, t, 1), lambda pr, qi: (pr, qi, 0)),
                  pl.BlockSpec((2, 1, s), lambda pr, qi: (pr, 0, 0))],
        out_specs=pl.BlockSpec((t, LANES), lambda pr, qi: (qi, pr)),
        scratch_shapes=[pltpu.VMEM((t, 1), F32), pltpu.VMEM((t, 1), F32),
                        pltpu.VMEM((t, HEAD_DIM), F32)],
        compiler_params=_params(("arbitrary", "arbitrary"), vmem),
        name="fox_attention",
    )(qkv, qkv, qkv, cq, ck)


def _ssd_kernel(xs_ref, z_ref, bc_ref, dt_ref, dtT_ref, cwx_ref, cbx_ref, cwbc_ref, cbbc_ref,
                dtb_ref, dtbT_ref, alog_ref, alogT_ref, dskip_ref, nw_ref, y_ref,
                xbuf, bcbuf, state_sc, ybuf, *, L):
    i = pl.program_id(0)
    halo = SUBLANES

    @pl.when(i == 0)
    def _():
        xbuf[0:halo, :] = jnp.zeros((halo, xbuf.shape[1]), F32)
        bcbuf[0:halo, :] = jnp.zeros((halo, bcbuf.shape[1]), F32)
        state_sc[...] = jnp.zeros_like(state_sc)

    @pl.when(i > 0)
    def _():
        xbuf[0:halo, :] = xbuf[L:L + halo, :]
        bcbuf[0:halo, :] = bcbuf[L:L + halo, :]

    xbuf[halo:halo + L, :] = xs_ref[...]
    bcbuf[halo:halo + L, :] = bc_ref[...]

    def conv_silu(buf, w_ref, b_ref):
        acc = b_ref[...]
        for k in range(SSD_CONV):
            off = halo - (SSD_CONV - 1) + k
            acc = acc + buf[off:off + L, :] * w_ref[k:k + 1, :]
        return _silu(acc)

    xc = conv_silu(xbuf, cwx_ref, cbx_ref)
    bcc = conv_silu(bcbuf, cwbc_ref, cbbc_ref)

    dt = _softplus(dt_ref[...] + dtb_ref[...])
    dtT = _softplus(dtT_ref[...] + dtbT_ref[...])
    a = dt * (-jnp.exp(alog_ref[...]))
    aT = dtT * (-jnp.exp(alogT_ref[...]))

    row = lax.broadcasted_iota(jnp.int32, (L, L), 0)
    col = lax.broadcasted_iota(jnp.int32, (L, L), 1)
    causal = col <= row
    a_cum = _tri_left(causal.astype(BF16), a)
    a_cumT = _tri_right(aT, (row <= col).astype(BF16))
    a_last = a_cum[L - 1:L, :]
    a_lastT = a_cumT[:, L - 1:L]
    exp_acum = jnp.exp(a_cum)
    w_endT = jnp.exp(a_lastT - a_cumT) * dtT
    chunk_decay = jnp.exp(a_last)

    n = SSD_STATE
    heads_per_group = SSD_HEADS // SSD_GROUPS
    for g in range(SSD_GROUPS):
        bg = bcc[:, g * n:(g + 1) * n]
        cg = bcc[:, SSD_GROUPS * n + g * n:SSD_GROUPS * n + (g + 1) * n]
        bgT = bg.T
        cb = lax.dot_general(cg.astype(BF16), bg.astype(BF16), (((1,), (1,)), ((), ())),
                             preferred_element_type=F32)
        for hh in range(heads_per_group):
            h = g * heads_per_group + hh
            cols = slice(h * HEAD_DIM, (h + 1) * HEAD_DIM)
            seg = a_cum[:, h:h + 1] - a_cumT[h:h + 1, :]
            decay = jnp.exp(jnp.where(causal, seg, -jnp.inf))
            mix = cb * decay * dtT[h:h + 1, :]
            xh = xc[:, cols]
            xh_b = xh.astype(BF16)
            st = state_sc[:, cols]
            y = jnp.dot(mix.astype(BF16), xh_b, preferred_element_type=F32)
            y = y + jnp.dot((cg * exp_acum[:, h:h + 1]).astype(BF16), st.astype(BF16),
                            preferred_element_type=F32)
            ybuf[:, cols] = y + dskip_ref[:, cols] * xh
            upd = jnp.dot((bgT * w_endT[h:h + 1, :]).astype(BF16), xh_b,
                          preferred_element_type=F32)
            state_sc[:, cols] = chunk_decay[:, h:h + 1] * st + upd

    y = ybuf[...] * _silu(z_ref[...])
    y_ref[...] = _rms(y, nw_ref[...]).astype(y_ref.dtype)


def _ssd(u2, dt_raw, dt_rawT, cwx, cbx, cwbc, cbbc, dtb, dtbT, alog, alogT, dskip, nw, L=256):
    s = u2.shape[0]
    w = SSD_WIDTH
    full = lambda shape: pl.BlockSpec(shape, lambda i: (0,) * len(shape))
    vmem = 2 * (2 * L * w * 4 + L * _BC_WIDTH * 4 + L * w * 2) + (2 * L + 16) * w * 4 \
        + (L + 8) * _BC_WIDTH * 4 + SSD_STATE * w * 4 + 24 * L * L * 4 + 6 * L * w * 4
    return pl.pallas_call(
        functools.partial(_ssd_kernel, L=L),
        out_shape=jax.ShapeDtypeStruct((s, w), BF16),
        grid=(s // L,),
        in_specs=[pl.BlockSpec((L, w), lambda i: (i, 0)),
                  pl.BlockSpec((L, w), lambda i: (i, 1)),
                  pl.BlockSpec((L, _BC_WIDTH), lambda i: (i, 2 * w // _BC_WIDTH)),
                  pl.BlockSpec((L, SSD_HEADS), lambda i: (i, 0)),
                  pl.BlockSpec((SSD_HEADS, L), lambda i: (0, i)),
                  full((SSD_CONV, w)), full((1, w)),
                  full((SSD_CONV, _BC_WIDTH)), full((1, _BC_WIDTH)),
                  full((1, SSD_HEADS)), full((SSD_HEADS, 1)),
                  full((1, SSD_HEADS)), full((SSD_HEADS, 1)),
                  full((1, w)), full((1, w))],
        out_specs=pl.BlockSpec((L, w), lambda i: (i, 0)),
        scratch_shapes=[pltpu.VMEM((L + SUBLANES, w), F32),
                        pltpu.VMEM((L + SUBLANES, _BC_WIDTH), F32),
                        pltpu.VMEM((SSD_STATE, w), F32),
                        pltpu.VMEM((L, w), F32)],
        compiler_params=_params(("arbitrary",), vmem),
        name="ssd_mixer",
    )(u2, u2, u2, dt_raw, dt_rawT, cwx, cbx, cwbc, cbbc, dtb, dtbT, alog, alogT, dskip, nw)


def _out_proj_kernel(x_ref, ya_ref, yb_ref, wa_ref, wb_ref, o_ref):
    o_ref[...] = (x_ref[...]
                  + jnp.dot(ya_ref[...], wa_ref[...], preferred_element_type=F32)
                  + jnp.dot(yb_ref[...], wb_ref[...], preferred_element_type=F32))


def _out_proj(x, ya, yb, wa, wb, tm=512):
    m, d = x.shape
    ka, kb = ya.shape[1], yb.shape[1]
    vmem = 2 * (2 * tm * d * 4 + tm * (ka + kb) * 2 + (ka + kb) * d * 2) + 2 * tm * d * 4
    return pl.pallas_call(
        _out_proj_kernel,
        out_shape=jax.ShapeDtypeStruct((m, d), F32),
        grid=(m // tm,),
        in_specs=[pl.BlockSpec((tm, d), lambda i: (i, 0)),
                  pl.BlockSpec((tm, ka), lambda i: (i, 0)),
                  pl.BlockSpec((tm, kb), lambda i: (i, 0)),
                  pl.BlockSpec((ka, d), lambda i: (0, 0)),
                  pl.BlockSpec((kb, d), lambda i: (0, 0))],
        out_specs=pl.BlockSpec((tm, d), lambda i: (i, 0)),
        compiler_params=_params(("arbitrary",), vmem),
        name="mix_out_proj",
    )(x, ya, yb, wa, wb)


def _gate_up_kernel(x_ref, nw_ref, wg_ref, wu_ref, cw_ref, cb_ref, o_ref,
                    h_sc, tail_sc, gbuf, *, tm):
    i = pl.program_id(0)
    j = pl.program_id(1)
    halo = SUBLANES

    @pl.when(j == 0)
    def _():
        h_sc[...] = _rms(x_ref[...], nw_ref[...]).astype(BF16)

    h = h_sc[...]
    g = jnp.dot(h, wg_ref[...], preferred_element_type=F32)
    up = jnp.dot(h, wu_ref[...], preferred_element_type=F32)

    @pl.when(i == 0)
    def _():
        gbuf[0:halo, :] = jnp.zeros((halo, gbuf.shape[1]), F32)

    @pl.when(i > 0)
    def _():
        gbuf[0:halo, :] = tail_sc[j]

    gbuf[halo:halo + tm, :] = g
    tail_sc[j] = gbuf[tm:tm + halo, :]

    acc = cb_ref[...]
    for k in range(FFN_CONV):
        off = halo - (FFN_CONV - 1) + k
        acc = acc + gbuf[off:off + tm, :] * cw_ref[k:k + 1, :]
    o_ref[...] = (_silu(acc) * up).astype(o_ref.dtype)


def _gate_up(x, nw, w_gu, cw, cb, tm=512, tn=512):
    m, d = x.shape
    nj = D_FF // tn
    vmem = 2 * (tm * d * 4 + 2 * d * tn * 2 + tm * tn * 2) + tm * d * 2 + 3 * tm * d * 4 \
        + (nj * 8 + tm + 8) * tn * 4 + 6 * tm * tn * 4
    return pl.pallas_call(
        functools.partial(_gate_up_kernel, tm=tm),
        out_shape=jax.ShapeDtypeStruct((m, D_FF), BF16),
        grid=(m // tm, nj),
        in_specs=[pl.BlockSpec((tm, d), lambda i, j: (i, 0)),
                  pl.BlockSpec((1, d), lambda i, j: (0, 0)),
                  pl.BlockSpec((d, tn), lambda i, j: (0, j)),
                  pl.BlockSpec((d, tn), lambda i, j: (0, j + nj)),
                  pl.BlockSpec((FFN_CONV, tn), lambda i, j: (0, j)),
                  pl.BlockSpec((1, tn), lambda i, j: (0, j))],
        out_specs=pl.BlockSpec((tm, tn), lambda i, j: (i, j)),
        scratch_shapes=[pltpu.VMEM((tm, d), BF16),
                        pltpu.VMEM((nj, SUBLANES, tn), F32),
                        pltpu.VMEM((tm + SUBLANES, tn), F32)],
        compiler_params=_params(("arbitrary", "arbitrary"), vmem),
        name="ffn_gate_up",
    )(x, nw, w_gu, w_gu, cw, cb)


def _down_kernel(a_ref, w_ref, x_ref, o_ref):
    @pl.when(pl.program_id(1) == 0)
    def _():
        o_ref[...] = x_ref[...]

    o_ref[...] += jnp.dot(a_ref[...], w_ref[...], preferred_element_type=F32)


def _down(act, w, x, tm=512, tk=512):
    m, d = x.shape
    kk = act.shape[1]
    vmem = 2 * (tm * tk * 2 + tk * d * 2 + 2 * tm * d * 4) + 2 * tm * d * 4
    return pl.pallas_call(
        _down_kernel,
        out_shape=jax.ShapeDtypeStruct((m, d), F32),
        grid=(m // tm, kk // tk),
        in_specs=[pl.BlockSpec((tm, tk), lambda i, k: (i, k)),
                  pl.BlockSpec((tk, d), lambda i, k: (k, 0)),
                  pl.BlockSpec((tm, d), lambda i, k: (i, 0))],
        out_specs=pl.BlockSpec((tm, d), lambda i, k: (i, 0)),
        compiler_params=_params(("arbitrary", "arbitrary"), vmem),
        name="ffn_down",
    )(act, w, x)


def _ple_kernel(x_ref, p_ref, nw_ref, wg_ref, wp_ref, fw_ref, o_ref, *, final):
    x = x_ref[...]
    h = _rms(x, nw_ref[...]).astype(BF16)
    gate = jax.nn.sigmoid(jnp.dot(h, wg_ref[...], preferred_element_type=F32))
    proj = jnp.dot(p_ref[...].astype(BF16), wp_ref[...], preferred_element_type=F32)
    x = x + gate * proj
    if final:
        x = _rms(x, fw_ref[...])
    o_ref[...] = x


def _ple(x, p, nw, wg, wp, fw, final, tm=256):
    m, d = x.shape
    dp = p.shape[1]
    vmem = 2 * (2 * tm * d * 4 + tm * dp * 4 + d * d * 2 + dp * d * 2) + 6 * tm * d * 4
    return pl.pallas_call(
        functools.partial(_ple_kernel, final=final),
        out_shape=jax.ShapeDtypeStruct((m, d), F32),
        grid=(m // tm,),
        in_specs=[pl.BlockSpec((tm, d), lambda i: (i, 0)),
                  pl.BlockSpec((tm, dp), lambda i: (i, 0)),
                  pl.BlockSpec((1, d), lambda i: (0, 0)),
                  pl.BlockSpec((d, d), lambda i: (0, 0)),
                  pl.BlockSpec((dp, d), lambda i: (0, 0)),
                  pl.BlockSpec((1, d), lambda i: (0, 0))],
        out_specs=pl.BlockSpec((tm, d), lambda i: (i, 0)),
        compiler_params=_params(("arbitrary",), vmem),
        name="ple_gate",
    )(x, p, nw, wg, wp, fw)


def _layer(x, p, mix_norm_w, w_in, fox_forget_bias, ssd_conv_w, ssd_conv_b, ssd_dt_bias,
           ssd_A_log, ssd_D, ssd_norm_w, w_out, ffn_norm_w, w_gate_up, ffn_conv_w,
           ffn_conv_b, w_down, ple_norm_w, w_ple_gate, w_ple_proj, final_norm_w, final):
    s = x.shape[0]
    row = lambda v: v.reshape(1, -1).astype(F32)

    w_qkv = w_in[:, _OFF_Q:_OFF_F].astype(BF16)
    w_u2 = jnp.concatenate(
        [w_in[:, _OFF_XS:_OFF_B], w_in[:, _OFF_Z:_OFF_DT], w_in[:, _OFF_B:_OFF_Z],
         w_in[:, _OFF_F:_OFF_XS], w_in[:, _OFF_DT:_IN_COLS],
         jnp.zeros((D_MODEL, LANES - FOX_HEADS - SSD_HEADS), w_in.dtype)], axis=1).astype(BF16)

    nw = row(mix_norm_w)
    qkv = _norm_matmul(x, nw, w_qkv, BF16, tm=1024, tn=512, name="in_proj_qkv")
    u2 = _norm_matmul(x, nw, w_u2, F32, tm=1024, tn=_U2_COLS // 3, name="in_proj_ssd")

    b_pad = jnp.zeros((1, LANES), F32).at[0, :FOX_HEADS].set(fox_forget_bias.astype(F32))
    c = _fox_c(u2, b_pad)
    cT = c[:, :FOX_HEADS].T
    y_fox = _fox_attn(qkv, cT[:, :, None], cT[:, None, :])

    dt_raw = u2[:, _U2_SMALL + FOX_HEADS:_U2_SMALL + FOX_HEADS + SSD_HEADS]
    col = lambda v: v.reshape(-1, 1).astype(F32)
    y_ssd = _ssd(u2, dt_raw, dt_raw.T,
                 ssd_conv_w[:, :SSD_WIDTH].astype(F32), row(ssd_conv_b[:SSD_WIDTH]),
                 ssd_conv_w[:, SSD_WIDTH:].astype(F32), row(ssd_conv_b[SSD_WIDTH:]),
                 row(ssd_dt_bias), col(ssd_dt_bias), row(ssd_A_log), col(ssd_A_log),
                 row(jnp.repeat(ssd_D, HEAD_DIM)), row(ssd_norm_w))

    w_out_b = w_out.astype(BF16)
    x = _out_proj(x, y_fox, y_ssd, w_out_b[:FOX_WIDTH], w_out_b[FOX_WIDTH:])

    act = _gate_up(x, row(ffn_norm_w), w_gate_up.astype(BF16),
                   ffn_conv_w.astype(F32), row(ffn_conv_b))
    x = _down(act, w_down.astype(BF16), x)

    return _ple(x, p, row(ple_norm_w), w_ple_gate.astype(BF16), w_ple_proj.astype(BF16),
                row(final_norm_w), final)


def kernel(x, p, mix_norm_w, w_in, fox_forget_bias, ssd_conv_w, ssd_conv_b, ssd_dt_bias,
           ssd_A_log, ssd_D, ssd_norm_w, w_out, ffn_norm_w, w_gate_up, ffn_conv_w,
           ffn_conv_b, w_down, ple_norm_w, w_ple_gate, w_ple_proj, final_norm_w):
    bsz, s, d = x.shape
    depth = p.shape[0]
    outs = []
    for b in range(bsz):
        xb = x[b]
        for i in range(depth):
            xb = _layer(xb, p[i, b], mix_norm_w[i], w_in[i], fox_forget_bias[i], ssd_conv_w[i],
                        ssd_conv_b[i], ssd_dt_bias[i], ssd_A_log[i], ssd_D[i], ssd_norm_w[i],
                        w_out[i], ffn_norm_w[i], w_gate_up[i], ffn_conv_w[i], ffn_conv_b[i],
                        w_down[i], ple_norm_w[i], w_ple_gate[i], w_ple_proj[i], final_norm_w,
                        final=(i == depth - 1))
        outs.append(xb)
    return jnp.stack(outs, axis=0)
```

```python
import functools

import jax
import jax.numpy as jnp
from jax import lax
from jax.experimental import pallas as pl
from jax.experimental.pallas import tpu as pltpu

F32 = jnp.float32
BF16 = jnp.bfloat16

D_MODEL = 2048
SEQ = 8192
D_PLE = 256
HEAD_DIM = 64
FOX_WIDTH = 1024
FOX_HEADS = 16
SSD_WIDTH = 1024
SSD_HEADS = 16
SSD_GROUPS = 2
SSD_STATE = 128
SSD_CONV = 4
D_FF = 5632
FFN_CONV = 3
EPS = 1e-6

_OFF_Q = 0
_OFF_F = 3 * FOX_WIDTH
_OFF_XS = _OFF_F + FOX_HEADS
_OFF_B = _OFF_XS + SSD_WIDTH
_OFF_C = _OFF_B + SSD_GROUPS * SSD_STATE
_OFF_Z = _OFF_C + SSD_GROUPS * SSD_STATE
_OFF_DT = _OFF_Z + SSD_WIDTH
_IN_COLS = _OFF_DT + SSD_HEADS

LANES = 128
SUBLANES = 8
VMEM_CAP = 60 * 1024 * 1024

_BC_WIDTH = 2 * SSD_GROUPS * SSD_STATE
_U2_COLS = 2 * SSD_WIDTH + _BC_WIDTH + LANES
_U2_SMALL = 2 * SSD_WIDTH + _BC_WIDTH


def _params(sem, vmem_bytes):
    return pltpu.CompilerParams(dimension_semantics=sem,
                                vmem_limit_bytes=int(min(VMEM_CAP, vmem_bytes)))


def _rms(xf, w):
    ms = jnp.mean(xf * xf, axis=-1, keepdims=True)
    return xf * lax.rsqrt(ms + EPS) * w


def _split3(a):
    hi = a.astype(BF16)
    r1 = a - hi.astype(F32)
    mid = r1.astype(BF16)
    lo = (r1 - mid.astype(F32)).astype(BF16)
    return hi, mid, lo


def _tri_left(tri, a):
    return sum(jnp.dot(tri, t, preferred_element_type=F32) for t in _split3(a))


def _tri_right(a, tri):
    return sum(jnp.dot(t, tri, preferred_element_type=F32) for t in _split3(a))


def _softplus(v):
    return jnp.maximum(v, 0.0) + jnp.log1p(jnp.exp(-jnp.abs(v)))


def _silu(v):
    return v * jax.nn.sigmoid(v)


def _norm_matmul_kernel(x_ref, nw_ref, w_ref, o_ref, h_sc):
    @pl.when(pl.program_id(1) == 0)
    def _():
        h_sc[...] = _rms(x_ref[...], nw_ref[...]).astype(BF16)

    o_ref[...] = jnp.dot(h_sc[...], w_ref[...],
                         preferred_element_type=F32).astype(o_ref.dtype)


def _norm_matmul(x, nw, w, out_dtype, tm, tn, name):
    m, k = x.shape
    n = w.shape[1]
    ob = jnp.dtype(out_dtype).itemsize
    vmem = 2 * (tm * k * 4 + k * tn * 2 + tm * tn * ob) + tm * k * 2 + 3 * tm * k * 4
    return pl.pallas_call(
        _norm_matmul_kernel,
        out_shape=jax.ShapeDtypeStruct((m, n), out_dtype),
        grid=(m // tm, n // tn),
        in_specs=[pl.BlockSpec((tm, k), lambda i, j: (i, 0)),
                  pl.BlockSpec((1, k), lambda i, j: (0, 0)),
                  pl.BlockSpec((k, tn), lambda i, j: (0, j))],
        out_specs=pl.BlockSpec((tm, tn), lambda i, j: (i, j)),
        scratch_shapes=[pltpu.VMEM((tm, k), BF16)],
        compiler_params=_params(("arbitrary", "arbitrary"), vmem),
        name=name,
    )(x, nw, w)


def _fox_c_kernel(f_ref, b_ref, c_ref, carry_sc):
    @pl.when(pl.program_id(0) == 0)
    def _():
        carry_sc[...] = jnp.zeros_like(carry_sc)

    z = f_ref[...] + b_ref[...]
    lf = jnp.minimum(z, 0.0) - jnp.log1p(jnp.exp(-jnp.abs(z)))
    t = lf.shape[0]
    row = lax.broadcasted_iota(jnp.int32, (t, t), 0)
    col = lax.broadcasted_iota(jnp.int32, (t, t), 1)
    tril = (col <= row).astype(BF16)
    cum = _tri_left(tril, lf) + carry_sc[...]
    c_ref[...] = cum
    carry_sc[...] = cum[t - 1:t, :]


def _fox_c(u2, b_pad, t=256):
    s = u2.shape[0]
    return pl.pallas_call(
        _fox_c_kernel,
        out_shape=jax.ShapeDtypeStruct((s, LANES), F32),
        grid=(s // t,),
        in_specs=[pl.BlockSpec((t, LANES), lambda i: (i, _U2_SMALL // LANES)),
                  pl.BlockSpec((1, LANES), lambda i: (0, 0))],
        out_specs=pl.BlockSpec((t, LANES), lambda i: (i, 0)),
        scratch_shapes=[pltpu.VMEM((1, LANES), F32)],
        compiler_params=_params(("arbitrary",), 8 * 1024 * 1024),
        name="fox_cumlogf",
    )(u2, b_pad)


def _fox_attn_kernel(q_ref, k_ref, v_ref, cq_ref, ck_ref, o_ref, vT_sc, acc_sc, *, t):
    qi = pl.program_id(1)
    n_kv = k_ref.shape[0] // t

    @pl.when(qi == 0)
    def _():
        def transpose_v(c, carry):
            st = pl.multiple_of(c * t, t)
            vT_sc[:, pl.ds(st, t)] = v_ref[pl.ds(st, t), :].astype(F32).T.astype(BF16)
            return carry
        lax.fori_loop(0, n_kv, transpose_v, 0)

    scale = HEAD_DIM ** -0.5
    qT = q_ref[...].astype(F32).T * scale
    head_of_row = lax.broadcasted_iota(jnp.int32, (LANES, t), 0) // HEAD_DIM
    key_idx = lax.broadcasted_iota(jnp.int32, (t, t), 0)
    qry_idx = lax.broadcasted_iota(jnp.int32, (t, t), 1)
    causal = key_idx <= qry_idx
    l_rows = []

    for e in range(2):
        rows = slice(e * HEAD_DIM, (e + 1) * HEAD_DIM)
        qTe = jnp.where(head_of_row == e, qT, 0.0).astype(BF16)
        cq = cq_ref[e]
        acc_sc[rows, :] = jnp.zeros((HEAD_DIM, t), F32)

        def block(kb, m_prev, l_prev, masked, e=e, rows=rows, qTe=qTe, cq=cq):
            start = pl.multiple_of(kb * t, t)
            s = jnp.dot(k_ref[pl.ds(start, t), :], qTe,
                        preferred_element_type=F32)
            s = s - ck_ref[e, pl.ds(start, t), :]
            if masked:
                s = jnp.where(causal, s, -jnp.inf)
            m_new = jnp.maximum(m_prev, jnp.max(s, axis=0, keepdims=True) + cq)
            p = jnp.exp(s + (cq - m_new))
            alpha = jnp.exp(m_prev - m_new)
            l_new = alpha * l_prev + jnp.sum(p, axis=0, keepdims=True)
            acc_sc[rows, :] = alpha * acc_sc[rows, :] + jnp.dot(
                vT_sc[rows, pl.ds(start, t)], p.astype(BF16), preferred_element_type=F32)
            return m_new, l_new

        init = (jnp.full((1, t), -1e30, F32), jnp.zeros((1, t), F32))
        m, l = lax.fori_loop(0, qi, lambda kb, c: block(kb, c[0], c[1], False), init)
        m, l = block(qi, m, l, True)
        l_rows.append(l)

    l_all = jnp.where(head_of_row == 0, l_rows[0], l_rows[1])
    o_ref[...] = (acc_sc[...] / l_all).T.astype(o_ref.dtype)


def _fox_attn(qkv, cq, ck, t=512):
    s = qkv.shape[0]
    n_pairs = FOX_HEADS // 2
    kcol = FOX_WIDTH // LANES
    vmem = 2 * (2 * t * LANES * 2 + 2 * s * LANES * 2 + 2 * s * LANES * 4) \
        + s * LANES * 2 + t * LANES * 4 + 10 * t * t * 4
    return pl.pallas_call(
        functools.partial(_fox_attn_kernel, t=t),
        out_shape=jax.ShapeDtypeStruct((s, FOX_WIDTH), BF16),
        grid=(n_pairs, s // t),
        in_specs=[pl.BlockSpec((t, LANES), lambda pr, qi: (qi, pr)),
                  pl.BlockSpec((s, LANES), lambda pr, qi: (0, kcol + pr)),
                  pl.BlockSpec((s, LANES), lambda pr, qi: (0, 2 * kcol + pr)),
                  pl.BlockSpec((2, 1, t), lambda pr, qi: (pr, 0, qi)),
                  pl.BlockSpec((2, s, 1), lambda pr, qi: (pr, 0, 0))],
        out_specs=pl.BlockSpec((t, LANES), lambda pr, qi: (qi, pr)),
        scratch_shapes=[pltpu.VMEM((LANES, s), BF16), pltpu.VMEM((LANES, t), F32)],
        compiler_params=_params(("arbitrary", "arbitrary"), vmem),
        name="fox_attention",
    )(qkv, qkv, qkv, cq, ck)


def _ssd_kernel(xs_ref, z_ref, bc_ref, dt_ref, dtT_ref, cwx_ref, cbx_ref, cwbc_ref, cbbc_ref,
                dtb_ref, dtbT_ref, alog_ref, alogT_ref, dskip_ref, nw_ref, y_ref,
                xbuf, bcbuf, state_sc, ybuf, *, L):
    i = pl.program_id(0)
    halo = SUBLANES

    @pl.when(i == 0)
    def _():
        xbuf[0:halo, :] = jnp.zeros((halo, xbuf.shape[1]), F32)
        bcbuf[0:halo, :] = jnp.zeros((halo, bcbuf.shape[1]), F32)
        state_sc[...] = jnp.zeros_like(state_sc)

    @pl.when(i > 0)
    def _():
        xbuf[0:halo, :] = xbuf[L:L + halo, :]
        bcbuf[0:halo, :] = bcbuf[L:L + halo, :]

    xbuf[halo:halo + L, :] = xs_ref[...]
    bcbuf[halo:halo + L, :] = bc_ref[...]

    def conv_silu(buf, w_ref, b_ref):
        acc = b_ref[...]
        for k in range(SSD_CONV):
            off = halo - (SSD_CONV - 1) + k
            acc = acc + buf[off:off + L, :] * w_ref[k:k + 1, :]
        return _silu(acc)

    xc = conv_silu(xbuf, cwx_ref, cbx_ref)
    bcc = conv_silu(bcbuf, cwbc_ref, cbbc_ref)

    dt = _softplus(dt_ref[...] + dtb_ref[...])
    dtT = _softplus(dtT_ref[...] + dtbT_ref[...])
    a = dt * (-jnp.exp(alog_ref[...]))
    aT = dtT * (-jnp.exp(alogT_ref[...]))

    row = lax.broadcasted_iota(jnp.int32, (L, L), 0)
    col = lax.broadcasted_iota(jnp.int32, (L, L), 1)
    causal = col <= row
    a_cum = _tri_left(causal.astype(BF16), a)
    a_cumT = _tri_right(aT, (row <= col).astype(BF16))
    a_last = a_cum[L - 1:L, :]
    a_lastT = a_cumT[:, L - 1:L]
    exp_acum = jnp.exp(a_cum)
    w_endT = jnp.exp(a_lastT - a_cumT) * dtT
    chunk_decay = jnp.exp(a_last)

    n = SSD_STATE
    heads_per_group = SSD_HEADS // SSD_GROUPS
    for g in range(SSD_GROUPS):
        bg = bcc[:, g * n:(g + 1) * n]
        cg = bcc[:, SSD_GROUPS * n + g * n:SSD_GROUPS * n + (g + 1) * n]
        bgT = bg.T
        cb = lax.dot_general(cg.astype(BF16), bg.astype(BF16), (((1,), (1,)), ((), ())),
                             preferred_element_type=F32)
        for hh in range(heads_per_group):
            h = g * heads_per_group + hh
            cols = slice(h * HEAD_DIM, (h + 1) * HEAD_DIM)
            seg = a_cum[:, h:h + 1] - a_cumT[h:h + 1, :]
            decay = jnp.exp(jnp.where(causal, seg, -jnp.inf))
            mix = cb * decay * dtT[h:h + 1, :]
            xh = xc[:, cols]
            xh_b = xh.astype(BF16)
            st = state_sc[:, cols]
            y = jnp.dot(mix.astype(BF16), xh_b, preferred_element_type=F32)
            y = y + jnp.dot((cg * exp_acum[:, h:h + 1]).astype(BF16), st.astype(BF16),
                            preferred_element_type=F32)
            ybuf[:, cols] = y + dskip_ref[:, cols] * xh
            upd = jnp.dot((bgT * w_endT[h:h + 1, :]).astype(BF16), xh_b,
                          preferred_element_type=F32)
            state_sc[:, cols] = chunk_decay[:, h:h + 1] * st + upd

    y = ybuf[...] * _silu(z_ref[...])
    y_ref[...] = _rms(y, nw_ref[...]).astype(y_ref.dtype)


def _ssd(u2, dt_raw, dt_rawT, cwx, cbx, cwbc, cbbc, dtb, dtbT, alog, alogT, dskip, nw, L=256):
    s = u2.shape[0]
    w = SSD_WIDTH
    full = lambda shape: pl.BlockSpec(shape, lambda i: (0,) * len(shape))
    vmem = 2 * (2 * L * w * 4 + L * _BC_WIDTH * 4 + L * w * 2) + (2 * L + 16) * w * 4 \
        + (L + 8) * _BC_WIDTH * 4 + SSD_STATE * w * 4 + 24 * L * L * 4 + 6 * L * w * 4
    return pl.pallas_call(
        functools.partial(_ssd_kernel, L=L),
        out_shape=jax.ShapeDtypeStruct((s, w), BF16),
        grid=(s // L,),
        in_specs=[pl.BlockSpec((L, w), lambda i: (i, 0)),
                  pl.BlockSpec((L, w), lambda i: (i, 1)),
                  pl.BlockSpec((L, _BC_WIDTH), lambda i: (i, 2 * w // _BC_WIDTH)),
                  pl.BlockSpec((L, SSD_HEADS), lambda i: (i, 0)),
                  pl.BlockSpec((SSD_HEADS, L), lambda i: (0, i)),
                  full((SSD_CONV, w)), full((1, w)),
                  full((SSD_CONV, _BC_WIDTH)), full((1, _BC_WIDTH)),
                  full((1, SSD_HEADS)), full((SSD_HEADS, 1)),
                  full((1, SSD_HEADS)), full((SSD_HEADS, 1)),
                  full((1, w)), full((1, w))],
        out_specs=pl.BlockSpec((L, w), lambda i: (i, 0)),
        scratch_shapes=[pltpu.VMEM((L + SUBLANES, w), F32),
                        pltpu.VMEM((L + SUBLANES, _BC_WIDTH), F32),
                        pltpu.VMEM((SSD_STATE, w), F32),
                        pltpu.VMEM((L, w), F32)],
        compiler_params=_params(("arbitrary",), vmem),
        name="ssd_mixer",
    )(u2, u2, u2, dt_raw, dt_rawT, cwx, cbx, cwbc, cbbc, dtb, dtbT, alog, alogT, dskip, nw)


def _out_proj_kernel(x_ref, ya_ref, yb_ref, wa_ref, wb_ref, o_ref):
    o_ref[...] = (x_ref[...]
                  + jnp.dot(ya_ref[...], wa_ref[...], preferred_element_type=F32)
                  + jnp.dot(yb_ref[...], wb_ref[...], preferred_element_type=F32))


def _out_proj(x, ya, yb, wa, wb, tm=512):
    m, d = x.shape
    ka, kb = ya.shape[1], yb.shape[1]
    vmem = 2 * (2 * tm * d * 4 + tm * (ka + kb) * 2 + (ka + kb) * d * 2) + 2 * tm * d * 4
    return pl.pallas_call(
        _out_proj_kernel,
        out_shape=jax.ShapeDtypeStruct((m, d), F32),
        grid=(m // tm,),
        in_specs=[pl.BlockSpec((tm, d), lambda i: (i, 0)),
                  pl.BlockSpec((tm, ka), lambda i: (i, 0)),
                  pl.BlockSpec((tm, kb), lambda i: (i, 0)),
                  pl.BlockSpec((ka, d), lambda i: (0, 0)),
                  pl.BlockSpec((kb, d), lambda i: (0, 0))],
        out_specs=pl.BlockSpec((tm, d), lambda i: (i, 0)),
        compiler_params=_params(("arbitrary",), vmem),
        name="mix_out_proj",
    )(x, ya, yb, wa, wb)


def _gate_up_kernel(x_ref, nw_ref, wg_ref, wu_ref, cw_ref, cb_ref, o_ref,
                    h_sc, tail_sc, gbuf, *, tm):
    i = pl.program_id(0)
    j = pl.program_id(1)
    halo = SUBLANES

    @pl.when(j == 0)
    def _():
        h_sc[...] = _rms(x_ref[...], nw_ref[...]).astype(BF16)

    h = h_sc[...]
    g = jnp.dot(h, wg_ref[...], preferred_element_type=F32)
    up = jnp.dot(h, wu_ref[...], preferred_element_type=F32)

    @pl.when(i == 0)
    def _():
        gbuf[0:halo, :] = jnp.zeros((halo, gbuf.shape[1]), F32)

    @pl.when(i > 0)
    def _():
        gbuf[0:halo, :] = tail_sc[j]

    gbuf[halo:halo + tm, :] = g
    tail_sc[j] = gbuf[tm:tm + halo, :]

    acc = cb_ref[...]
    for k in range(FFN_CONV):
        off = halo - (FFN_CONV - 1) + k
        acc = acc + gbuf[off:off + tm, :] * cw_ref[k:k + 1, :]
    o_ref[...] = (_silu(acc) * up).astype(o_ref.dtype)


def _gate_up(x, nw, w_gu, cw, cb, tm=512, tn=512):
    m, d = x.shape
    nj = D_FF // tn
    vmem = 2 * (tm * d * 4 + 2 * d * tn * 2 + tm * tn * 2) + tm * d * 2 + 3 * tm * d * 4 \
        + (nj * 8 + tm + 8) * tn * 4 + 6 * tm * tn * 4
    return pl.pallas_call(
        functools.partial(_gate_up_kernel, tm=tm),
        out_shape=jax.ShapeDtypeStruct((m, D_FF), BF16),
        grid=(m // tm, nj),
        in_specs=[pl.BlockSpec((tm, d), lambda i, j: (i, 0)),
                  pl.BlockSpec((1, d), lambda i, j: (0, 0)),
                  pl.BlockSpec((d, tn), lambda i, j: (0, j)),
                  pl.BlockSpec((d, tn), lambda i, j: (0, j + nj)),
                  pl.BlockSpec((FFN_CONV, tn), lambda i, j: (0, j)),
                  pl.BlockSpec((1, tn), lambda i, j: (0, j))],
        out_specs=pl.BlockSpec((tm, tn), lambda i, j: (i, j)),
        scratch_shapes=[pltpu.VMEM((tm, d), BF16),
                        pltpu.VMEM((nj, SUBLANES, tn), F32),
                        pltpu.VMEM((tm + SUBLANES, tn), F32)],
        compiler_params=_params(("arbitrary", "arbitrary"), vmem),
        name="ffn_gate_up",
    )(x, nw, w_gu, w_gu, cw, cb)


def _down_kernel(a_ref, w_ref, x_ref, o_ref):
    @pl.when(pl.program_id(1) == 0)
    def _():
        o_ref[...] = x_ref[...]

    o_ref[...] += jnp.dot(a_ref[...], w_ref[...], preferred_element_type=F32)


def _down(act, w, x, tm=512, tk=512):
    m, d = x.shape
    kk = act.shape[1]
    vmem = 2 * (tm * tk * 2 + tk * d * 2 + 2 * tm * d * 4) + 2 * tm * d * 4
    return pl.pallas_call(
        _down_kernel,
        out_shape=jax.ShapeDtypeStruct((m, d), F32),
        grid=(m // tm, kk // tk),
        in_specs=[pl.BlockSpec((tm, tk), lambda i, k: (i, k)),
                  pl.BlockSpec((tk, d), lambda i, k: (k, 0)),
                  pl.BlockSpec((tm, d), lambda i, k: (i, 0))],
        out_specs=pl.BlockSpec((tm, d), lambda i, k: (i, 0)),
        compiler_params=_params(("arbitrary", "arbitrary"), vmem),
        name="ffn_down",
    )(act, w, x)


def _ple_kernel(x_ref, p_ref, nw_ref, wg_ref, wp_ref, fw_ref, o_ref, *, final):
    x = x_ref[...]
    h = _rms(x, nw_ref[...]).astype(BF16)
    gate = jax.nn.sigmoid(jnp.dot(h, wg_ref[...], preferred_element_type=F32))
    proj = jnp.dot(p_ref[...].astype(BF16), wp_ref[...], preferred_element_type=F32)
    x = x + gate * proj
    if final:
        x = _rms(x, fw_ref[...])
    o_ref[...] = x


def _ple(x, p, nw, wg, wp, fw, final, tm=256):
    m, d = x.shape
    dp = p.shape[1]
    vmem = 2 * (2 * tm * d * 4 + tm * dp * 4 + d * d * 2 + dp * d * 2) + 6 * tm * d * 4
    return pl.pallas_call(
        functools.partial(_ple_kernel, final=final),
        out_shape=jax.ShapeDtypeStruct((m, d), F32),
        grid=(m // tm,),
        in_specs=[pl.BlockSpec((tm, d), lambda i: (i, 0)),
                  pl.BlockSpec((tm, dp), lambda i: (i, 0)),
                  pl.BlockSpec((1, d), lambda i: (0, 0)),
                  pl.BlockSpec((d, d), lambda i: (0, 0)),
                  pl.BlockSpec((dp, d), lambda i: (0, 0)),
                  pl.BlockSpec((1, d), lambda i: (0, 0))],
        out_specs=pl.BlockSpec((tm, d), lambda i: (i, 0)),
        compiler_params=_params(("arbitrary",), vmem),
        name="ple_gate",
    )(x, p, nw, wg, wp, fw)


def _layer(x, p, mix_norm_w, w_in, fox_forget_bias, ssd_conv_w, ssd_conv_b, ssd_dt_bias,
           ssd_A_log, ssd_D, ssd_norm_w, w_out, ffn_norm_w, w_gate_up, ffn_conv_w,
           ffn_conv_b, w_down, ple_norm_w, w_ple_gate, w_ple_proj, final_norm_w, final):
    s = x.shape[0]
    row = lambda v: v.reshape(1, -1).astype(F32)

    w_qkv = w_in[:, _OFF_Q:_OFF_F].astype(BF16)
    w_u2 = jnp.concatenate(
        [w_in[:, _OFF_XS:_OFF_B], w_in[:, _OFF_Z:_OFF_DT], w_in[:, _OFF_B:_OFF_Z],
         w_in[:, _OFF_F:_OFF_XS], w_in[:, _OFF_DT:_IN_COLS],
         jnp.zeros((D_MODEL, LANES - FOX_HEADS - SSD_HEADS), w_in.dtype)], axis=1).astype(BF16)

    nw = row(mix_norm_w)
    qkv = _norm_matmul(x, nw, w_qkv, BF16, tm=1024, tn=512, name="in_proj_qkv")
    u2 = _norm_matmul(x, nw, w_u2, F32, tm=1024, tn=_U2_COLS // 3, name="in_proj_ssd")

    b_pad = jnp.zeros((1, LANES), F32).at[0, :FOX_HEADS].set(fox_forget_bias.astype(F32))
    c = _fox_c(u2, b_pad)
    cT = c[:, :FOX_HEADS].T
    y_fox = _fox_attn(qkv, cT[:, None, :], cT[:, :, None])

    dt_raw = u2[:, _U2_SMALL + FOX_HEADS:_U2_SMALL + FOX_HEADS + SSD_HEADS]
    col = lambda v: v.reshape(-1, 1).astype(F32)
    y_ssd = _ssd(u2, dt_raw, dt_raw.T,
                 ssd_conv_w[:, :SSD_WIDTH].astype(F32), row(ssd_conv_b[:SSD_WIDTH]),
                 ssd_conv_w[:, SSD_WIDTH:].astype(F32), row(ssd_conv_b[SSD_WIDTH:]),
                 row(ssd_dt_bias), col(ssd_dt_bias), row(ssd_A_log), col(ssd_A_log),
                 row(jnp.repeat(ssd_D, HEAD_DIM)), row(ssd_norm_w))

    w_out_b = w_out.astype(BF16)
    x = _out_proj(x, y_fox, y_ssd, w_out_b[:FOX_WIDTH], w_out_b[FOX_WIDTH:])

    act = _gate_up(x, row(ffn_norm_w), w_gate_up.astype(BF16),
                   ffn_conv_w.astype(F32), row(ffn_conv_b))
    x = _down(act, w_down.astype(BF16), x)

    return _ple(x, p, row(ple_norm_w), w_ple_gate.astype(BF16), w_ple_proj.astype(BF16),
                row(final_norm_w), final)


def kernel(x, p, mix_norm_w, w_in, fox_forget_bias, ssd_conv_w, ssd_conv_b, ssd_dt_bias,
           ssd_A_log, ssd_D, ssd_norm_w, w_out, ffn_norm_w, w_gate_up, ffn_conv_w,
           ffn_conv_b, w_down, ple_norm_w, w_ple_gate, w_ple_proj, final_norm_w):
    bsz, s, d = x.shape
    depth = p.shape[0]
    outs = []
    for b in range(bsz):
        xb = x[b]
        for i in range(depth):
            xb = _layer(xb, p[i, b], mix_norm_w[i], w_in[i], fox_forget_bias[i], ssd_conv_w[i],
                        ssd_conv_b[i], ssd_dt_bias[i], ssd_A_log[i], ssd_D[i], ssd_norm_w[i],
                        w_out[i], ffn_norm_w[i], w_gate_up[i], ffn_conv_w[i], ffn_conv_b[i],
                        w_down[i], ple_norm_w[i], w_ple_gate[i], w_ple_proj[i], final_norm_w,
                        final=(i == depth - 1))
        outs.append(xb)
    return jnp.stack(outs, axis=0)
```

```python
import functools

import numpy as np
import jax
import jax.numpy as jnp
from jax import lax
from jax.experimental import pallas as pl
from jax.experimental.pallas import tpu as pltpu

F32 = jnp.float32
BF16 = jnp.bfloat16

D_MODEL = 2048
SEQ = 8192
D_PLE = 256
HEAD_DIM = 64
FOX_WIDTH = 1024
FOX_HEADS = 16
SSD_WIDTH = 1024
SSD_HEADS = 16
SSD_GROUPS = 2
SSD_STATE = 128
SSD_CONV = 4
D_FF = 5632
FFN_CONV = 3
EPS = 1e-6

_OFF_Q = 0
_OFF_F = 3 * FOX_WIDTH
_OFF_XS = _OFF_F + FOX_HEADS
_OFF_B = _OFF_XS + SSD_WIDTH
_OFF_C = _OFF_B + SSD_GROUPS * SSD_STATE
_OFF_Z = _OFF_C + SSD_GROUPS * SSD_STATE
_OFF_DT = _OFF_Z + SSD_WIDTH
_IN_COLS = _OFF_DT + SSD_HEADS

LANES = 128
SUBLANES = 8
VMEM_CAP = 60 * 1024 * 1024

_BC_WIDTH = 2 * SSD_GROUPS * SSD_STATE
_U2_COLS = 2 * SSD_WIDTH + _BC_WIDTH + LANES
_U2_SMALL = 2 * SSD_WIDTH + _BC_WIDTH


def _params(sem, vmem_bytes):
    return pltpu.CompilerParams(dimension_semantics=sem,
                                vmem_limit_bytes=int(min(VMEM_CAP, vmem_bytes)))


def _rms(xf, w):
    ms = jnp.mean(xf * xf, axis=-1, keepdims=True)
    return xf * lax.rsqrt(ms + EPS) * w


def _split3(a):
    hi = a.astype(BF16)
    r1 = a - hi.astype(F32)
    mid = r1.astype(BF16)
    lo = (r1 - mid.astype(F32)).astype(BF16)
    return hi, mid, lo


def _tri_left(tri, a):
    return sum(jnp.dot(tri, t, preferred_element_type=F32) for t in _split3(a))


def _tri_right(a, tri):
    return sum(jnp.dot(t, tri, preferred_element_type=F32) for t in _split3(a))


def _softplus(v):
    return jnp.maximum(v, 0.0) + jnp.log1p(jnp.exp(-jnp.abs(v)))


def _silu(v):
    return v * jax.nn.sigmoid(v)


def _norm_matmul_kernel(x_ref, nw_ref, w_ref, o_ref, h_sc):
    @pl.when(pl.program_id(1) == 0)
    def _():
        h_sc[...] = _rms(x_ref[...], nw_ref[...]).astype(BF16)

    o_ref[...] = jnp.dot(h_sc[...], w_ref[...],
                         preferred_element_type=F32).astype(o_ref.dtype)


def _norm_matmul(x, nw, w, out_dtype, tm, tn, name):
    m, k = x.shape
    n = w.shape[1]
    ob = jnp.dtype(out_dtype).itemsize
    vmem = 2 * (tm * k * 4 + k * tn * 2 + tm * tn * ob) + tm * k * 2 + 3 * tm * k * 4
    return pl.pallas_call(
        _norm_matmul_kernel,
        out_shape=jax.ShapeDtypeStruct((m, n), out_dtype),
        grid=(m // tm, n // tn),
        in_specs=[pl.BlockSpec((tm, k), lambda i, j: (i, 0)),
                  pl.BlockSpec((1, k), lambda i, j: (0, 0)),
                  pl.BlockSpec((k, tn), lambda i, j: (0, j))],
        out_specs=pl.BlockSpec((tm, tn), lambda i, j: (i, j)),
        scratch_shapes=[pltpu.VMEM((tm, k), BF16)],
        compiler_params=_params(("arbitrary", "arbitrary"), vmem),
        name=name,
    )(x, nw, w)


_AUG_ONES = 6
_N_PAIRS = FOX_HEADS // 2


def _aug_selector():
    sel = np.zeros((4 * LANES, _N_PAIRS * LANES), np.float32)
    for pr in range(_N_PAIRS):
        for e in range(2):
            for j in range(3):
                sel[j * LANES + 2 * pr + e, pr * LANES + 3 * e + j] = -1.0
        sel[3 * LANES, pr * LANES + _AUG_ONES:pr * LANES + _AUG_ONES + 3] = 1.0
    return jnp.asarray(sel, BF16)


def _fox_c_kernel(f_ref, b_ref, sel_ref, c_ref, aug_ref, carry_sc):
    @pl.when(pl.program_id(0) == 0)
    def _():
        carry_sc[...] = jnp.zeros_like(carry_sc)

    z = f_ref[...] + b_ref[...]
    lf = jnp.minimum(z, 0.0) - jnp.log1p(jnp.exp(-jnp.abs(z)))
    t = lf.shape[0]
    row = lax.broadcasted_iota(jnp.int32, (t, t), 0)
    col = lax.broadcasted_iota(jnp.int32, (t, t), 1)
    tril = (col <= row).astype(BF16)
    cum = _tri_left(tril, lf) + carry_sc[...]
    c_ref[...] = cum
    carry_sc[...] = cum[t - 1:t, :]
    hi, mid, lo = _split3(cum)
    parts = jnp.concatenate([hi, mid, lo, jnp.ones_like(hi)], axis=1)
    aug_ref[...] = jnp.dot(parts, sel_ref[...], preferred_element_type=F32).astype(BF16)


def _fox_c(u2, b_pad, t=256):
    s = u2.shape[0]
    aug_cols = _N_PAIRS * LANES
    return pl.pallas_call(
        _fox_c_kernel,
        out_shape=(jax.ShapeDtypeStruct((s, LANES), F32),
                   jax.ShapeDtypeStruct((s, aug_cols), BF16)),
        grid=(s // t,),
        in_specs=[pl.BlockSpec((t, LANES), lambda i: (i, _U2_SMALL // LANES)),
                  pl.BlockSpec((1, LANES), lambda i: (0, 0)),
                  pl.BlockSpec((4 * LANES, aug_cols), lambda i: (0, 0))],
        out_specs=(pl.BlockSpec((t, LANES), lambda i: (i, 0)),
                   pl.BlockSpec((t, aug_cols), lambda i: (i, 0))),
        scratch_shapes=[pltpu.VMEM((1, LANES), F32)],
        compiler_params=_params(("arbitrary",), 16 * 1024 * 1024),
        name="fox_cumlogf",
    )(u2, b_pad, _aug_selector())


_V_ROWS = HEAD_DIM + 16


def _fox_attn_kernel(q_ref, k_ref, v_ref, aug_ref, cq_ref, o_ref, vT_sc, acc_sc, *, t):
    qi = pl.program_id(1)
    n_kv = k_ref.shape[0] // t

    @pl.when(qi == 0)
    def _():
        pad_row = lax.broadcasted_iota(jnp.int32, (_V_ROWS - HEAD_DIM, t), 0)
        ones_rows = jnp.where(pad_row == 0, 1.0, 0.0).astype(BF16)

        def transpose_v(c, carry):
            st = pl.multiple_of(c * t, t)
            vT = v_ref[pl.ds(st, t), :].astype(F32).T.astype(BF16)
            for e in range(2):
                vT_sc[e, 0:HEAD_DIM, pl.ds(st, t)] = vT[e * HEAD_DIM:(e + 1) * HEAD_DIM, :]
                vT_sc[e, HEAD_DIM:_V_ROWS, pl.ds(st, t)] = ones_rows
            return carry
        lax.fori_loop(0, n_kv, transpose_v, 0)

    scale = HEAD_DIM ** -0.5
    qT = q_ref[...].astype(F32).T * scale
    row = lax.broadcasted_iota(jnp.int32, (LANES, t), 0)
    key_idx = lax.broadcasted_iota(jnp.int32, (t, t), 0)
    qry_idx = lax.broadcasted_iota(jnp.int32, (t, t), 1)
    causal = key_idx <= qry_idx

    ws = []
    for e in range(2):
        qTe = jnp.where(row // HEAD_DIM == e, qT, 0.0)
        cq_hi, cq_mid, cq_lo = (part.astype(F32) for part in _split3(cq_ref[e]))
        aug_rows = jnp.where(
            row // 3 == e, 1.0,
            jnp.where(row == _AUG_ONES, cq_hi,
                      jnp.where(row == _AUG_ONES + 1, cq_mid,
                                jnp.where(row == _AUG_ONES + 2, cq_lo, 0.0))))
        ws.append(jnp.concatenate([qTe, aug_rows], axis=0).astype(BF16))
        acc_sc[e] = jnp.zeros((_V_ROWS, t), F32)

    def block(kb, m_prev, masked):
        start = pl.multiple_of(kb * t, t)
        keys = jnp.concatenate([k_ref[pl.ds(start, t), :], aug_ref[pl.ds(start, t), :]],
                               axis=1)
        scores = [jnp.dot(keys, ws[e], preferred_element_type=F32) for e in range(2)]
        m_out, probs, alphas = [], [], []
        for e in range(2):
            s = scores[e]
            if masked:
                s = jnp.where(causal, s, -jnp.inf)
            m_new = jnp.maximum(m_prev[e], jnp.max(s, axis=0, keepdims=True))
            probs.append(jnp.exp(s - m_new).astype(BF16))
            alphas.append(jnp.exp(m_prev[e] - m_new))
            m_out.append(m_new)
        for e in range(2):
            acc_sc[e] = alphas[e] * acc_sc[e] + jnp.dot(
                vT_sc[e, :, pl.ds(start, t)], probs[e], preferred_element_type=F32)
        return tuple(m_out)

    m_init = jnp.full((1, t), -1e30, F32)
    m = lax.fori_loop(0, qi, lambda kb, m_prev: block(kb, m_prev, False), (m_init, m_init))
    block(qi, m, True)
    outs = []
    for e in range(2):
        acc = acc_sc[e]
        outs.append(acc[0:HEAD_DIM, :] / acc[HEAD_DIM:HEAD_DIM + 1, :])
    o_ref[...] = jnp.concatenate(outs, axis=0).T.astype(o_ref.dtype)


def _fox_attn(qkv, aug, cq, t=512):
    s = qkv.shape[0]
    kcol = FOX_WIDTH // LANES
    vmem = 2 * (2 * t * LANES * 2 + 3 * s * LANES * 2 + 2 * t * 4) \
        + 2 * _V_ROWS * (s * 2 + t * 4) + 10 * t * t * 4
    return pl.pallas_call(
        functools.partial(_fox_attn_kernel, t=t),
        out_shape=jax.ShapeDtypeStruct((s, FOX_WIDTH), BF16),
        grid=(_N_PAIRS, s // t),
        in_specs=[pl.BlockSpec((t, LANES), lambda pr, qi: (qi, pr)),
                  pl.BlockSpec((s, LANES), lambda pr, qi: (0, kcol + pr)),
                  pl.BlockSpec((s, LANES), lambda pr, qi: (0, 2 * kcol + pr)),
                  pl.BlockSpec((s, LANES), lambda pr, qi: (0, pr)),
                  pl.BlockSpec((2, 1, t), lambda pr, qi: (pr, 0, qi))],
        out_specs=pl.BlockSpec((t, LANES), lambda pr, qi: (qi, pr)),
        scratch_shapes=[pltpu.VMEM((2, _V_ROWS, s), BF16), pltpu.VMEM((2, _V_ROWS, t), F32)],
        compiler_params=_params(("arbitrary", "arbitrary"), vmem),
        name="fox_attention",
    )(qkv, qkv, qkv, aug, cq)


def _ssd_kernel(xs_ref, z_ref, bc_ref, dt_ref, dtT_ref, cwx_ref, cbx_ref, cwbc_ref, cbbc_ref,
                dtb_ref, dtbT_ref, alog_ref, alogT_ref, dskip_ref, nw_ref, y_ref,
                xbuf, bcbuf, state_sc, ybuf, *, L):
    i = pl.program_id(0)
    halo = SUBLANES

    @pl.when(i == 0)
    def _():
        xbuf[0:halo, :] = jnp.zeros((halo, xbuf.shape[1]), F32)
        bcbuf[0:halo, :] = jnp.zeros((halo, bcbuf.shape[1]), F32)
        state_sc[...] = jnp.zeros_like(state_sc)

    @pl.when(i > 0)
    def _():
        xbuf[0:halo, :] = xbuf[L:L + halo, :]
        bcbuf[0:halo, :] = bcbuf[L:L + halo, :]

    xbuf[halo:halo + L, :] = xs_ref[...]
    bcbuf[halo:halo + L, :] = bc_ref[...]

    def conv_silu(buf, w_ref, b_ref):
        acc = b_ref[...]
        for k in range(SSD_CONV):
            off = halo - (SSD_CONV - 1) + k
            acc = acc + buf[off:off + L, :] * w_ref[k:k + 1, :]
        return _silu(acc)

    xc = conv_silu(xbuf, cwx_ref, cbx_ref)
    bcc = conv_silu(bcbuf, cwbc_ref, cbbc_ref)

    dt = _softplus(dt_ref[...] + dtb_ref[...])
    dtT = _softplus(dtT_ref[...] + dtbT_ref[...])
    a = dt * (-jnp.exp(alog_ref[...]))
    aT = dtT * (-jnp.exp(alogT_ref[...]))

    row = lax.broadcasted_iota(jnp.int32, (L, L), 0)
    col = lax.broadcasted_iota(jnp.int32, (L, L), 1)
    causal = col <= row
    a_cum = _tri_left(causal.astype(BF16), a)
    a_cumT = _tri_right(aT, (row <= col).astype(BF16))
    a_last = a_cum[L - 1:L, :]
    a_lastT = a_cumT[:, L - 1:L]
    exp_acum = jnp.exp(a_cum)
    w_endT = jnp.exp(a_lastT - a_cumT) * dtT
    chunk_decay = jnp.exp(a_last)

    n = SSD_STATE
    heads_per_group = SSD_HEADS // SSD_GROUPS
    for g in range(SSD_GROUPS):
        bg = bcc[:, g * n:(g + 1) * n]
        cg = bcc[:, SSD_GROUPS * n + g * n:SSD_GROUPS * n + (g + 1) * n]
        bgT = bg.T
        cb = lax.dot_general(cg.astype(BF16), bg.astype(BF16), (((1,), (1,)), ((), ())),
                             preferred_element_type=F32)
        for hh in range(heads_per_group):
            h = g * heads_per_group + hh
            cols = slice(h * HEAD_DIM, (h + 1) * HEAD_DIM)
            seg = a_cum[:, h:h + 1] - a_cumT[h:h + 1, :]
            decay = jnp.exp(jnp.where(causal, seg, -jnp.inf))
            mix = cb * decay * dtT[h:h + 1, :]
            xh = xc[:, cols]
            xh_b = xh.astype(BF16)
            st = state_sc[:, cols]
            y = jnp.dot(mix.astype(BF16), xh_b, preferred_element_type=F32)
            y = y + jnp.dot((cg * exp_acum[:, h:h + 1]).astype(BF16), st.astype(BF16),
                            preferred_element_type=F32)
            ybuf[:, cols] = y + dskip_ref[:, cols] * xh
            upd = jnp.dot((bgT * w_endT[h:h + 1, :]).astype(BF16), xh_b,
                          preferred_element_type=F32)
            state_sc[:, cols] = chunk_decay[:, h:h + 1] * st + upd

    y = ybuf[...] * _silu(z_ref[...])
    y_ref[...] = _rms(y, nw_ref[...]).astype(y_ref.dtype)


def _ssd(u2, dt_raw, dt_rawT, cwx, cbx, cwbc, cbbc, dtb, dtbT, alog, alogT, dskip, nw, L=256):
    s = u2.shape[0]
    w = SSD_WIDTH
    full = lambda shape: pl.BlockSpec(shape, lambda i: (0,) * len(shape))
    vmem = 2 * (2 * L * w * 4 + L * _BC_WIDTH * 4 + L * w * 2) + (2 * L + 16) * w * 4 \
        + (L + 8) * _BC_WIDTH * 4 + SSD_STATE * w * 4 + 24 * L * L * 4 + 6 * L * w * 4
    return pl.pallas_call(
        functools.partial(_ssd_kernel, L=L),
        out_shape=jax.ShapeDtypeStruct((s, w), BF16),
        grid=(s // L,),
        in_specs=[pl.BlockSpec((L, w), lambda i: (i, 0)),
                  pl.BlockSpec((L, w), lambda i: (i, 1)),
                  pl.BlockSpec((L, _BC_WIDTH), lambda i: (i, 2 * w // _BC_WIDTH)),
                  pl.BlockSpec((L, SSD_HEADS), lambda i: (i, 0)),
                  pl.BlockSpec((SSD_HEADS, L), lambda i: (0, i)),
                  full((SSD_CONV, w)), full((1, w)),
                  full((SSD_CONV, _BC_WIDTH)), full((1, _BC_WIDTH)),
                  full((1, SSD_HEADS)), full((SSD_HEADS, 1)),
                  full((1, SSD_HEADS)), full((SSD_HEADS, 1)),
                  full((1, w)), full((1, w))],
        out_specs=pl.BlockSpec((L, w), lambda i: (i, 0)),
        scratch_shapes=[pltpu.VMEM((L + SUBLANES, w), F32),
                        pltpu.VMEM((L + SUBLANES, _BC_WIDTH), F32),
                        pltpu.VMEM((SSD_STATE, w), F32),
                        pltpu.VMEM((L, w), F32)],
        compiler_params=_params(("arbitrary",), vmem),
        name="ssd_mixer",
    )(u2, u2, u2, dt_raw, dt_rawT, cwx, cbx, cwbc, cbbc, dtb, dtbT, alog, alogT, dskip, nw)


def _out_proj_kernel(x_ref, ya_ref, yb_ref, wa_ref, wb_ref, o_ref):
    o_ref[...] = (x_ref[...]
                  + jnp.dot(ya_ref[...], wa_ref[...], preferred_element_type=F32)
                  + jnp.dot(yb_ref[...], wb_ref[...], preferred_element_type=F32))


def _out_proj(x, ya, yb, wa, wb, tm=512):
    m, d = x.shape
    ka, kb = ya.shape[1], yb.shape[1]
    vmem = 2 * (2 * tm * d * 4 + tm * (ka + kb) * 2 + (ka + kb) * d * 2) + 2 * tm * d * 4
    return pl.pallas_call(
        _out_proj_kernel,
        out_shape=jax.ShapeDtypeStruct((m, d), F32),
        grid=(m // tm,),
        in_specs=[pl.BlockSpec((tm, d), lambda i: (i, 0)),
                  pl.BlockSpec((tm, ka), lambda i: (i, 0)),
                  pl.BlockSpec((tm, kb), lambda i: (i, 0)),
                  pl.BlockSpec((ka, d), lambda i: (0, 0)),
                  pl.BlockSpec((kb, d), lambda i: (0, 0))],
        out_specs=pl.BlockSpec((tm, d), lambda i: (i, 0)),
        compiler_params=_params(("arbitrary",), vmem),
        name="mix_out_proj",
    )(x, ya, yb, wa, wb)


def _gate_up_kernel(x_ref, nw_ref, wg_ref, wu_ref, cw_ref, cb_ref, o_ref,
                    h_sc, tail_sc, gbuf, *, tm):
    i = pl.program_id(0)
    j = pl.program_id(1)
    halo = SUBLANES

    @pl.when(j == 0)
    def _():
        h_sc[...] = _rms(x_ref[...], nw_ref[...]).astype(BF16)

    h = h_sc[...]
    g = jnp.dot(h, wg_ref[...], preferred_element_type=F32)
    up = jnp.dot(h, wu_ref[...], preferred_element_type=F32)

    @pl.when(i == 0)
    def _():
        gbuf[0:halo, :] = jnp.zeros((halo, gbuf.shape[1]), F32)

    @pl.when(i > 0)
    def _():
        gbuf[0:halo, :] = tail_sc[j]

    gbuf[halo:halo + tm, :] = g
    tail_sc[j] = gbuf[tm:tm + halo, :]

    acc = cb_ref[...]
    for k in range(FFN_CONV):
        off = halo - (FFN_CONV - 1) + k
        acc = acc + gbuf[off:off + tm, :] * cw_ref[k:k + 1, :]
    o_ref[...] = (_silu(acc) * up).astype(o_ref.dtype)


def _gate_up(x, nw, w_gu, cw, cb, tm=512, tn=512):
    m, d = x.shape
    nj = D_FF // tn
    vmem = 2 * (tm * d * 4 + 2 * d * tn * 2 + tm * tn * 2) + tm * d * 2 + 3 * tm * d * 4 \
        + (nj * 8 + tm + 8) * tn * 4 + 6 * tm * tn * 4
    return pl.pallas_call(
        functools.partial(_gate_up_kernel, tm=tm),
        out_shape=jax.ShapeDtypeStruct((m, D_FF), BF16),
        grid=(m // tm, nj),
        in_specs=[pl.BlockSpec((tm, d), lambda i, j: (i, 0)),
                  pl.BlockSpec((1, d), lambda i, j: (0, 0)),
                  pl.BlockSpec((d, tn), lambda i, j: (0, j)),
                  pl.BlockSpec((d, tn), lambda i, j: (0, j + nj)),
                  pl.BlockSpec((FFN_CONV, tn), lambda i, j: (0, j)),
                  pl.BlockSpec((1, tn), lambda i, j: (0, j))],
        out_specs=pl.BlockSpec((tm, tn), lambda i, j: (i, j)),
        scratch_shapes=[pltpu.VMEM((tm, d), BF16),
                        pltpu.VMEM((nj, SUBLANES, tn), F32),
                        pltpu.VMEM((tm + SUBLANES, tn), F32)],
        compiler_params=_params(("arbitrary", "arbitrary"), vmem),
        name="ffn_gate_up",
    )(x, nw, w_gu, w_gu, cw, cb)


def _down_kernel(a_ref, w_ref, x_ref, o_ref):
    @pl.when(pl.program_id(1) == 0)
    def _():
        o_ref[...] = x_ref[...]

    o_ref[...] += jnp.dot(a_ref[...], w_ref[...], preferred_element_type=F32)


def _down(act, w, x, tm=512, tk=512):
    m, d = x.shape
    kk = act.shape[1]
    vmem = 2 * (tm * tk * 2 + tk * d * 2 + 2 * tm * d * 4) + 2 * tm * d * 4
    return pl.pallas_call(
        _down_kernel,
        out_shape=jax.ShapeDtypeStruct((m, d), F32),
        grid=(m // tm, kk // tk),
        in_specs=[pl.BlockSpec((tm, tk), lambda i, k: (i, k)),
                  pl.BlockSpec((tk, d), lambda i, k: (k, 0)),
                  pl.BlockSpec((tm, d), lambda i, k: (i, 0))],
        out_specs=pl.BlockSpec((tm, d), lambda i, k: (i, 0)),
        compiler_params=_params(("arbitrary", "arbitrary"), vmem),
        name="ffn_down",
    )(act, w, x)


def _ple_kernel(x_ref, p_ref, nw_ref, wg_ref, wp_ref, fw_ref, o_ref, *, final):
    x = x_ref[...]
    h = _rms(x, nw_ref[...]).astype(BF16)
    gate = jax.nn.sigmoid(jnp.dot(h, wg_ref[...], preferred_element_type=F32))
    proj = jnp.dot(p_ref[...].astype(BF16), wp_ref[...], preferred_element_type=F32)
    x = x + gate * proj
    if final:
        x = _rms(x, fw_ref[...])
    o_ref[...] = x


def _ple(x, p, nw, wg, wp, fw, final, tm=256):
    m, d = x.shape
    dp = p.shape[1]
    vmem = 2 * (2 * tm * d * 4 + tm * dp * 4 + d * d * 2 + dp * d * 2) + 6 * tm * d * 4
    return pl.pallas_call(
        functools.partial(_ple_kernel, final=final),
        out_shape=jax.ShapeDtypeStruct((m, d), F32),
        grid=(m // tm,),
        in_specs=[pl.BlockSpec((tm, d), lambda i: (i, 0)),
                  pl.BlockSpec((tm, dp), lambda i: (i, 0)),
                  pl.BlockSpec((1, d), lambda i: (0, 0)),
                  pl.BlockSpec((d, d), lambda i: (0, 0)),
                  pl.BlockSpec((dp, d), lambda i: (0, 0)),
                  pl.BlockSpec((1, d), lambda i: (0, 0))],
        out_specs=pl.BlockSpec((tm, d), lambda i: (i, 0)),
        compiler_params=_params(("arbitrary",), vmem),
        name="ple_gate",
    )(x, p, nw, wg, wp, fw)


def _layer(x, p, mix_norm_w, w_in, fox_forget_bias, ssd_conv_w, ssd_conv_b, ssd_dt_bias,
           ssd_A_log, ssd_D, ssd_norm_w, w_out, ffn_norm_w, w_gate_up, ffn_conv_w,
           ffn_conv_b, w_down, ple_norm_w, w_ple_gate, w_ple_proj, final_norm_w, final):
    s = x.shape[0]
    row = lambda v: v.reshape(1, -1).astype(F32)

    w_qkv = w_in[:, _OFF_Q:_OFF_F].astype(BF16)
    w_u2 = jnp.concatenate(
        [w_in[:, _OFF_XS:_OFF_B], w_in[:, _OFF_Z:_OFF_DT], w_in[:, _OFF_B:_OFF_Z],
         w_in[:, _OFF_F:_OFF_XS], w_in[:, _OFF_DT:_IN_COLS],
         jnp.zeros((D_MODEL, LANES - FOX_HEADS - SSD_HEADS), w_in.dtype)], axis=1).astype(BF16)

    nw = row(mix_norm_w)
    qkv = _norm_matmul(x, nw, w_qkv, BF16, tm=1024, tn=512, name="in_proj_qkv")
    u2 = _norm_matmul(x, nw, w_u2, F32, tm=1024, tn=_U2_COLS // 3, name="in_proj_ssd")

    b_pad = jnp.zeros((1, LANES), F32).at[0, :FOX_HEADS].set(fox_forget_bias.astype(F32))
    c, aug = _fox_c(u2, b_pad)
    cT = c[:, :FOX_HEADS].T
    y_fox = _fox_attn(qkv, aug, cT[:, None, :])

    dt_raw = u2[:, _U2_SMALL + FOX_HEADS:_U2_SMALL + FOX_HEADS + SSD_HEADS]
    col = lambda v: v.reshape(-1, 1).astype(F32)
    y_ssd = _ssd(u2, dt_raw, dt_raw.T,
                 ssd_conv_w[:, :SSD_WIDTH].astype(F32), row(ssd_conv_b[:SSD_WIDTH]),
                 ssd_conv_w[:, SSD_WIDTH:].astype(F32), row(ssd_conv_b[SSD_WIDTH:]),
                 row(ssd_dt_bias), col(ssd_dt_bias), row(ssd_A_log), col(ssd_A_log),
                 row(jnp.repeat(ssd_D, HEAD_DIM)), row(ssd_norm_w))

    w_out_b = w_out.astype(BF16)
    x = _out_proj(x, y_fox, y_ssd, w_out_b[:FOX_WIDTH], w_out_b[FOX_WIDTH:])

    act = _gate_up(x, row(ffn_norm_w), w_gate_up.astype(BF16),
                   ffn_conv_w.astype(F32), row(ffn_conv_b))
    x = _down(act, w_down.astype(BF16), x)

    return _ple(x, p, row(ple_norm_w), w_ple_gate.astype(BF16), w_ple_proj.astype(BF16),
                row(final_norm_w), final)


def kernel(x, p, mix_norm_w, w_in, fox_forget_bias, ssd_conv_w, ssd_conv_b, ssd_dt_bias,
           ssd_A_log, ssd_D, ssd_norm_w, w_out, ffn_norm_w, w_gate_up, ffn_conv_w,
           ffn_conv_b, w_down, ple_norm_w, w_ple_gate, w_ple_proj, final_norm_w):
    bsz, s, d = x.shape
    depth = p.shape[0]
    outs = []
    for b in range(bsz):
        xb = x[b]
        for i in range(depth):
            xb = _layer(xb, p[i, b], mix_norm_w[i], w_in[i], fox_forget_bias[i], ssd_conv_w[i],
                        ssd_conv_b[i], ssd_dt_bias[i], ssd_A_log[i], ssd_D[i], ssd_norm_w[i],
                        w_out[i], ffn_norm_w[i], w_gate_up[i], ffn_conv_w[i], ffn_conv_b[i],
                        w_down[i], ple_norm_w[i], w_ple_gate[i], w_ple_proj[i], final_norm_w,
                        final=(i == depth - 1))
        outs.append(xb)
    return jnp.stack(outs, axis=0)
```

```python
import functools

import numpy as np
import jax
import jax.numpy as jnp
from jax import lax
from jax.experimental import pallas as pl
from jax.experimental.pallas import tpu as pltpu

F32 = jnp.float32
BF16 = jnp.bfloat16

D_MODEL = 2048
SEQ = 8192
D_PLE = 256
HEAD_DIM = 64
FOX_WIDTH = 1024
FOX_HEADS = 16
SSD_WIDTH = 1024
SSD_HEADS = 16
SSD_GROUPS = 2
SSD_STATE = 128
SSD_CONV = 4
D_FF = 5632
FFN_CONV = 3
EPS = 1e-6

_OFF_Q = 0
_OFF_F = 3 * FOX_WIDTH
_OFF_XS = _OFF_F + FOX_HEADS
_OFF_B = _OFF_XS + SSD_WIDTH
_OFF_C = _OFF_B + SSD_GROUPS * SSD_STATE
_OFF_Z = _OFF_C + SSD_GROUPS * SSD_STATE
_OFF_DT = _OFF_Z + SSD_WIDTH
_IN_COLS = _OFF_DT + SSD_HEADS

LANES = 128
SUBLANES = 8
VMEM_CAP = 60 * 1024 * 1024

_BC_WIDTH = 2 * SSD_GROUPS * SSD_STATE
_U2_COLS = 2 * SSD_WIDTH + _BC_WIDTH + LANES
_U2_SMALL = 2 * SSD_WIDTH + _BC_WIDTH


def _params(sem, vmem_bytes):
    return pltpu.CompilerParams(dimension_semantics=sem,
                                vmem_limit_bytes=int(min(VMEM_CAP, vmem_bytes)))


def _rms(xf, w):
    ms = jnp.mean(xf * xf, axis=-1, keepdims=True)
    return xf * lax.rsqrt(ms + EPS) * w


def _split3(a):
    hi = a.astype(BF16)
    r1 = a - hi.astype(F32)
    mid = r1.astype(BF16)
    lo = (r1 - mid.astype(F32)).astype(BF16)
    return hi, mid, lo


def _tri_left(tri, a):
    return sum(jnp.dot(tri, t, preferred_element_type=F32) for t in _split3(a))


def _tri_right(a, tri):
    return sum(jnp.dot(t, tri, preferred_element_type=F32) for t in _split3(a))


def _softplus(v):
    return jnp.maximum(v, 0.0) + jnp.log1p(jnp.exp(-jnp.abs(v)))


def _silu(v):
    return v * jax.nn.sigmoid(v)


def _norm_matmul_kernel(x_ref, nw_ref, w_ref, o_ref, h_sc):
    @pl.when(pl.program_id(1) == 0)
    def _():
        h_sc[...] = _rms(x_ref[...], nw_ref[...]).astype(BF16)

    o_ref[...] = jnp.dot(h_sc[...], w_ref[...],
                         preferred_element_type=F32).astype(o_ref.dtype)


def _norm_matmul(x, nw, w, out_dtype, tm, tn, name):
    m, k = x.shape
    n = w.shape[1]
    ob = jnp.dtype(out_dtype).itemsize
    vmem = 2 * (tm * k * 4 + k * tn * 2 + tm * tn * ob) + tm * k * 2 + 3 * tm * k * 4
    return pl.pallas_call(
        _norm_matmul_kernel,
        out_shape=jax.ShapeDtypeStruct((m, n), out_dtype),
        grid=(m // tm, n // tn),
        in_specs=[pl.BlockSpec((tm, k), lambda i, j: (i, 0)),
                  pl.BlockSpec((1, k), lambda i, j: (0, 0)),
                  pl.BlockSpec((k, tn), lambda i, j: (0, j))],
        out_specs=pl.BlockSpec((tm, tn), lambda i, j: (i, j)),
        scratch_shapes=[pltpu.VMEM((tm, k), BF16)],
        compiler_params=_params(("arbitrary", "arbitrary"), vmem),
        name=name,
    )(x, nw, w)


_AUG_ONES = 6
_N_PAIRS = FOX_HEADS // 2


def _aug_selector():
    sel = np.zeros((4 * LANES, _N_PAIRS * LANES), np.float32)
    for pr in range(_N_PAIRS):
        for e in range(2):
            for j in range(3):
                sel[j * LANES + 2 * pr + e, pr * LANES + 3 * e + j] = -1.0
        sel[3 * LANES, pr * LANES + _AUG_ONES:pr * LANES + _AUG_ONES + 3] = 1.0
    return jnp.asarray(sel, BF16)


def _fox_c_kernel(f_ref, b_ref, sel_ref, c_ref, aug_ref, carry_sc):
    @pl.when(pl.program_id(0) == 0)
    def _():
        carry_sc[...] = jnp.zeros_like(carry_sc)

    z = f_ref[...] + b_ref[...]
    lf = jnp.minimum(z, 0.0) - jnp.log1p(jnp.exp(-jnp.abs(z)))
    t = lf.shape[0]
    row = lax.broadcasted_iota(jnp.int32, (t, t), 0)
    col = lax.broadcasted_iota(jnp.int32, (t, t), 1)
    tril = (col <= row).astype(BF16)
    cum = _tri_left(tril, lf) + carry_sc[...]
    c_ref[...] = cum
    carry_sc[...] = cum[t - 1:t, :]
    hi, mid, lo = _split3(cum)
    parts = jnp.concatenate([hi, mid, lo, jnp.ones_like(hi)], axis=1)
    aug_ref[...] = jnp.dot(parts, sel_ref[...], preferred_element_type=F32).astype(BF16)


def _fox_c(u2, b_pad, t=256):
    s = u2.shape[0]
    aug_cols = _N_PAIRS * LANES
    return pl.pallas_call(
        _fox_c_kernel,
        out_shape=(jax.ShapeDtypeStruct((s, LANES), F32),
                   jax.ShapeDtypeStruct((s, aug_cols), BF16)),
        grid=(s // t,),
        in_specs=[pl.BlockSpec((t, LANES), lambda i: (i, _U2_SMALL // LANES)),
                  pl.BlockSpec((1, LANES), lambda i: (0, 0)),
                  pl.BlockSpec((4 * LANES, aug_cols), lambda i: (0, 0))],
        out_specs=(pl.BlockSpec((t, LANES), lambda i: (i, 0)),
                   pl.BlockSpec((t, aug_cols), lambda i: (i, 0))),
        scratch_shapes=[pltpu.VMEM((1, LANES), F32)],
        compiler_params=_params(("arbitrary",), 16 * 1024 * 1024),
        name="fox_cumlogf",
    )(u2, b_pad, _aug_selector())


_V_ROWS = HEAD_DIM + 16


def _fox_attn_kernel(q_ref, k_ref, v_ref, aug_ref, cq_ref, o_ref, vT_sc, acc_sc, s0_sc, s1_sc,
                     *, t):
    qi = pl.program_id(1)
    n_kv = k_ref.shape[0] // t

    @pl.when(qi == 0)
    def _():
        pad_row = lax.broadcasted_iota(jnp.int32, (_V_ROWS - HEAD_DIM, t), 0)
        ones_rows = jnp.where(pad_row == 0, 1.0, 0.0).astype(BF16)

        def transpose_v(c, carry):
            st = pl.multiple_of(c * t, t)
            vT = v_ref[pl.ds(st, t), :].astype(F32).T.astype(BF16)
            for e in range(2):
                vT_sc[e, 0:HEAD_DIM, pl.ds(st, t)] = vT[e * HEAD_DIM:(e + 1) * HEAD_DIM, :]
                vT_sc[e, HEAD_DIM:_V_ROWS, pl.ds(st, t)] = ones_rows
            return carry
        lax.fori_loop(0, n_kv, transpose_v, 0)

    scale = HEAD_DIM ** -0.5
    qT = q_ref[...].astype(F32).T * scale
    row = lax.broadcasted_iota(jnp.int32, (LANES, t), 0)
    tk = s0_sc.shape[1]
    key_idx = lax.broadcasted_iota(jnp.int32, (tk, t), 0)
    qry_idx = lax.broadcasted_iota(jnp.int32, (tk, t), 1)

    ws = []
    for e in range(2):
        qTe = jnp.where(row // HEAD_DIM == e, qT, 0.0)
        cq_hi, cq_mid, cq_lo = (part.astype(F32) for part in _split3(cq_ref[e]))
        aug_rows = jnp.where(
            row // 3 == e, 1.0,
            jnp.where(row == _AUG_ONES, cq_hi,
                      jnp.where(row == _AUG_ONES + 1, cq_mid,
                                jnp.where(row == _AUG_ONES + 2, cq_lo, 0.0))))
        ws.append(jnp.concatenate([qTe, aug_rows], axis=0).astype(BF16))
        acc_sc[e] = jnp.zeros((_V_ROWS, t), F32)

    def scores_into(j, s_sc):
        start = pl.multiple_of(j * tk, tk)
        keys = jnp.concatenate([k_ref[pl.ds(start, tk), :], aug_ref[pl.ds(start, tk), :]],
                               axis=1)
        for e in range(2):
            s_sc[e] = jnp.dot(keys, ws[e], preferred_element_type=F32)

    def softmax_pv(j, s_sc, m_prev, masked):
        start = pl.multiple_of(j * tk, tk)
        if masked:
            visible = key_idx + start <= qry_idx + qi * t
        m_out, probs, alphas = [], [], []
        for e in range(2):
            s = s_sc[e]
            if masked:
                s = jnp.where(visible, s, -jnp.inf)
            m_new = jnp.maximum(m_prev[e], jnp.max(s, axis=0, keepdims=True))
            probs.append(jnp.exp(s - m_new).astype(BF16))
            alphas.append(jnp.exp(m_prev[e] - m_new))
            m_out.append(m_new)
        for e in range(2):
            acc_sc[e] = alphas[e] * acc_sc[e] + jnp.dot(
                vT_sc[e, :, pl.ds(start, tk)], probs[e], preferred_element_type=F32)
        return tuple(m_out)

    def two_blocks(i, m):
        scores_into(2 * i + 1, s1_sc)
        m = softmax_pv(2 * i, s0_sc, m, False)
        scores_into(2 * i + 2, s0_sc)
        return softmax_pv(2 * i + 1, s1_sc, m, False)

    m_init = jnp.full((1, t), -1e30, F32)
    scores_into(0, s0_sc)
    m = lax.fori_loop(0, qi, two_blocks, (m_init, m_init))
    scores_into(2 * qi + 1, s1_sc)
    m = softmax_pv(2 * qi, s0_sc, m, True)
    softmax_pv(2 * qi + 1, s1_sc, m, True)
    outs = []
    for e in range(2):
        acc = acc_sc[e]
        outs.append(acc[0:HEAD_DIM, :] / acc[HEAD_DIM:HEAD_DIM + 1, :])
    o_ref[...] = jnp.concatenate(outs, axis=0).T.astype(o_ref.dtype)


def _fox_attn(qkv, aug, cq, t=512):
    s = qkv.shape[0]
    kcol = FOX_WIDTH // LANES
    vmem = 2 * (2 * t * LANES * 2 + 3 * s * LANES * 2 + 2 * t * 4) \
        + 2 * _V_ROWS * (s * 2 + t * 4) + 10 * t * t * 4
    return pl.pallas_call(
        functools.partial(_fox_attn_kernel, t=t),
        out_shape=jax.ShapeDtypeStruct((s, FOX_WIDTH), BF16),
        grid=(_N_PAIRS, s // t),
        in_specs=[pl.BlockSpec((t, LANES), lambda pr, qi: (qi, pr)),
                  pl.BlockSpec((s, LANES), lambda pr, qi: (0, kcol + pr)),
                  pl.BlockSpec((s, LANES), lambda pr, qi: (0, 2 * kcol + pr)),
                  pl.BlockSpec((s, LANES), lambda pr, qi: (0, pr)),
                  pl.BlockSpec((2, 1, t), lambda pr, qi: (pr, 0, qi))],
        out_specs=pl.BlockSpec((t, LANES), lambda pr, qi: (qi, pr)),
        scratch_shapes=[pltpu.VMEM((2, _V_ROWS, s), BF16), pltpu.VMEM((2, _V_ROWS, t), F32),
                        pltpu.VMEM((2, t // 2, t), F32), pltpu.VMEM((2, t // 2, t), F32)],
        compiler_params=_params(("arbitrary", "arbitrary"), vmem),
        name="fox_attention",
    )(qkv, qkv, qkv, aug, cq)


def _ssd_kernel(xs_ref, z_ref, bc_ref, dt_ref, dtT_ref, cwx_ref, cbx_ref, cwbc_ref, cbbc_ref,
                dtb_ref, dtbT_ref, alog_ref, alogT_ref, dskip_ref, nw_ref, y_ref,
                xbuf, bcbuf, state_sc, ybuf, *, L):
    i = pl.program_id(0)
    halo = SUBLANES

    @pl.when(i == 0)
    def _():
        xbuf[0:halo, :] = jnp.zeros((halo, xbuf.shape[1]), F32)
        bcbuf[0:halo, :] = jnp.zeros((halo, bcbuf.shape[1]), F32)
        state_sc[...] = jnp.zeros_like(state_sc)

    @pl.when(i > 0)
    def _():
        xbuf[0:halo, :] = xbuf[L:L + halo, :]
        bcbuf[0:halo, :] = bcbuf[L:L + halo, :]

    xbuf[halo:halo + L, :] = xs_ref[...]
    bcbuf[halo:halo + L, :] = bc_ref[...]

    def conv_silu(buf, w_ref, b_ref):
        acc = b_ref[...]
        for k in range(SSD_CONV):
            off = halo - (SSD_CONV - 1) + k
            acc = acc + buf[off:off + L, :] * w_ref[k:k + 1, :]
        return _silu(acc)

    xc = conv_silu(xbuf, cwx_ref, cbx_ref)
    bcc = conv_silu(bcbuf, cwbc_ref, cbbc_ref)

    dt = _softplus(dt_ref[...] + dtb_ref[...])
    dtT = _softplus(dtT_ref[...] + dtbT_ref[...])
    a = dt * (-jnp.exp(alog_ref[...]))
    aT = dtT * (-jnp.exp(alogT_ref[...]))

    row = lax.broadcasted_iota(jnp.int32, (L, L), 0)
    col = lax.broadcasted_iota(jnp.int32, (L, L), 1)
    causal = col <= row
    a_cum = _tri_left(causal.astype(BF16), a)
    a_cumT = _tri_right(aT, (row <= col).astype(BF16))
    a_last = a_cum[L - 1:L, :]
    a_lastT = a_cumT[:, L - 1:L]
    exp_acum = jnp.exp(a_cum)
    w_endT = jnp.exp(a_lastT - a_cumT) * dtT
    chunk_decay = jnp.exp(a_last)

    n = SSD_STATE
    heads_per_group = SSD_HEADS // SSD_GROUPS
    for g in range(SSD_GROUPS):
        bg = bcc[:, g * n:(g + 1) * n]
        cg = bcc[:, SSD_GROUPS * n + g * n:SSD_GROUPS * n + (g + 1) * n]
        bgT = bg.T
        cb = lax.dot_general(cg.astype(BF16), bg.astype(BF16), (((1,), (1,)), ((), ())),
                             preferred_element_type=F32)
        for hh in range(heads_per_group):
            h = g * heads_per_group + hh
            cols = slice(h * HEAD_DIM, (h + 1) * HEAD_DIM)
            seg = a_cum[:, h:h + 1] - a_cumT[h:h + 1, :]
            decay = jnp.exp(jnp.where(causal, seg, -jnp.inf))
            mix = cb * decay * dtT[h:h + 1, :]
            xh = xc[:, cols]
            xh_b = xh.astype(BF16)
            st = state_sc[:, cols]
            y = jnp.dot(mix.astype(BF16), xh_b, preferred_element_type=F32)
            y = y + jnp.dot((cg * exp_acum[:, h:h + 1]).astype(BF16), st.astype(BF16),
                            preferred_element_type=F32)
            ybuf[:, cols] = y + dskip_ref[:, cols] * xh
            upd = jnp.dot((bgT * w_endT[h:h + 1, :]).astype(BF16), xh_b,
                          preferred_element_type=F32)
            state_sc[:, cols] = chunk_decay[:, h:h + 1] * st + upd

    y = ybuf[...] * _silu(z_ref[...])
    y_ref[...] = _rms(y, nw_ref[...]).astype(y_ref.dtype)


def _ssd(u2, dt_raw, dt_rawT, cwx, cbx, cwbc, cbbc, dtb, dtbT, alog, alogT, dskip, nw, L=256):
    s = u2.shape[0]
    w = SSD_WIDTH
    full = lambda shape: pl.BlockSpec(shape, lambda i: (0,) * len(shape))
    vmem = 2 * (2 * L * w * 4 + L * _BC_WIDTH * 4 + L * w * 2) + (2 * L + 16) * w * 4 \
        + (L + 8) * _BC_WIDTH * 4 + SSD_STATE * w * 4 + 24 * L * L * 4 + 6 * L * w * 4
    return pl.pallas_call(
        functools.partial(_ssd_kernel, L=L),
        out_shape=jax.ShapeDtypeStruct((s, w), BF16),
        grid=(s // L,),
        in_specs=[pl.BlockSpec((L, w), lambda i: (i, 0)),
                  pl.BlockSpec((L, w), lambda i: (i, 1)),
                  pl.BlockSpec((L, _BC_WIDTH), lambda i: (i, 2 * w // _BC_WIDTH)),
                  pl.BlockSpec((L, SSD_HEADS), lambda i: (i, 0)),
                  pl.BlockSpec((SSD_HEADS, L), lambda i: (0, i)),
                  full((SSD_CONV, w)), full((1, w)),
                  full((SSD_CONV, _BC_WIDTH)), full((1, _BC_WIDTH)),
                  full((1, SSD_HEADS)), full((SSD_HEADS, 1)),
                  full((1, SSD_HEADS)), full((SSD_HEADS, 1)),
                  full((1, w)), full((1, w))],
        out_specs=pl.BlockSpec((L, w), lambda i: (i, 0)),
        scratch_shapes=[pltpu.VMEM((L + SUBLANES, w), F32),
                        pltpu.VMEM((L + SUBLANES, _BC_WIDTH), F32),
                        pltpu.VMEM((SSD_STATE, w), F32),
                        pltpu.VMEM((L, w), F32)],
        compiler_params=_params(("arbitrary",), vmem),
        name="ssd_mixer",
    )(u2, u2, u2, dt_raw, dt_rawT, cwx, cbx, cwbc, cbbc, dtb, dtbT, alog, alogT, dskip, nw)


def _out_proj_kernel(x_ref, ya_ref, yb_ref, wa_ref, wb_ref, o_ref):
    o_ref[...] = (x_ref[...]
                  + jnp.dot(ya_ref[...], wa_ref[...], preferred_element_type=F32)
                  + jnp.dot(yb_ref[...], wb_ref[...], preferred_element_type=F32))


def _out_proj(x, ya, yb, wa, wb, tm=512):
    m, d = x.shape
    ka, kb = ya.shape[1], yb.shape[1]
    vmem = 2 * (2 * tm * d * 4 + tm * (ka + kb) * 2 + (ka + kb) * d * 2) + 2 * tm * d * 4
    return pl.pallas_call(
        _out_proj_kernel,
        out_shape=jax.ShapeDtypeStruct((m, d), F32),
        grid=(m // tm,),
        in_specs=[pl.BlockSpec((tm, d), lambda i: (i, 0)),
                  pl.BlockSpec((tm, ka), lambda i: (i, 0)),
                  pl.BlockSpec((tm, kb), lambda i: (i, 0)),
                  pl.BlockSpec((ka, d), lambda i: (0, 0)),
                  pl.BlockSpec((kb, d), lambda i: (0, 0))],
        out_specs=pl.BlockSpec((tm, d), lambda i: (i, 0)),
        compiler_params=_params(("arbitrary",), vmem),
        name="mix_out_proj",
    )(x, ya, yb, wa, wb)


def _gate_up_kernel(x_ref, nw_ref, wg_ref, wu_ref, cw_ref, cb_ref, o_ref,
                    h_sc, tail_sc, gbuf, *, tm):
    i = pl.program_id(0)
    j = pl.program_id(1)
    halo = SUBLANES

    @pl.when(j == 0)
    def _():
        h_sc[...] = _rms(x_ref[...], nw_ref[...]).astype(BF16)

    h = h_sc[...]
    g = jnp.dot(h, wg_ref[...], preferred_element_type=F32)
    up = jnp.dot(h, wu_ref[...], preferred_element_type=F32)

    @pl.when(i == 0)
    def _():
        gbuf[0:halo, :] = jnp.zeros((halo, gbuf.shape[1]), F32)

    @pl.when(i > 0)
    def _():
        gbuf[0:halo, :] = tail_sc[j]

    gbuf[halo:halo + tm, :] = g
    tail_sc[j] = gbuf[tm:tm + halo, :]

    acc = cb_ref[...]
    for k in range(FFN_CONV):
        off = halo - (FFN_CONV - 1) + k
        acc = acc + gbuf[off:off + tm, :] * cw_ref[k:k + 1, :]
    o_ref[...] = (_silu(acc) * up).astype(o_ref.dtype)


def _gate_up(x, nw, w_gu, cw, cb, tm=512, tn=512):
    m, d = x.shape
    nj = D_FF // tn
    vmem = 2 * (tm * d * 4 + 2 * d * tn * 2 + tm * tn * 2) + tm * d * 2 + 3 * tm * d * 4 \
        + (nj * 8 + tm + 8) * tn * 4 + 6 * tm * tn * 4
    return pl.pallas_call(
        functools.partial(_gate_up_kernel, tm=tm),
        out_shape=jax.ShapeDtypeStruct((m, D_FF), BF16),
        grid=(m // tm, nj),
        in_specs=[pl.BlockSpec((tm, d), lambda i, j: (i, 0)),
                  pl.BlockSpec((1, d), lambda i, j: (0, 0)),
                  pl.BlockSpec((d, tn), lambda i, j: (0, j)),
                  pl.BlockSpec((d, tn), lambda i, j: (0, j + nj)),
                  pl.BlockSpec((FFN_CONV, tn), lambda i, j: (0, j)),
                  pl.BlockSpec((1, tn), lambda i, j: (0, j))],
        out_specs=pl.BlockSpec((tm, tn), lambda i, j: (i, j)),
        scratch_shapes=[pltpu.VMEM((tm, d), BF16),
                        pltpu.VMEM((nj, SUBLANES, tn), F32),
                        pltpu.VMEM((tm + SUBLANES, tn), F32)],
        compiler_params=_params(("arbitrary", "arbitrary"), vmem),
        name="ffn_gate_up",
    )(x, nw, w_gu, w_gu, cw, cb)


def _down_kernel(a_ref, w_ref, x_ref, o_ref):
    @pl.when(pl.program_id(1) == 0)
    def _():
        o_ref[...] = x_ref[...]

    o_ref[...] += jnp.dot(a_ref[...], w_ref[...], preferred_element_type=F32)


def _down(act, w, x, tm=512, tk=512):
    m, d = x.shape
    kk = act.shape[1]
    vmem = 2 * (tm * tk * 2 + tk * d * 2 + 2 * tm * d * 4) + 2 * tm * d * 4
    return pl.pallas_call(
        _down_kernel,
        out_shape=jax.ShapeDtypeStruct((m, d), F32),
        grid=(m // tm, kk // tk),
        in_specs=[pl.BlockSpec((tm, tk), lambda i, k: (i, k)),
                  pl.BlockSpec((tk, d), lambda i, k: (k, 0)),
                  pl.BlockSpec((tm, d), lambda i, k: (i, 0))],
        out_specs=pl.BlockSpec((tm, d), lambda i, k: (i, 0)),
        compiler_params=_params(("arbitrary", "arbitrary"), vmem),
        name="ffn_down",
    )(act, w, x)


def _ple_kernel(x_ref, p_ref, nw_ref, wg_ref, wp_ref, fw_ref, o_ref, *, final):
    x = x_ref[...]
    h = _rms(x, nw_ref[...]).astype(BF16)
    gate = jax.nn.sigmoid(jnp.dot(h, wg_ref[...], preferred_element_type=F32))
    proj = jnp.dot(p_ref[...].astype(BF16), wp_ref[...], preferred_element_type=F32)
    x = x + gate * proj
    if final:
        x = _rms(x, fw_ref[...])
    o_ref[...] = x


def _ple(x, p, nw, wg, wp, fw, final, tm=256):
    m, d = x.shape
    dp = p.shape[1]
    vmem = 2 * (2 * tm * d * 4 + tm * dp * 4 + d * d * 2 + dp * d * 2) + 6 * tm * d * 4
    return pl.pallas_call(
        functools.partial(_ple_kernel, final=final),
        out_shape=jax.ShapeDtypeStruct((m, d), F32),
        grid=(m // tm,),
        in_specs=[pl.BlockSpec((tm, d), lambda i: (i, 0)),
                  pl.BlockSpec((tm, dp), lambda i: (i, 0)),
                  pl.BlockSpec((1, d), lambda i: (0, 0)),
                  pl.BlockSpec((d, d), lambda i: (0, 0)),
                  pl.BlockSpec((dp, d), lambda i: (0, 0)),
                  pl.BlockSpec((1, d), lambda i: (0, 0))],
        out_specs=pl.BlockSpec((tm, d), lambda i: (i, 0)),
        compiler_params=_params(("arbitrary",), vmem),
        name="ple_gate",
    )(x, p, nw, wg, wp, fw)


def _layer(x, p, mix_norm_w, w_in, fox_forget_bias, ssd_conv_w, ssd_conv_b, ssd_dt_bias,
           ssd_A_log, ssd_D, ssd_norm_w, w_out, ffn_norm_w, w_gate_up, ffn_conv_w,
           ffn_conv_b, w_down, ple_norm_w, w_ple_gate, w_ple_proj, final_norm_w, final):
    s = x.shape[0]
    row = lambda v: v.reshape(1, -1).astype(F32)

    w_qkv = w_in[:, _OFF_Q:_OFF_F].astype(BF16)
    w_u2 = jnp.concatenate(
        [w_in[:, _OFF_XS:_OFF_B], w_in[:, _OFF_Z:_OFF_DT], w_in[:, _OFF_B:_OFF_Z],
         w_in[:, _OFF_F:_OFF_XS], w_in[:, _OFF_DT:_IN_COLS],
         jnp.zeros((D_MODEL, LANES - FOX_HEADS - SSD_HEADS), w_in.dtype)], axis=1).astype(BF16)

    nw = row(mix_norm_w)
    qkv = _norm_matmul(x, nw, w_qkv, BF16, tm=1024, tn=512, name="in_proj_qkv")
    u2 = _norm_matmul(x, nw, w_u2, F32, tm=1024, tn=_U2_COLS // 3, name="in_proj_ssd")

    b_pad = jnp.zeros((1, LANES), F32).at[0, :FOX_HEADS].set(fox_forget_bias.astype(F32))
    c, aug = _fox_c(u2, b_pad)
    cT = c[:, :FOX_HEADS].T
    y_fox = _fox_attn(qkv, aug, cT[:, None, :])

    dt_raw = u2[:, _U2_SMALL + FOX_HEADS:_U2_SMALL + FOX_HEADS + SSD_HEADS]
    col = lambda v: v.reshape(-1, 1).astype(F32)
    y_ssd = _ssd(u2, dt_raw, dt_raw.T,
                 ssd_conv_w[:, :SSD_WIDTH].astype(F32), row(ssd_conv_b[:SSD_WIDTH]),
                 ssd_conv_w[:, SSD_WIDTH:].astype(F32), row(ssd_conv_b[SSD_WIDTH:]),
                 row(ssd_dt_bias), col(ssd_dt_bias), row(ssd_A_log), col(ssd_A_log),
                 row(jnp.repeat(ssd_D, HEAD_DIM)), row(ssd_norm_w))

    w_out_b = w_out.astype(BF16)
    x = _out_proj(x, y_fox, y_ssd, w_out_b[:FOX_WIDTH], w_out_b[FOX_WIDTH:])

    act = _gate_up(x, row(ffn_norm_w), w_gate_up.astype(BF16),
                   ffn_conv_w.astype(F32), row(ffn_conv_b))
    x = _down(act, w_down.astype(BF16), x)

    return _ple(x, p, row(ple_norm_w), w_ple_gate.astype(BF16), w_ple_proj.astype(BF16),
                row(final_norm_w), final)


def kernel(x, p, mix_norm_w, w_in, fox_forget_bias, ssd_conv_w, ssd_conv_b, ssd_dt_bias,
           ssd_A_log, ssd_D, ssd_norm_w, w_out, ffn_norm_w, w_gate_up, ffn_conv_w,
           ffn_conv_b, w_down, ple_norm_w, w_ple_gate, w_ple_proj, final_norm_w):
    bsz, s, d = x.shape
    depth = p.shape[0]
    outs = []
    for b in range(bsz):
        xb = x[b]
        for i in range(depth):
            xb = _layer(xb, p[i, b], mix_norm_w[i], w_in[i], fox_forget_bias[i], ssd_conv_w[i],
                        ssd_conv_b[i], ssd_dt_bias[i], ssd_A_log[i], ssd_D[i], ssd_norm_w[i],
                        w_out[i], ffn_norm_w[i], w_gate_up[i], ffn_conv_w[i], ffn_conv_b[i],
                        w_down[i], ple_norm_w[i], w_ple_gate[i], w_ple_proj[i], final_norm_w,
                        final=(i == depth - 1))
        outs.append(xb)
    return jnp.stack(outs, axis=0)
```

```python
import functools

import numpy as np
import jax
import jax.numpy as jnp
from jax import lax
from jax.experimental import pallas as pl
from jax.experimental.pallas import tpu as pltpu

F32 = jnp.float32
BF16 = jnp.bfloat16

D_MODEL = 2048
SEQ = 8192
D_PLE = 256
HEAD_DIM = 64
FOX_WIDTH = 1024
FOX_HEADS = 16
SSD_WIDTH = 1024
SSD_HEADS = 16
SSD_GROUPS = 2
SSD_STATE = 128
SSD_CONV = 4
D_FF = 5632
FFN_CONV = 3
EPS = 1e-6

_OFF_Q = 0
_OFF_F = 3 * FOX_WIDTH
_OFF_XS = _OFF_F + FOX_HEADS
_OFF_B = _OFF_XS + SSD_WIDTH
_OFF_C = _OFF_B + SSD_GROUPS * SSD_STATE
_OFF_Z = _OFF_C + SSD_GROUPS * SSD_STATE
_OFF_DT = _OFF_Z + SSD_WIDTH
_IN_COLS = _OFF_DT + SSD_HEADS

LANES = 128
SUBLANES = 8
VMEM_CAP = 60 * 1024 * 1024

_BC_WIDTH = 2 * SSD_GROUPS * SSD_STATE
_U2_COLS = 2 * SSD_WIDTH + _BC_WIDTH + LANES
_U2_SMALL = 2 * SSD_WIDTH + _BC_WIDTH


def _params(sem, vmem_bytes):
    return pltpu.CompilerParams(dimension_semantics=sem,
                                vmem_limit_bytes=int(min(VMEM_CAP, vmem_bytes)))


def _rms(xf, w):
    ms = jnp.mean(xf * xf, axis=-1, keepdims=True)
    return xf * lax.rsqrt(ms + EPS) * w


def _split3(a):
    hi = a.astype(BF16)
    r1 = a - hi.astype(F32)
    mid = r1.astype(BF16)
    lo = (r1 - mid.astype(F32)).astype(BF16)
    return hi, mid, lo


def _tri_left(tri, a):
    return sum(jnp.dot(tri, t, preferred_element_type=F32) for t in _split3(a))


def _tri_right(a, tri):
    return sum(jnp.dot(t, tri, preferred_element_type=F32) for t in _split3(a))


def _softplus(v):
    return jnp.maximum(v, 0.0) + jnp.log1p(jnp.exp(-jnp.abs(v)))


def _silu(v):
    return v * jax.nn.sigmoid(v)


def _norm_matmul_kernel(x_ref, nw_ref, w_ref, o_ref, h_sc):
    @pl.when(pl.program_id(1) == 0)
    def _():
        h_sc[...] = _rms(x_ref[...], nw_ref[...]).astype(BF16)

    o_ref[...] = jnp.dot(h_sc[...], w_ref[...],
                         preferred_element_type=F32).astype(o_ref.dtype)


def _norm_matmul(x, nw, w, out_dtype, tm, tn, name):
    m, k = x.shape
    n = w.shape[1]
    ob = jnp.dtype(out_dtype).itemsize
    vmem = 2 * (tm * k * 4 + k * tn * 2 + tm * tn * ob) + tm * k * 2 + 3 * tm * k * 4
    return pl.pallas_call(
        _norm_matmul_kernel,
        out_shape=jax.ShapeDtypeStruct((m, n), out_dtype),
        grid=(m // tm, n // tn),
        in_specs=[pl.BlockSpec((tm, k), lambda i, j: (i, 0)),
                  pl.BlockSpec((1, k), lambda i, j: (0, 0)),
                  pl.BlockSpec((k, tn), lambda i, j: (0, j))],
        out_specs=pl.BlockSpec((tm, tn), lambda i, j: (i, j)),
        scratch_shapes=[pltpu.VMEM((tm, k), BF16)],
        compiler_params=_params(("arbitrary", "arbitrary"), vmem),
        name=name,
    )(x, nw, w)


_AUG_ONES = 6
_N_PAIRS = FOX_HEADS // 2


def _aug_selector():
    sel = np.zeros((4 * LANES, _N_PAIRS * LANES), np.float32)
    for pr in range(_N_PAIRS):
        for e in range(2):
            for j in range(3):
                sel[j * LANES + 2 * pr + e, pr * LANES + 3 * e + j] = -1.0
        sel[3 * LANES, pr * LANES + _AUG_ONES:pr * LANES + _AUG_ONES + 3] = 1.0
    return jnp.asarray(sel, BF16)


def _fox_c_kernel(f_ref, b_ref, sel_ref, c_ref, aug_ref, carry_sc):
    @pl.when(pl.program_id(0) == 0)
    def _():
        carry_sc[...] = jnp.zeros_like(carry_sc)

    z = f_ref[...] + b_ref[...]
    lf = jnp.minimum(z, 0.0) - jnp.log1p(jnp.exp(-jnp.abs(z)))
    t = lf.shape[0]
    row = lax.broadcasted_iota(jnp.int32, (t, t), 0)
    col = lax.broadcasted_iota(jnp.int32, (t, t), 1)
    tril = (col <= row).astype(BF16)
    cum = _tri_left(tril, lf) + carry_sc[...]
    c_ref[...] = cum
    carry_sc[...] = cum[t - 1:t, :]
    hi, mid, lo = _split3(cum)
    parts = jnp.concatenate([hi, mid, lo, jnp.ones_like(hi)], axis=1)
    aug_ref[...] = jnp.dot(parts, sel_ref[...], preferred_element_type=F32).astype(BF16)


def _fox_c(u2, b_pad, t=256):
    s = u2.shape[0]
    aug_cols = _N_PAIRS * LANES
    return pl.pallas_call(
        _fox_c_kernel,
        out_shape=(jax.ShapeDtypeStruct((s, LANES), F32),
                   jax.ShapeDtypeStruct((s, aug_cols), BF16)),
        grid=(s // t,),
        in_specs=[pl.BlockSpec((t, LANES), lambda i: (i, _U2_SMALL // LANES)),
                  pl.BlockSpec((1, LANES), lambda i: (0, 0)),
                  pl.BlockSpec((4 * LANES, aug_cols), lambda i: (0, 0))],
        out_specs=(pl.BlockSpec((t, LANES), lambda i: (i, 0)),
                   pl.BlockSpec((t, aug_cols), lambda i: (i, 0))),
        scratch_shapes=[pltpu.VMEM((1, LANES), F32)],
        compiler_params=_params(("arbitrary",), 16 * 1024 * 1024),
        name="fox_cumlogf",
    )(u2, b_pad, _aug_selector())


_V_ROWS = HEAD_DIM + 16


def _fox_attn_kernel(q_ref, k_ref, v_ref, aug_ref, cq_ref, o_ref, vT_sc, acc_sc, s0_sc, s1_sc,
                     *, t):
    qi = pl.program_id(1)
    n_kv = k_ref.shape[0] // t

    @pl.when(qi == 0)
    def _():
        pad_row = lax.broadcasted_iota(jnp.int32, (_V_ROWS - HEAD_DIM, t), 0)
        ones_rows = jnp.where(pad_row == 0, 1.0, 0.0).astype(BF16)

        def transpose_v(c, carry):
            st = pl.multiple_of(c * t, t)
            vT = v_ref[pl.ds(st, t), :].astype(F32).T.astype(BF16)
            for e in range(2):
                vT_sc[e, 0:HEAD_DIM, pl.ds(st, t)] = vT[e * HEAD_DIM:(e + 1) * HEAD_DIM, :]
                vT_sc[e, HEAD_DIM:_V_ROWS, pl.ds(st, t)] = ones_rows
            return carry
        lax.fori_loop(0, n_kv, transpose_v, 0)

    scale = HEAD_DIM ** -0.5
    qT = q_ref[...].astype(F32).T * scale
    row = lax.broadcasted_iota(jnp.int32, (LANES, t), 0)
    tk = s0_sc.shape[1]
    key_idx = lax.broadcasted_iota(jnp.int32, (tk, t), 0)
    qry_idx = lax.broadcasted_iota(jnp.int32, (tk, t), 1)

    ws = []
    for e in range(2):
        qTe = jnp.where(row // HEAD_DIM == e, qT, 0.0)
        cq_hi, cq_mid, cq_lo = (part.astype(F32) for part in _split3(cq_ref[e]))
        aug_rows = jnp.where(
            row // 3 == e, 1.0,
            jnp.where(row == _AUG_ONES, cq_hi,
                      jnp.where(row == _AUG_ONES + 1, cq_mid,
                                jnp.where(row == _AUG_ONES + 2, cq_lo, 0.0))))
        ws.append(jnp.concatenate([qTe, aug_rows], axis=0).astype(BF16))
        acc_sc[e] = jnp.zeros((_V_ROWS, t), F32)

    def scores_into(j, s_sc):
        start = pl.multiple_of(j * tk, tk)
        keys = jnp.concatenate([k_ref[pl.ds(start, tk), :], aug_ref[pl.ds(start, tk), :]],
                               axis=1)
        for e in range(2):
            s_sc[e] = jnp.dot(keys, ws[e], preferred_element_type=F32)

    def softmax_pv(j, s_sc, m_prev, masked):
        start = pl.multiple_of(j * tk, tk)
        if masked:
            visible = key_idx + start <= qry_idx + qi * t
        m_out, probs, alphas = [], [], []
        for e in range(2):
            s = s_sc[e]
            if masked:
                s = jnp.where(visible, s, -jnp.inf)
            m_new = jnp.maximum(m_prev[e], jnp.max(s, axis=0, keepdims=True))
            probs.append(jnp.exp(s - m_new).astype(BF16))
            alphas.append(jnp.exp(m_prev[e] - m_new))
            m_out.append(m_new)
        for e in range(2):
            acc_sc[e] = alphas[e] * acc_sc[e] + jnp.dot(
                vT_sc[e, :, pl.ds(start, tk)], probs[e], preferred_element_type=F32)
        return tuple(m_out)

    def two_blocks(i, m):
        scores_into(2 * i + 1, s1_sc)
        m = softmax_pv(2 * i, s0_sc, m, False)
        scores_into(2 * i + 2, s0_sc)
        return softmax_pv(2 * i + 1, s1_sc, m, False)

    m_init = jnp.full((1, t), -1e30, F32)
    scores_into(0, s0_sc)
    m = lax.fori_loop(0, qi, two_blocks, (m_init, m_init))
    scores_into(2 * qi + 1, s1_sc)
    m = softmax_pv(2 * qi, s0_sc, m, True)
    softmax_pv(2 * qi + 1, s1_sc, m, True)
    outs = []
    for e in range(2):
        acc = acc_sc[e]
        outs.append(acc[0:HEAD_DIM, :] / acc[HEAD_DIM:HEAD_DIM + 1, :])
    o_ref[...] = jnp.concatenate(outs, axis=0).T.astype(o_ref.dtype)


def _fox_attn(qkv, aug, cq, t=512):
    s = qkv.shape[0]
    kcol = FOX_WIDTH // LANES
    vmem = 2 * (2 * t * LANES * 2 + 3 * s * LANES * 2 + 2 * t * 4) \
        + 2 * _V_ROWS * (s * 2 + t * 4) + 10 * t * t * 4
    return pl.pallas_call(
        functools.partial(_fox_attn_kernel, t=t),
        out_shape=jax.ShapeDtypeStruct((s, FOX_WIDTH), BF16),
        grid=(_N_PAIRS, s // t),
        in_specs=[pl.BlockSpec((t, LANES), lambda pr, qi: (qi, pr)),
                  pl.BlockSpec((s, LANES), lambda pr, qi: (0, kcol + pr)),
                  pl.BlockSpec((s, LANES), lambda pr, qi: (0, 2 * kcol + pr)),
                  pl.BlockSpec((s, LANES), lambda pr, qi: (0, pr)),
                  pl.BlockSpec((2, 1, t), lambda pr, qi: (pr, 0, qi))],
        out_specs=pl.BlockSpec((t, LANES), lambda pr, qi: (qi, pr)),
        scratch_shapes=[pltpu.VMEM((2, _V_ROWS, s), BF16), pltpu.VMEM((2, _V_ROWS, t), F32),
                        pltpu.VMEM((2, t // 2, t), F32), pltpu.VMEM((2, t // 2, t), F32)],
        compiler_params=_params(("arbitrary", "arbitrary"), vmem),
        name="fox_attention",
    )(qkv, qkv, qkv, aug, cq)


def _ssd_kernel(xs_ref, z_ref, bc_ref, dt_ref, dtT_ref, cwx_ref, cbx_ref, cwbc_ref, cbbc_ref,
                dtb_ref, dtbT_ref, alog_ref, alogT_ref, dskip_ref, nw_ref, y_ref,
                xbuf, bcbuf, state_sc, ybuf, *, L):
    i = pl.program_id(0)
    halo = SUBLANES

    @pl.when(i == 0)
    def _():
        xbuf[0:halo, :] = jnp.zeros((halo, xbuf.shape[1]), F32)
        bcbuf[0:halo, :] = jnp.zeros((halo, bcbuf.shape[1]), F32)
        state_sc[...] = jnp.zeros_like(state_sc)

    @pl.when(i > 0)
    def _():
        xbuf[0:halo, :] = xbuf[L:L + halo, :]
        bcbuf[0:halo, :] = bcbuf[L:L + halo, :]

    xbuf[halo:halo + L, :] = xs_ref[...]
    bcbuf[halo:halo + L, :] = bc_ref[...]

    def conv_silu(buf, w_ref, b_ref):
        acc = b_ref[...]
        for k in range(SSD_CONV):
            off = halo - (SSD_CONV - 1) + k
            acc = acc + buf[off:off + L, :] * w_ref[k:k + 1, :]
        return _silu(acc)

    xc = conv_silu(xbuf, cwx_ref, cbx_ref)
    bcc = conv_silu(bcbuf, cwbc_ref, cbbc_ref)

    dt = _softplus(dt_ref[...] + dtb_ref[...])
    dtT = _softplus(dtT_ref[...] + dtbT_ref[...])
    a = dt * (-jnp.exp(alog_ref[...]))
    aT = dtT * (-jnp.exp(alogT_ref[...]))

    row = lax.broadcasted_iota(jnp.int32, (L, L), 0)
    col = lax.broadcasted_iota(jnp.int32, (L, L), 1)
    causal = col <= row
    a_cum = _tri_left(causal.astype(BF16), a)
    a_cumT = _tri_right(aT, (row <= col).astype(BF16))
    a_last = a_cum[L - 1:L, :]
    a_lastT = a_cumT[:, L - 1:L]
    exp_acum = jnp.exp(a_cum)
    w_endT = jnp.exp(a_lastT - a_cumT) * dtT
    chunk_decay = jnp.exp(a_last)

    n = SSD_STATE
    heads_per_group = SSD_HEADS // SSD_GROUPS
    for g in range(SSD_GROUPS):
        bg = bcc[:, g * n:(g + 1) * n]
        cg = bcc[:, SSD_GROUPS * n + g * n:SSD_GROUPS * n + (g + 1) * n]
        bgT = bg.T
        cb = lax.dot_general(cg.astype(BF16), bg.astype(BF16), (((1,), (1,)), ((), ())),
                             preferred_element_type=F32)
        for hh in range(heads_per_group):
            h = g * heads_per_group + hh
            cols = slice(h * HEAD_DIM, (h + 1) * HEAD_DIM)
            seg = a_cum[:, h:h + 1] - a_cumT[h:h + 1, :]
            decay = jnp.exp(jnp.where(causal, seg, -jnp.inf))
            mix = cb * decay * dtT[h:h + 1, :]
            xh = xc[:, cols]
            xh_b = xh.astype(BF16)
            st = state_sc[:, cols]
            y = jnp.dot(mix.astype(BF16), xh_b, preferred_element_type=F32)
            y = y + jnp.dot((cg * exp_acum[:, h:h + 1]).astype(BF16), st.astype(BF16),
                            preferred_element_type=F32)
            ybuf[:, cols] = y + dskip_ref[:, cols] * xh
            upd = jnp.dot((bgT * w_endT[h:h + 1, :]).astype(BF16), xh_b,
                          preferred_element_type=F32)
            state_sc[:, cols] = chunk_decay[:, h:h + 1] * st + upd

    y = ybuf[...] * _silu(z_ref[...])
    y_ref[...] = _rms(y, nw_ref[...]).astype(y_ref.dtype)


def _ssd(u2, dt_raw, dt_rawT, cwx, cbx, cwbc, cbbc, dtb, dtbT, alog, alogT, dskip, nw, L=256):
    s = u2.shape[0]
    w = SSD_WIDTH
    full = lambda shape: pl.BlockSpec(shape, lambda i: (0,) * len(shape))
    vmem = 2 * (2 * L * w * 4 + L * _BC_WIDTH * 4 + L * w * 2) + (2 * L + 16) * w * 4 \
        + (L + 8) * _BC_WIDTH * 4 + SSD_STATE * w * 4 + 24 * L * L * 4 + 6 * L * w * 4
    return pl.pallas_call(
        functools.partial(_ssd_kernel, L=L),
        out_shape=jax.ShapeDtypeStruct((s, w), BF16),
        grid=(s // L,),
        in_specs=[pl.BlockSpec((L, w), lambda i: (i, 0)),
                  pl.BlockSpec((L, w), lambda i: (i, 1)),
                  pl.BlockSpec((L, _BC_WIDTH), lambda i: (i, 2 * w // _BC_WIDTH)),
                  pl.BlockSpec((L, SSD_HEADS), lambda i: (i, 0)),
                  pl.BlockSpec((SSD_HEADS, L), lambda i: (0, i)),
                  full((SSD_CONV, w)), full((1, w)),
                  full((SSD_CONV, _BC_WIDTH)), full((1, _BC_WIDTH)),
                  full((1, SSD_HEADS)), full((SSD_HEADS, 1)),
                  full((1, SSD_HEADS)), full((SSD_HEADS, 1)),
                  full((1, w)), full((1, w))],
        out_specs=pl.BlockSpec((L, w), lambda i: (i, 0)),
        scratch_shapes=[pltpu.VMEM((L + SUBLANES, w), F32),
                        pltpu.VMEM((L + SUBLANES, _BC_WIDTH), F32),
                        pltpu.VMEM((SSD_STATE, w), F32),
                        pltpu.VMEM((L, w), F32)],
        compiler_params=_params(("arbitrary",), vmem),
        name="ssd_mixer",
    )(u2, u2, u2, dt_raw, dt_rawT, cwx, cbx, cwbc, cbbc, dtb, dtbT, alog, alogT, dskip, nw)


def _out_proj_kernel(x_ref, ya_ref, yb_ref, wa_ref, wb_ref, o_ref):
    o_ref[...] = (x_ref[...]
                  + jnp.dot(ya_ref[...], wa_ref[...], preferred_element_type=F32)
                  + jnp.dot(yb_ref[...], wb_ref[...], preferred_element_type=F32))


def _out_proj(x, ya, yb, wa, wb, tm=512):
    m, d = x.shape
    ka, kb = ya.shape[1], yb.shape[1]
    vmem = 2 * (2 * tm * d * 4 + tm * (ka + kb) * 2 + (ka + kb) * d * 2) + 2 * tm * d * 4
    return pl.pallas_call(
        _out_proj_kernel,
        out_shape=jax.ShapeDtypeStruct((m, d), F32),
        grid=(m // tm,),
        in_specs=[pl.BlockSpec((tm, d), lambda i: (i, 0)),
                  pl.BlockSpec((tm, ka), lambda i: (i, 0)),
                  pl.BlockSpec((tm, kb), lambda i: (i, 0)),
                  pl.BlockSpec((ka, d), lambda i: (0, 0)),
                  pl.BlockSpec((kb, d), lambda i: (0, 0))],
        out_specs=pl.BlockSpec((tm, d), lambda i: (i, 0)),
        compiler_params=_params(("arbitrary",), vmem),
        name="mix_out_proj",
    )(x, ya, yb, wa, wb)


def _gate_up_kernel(x_ref, nw_ref, wg_ref, wu_ref, cw_ref, cb_ref, o_ref,
                    h_sc, tail_sc, gbuf, *, tm):
    i = pl.program_id(0)
    j = pl.program_id(1)
    halo = SUBLANES

    @pl.when(j == 0)
    def _():
        h_sc[...] = _rms(x_ref[...], nw_ref[...]).astype(BF16)

    h = h_sc[...]
    g = jnp.dot(h, wg_ref[...], preferred_element_type=F32)
    up = jnp.dot(h, wu_ref[...], preferred_element_type=F32)

    @pl.when(i == 0)
    def _():
        tail_sc[j] = jnp.zeros((halo, g.shape[1]), F32)

    prev = tail_sc[j]
    tail_sc[j] = g[tm - halo:tm, :]

    def conv(cur, back1, back2):
        return (cb_ref[...] + cur * cw_ref[2:3, :] + back1 * cw_ref[1:2, :]
                + back2 * cw_ref[0:1, :])

    gbuf[...] = conv(g, pltpu.roll(g, 1, axis=0), pltpu.roll(g, 2, axis=0))
    top = g[0:halo, :]
    r = lax.broadcasted_iota(jnp.int32, top.shape, 0)
    back1 = jnp.where(r < 1, pltpu.roll(prev, 1, axis=0), pltpu.roll(top, 1, axis=0))
    back2 = jnp.where(r < 2, pltpu.roll(prev, 2, axis=0), pltpu.roll(top, 2, axis=0))
    gbuf[0:halo, :] = conv(top, back1, back2)
    o_ref[...] = (_silu(gbuf[...]) * up).astype(o_ref.dtype)


def _gate_up(x, nw, w_gu, cw, cb, tm=512, tn=512):
    m, d = x.shape
    nj = D_FF // tn
    vmem = 2 * (tm * d * 4 + 2 * d * tn * 2 + tm * tn * 2) + tm * d * 2 + 3 * tm * d * 4 \
        + (nj * 8 + tm + 8) * tn * 4 + 6 * tm * tn * 4
    return pl.pallas_call(
        functools.partial(_gate_up_kernel, tm=tm),
        out_shape=jax.ShapeDtypeStruct((m, D_FF), BF16),
        grid=(m // tm, nj),
        in_specs=[pl.BlockSpec((tm, d), lambda i, j: (i, 0)),
                  pl.BlockSpec((1, d), lambda i, j: (0, 0)),
                  pl.BlockSpec((d, tn), lambda i, j: (0, j)),
                  pl.BlockSpec((d, tn), lambda i, j: (0, j + nj)),
                  pl.BlockSpec((FFN_CONV, tn), lambda i, j: (0, j)),
                  pl.BlockSpec((1, tn), lambda i, j: (0, j))],
        out_specs=pl.BlockSpec((tm, tn), lambda i, j: (i, j)),
        scratch_shapes=[pltpu.VMEM((tm, d), BF16),
                        pltpu.VMEM((nj, SUBLANES, tn), F32),
                        pltpu.VMEM((tm, tn), F32)],
        compiler_params=_params(("arbitrary", "arbitrary"), vmem),
        name="ffn_gate_up",
    )(x, nw, w_gu, w_gu, cw, cb)


def _down_kernel(a_ref, w_ref, x_ref, o_ref, w_sc):
    @pl.when(pl.program_id(1) == 0)
    def _():
        w_sc[...] = w_ref[...].astype(BF16)

    o_ref[...] = x_ref[...] + jnp.dot(a_ref[...], w_sc[...], preferred_element_type=F32)


def _down(act, w, x, tm=512, tn=512):
    m, d = x.shape
    kk = act.shape[1]
    vmem = 2 * (tm * kk * 2 + kk * tn * 4 + 2 * tm * tn * 4) + kk * tn * 2 + 2 * tm * tn * 4
    return pl.pallas_call(
        _down_kernel,
        out_shape=jax.ShapeDtypeStruct((m, d), F32),
        grid=(d // tn, m // tm),
        in_specs=[pl.BlockSpec((tm, kk), lambda j, i: (i, 0)),
                  pl.BlockSpec((kk, tn), lambda j, i: (0, j)),
                  pl.BlockSpec((tm, tn), lambda j, i: (i, j))],
        out_specs=pl.BlockSpec((tm, tn), lambda j, i: (i, j)),
        scratch_shapes=[pltpu.VMEM((kk, tn), BF16)],
        compiler_params=_params(("arbitrary", "arbitrary"), vmem),
        name="ffn_down",
    )(act, w, x)


def _ple_kernel(x_ref, p_ref, nw_ref, wg_ref, wp_ref, fw_ref, o_ref, *, final):
    x = x_ref[...]
    h = _rms(x, nw_ref[...]).astype(BF16)
    gate = jax.nn.sigmoid(jnp.dot(h, wg_ref[...], preferred_element_type=F32))
    proj = jnp.dot(p_ref[...].astype(BF16), wp_ref[...], preferred_element_type=F32)
    x = x + gate * proj
    if final:
        x = _rms(x, fw_ref[...])
    o_ref[...] = x


def _ple(x, p, nw, wg, wp, fw, final, tm=256):
    m, d = x.shape
    dp = p.shape[1]
    vmem = 2 * (2 * tm * d * 4 + tm * dp * 4 + d * d * 2 + dp * d * 2) + 6 * tm * d * 4
    return pl.pallas_call(
        functools.partial(_ple_kernel, final=final),
        out_shape=jax.ShapeDtypeStruct((m, d), F32),
        grid=(m // tm,),
        in_specs=[pl.BlockSpec((tm, d), lambda i: (i, 0)),
                  pl.BlockSpec((tm, dp), lambda i: (i, 0)),
                  pl.BlockSpec((1, d), lambda i: (0, 0)),
                  pl.BlockSpec((d, d), lambda i: (0, 0)),
                  pl.BlockSpec((dp, d), lambda i: (0, 0)),
                  pl.BlockSpec((1, d), lambda i: (0, 0))],
        out_specs=pl.BlockSpec((tm, d), lambda i: (i, 0)),
        compiler_params=_params(("arbitrary",), vmem),
        name="ple_gate",
    )(x, p, nw, wg, wp, fw)


def _layer(x, p, mix_norm_w, w_in, fox_forget_bias, ssd_conv_w, ssd_conv_b, ssd_dt_bias,
           ssd_A_log, ssd_D, ssd_norm_w, w_out, ffn_norm_w, w_gate_up, ffn_conv_w,
           ffn_conv_b, w_down, ple_norm_w, w_ple_gate, w_ple_proj, final_norm_w, final):
    s = x.shape[0]
    row = lambda v: v.reshape(1, -1).astype(F32)

    w_qkv = w_in[:, _OFF_Q:_OFF_F].astype(BF16)
    w_u2 = jnp.concatenate(
        [w_in[:, _OFF_XS:_OFF_B], w_in[:, _OFF_Z:_OFF_DT], w_in[:, _OFF_B:_OFF_Z],
         w_in[:, _OFF_F:_OFF_XS], w_in[:, _OFF_DT:_IN_COLS],
         jnp.zeros((D_MODEL, LANES - FOX_HEADS - SSD_HEADS), w_in.dtype)], axis=1).astype(BF16)

    nw = row(mix_norm_w)
    qkv = _norm_matmul(x, nw, w_qkv, BF16, tm=1024, tn=512, name="in_proj_qkv")
    u2 = _norm_matmul(x, nw, w_u2, F32, tm=1024, tn=_U2_COLS // 3, name="in_proj_ssd")

    b_pad = jnp.zeros((1, LANES), F32).at[0, :FOX_HEADS].set(fox_forget_bias.astype(F32))
    c, aug = _fox_c(u2, b_pad)
    cT = c[:, :FOX_HEADS].T
    y_fox = _fox_attn(qkv, aug, cT[:, None, :])

    dt_raw = u2[:, _U2_SMALL + FOX_HEADS:_U2_SMALL + FOX_HEADS + SSD_HEADS]
    col = lambda v: v.reshape(-1, 1).astype(F32)
    y_ssd = _ssd(u2, dt_raw, dt_raw.T,
                 ssd_conv_w[:, :SSD_WIDTH].astype(F32), row(ssd_conv_b[:SSD_WIDTH]),
                 ssd_conv_w[:, SSD_WIDTH:].astype(F32), row(ssd_conv_b[SSD_WIDTH:]),
                 row(ssd_dt_bias), col(ssd_dt_bias), row(ssd_A_log), col(ssd_A_log),
                 row(jnp.repeat(ssd_D, HEAD_DIM)), row(ssd_norm_w))

    w_out_b = w_out.astype(BF16)
    x = _out_proj(x, y_fox, y_ssd, w_out_b[:FOX_WIDTH], w_out_b[FOX_WIDTH:])

    act = _gate_up(x, row(ffn_norm_w), w_gate_up.astype(BF16),
                   ffn_conv_w.astype(F32), row(ffn_conv_b))
    x = _down(act, w_down.astype(F32), x)

    return _ple(x, p, row(ple_norm_w), w_ple_gate.astype(BF16), w_ple_proj.astype(BF16),
                row(final_norm_w), final)


def kernel(x, p, mix_norm_w, w_in, fox_forget_bias, ssd_conv_w, ssd_conv_b, ssd_dt_bias,
           ssd_A_log, ssd_D, ssd_norm_w, w_out, ffn_norm_w, w_gate_up, ffn_conv_w,
           ffn_conv_b, w_down, ple_norm_w, w_ple_gate, w_ple_proj, final_norm_w):
    bsz, s, d = x.shape
    depth = p.shape[0]
    outs = []
    for b in range(bsz):
        xb = x[b]
        for i in range(depth):
            xb = _layer(xb, p[i, b], mix_norm_w[i], w_in[i], fox_forget_bias[i], ssd_conv_w[i],
                        ssd_conv_b[i], ssd_dt_bias[i], ssd_A_log[i], ssd_D[i], ssd_norm_w[i],
                        w_out[i], ffn_norm_w[i], w_gate_up[i], ffn_conv_w[i], ffn_conv_b[i],
                        w_down[i], ple_norm_w[i], w_ple_gate[i], w_ple_proj[i], final_norm_w,
                        final=(i == depth - 1))
        outs.append(xb)
    return outs[0][None] if bsz == 1 else jnp.stack(outs, axis=0)
```

```python
import functools

import numpy as np
import jax
import jax.numpy as jnp
from jax import lax
from jax.experimental import pallas as pl
from jax.experimental.pallas import tpu as pltpu

F32 = jnp.float32
BF16 = jnp.bfloat16

D_MODEL = 2048
SEQ = 8192
D_PLE = 256
HEAD_DIM = 64
FOX_WIDTH = 1024
FOX_HEADS = 16
SSD_WIDTH = 1024
SSD_HEADS = 16
SSD_GROUPS = 2
SSD_STATE = 128
SSD_CONV = 4
D_FF = 5632
FFN_CONV = 3
EPS = 1e-6

_OFF_Q = 0
_OFF_F = 3 * FOX_WIDTH
_OFF_XS = _OFF_F + FOX_HEADS
_OFF_B = _OFF_XS + SSD_WIDTH
_OFF_C = _OFF_B + SSD_GROUPS * SSD_STATE
_OFF_Z = _OFF_C + SSD_GROUPS * SSD_STATE
_OFF_DT = _OFF_Z + SSD_WIDTH
_IN_COLS = _OFF_DT + SSD_HEADS

LANES = 128
SUBLANES = 8
VMEM_CAP = 60 * 1024 * 1024

_BC_WIDTH = 2 * SSD_GROUPS * SSD_STATE
_U2_COLS = 2 * SSD_WIDTH + _BC_WIDTH + LANES
_U2_SMALL = 2 * SSD_WIDTH + _BC_WIDTH


def _params(sem, vmem_bytes):
    return pltpu.CompilerParams(dimension_semantics=sem,
                                vmem_limit_bytes=int(min(VMEM_CAP, vmem_bytes)))


def _rms(xf, w):
    ms = jnp.mean(xf * xf, axis=-1, keepdims=True)
    return xf * lax.rsqrt(ms + EPS) * w


def _split3(a):
    hi = a.astype(BF16)
    r1 = a - hi.astype(F32)
    mid = r1.astype(BF16)
    lo = (r1 - mid.astype(F32)).astype(BF16)
    return hi, mid, lo


def _tri_left(tri, a):
    return sum(jnp.dot(tri, t, preferred_element_type=F32) for t in _split3(a))


def _tri_right(a, tri):
    return sum(jnp.dot(t, tri, preferred_element_type=F32) for t in _split3(a))


def _softplus(v):
    return jnp.maximum(v, 0.0) + jnp.log1p(jnp.exp(-jnp.abs(v)))


def _silu(v):
    return v * jax.nn.sigmoid(v)


def _norm_kernel(x_ref, nw_ref, h_ref):
    h_ref[...] = _rms(x_ref[...], nw_ref[...]).astype(h_ref.dtype)


def _norm(x, nw, tm=512):
    m, d = x.shape
    return pl.pallas_call(
        _norm_kernel,
        out_shape=jax.ShapeDtypeStruct((m, d), BF16),
        grid=(m // tm,),
        in_specs=[pl.BlockSpec((tm, d), lambda i: (i, 0)),
                  pl.BlockSpec((1, d), lambda i: (0, 0))],
        out_specs=pl.BlockSpec((tm, d), lambda i: (i, 0)),
        compiler_params=_params(("arbitrary",), 2 * tm * d * 6 + 4 * tm * d * 4),
        name="mix_norm",
    )(x, nw)


def _proj_kernel(h_ref, w_ref, o_ref, w_sc):
    @pl.when(pl.program_id(1) == 0)
    def _():
        w_sc[...] = w_ref[...].astype(BF16)

    o_ref[...] = jnp.dot(h_ref[...], w_sc[...],
                         preferred_element_type=F32).astype(o_ref.dtype)


def _proj(h, w, n_cols, out_dtype, tm, tn, name):
    m, k = h.shape
    ob = jnp.dtype(out_dtype).itemsize
    vmem = 2 * (tm * k * 2 + k * tn * 4 + tm * tn * ob) + k * tn * 2 + 2 * tm * tn * 4
    return pl.pallas_call(
        _proj_kernel,
        out_shape=jax.ShapeDtypeStruct((m, n_cols), out_dtype),
        grid=(n_cols // tn, m // tm),
        in_specs=[pl.BlockSpec((tm, k), lambda j, i: (i, 0)),
                  pl.BlockSpec((k, tn), lambda j, i: (0, j))],
        out_specs=pl.BlockSpec((tm, tn), lambda j, i: (i, j)),
        scratch_shapes=[pltpu.VMEM((k, tn), BF16)],
        compiler_params=_params(("arbitrary", "arbitrary"), vmem),
        name=name,
    )(h, w)


_AUG_ONES = 6
_N_PAIRS = FOX_HEADS // 2


def _aug_selector():
    sel = np.zeros((4 * LANES, _N_PAIRS * LANES), np.float32)
    for pr in range(_N_PAIRS):
        for e in range(2):
            for j in range(3):
                sel[j * LANES + 2 * pr + e, pr * LANES + 3 * e + j] = -1.0
        sel[3 * LANES, pr * LANES + _AUG_ONES:pr * LANES + _AUG_ONES + 3] = 1.0
    return jnp.asarray(sel, BF16)


def _fox_c_kernel(f_ref, b_ref, sel_ref, c_ref, aug_ref, carry_sc):
    @pl.when(pl.program_id(0) == 0)
    def _():
        carry_sc[...] = jnp.zeros_like(carry_sc)

    z = f_ref[...] + b_ref[...]
    lf = jnp.minimum(z, 0.0) - jnp.log1p(jnp.exp(-jnp.abs(z)))
    t = lf.shape[0]
    row = lax.broadcasted_iota(jnp.int32, (t, t), 0)
    col = lax.broadcasted_iota(jnp.int32, (t, t), 1)
    tril = (col <= row).astype(BF16)
    cum = _tri_left(tril, lf) + carry_sc[...]
    c_ref[...] = cum
    carry_sc[...] = cum[t - 1:t, :]
    hi, mid, lo = _split3(cum)
    parts = jnp.concatenate([hi, mid, lo, jnp.ones_like(hi)], axis=1)
    aug_ref[...] = jnp.dot(parts, sel_ref[...], preferred_element_type=F32).astype(BF16)


def _fox_c(u2, b_pad, t=256):
    s = u2.shape[0]
    aug_cols = _N_PAIRS * LANES
    return pl.pallas_call(
        _fox_c_kernel,
        out_shape=(jax.ShapeDtypeStruct((s, LANES), F32),
                   jax.ShapeDtypeStruct((s, aug_cols), BF16)),
        grid=(s // t,),
        in_specs=[pl.BlockSpec((t, LANES), lambda i: (i, _U2_SMALL // LANES)),
                  pl.BlockSpec((1, LANES), lambda i: (0, 0)),
                  pl.BlockSpec((4 * LANES, aug_cols), lambda i: (0, 0))],
        out_specs=(pl.BlockSpec((t, LANES), lambda i: (i, 0)),
                   pl.BlockSpec((t, aug_cols), lambda i: (i, 0))),
        scratch_shapes=[pltpu.VMEM((1, LANES), F32)],
        compiler_params=_params(("arbitrary",), 16 * 1024 * 1024),
        name="fox_cumlogf",
    )(u2, b_pad, _aug_selector())


_V_ROWS = HEAD_DIM + 16


def _fox_attn_kernel(q_ref, k_ref, v_ref, aug_ref, cq_ref, o_ref, vT_sc, acc_sc, s0_sc, s1_sc,
                     *, t):
    qi = pl.program_id(1)
    n_kv = k_ref.shape[0] // t

    @pl.when(qi == 0)
    def _():
        pad_row = lax.broadcasted_iota(jnp.int32, (_V_ROWS - HEAD_DIM, t), 0)
        ones_rows = jnp.where(pad_row == 0, 1.0, 0.0).astype(BF16)

        def transpose_v(c, carry):
            st = pl.multiple_of(c * t, t)
            vT = v_ref[pl.ds(st, t), :].astype(F32).T.astype(BF16)
            for e in range(2):
                vT_sc[e, 0:HEAD_DIM, pl.ds(st, t)] = vT[e * HEAD_DIM:(e + 1) * HEAD_DIM, :]
                vT_sc[e, HEAD_DIM:_V_ROWS, pl.ds(st, t)] = ones_rows
            return carry
        lax.fori_loop(0, n_kv, transpose_v, 0)

    scale = HEAD_DIM ** -0.5
    qT = q_ref[...].astype(F32).T * scale
    row = lax.broadcasted_iota(jnp.int32, (LANES, t), 0)
    tk = s0_sc.shape[1]
    key_idx = lax.broadcasted_iota(jnp.int32, (tk, t), 0)
    qry_idx = lax.broadcasted_iota(jnp.int32, (tk, t), 1)

    ws = []
    for e in range(2):
        qTe = jnp.where(row // HEAD_DIM == e, qT, 0.0)
        cq_hi, cq_mid, cq_lo = (part.astype(F32) for part in _split3(cq_ref[e]))
        aug_rows = jnp.where(
            row // 3 == e, 1.0,
            jnp.where(row == _AUG_ONES, cq_hi,
                      jnp.where(row == _AUG_ONES + 1, cq_mid,
                                jnp.where(row == _AUG_ONES + 2, cq_lo, 0.0))))
        ws.append(jnp.concatenate([qTe, aug_rows], axis=0).astype(BF16))
        acc_sc[e] = jnp.zeros((_V_ROWS, t), F32)

    def scores_into(j, s_sc):
        start = pl.multiple_of(j * tk, tk)
        keys = jnp.concatenate([k_ref[pl.ds(start, tk), :], aug_ref[pl.ds(start, tk), :]],
                               axis=1)
        for e in range(2):
            s_sc[e] = jnp.dot(keys, ws[e], preferred_element_type=F32)

    def softmax_pv(j, s_sc, m_prev, masked):
        start = pl.multiple_of(j * tk, tk)
        if masked:
            visible = key_idx + start <= qry_idx + qi * t
        m_out, probs, alphas = [], [], []
        for e in range(2):
            s = s_sc[e]
            if masked:
                s = jnp.where(visible, s, -jnp.inf)
            m_new = jnp.maximum(m_prev[e], jnp.max(s, axis=0, keepdims=True))
            probs.append(jnp.exp(s - m_new).astype(BF16))
            alphas.append(jnp.exp(m_prev[e] - m_new))
            m_out.append(m_new)
        for e in range(2):
            acc_sc[e] = alphas[e] * acc_sc[e] + jnp.dot(
                vT_sc[e, :, pl.ds(start, tk)], probs[e], preferred_element_type=F32)
        return tuple(m_out)

    def two_blocks(i, m):
        scores_into(2 * i + 1, s1_sc)
        m = softmax_pv(2 * i, s0_sc, m, False)
        scores_into(2 * i + 2, s0_sc)
        return softmax_pv(2 * i + 1, s1_sc, m, False)

    m_init = jnp.full((1, t), -1e30, F32)
    scores_into(0, s0_sc)
    m = lax.fori_loop(0, qi, two_blocks, (m_init, m_init))
    scores_into(2 * qi + 1, s1_sc)
    m = softmax_pv(2 * qi, s0_sc, m, True)
    softmax_pv(2 * qi + 1, s1_sc, m, True)
    outs = []
    for e in range(2):
        acc = acc_sc[e]
        outs.append(acc[0:HEAD_DIM, :] / acc[HEAD_DIM:HEAD_DIM + 1, :])
    o_ref[...] = jnp.concatenate(outs, axis=0).T.astype(o_ref.dtype)


def _fox_attn(qkv, aug, cq, t=512):
    s = qkv.shape[0]
    kcol = FOX_WIDTH // LANES
    vmem = 2 * (2 * t * LANES * 2 + 3 * s * LANES * 2 + 2 * t * 4) \
        + 2 * _V_ROWS * (s * 2 + t * 4) + 10 * t * t * 4
    return pl.pallas_call(
        functools.partial(_fox_attn_kernel, t=t),
        out_shape=jax.ShapeDtypeStruct((s, FOX_WIDTH), BF16),
        grid=(_N_PAIRS, s // t),
        in_specs=[pl.BlockSpec((t, LANES), lambda pr, qi: (qi, pr)),
                  pl.BlockSpec((s, LANES), lambda pr, qi: (0, kcol + pr)),
                  pl.BlockSpec((s, LANES), lambda pr, qi: (0, 2 * kcol + pr)),
                  pl.BlockSpec((s, LANES), lambda pr, qi: (0, pr)),
                  pl.BlockSpec((2, 1, t), lambda pr, qi: (pr, 0, qi))],
        out_specs=pl.BlockSpec((t, LANES), lambda pr, qi: (qi, pr)),
        scratch_shapes=[pltpu.VMEM((2, _V_ROWS, s), BF16), pltpu.VMEM((2, _V_ROWS, t), F32),
                        pltpu.VMEM((2, t // 2, t), F32), pltpu.VMEM((2, t // 2, t), F32)],
        compiler_params=_params(("arbitrary", "arbitrary"), vmem),
        name="fox_attention",
    )(qkv, qkv, qkv, aug, cq)


def _ssd_kernel(xs_ref, z_ref, bc_ref, dt_ref, dtT_ref, cwx_ref, cbx_ref, cwbc_ref, cbbc_ref,
                dtb_ref, dtbT_ref, alog_ref, alogT_ref, dskip_ref, nw_ref, y_ref,
                xbuf, bcbuf, state_sc, ybuf, *, L):
    i = pl.program_id(0)
    halo = SUBLANES

    @pl.when(i == 0)
    def _():
        xbuf[0:halo, :] = jnp.zeros((halo, xbuf.shape[1]), F32)
        bcbuf[0:halo, :] = jnp.zeros((halo, bcbuf.shape[1]), F32)
        state_sc[...] = jnp.zeros_like(state_sc)

    @pl.when(i > 0)
    def _():
        xbuf[0:halo, :] = xbuf[L:L + halo, :]
        bcbuf[0:halo, :] = bcbuf[L:L + halo, :]

    xbuf[halo:halo + L, :] = xs_ref[...]
    bcbuf[halo:halo + L, :] = bc_ref[...]

    def conv_silu(buf, w_ref, b_ref):
        acc = b_ref[...]
        for k in range(SSD_CONV):
            off = halo - (SSD_CONV - 1) + k
            acc = acc + buf[off:off + L, :] * w_ref[k:k + 1, :]
        return _silu(acc)

    xc = conv_silu(xbuf, cwx_ref, cbx_ref)
    bcc = conv_silu(bcbuf, cwbc_ref, cbbc_ref)

    dt = _softplus(dt_ref[...] + dtb_ref[...])
    dtT = _softplus(dtT_ref[...] + dtbT_ref[...])
    a = dt * (-jnp.exp(alog_ref[...]))
    aT = dtT * (-jnp.exp(alogT_ref[...]))

    row = lax.broadcasted_iota(jnp.int32, (L, L), 0)
    col = lax.broadcasted_iota(jnp.int32, (L, L), 1)
    causal = col <= row
    a_cum = _tri_left(causal.astype(BF16), a)
    a_cumT = _tri_right(aT, (row <= col).astype(BF16))
    a_last = a_cum[L - 1:L, :]
    a_lastT = a_cumT[:, L - 1:L]
    exp_acum = jnp.exp(a_cum)
    w_endT = jnp.exp(a_lastT - a_cumT) * dtT
    chunk_decay = jnp.exp(a_last)

    n = SSD_STATE
    heads_per_group = SSD_HEADS // SSD_GROUPS
    for g in range(SSD_GROUPS):
        bg = bcc[:, g * n:(g + 1) * n]
        cg = bcc[:, SSD_GROUPS * n + g * n:SSD_GROUPS * n + (g + 1) * n]
        bgT = bg.T
        cb = lax.dot_general(cg.astype(BF16), bg.astype(BF16), (((1,), (1,)), ((), ())),
                             preferred_element_type=F32)
        for hh in range(heads_per_group):
            h = g * heads_per_group + hh
            cols = slice(h * HEAD_DIM, (h + 1) * HEAD_DIM)
            seg = a_cum[:, h:h + 1] - a_cumT[h:h + 1, :]
            decay = jnp.exp(jnp.where(causal, seg, -jnp.inf))
            mix = cb * decay * dtT[h:h + 1, :]
            xh = xc[:, cols]
            xh_b = xh.astype(BF16)
            st = state_sc[:, cols]
            y = jnp.dot(mix.astype(BF16), xh_b, preferred_element_type=F32)
            y = y + jnp.dot((cg * exp_acum[:, h:h + 1]).astype(BF16), st.astype(BF16),
                            preferred_element_type=F32)
            ybuf[:, cols] = y + dskip_ref[:, cols] * xh
            upd = jnp.dot((bgT * w_endT[h:h + 1, :]).astype(BF16), xh_b,
                          preferred_element_type=F32)
            state_sc[:, cols] = chunk_decay[:, h:h + 1] * st + upd

    y = ybuf[...] * _silu(z_ref[...])
    y_ref[...] = _rms(y, nw_ref[...]).astype(y_ref.dtype)


def _ssd(u2, dt_raw, dt_rawT, cwx, cbx, cwbc, cbbc, dtb, dtbT, alog, alogT, dskip, nw, L=256):
    s = u2.shape[0]
    w = SSD_WIDTH
    full = lambda shape: pl.BlockSpec(shape, lambda i: (0,) * len(shape))
    vmem = 2 * (2 * L * w * 4 + L * _BC_WIDTH * 4 + L * w * 2) + (2 * L + 16) * w * 4 \
        + (L + 8) * _BC_WIDTH * 4 + SSD_STATE * w * 4 + 24 * L * L * 4 + 6 * L * w * 4
    return pl.pallas_call(
        functools.partial(_ssd_kernel, L=L),
        out_shape=jax.ShapeDtypeStruct((s, w), BF16),
        grid=(s // L,),
        in_specs=[pl.BlockSpec((L, w), lambda i: (i, 0)),
                  pl.BlockSpec((L, w), lambda i: (i, 1)),
                  pl.BlockSpec((L, _BC_WIDTH), lambda i: (i, 2 * w // _BC_WIDTH)),
                  pl.BlockSpec((L, SSD_HEADS), lambda i: (i, 0)),
                  pl.BlockSpec((SSD_HEADS, L), lambda i: (0, i)),
                  full((SSD_CONV, w)), full((1, w)),
                  full((SSD_CONV, _BC_WIDTH)), full((1, _BC_WIDTH)),
                  full((1, SSD_HEADS)), full((SSD_HEADS, 1)),
                  full((1, SSD_HEADS)), full((SSD_HEADS, 1)),
                  full((1, w)), full((1, w))],
        out_specs=pl.BlockSpec((L, w), lambda i: (i, 0)),
        scratch_shapes=[pltpu.VMEM((L + SUBLANES, w), F32),
                        pltpu.VMEM((L + SUBLANES, _BC_WIDTH), F32),
                        pltpu.VMEM((SSD_STATE, w), F32),
                        pltpu.VMEM((L, w), F32)],
        compiler_params=_params(("arbitrary",), vmem),
        name="ssd_mixer",
    )(u2, u2, u2, dt_raw, dt_rawT, cwx, cbx, cwbc, cbbc, dtb, dtbT, alog, alogT, dskip, nw)


def _out_proj_kernel(x_ref, ya_ref, yb_ref, wa_ref, wb_ref, nw_ref, o_ref, h_ref):
    o = (x_ref[...]
         + jnp.dot(ya_ref[...], wa_ref[...], preferred_element_type=F32)
         + jnp.dot(yb_ref[...], wb_ref[...], preferred_element_type=F32))
    o_ref[...] = o
    h_ref[...] = _rms(o, nw_ref[...]).astype(h_ref.dtype)


def _out_proj(x, ya, yb, wa, wb, nw, tm=512):
    m, d = x.shape
    ka, kb = ya.shape[1], yb.shape[1]
    vmem = 2 * (2 * tm * d * 4 + tm * d * 2 + tm * (ka + kb) * 2 + (ka + kb) * d * 2) \
        + 4 * tm * d * 4
    return pl.pallas_call(
        _out_proj_kernel,
        out_shape=(jax.ShapeDtypeStruct((m, d), F32), jax.ShapeDtypeStruct((m, d), BF16)),
        grid=(m // tm,),
        in_specs=[pl.BlockSpec((tm, d), lambda i: (i, 0)),
                  pl.BlockSpec((tm, ka), lambda i: (i, 0)),
                  pl.BlockSpec((tm, kb), lambda i: (i, 0)),
                  pl.BlockSpec((ka, d), lambda i: (0, 0)),
                  pl.BlockSpec((kb, d), lambda i: (0, 0)),
                  pl.BlockSpec((1, d), lambda i: (0, 0))],
        out_specs=(pl.BlockSpec((tm, d), lambda i: (i, 0)),
                   pl.BlockSpec((tm, d), lambda i: (i, 0))),
        compiler_params=_params(("arbitrary",), vmem),
        name="mix_out_proj",
    )(x, ya, yb, wa, wb, nw)


def _gate_up_kernel(h_ref, wg_ref, wu_ref, cw_ref, cb_ref, o_ref, wg_sc, wu_sc, tail_sc, conv_sc):
    halo = SUBLANES
    tm = conv_sc.shape[0]

    @pl.when(pl.program_id(1) == 0)
    def _():
        wg_sc[...] = wg_ref[...].astype(BF16)
        wu_sc[...] = wu_ref[...].astype(BF16)
        tail_sc[...] = jnp.zeros_like(tail_sc)

    h = h_ref[...]
    g = jnp.dot(h, wg_sc[...], preferred_element_type=F32)
    up = jnp.dot(h, wu_sc[...], preferred_element_type=F32)

    prev = tail_sc[...]
    tail_sc[...] = g[tm - halo:tm, :]

    def conv(cur, back1, back2):
        return (cb_ref[...] + cur * cw_ref[2:3, :] + back1 * cw_ref[1:2, :]
                + back2 * cw_ref[0:1, :])

    conv_sc[...] = conv(g, pltpu.roll(g, 1, axis=0), pltpu.roll(g, 2, axis=0))
    top = g[0:halo, :]
    r = lax.broadcasted_iota(jnp.int32, top.shape, 0)
    back1 = jnp.where(r < 1, pltpu.roll(prev, 1, axis=0), pltpu.roll(top, 1, axis=0))
    back2 = jnp.where(r < 2, pltpu.roll(prev, 2, axis=0), pltpu.roll(top, 2, axis=0))
    conv_sc[0:halo, :] = conv(top, back1, back2)
    o_ref[...] = (_silu(conv_sc[...]) * up).astype(o_ref.dtype)


def _gate_up(h, w_gu, cw, cb, tm=512, tn=512):
    m, d = h.shape
    nj = D_FF // tn
    vmem = 2 * (tm * d * 2 + 2 * d * tn * 4 + tm * tn * 2) + 2 * d * tn * 2 \
        + 10 * tm * tn * 4
    return pl.pallas_call(
        _gate_up_kernel,
        out_shape=jax.ShapeDtypeStruct((m, D_FF), BF16),
        grid=(nj, m // tm),
        in_specs=[pl.BlockSpec((tm, d), lambda j, i: (i, 0)),
                  pl.BlockSpec((d, tn), lambda j, i: (0, j)),
                  pl.BlockSpec((d, tn), lambda j, i: (0, j + nj)),
                  pl.BlockSpec((FFN_CONV, tn), lambda j, i: (0, j)),
                  pl.BlockSpec((1, tn), lambda j, i: (0, j))],
        out_specs=pl.BlockSpec((tm, tn), lambda j, i: (i, j)),
        scratch_shapes=[pltpu.VMEM((d, tn), BF16), pltpu.VMEM((d, tn), BF16),
                        pltpu.VMEM((SUBLANES, tn), F32), pltpu.VMEM((tm, tn), F32)],
        compiler_params=_params(("arbitrary", "arbitrary"), vmem),
        name="ffn_gate_up",
    )(h, w_gu, w_gu, cw, cb)


def _down_kernel(a_ref, w_ref, x_ref, o_ref, w_sc):
    @pl.when(pl.program_id(1) == 0)
    def _():
        w_sc[...] = w_ref[...].astype(BF16)

    o_ref[...] = x_ref[...] + jnp.dot(a_ref[...], w_sc[...], preferred_element_type=F32)


def _down(act, w, x, tm=512, tn=512):
    m, d = x.shape
    kk = act.shape[1]
    vmem = 2 * (tm * kk * 2 + kk * tn * 4 + 2 * tm * tn * 4) + kk * tn * 2 + 2 * tm * tn * 4
    return pl.pallas_call(
        _down_kernel,
        out_shape=jax.ShapeDtypeStruct((m, d), F32),
        grid=(d // tn, m // tm),
        in_specs=[pl.BlockSpec((tm, kk), lambda j, i: (i, 0)),
                  pl.BlockSpec((kk, tn), lambda j, i: (0, j)),
                  pl.BlockSpec((tm, tn), lambda j, i: (i, j))],
        out_specs=pl.BlockSpec((tm, tn), lambda j, i: (i, j)),
        scratch_shapes=[pltpu.VMEM((kk, tn), BF16)],
        compiler_params=_params(("arbitrary", "arbitrary"), vmem),
        name="ffn_down",
    )(act, w, x)


def _ple_kernel(x_ref, p_ref, nw_ref, wg_ref, wp_ref, fw_ref, o_ref, *, final):
    x = x_ref[...]
    h = _rms(x, nw_ref[...]).astype(BF16)
    gate = jax.nn.sigmoid(jnp.dot(h, wg_ref[...], preferred_element_type=F32))
    proj = jnp.dot(p_ref[...].astype(BF16), wp_ref[...], preferred_element_type=F32)
    x = x + gate * proj
    if final:
        x = _rms(x, fw_ref[...])
    o_ref[...] = x


def _ple(x, p, nw, wg, wp, fw, final, tm=256):
    m, d = x.shape
    dp = p.shape[1]
    vmem = 2 * (2 * tm * d * 4 + tm * dp * 4 + d * d * 2 + dp * d * 2) + 6 * tm * d * 4
    return pl.pallas_call(
        functools.partial(_ple_kernel, final=final),
        out_shape=jax.ShapeDtypeStruct((m, d), F32),
        grid=(m // tm,),
        in_specs=[pl.BlockSpec((tm, d), lambda i: (i, 0)),
                  pl.BlockSpec((tm, dp), lambda i: (i, 0)),
                  pl.BlockSpec((1, d), lambda i: (0, 0)),
                  pl.BlockSpec((d, d), lambda i: (0, 0)),
                  pl.BlockSpec((dp, d), lambda i: (0, 0)),
                  pl.BlockSpec((1, d), lambda i: (0, 0))],
        out_specs=pl.BlockSpec((tm, d), lambda i: (i, 0)),
        compiler_params=_params(("arbitrary",), vmem),
        name="ple_gate",
    )(x, p, nw, wg, wp, fw)


def _layer(x, p, mix_norm_w, w_in, fox_forget_bias, ssd_conv_w, ssd_conv_b, ssd_dt_bias,
           ssd_A_log, ssd_D, ssd_norm_w, w_out, ffn_norm_w, w_gate_up, ffn_conv_w,
           ffn_conv_b, w_down, ple_norm_w, w_ple_gate, w_ple_proj, final_norm_w, final):
    s = x.shape[0]
    row = lambda v: v.reshape(1, -1).astype(F32)

    w_u2 = jnp.concatenate(
        [w_in[:, _OFF_XS:_OFF_B], w_in[:, _OFF_Z:_OFF_DT], w_in[:, _OFF_B:_OFF_Z],
         w_in[:, _OFF_F:_OFF_XS], w_in[:, _OFF_DT:_IN_COLS],
         jnp.zeros((D_MODEL, LANES - FOX_HEADS - SSD_HEADS), w_in.dtype)], axis=1).astype(F32)

    h_mix = _norm(x, row(mix_norm_w))
    qkv = _proj(h_mix, w_in.astype(F32), _OFF_F, BF16, tm=1024, tn=512, name="in_proj_qkv")
    u2 = _proj(h_mix, w_u2, _U2_COLS, F32, tm=1024, tn=_U2_COLS // 3, name="in_proj_ssd")

    b_pad = jnp.zeros((1, LANES), F32).at[0, :FOX_HEADS].set(fox_forget_bias.astype(F32))
    c, aug = _fox_c(u2, b_pad)
    cT = c[:, :FOX_HEADS].T
    y_fox = _fox_attn(qkv, aug, cT[:, None, :])

    dt_raw = u2[:, _U2_SMALL + FOX_HEADS:_U2_SMALL + FOX_HEADS + SSD_HEADS]
    col = lambda v: v.reshape(-1, 1).astype(F32)
    y_ssd = _ssd(u2, dt_raw, dt_raw.T,
                 ssd_conv_w[:, :SSD_WIDTH].astype(F32), row(ssd_conv_b[:SSD_WIDTH]),
                 ssd_conv_w[:, SSD_WIDTH:].astype(F32), row(ssd_conv_b[SSD_WIDTH:]),
                 row(ssd_dt_bias), col(ssd_dt_bias), row(ssd_A_log), col(ssd_A_log),
                 row(jnp.repeat(ssd_D, HEAD_DIM)), row(ssd_norm_w))

    w_out_b = w_out.astype(BF16)
    x, h_ffn = _out_proj(x, y_fox, y_ssd, w_out_b[:FOX_WIDTH], w_out_b[FOX_WIDTH:],
                         row(ffn_norm_w))

    act = _gate_up(h_ffn, w_gate_up.astype(F32), ffn_conv_w.astype(F32), row(ffn_conv_b))
    x = _down(act, w_down.astype(F32), x)

    return _ple(x, p, row(ple_norm_w), w_ple_gate.astype(BF16), w_ple_proj.astype(BF16),
                row(final_norm_w), final)


def kernel(x, p, mix_norm_w, w_in, fox_forget_bias, ssd_conv_w, ssd_conv_b, ssd_dt_bias,
           ssd_A_log, ssd_D, ssd_norm_w, w_out, ffn_norm_w, w_gate_up, ffn_conv_w,
           ffn_conv_b, w_down, ple_norm_w, w_ple_gate, w_ple_proj, final_norm_w):
    bsz, s, d = x.shape
    depth = p.shape[0]
    outs = []
    for b in range(bsz):
        xb = x[b]
        for i in range(depth):
            xb = _layer(xb, p[i, b], mix_norm_w[i], w_in[i], fox_forget_bias[i], ssd_conv_w[i],
                        ssd_conv_b[i], ssd_dt_bias[i], ssd_A_log[i], ssd_D[i], ssd_norm_w[i],
                        w_out[i], ffn_norm_w[i], w_gate_up[i], ffn_conv_w[i], ffn_conv_b[i],
                        w_down[i], ple_norm_w[i], w_ple_gate[i], w_ple_proj[i], final_norm_w,
                        final=(i == depth - 1))
        outs.append(xb)
    return outs[0][None] if bsz == 1 else jnp.stack(outs, axis=0)
```

```python
import functools

import numpy as np
import jax
import jax.numpy as jnp
from jax import lax
from jax.experimental import pallas as pl
from jax.experimental.pallas import tpu as pltpu

F32 = jnp.float32
BF16 = jnp.bfloat16

D_MODEL = 2048
SEQ = 8192
D_PLE = 256
HEAD_DIM = 64
FOX_WIDTH = 1024
FOX_HEADS = 16
SSD_WIDTH = 1024
SSD_HEADS = 16
SSD_GROUPS = 2
SSD_STATE = 128
SSD_CONV = 4
D_FF = 5632
FFN_CONV = 3
EPS = 1e-6

_OFF_Q = 0
_OFF_F = 3 * FOX_WIDTH
_OFF_XS = _OFF_F + FOX_HEADS
_OFF_B = _OFF_XS + SSD_WIDTH
_OFF_C = _OFF_B + SSD_GROUPS * SSD_STATE
_OFF_Z = _OFF_C + SSD_GROUPS * SSD_STATE
_OFF_DT = _OFF_Z + SSD_WIDTH
_IN_COLS = _OFF_DT + SSD_HEADS

LANES = 128
SUBLANES = 8
VMEM_CAP = 60 * 1024 * 1024

_BC_WIDTH = 2 * SSD_GROUPS * SSD_STATE
_U2_COLS = 2 * SSD_WIDTH + _BC_WIDTH + LANES
_U2_SMALL = 2 * SSD_WIDTH + _BC_WIDTH


def _params(sem, vmem_bytes):
    return pltpu.CompilerParams(dimension_semantics=sem,
                                vmem_limit_bytes=int(min(VMEM_CAP, vmem_bytes)))


def _rms(xf, w):
    ms = jnp.mean(xf * xf, axis=-1, keepdims=True)
    return xf * lax.rsqrt(ms + EPS) * w


def _split3(a):
    hi = a.astype(BF16)
    r1 = a - hi.astype(F32)
    mid = r1.astype(BF16)
    lo = (r1 - mid.astype(F32)).astype(BF16)
    return hi, mid, lo


def _tri_left(tri, a):
    return sum(jnp.dot(tri, t, preferred_element_type=F32) for t in _split3(a))


def _tri_right(a, tri):
    return sum(jnp.dot(t, tri, preferred_element_type=F32) for t in _split3(a))


def _softplus(v):
    return jnp.maximum(v, 0.0) + jnp.log1p(jnp.exp(-jnp.abs(v)))


def _silu(v):
    return v * jax.nn.sigmoid(v)


def _norm_kernel(x_ref, nw_ref, h_ref):
    h_ref[...] = _rms(x_ref[...], nw_ref[...]).astype(h_ref.dtype)


def _norm(x, nw, tm=512):
    m, d = x.shape
    return pl.pallas_call(
        _norm_kernel,
        out_shape=jax.ShapeDtypeStruct((m, d), BF16),
        grid=(m // tm,),
        in_specs=[pl.BlockSpec((tm, d), lambda i: (i, 0)),
                  pl.BlockSpec((1, d), lambda i: (0, 0))],
        out_specs=pl.BlockSpec((tm, d), lambda i: (i, 0)),
        compiler_params=_params(("arbitrary",), 2 * tm * d * 6 + 4 * tm * d * 4),
        name="mix_norm",
    )(x, nw)


def _proj_kernel(h_ref, w_ref, o_ref, w_sc):
    @pl.when(pl.program_id(1) == 0)
    def _():
        w_sc[...] = w_ref[...].astype(BF16)

    o_ref[...] = jnp.dot(h_ref[...], w_sc[...],
                         preferred_element_type=F32).astype(o_ref.dtype)


def _proj(h, w, n_cols, out_dtype, tm, tn, name):
    m, k = h.shape
    ob = jnp.dtype(out_dtype).itemsize
    vmem = 2 * (tm * k * 2 + k * tn * 4 + tm * tn * ob) + k * tn * 2 + 2 * tm * tn * 4
    return pl.pallas_call(
        _proj_kernel,
        out_shape=jax.ShapeDtypeStruct((m, n_cols), out_dtype),
        grid=(n_cols // tn, m // tm),
        in_specs=[pl.BlockSpec((tm, k), lambda j, i: (i, 0)),
                  pl.BlockSpec((k, tn), lambda j, i: (0, j))],
        out_specs=pl.BlockSpec((tm, tn), lambda j, i: (i, j)),
        scratch_shapes=[pltpu.VMEM((k, tn), BF16)],
        compiler_params=_params(("arbitrary", "arbitrary"), vmem),
        name=name,
    )(h, w)


_AUG_ONES = 6
_N_PAIRS = FOX_HEADS // 2


def _aug_selector():
    sel = np.zeros((4 * LANES, _N_PAIRS * LANES), np.float32)
    for pr in range(_N_PAIRS):
        for e in range(2):
            for j in range(3):
                sel[j * LANES + 2 * pr + e, pr * LANES + 3 * e + j] = -1.0
        sel[3 * LANES, pr * LANES + _AUG_ONES:pr * LANES + _AUG_ONES + 3] = 1.0
    return jnp.asarray(sel, BF16)


def _fox_c_kernel(f_ref, b_ref, sel_ref, c_ref, aug_ref, carry_sc):
    @pl.when(pl.program_id(0) == 0)
    def _():
        carry_sc[...] = jnp.zeros_like(carry_sc)

    z = f_ref[...] + b_ref[...]
    lf = jnp.minimum(z, 0.0) - jnp.log1p(jnp.exp(-jnp.abs(z)))
    t = lf.shape[0]
    row = lax.broadcasted_iota(jnp.int32, (t, t), 0)
    col = lax.broadcasted_iota(jnp.int32, (t, t), 1)
    tril = (col <= row).astype(BF16)
    cum = _tri_left(tril, lf) + carry_sc[...]
    c_ref[...] = cum
    carry_sc[...] = cum[t - 1:t, :]
    hi, mid, lo = _split3(cum)
    parts = jnp.concatenate([hi, mid, lo, jnp.ones_like(hi)], axis=1)
    aug_ref[...] = jnp.dot(parts, sel_ref[...], preferred_element_type=F32).astype(BF16)


def _fox_c(u2, b_pad, t=256):
    s = u2.shape[0]
    aug_cols = _N_PAIRS * LANES
    return pl.pallas_call(
        _fox_c_kernel,
        out_shape=(jax.ShapeDtypeStruct((s, LANES), F32),
                   jax.ShapeDtypeStruct((s, aug_cols), BF16)),
        grid=(s // t,),
        in_specs=[pl.BlockSpec((t, LANES), lambda i: (i, _U2_SMALL // LANES)),
                  pl.BlockSpec((1, LANES), lambda i: (0, 0)),
                  pl.BlockSpec((4 * LANES, aug_cols), lambda i: (0, 0))],
        out_specs=(pl.BlockSpec((t, LANES), lambda i: (i, 0)),
                   pl.BlockSpec((t, aug_cols), lambda i: (i, 0))),
        scratch_shapes=[pltpu.VMEM((1, LANES), F32)],
        compiler_params=_params(("arbitrary",), 16 * 1024 * 1024),
        name="fox_cumlogf",
    )(u2, b_pad, _aug_selector())


_V_ROWS = HEAD_DIM + 16


def _fox_attn_kernel(q_ref, k_ref, v_ref, aug_ref, cq_ref, o_ref, vT_sc, acc_sc, s0_sc, s1_sc,
                     *, t):
    qi = pl.program_id(1)
    n_kv = k_ref.shape[0] // t

    @pl.when(qi == 0)
    def _():
        pad_row = lax.broadcasted_iota(jnp.int32, (_V_ROWS - HEAD_DIM, t), 0)
        ones_rows = jnp.where(pad_row == 0, 1.0, 0.0).astype(BF16)

        def transpose_v(c, carry):
            st = pl.multiple_of(c * t, t)
            vT = v_ref[pl.ds(st, t), :].astype(F32).T.astype(BF16)
            for e in range(2):
                vT_sc[e, 0:HEAD_DIM, pl.ds(st, t)] = vT[e * HEAD_DIM:(e + 1) * HEAD_DIM, :]
                vT_sc[e, HEAD_DIM:_V_ROWS, pl.ds(st, t)] = ones_rows
            return carry
        lax.fori_loop(0, n_kv, transpose_v, 0)

    scale = HEAD_DIM ** -0.5
    qT = q_ref[...].astype(F32).T * scale
    row = lax.broadcasted_iota(jnp.int32, (LANES, t), 0)
    tk = s0_sc.shape[1]
    key_idx = lax.broadcasted_iota(jnp.int32, (tk, t), 0)
    qry_idx = lax.broadcasted_iota(jnp.int32, (tk, t), 1)

    ws = []
    for e in range(2):
        qTe = jnp.where(row // HEAD_DIM == e, qT, 0.0)
        cq_hi, cq_mid, cq_lo = (part.astype(F32) for part in _split3(cq_ref[e]))
        aug_rows = jnp.where(
            row // 3 == e, 1.0,
            jnp.where(row == _AUG_ONES, cq_hi,
                      jnp.where(row == _AUG_ONES + 1, cq_mid,
                                jnp.where(row == _AUG_ONES + 2, cq_lo, 0.0))))
        ws.append(jnp.concatenate([qTe, aug_rows], axis=0).astype(BF16))
        acc_sc[e] = jnp.zeros((_V_ROWS, t), F32)

    def scores_into(j, s_sc):
        start = pl.multiple_of(j * tk, tk)
        keys = jnp.concatenate([k_ref[pl.ds(start, tk), :], aug_ref[pl.ds(start, tk), :]],
                               axis=1)
        for e in range(2):
            s_sc[e] = jnp.dot(keys, ws[e], preferred_element_type=F32)

    def softmax_pv(j, s_sc, m_prev, masked):
        start = pl.multiple_of(j * tk, tk)
        if masked:
            visible = key_idx + start <= qry_idx + qi * t
        m_out, probs, alphas = [], [], []
        for e in range(2):
            s = s_sc[e]
            if masked:
                s = jnp.where(visible, s, -jnp.inf)
            m_new = jnp.maximum(m_prev[e], jnp.max(s, axis=0, keepdims=True))
            probs.append(jnp.exp(s - m_new).astype(BF16))
            alphas.append(jnp.exp(m_prev[e] - m_new))
            m_out.append(m_new)
        for e in range(2):
            acc_sc[e] = alphas[e] * acc_sc[e] + jnp.dot(
                vT_sc[e, :, pl.ds(start, tk)], probs[e], preferred_element_type=F32)
        return tuple(m_out)

    def two_blocks(i, m):
        scores_into(2 * i + 1, s1_sc)
        m = softmax_pv(2 * i, s0_sc, m, False)
        scores_into(2 * i + 2, s0_sc)
        return softmax_pv(2 * i + 1, s1_sc, m, False)

    def four_blocks(i, m):
        return two_blocks(2 * i + 1, two_blocks(2 * i, m))

    m_init = jnp.full((1, t), -1e30, F32)
    scores_into(0, s0_sc)
    m = lax.fori_loop(0, qi // 2, four_blocks, (m_init, m_init))
    m = lax.cond(qi % 2 == 1, lambda mm: two_blocks(qi - 1, mm), lambda mm: mm, m)
    scores_into(2 * qi + 1, s1_sc)
    m = softmax_pv(2 * qi, s0_sc, m, True)
    softmax_pv(2 * qi + 1, s1_sc, m, True)
    outs = []
    for e in range(2):
        acc = acc_sc[e]
        outs.append(acc[0:HEAD_DIM, :] / acc[HEAD_DIM:HEAD_DIM + 1, :])
    o_ref[...] = jnp.concatenate(outs, axis=0).T.astype(o_ref.dtype)


def _fox_attn(qkv, aug, cq, t=512):
    s = qkv.shape[0]
    kcol = FOX_WIDTH // LANES
    vmem = 2 * (2 * t * LANES * 2 + 3 * s * LANES * 2 + 2 * t * 4) \
        + 2 * _V_ROWS * (s * 2 + t * 4) + 10 * t * t * 4
    return pl.pallas_call(
        functools.partial(_fox_attn_kernel, t=t),
        out_shape=jax.ShapeDtypeStruct((s, FOX_WIDTH), BF16),
        grid=(_N_PAIRS, s // t),
        in_specs=[pl.BlockSpec((t, LANES), lambda pr, qi: (qi, pr)),
                  pl.BlockSpec((s, LANES), lambda pr, qi: (0, kcol + pr)),
                  pl.BlockSpec((s, LANES), lambda pr, qi: (0, 2 * kcol + pr)),
                  pl.BlockSpec((s, LANES), lambda pr, qi: (0, pr)),
                  pl.BlockSpec((2, 1, t), lambda pr, qi: (pr, 0, qi))],
        out_specs=pl.BlockSpec((t, LANES), lambda pr, qi: (qi, pr)),
        scratch_shapes=[pltpu.VMEM((2, _V_ROWS, s), BF16), pltpu.VMEM((2, _V_ROWS, t), F32),
                        pltpu.VMEM((2, t // 2, t), F32), pltpu.VMEM((2, t // 2, t), F32)],
        compiler_params=_params(("arbitrary", "arbitrary"), vmem),
        name="fox_attention",
    )(qkv, qkv, qkv, aug, cq)


def _ssd_kernel(xs_ref, z_ref, bc_ref, dt_ref, dtT_ref, cwx_ref, cbx_ref, cwbc_ref, cbbc_ref,
                dtb_ref, dtbT_ref, alog_ref, alogT_ref, dskip_ref, nw_ref, y_ref,
                xbuf, bcbuf, state_sc, ybuf, *, L):
    i = pl.program_id(0)
    halo = SUBLANES

    @pl.when(i == 0)
    def _():
        xbuf[0:halo, :] = jnp.zeros((halo, xbuf.shape[1]), F32)
        bcbuf[0:halo, :] = jnp.zeros((halo, bcbuf.shape[1]), F32)
        state_sc[...] = jnp.zeros_like(state_sc)

    @pl.when(i > 0)
    def _():
        xbuf[0:halo, :] = xbuf[L:L + halo, :]
        bcbuf[0:halo, :] = bcbuf[L:L + halo, :]

    xbuf[halo:halo + L, :] = xs_ref[...]
    bcbuf[halo:halo + L, :] = bc_ref[...]

    def conv_silu(buf, w_ref, b_ref):
        acc = b_ref[...]
        for k in range(SSD_CONV):
            off = halo - (SSD_CONV - 1) + k
            acc = acc + buf[off:off + L, :] * w_ref[k:k + 1, :]
        return _silu(acc)

    xc = conv_silu(xbuf, cwx_ref, cbx_ref)
    bcc = conv_silu(bcbuf, cwbc_ref, cbbc_ref)

    dt = _softplus(dt_ref[...] + dtb_ref[...])
    dtT = _softplus(dtT_ref[...] + dtbT_ref[...])
    a = dt * (-jnp.exp(alog_ref[...]))
    aT = dtT * (-jnp.exp(alogT_ref[...]))

    row = lax.broadcasted_iota(jnp.int32, (L, L), 0)
    col = lax.broadcasted_iota(jnp.int32, (L, L), 1)
    causal = col <= row
    a_cum = _tri_left(causal.astype(BF16), a)
    a_cumT = _tri_right(aT, (row <= col).astype(BF16))
    a_last = a_cum[L - 1:L, :]
    a_lastT = a_cumT[:, L - 1:L]
    exp_acum = jnp.exp(a_cum)
    w_endT = jnp.exp(a_lastT - a_cumT) * dtT
    chunk_decay = jnp.exp(a_last)

    n = SSD_STATE
    heads_per_group = SSD_HEADS // SSD_GROUPS
    for g in range(SSD_GROUPS):
        bg = bcc[:, g * n:(g + 1) * n]
        cg = bcc[:, SSD_GROUPS * n + g * n:SSD_GROUPS * n + (g + 1) * n]
        bgT = bg.T
        cb = lax.dot_general(cg.astype(BF16), bg.astype(BF16), (((1,), (1,)), ((), ())),
                             preferred_element_type=F32)
        for hh in range(heads_per_group):
            h = g * heads_per_group + hh
            cols = slice(h * HEAD_DIM, (h + 1) * HEAD_DIM)
            seg = a_cum[:, h:h + 1] - a_cumT[h:h + 1, :]
            decay = jnp.exp(jnp.where(causal, seg, -jnp.inf))
            mix = cb * decay * dtT[h:h + 1, :]
            xh = xc[:, cols]
            xh_b = xh.astype(BF16)
            st = state_sc[:, cols]
            y = jnp.dot(mix.astype(BF16), xh_b, preferred_element_type=F32)
            y = y + jnp.dot((cg * exp_acum[:, h:h + 1]).astype(BF16), st.astype(BF16),
                            preferred_element_type=F32)
            ybuf[:, cols] = y + dskip_ref[:, cols] * xh
            upd = jnp.dot((bgT * w_endT[h:h + 1, :]).astype(BF16), xh_b,
                          preferred_element_type=F32)
            state_sc[:, cols] = chunk_decay[:, h:h + 1] * st + upd

    y = ybuf[...] * _silu(z_ref[...])
    y_ref[...] = _rms(y, nw_ref[...]).astype(y_ref.dtype)


def _ssd(u2, dt_raw, dt_rawT, cwx, cbx, cwbc, cbbc, dtb, dtbT, alog, alogT, dskip, nw, L=256):
    s = u2.shape[0]
    w = SSD_WIDTH
    full = lambda shape: pl.BlockSpec(shape, lambda i: (0,) * len(shape))
    vmem = 2 * (2 * L * w * 4 + L * _BC_WIDTH * 4 + L * w * 2) + (2 * L + 16) * w * 4 \
        + (L + 8) * _BC_WIDTH * 4 + SSD_STATE * w * 4 + 24 * L * L * 4 + 6 * L * w * 4
    return pl.pallas_call(
        functools.partial(_ssd_kernel, L=L),
        out_shape=jax.ShapeDtypeStruct((s, w), BF16),
        grid=(s // L,),
        in_specs=[pl.BlockSpec((L, w), lambda i: (i, 0)),
                  pl.BlockSpec((L, w), lambda i: (i, 1)),
                  pl.BlockSpec((L, _BC_WIDTH), lambda i: (i, 2 * w // _BC_WIDTH)),
                  pl.BlockSpec((L, SSD_HEADS), lambda i: (i, 0)),
                  pl.BlockSpec((SSD_HEADS, L), lambda i: (0, i)),
                  full((SSD_CONV, w)), full((1, w)),
                  full((SSD_CONV, _BC_WIDTH)), full((1, _BC_WIDTH)),
                  full((1, SSD_HEADS)), full((SSD_HEADS, 1)),
                  full((1, SSD_HEADS)), full((SSD_HEADS, 1)),
                  full((1, w)), full((1, w))],
        out_specs=pl.BlockSpec((L, w), lambda i: (i, 0)),
        scratch_shapes=[pltpu.VMEM((L + SUBLANES, w), F32),
                        pltpu.VMEM((L + SUBLANES, _BC_WIDTH), F32),
                        pltpu.VMEM((SSD_STATE, w), F32),
                        pltpu.VMEM((L, w), F32)],
        compiler_params=_params(("arbitrary",), vmem),
        name="ssd_mixer",
    )(u2, u2, u2, dt_raw, dt_rawT, cwx, cbx, cwbc, cbbc, dtb, dtbT, alog, alogT, dskip, nw)


def _out_proj_kernel(x_ref, ya_ref, yb_ref, wa_ref, wb_ref, nw_ref, o_ref, h_ref):
    o = (x_ref[...]
         + jnp.dot(ya_ref[...], wa_ref[...], preferred_element_type=F32)
         + jnp.dot(yb_ref[...], wb_ref[...], preferred_element_type=F32))
    o_ref[...] = o
    h_ref[...] = _rms(o, nw_ref[...]).astype(h_ref.dtype)


def _out_proj(x, ya, yb, wa, wb, nw, tm=512):
    m, d = x.shape
    ka, kb = ya.shape[1], yb.shape[1]
    vmem = 2 * (2 * tm * d * 4 + tm * d * 2 + tm * (ka + kb) * 2 + (ka + kb) * d * 2) \
        + 4 * tm * d * 4
    return pl.pallas_call(
        _out_proj_kernel,
        out_shape=(jax.ShapeDtypeStruct((m, d), F32), jax.ShapeDtypeStruct((m, d), BF16)),
        grid=(m // tm,),
        in_specs=[pl.BlockSpec((tm, d), lambda i: (i, 0)),
                  pl.BlockSpec((tm, ka), lambda i: (i, 0)),
                  pl.BlockSpec((tm, kb), lambda i: (i, 0)),
                  pl.BlockSpec((ka, d), lambda i: (0, 0)),
                  pl.BlockSpec((kb, d), lambda i: (0, 0)),
                  pl.BlockSpec((1, d), lambda i: (0, 0))],
        out_specs=(pl.BlockSpec((tm, d), lambda i: (i, 0)),
                   pl.BlockSpec((tm, d), lambda i: (i, 0))),
        compiler_params=_params(("arbitrary",), vmem),
        name="mix_out_proj",
    )(x, ya, yb, wa, wb, nw)


def _gate_up_kernel(h_ref, wg_ref, wu_ref, cw_ref, cb_ref, o_ref, wg_sc, wu_sc, tail_sc, conv_sc):
    halo = SUBLANES
    tm = conv_sc.shape[0]

    @pl.when(pl.program_id(1) == 0)
    def _():
        wg_sc[...] = wg_ref[...].astype(BF16)
        wu_sc[...] = wu_ref[...].astype(BF16)
        tail_sc[...] = jnp.zeros_like(tail_sc)

    h = h_ref[...]
    g = jnp.dot(h, wg_sc[...], preferred_element_type=F32)
    up = jnp.dot(h, wu_sc[...], preferred_element_type=F32)

    prev = tail_sc[...]
    tail_sc[...] = g[tm - halo:tm, :]

    def conv(cur, back1, back2):
        return (cb_ref[...] + cur * cw_ref[2:3, :] + back1 * cw_ref[1:2, :]
                + back2 * cw_ref[0:1, :])

    conv_sc[...] = conv(g, pltpu.roll(g, 1, axis=0), pltpu.roll(g, 2, axis=0))
    top = g[0:halo, :]
    r = lax.broadcasted_iota(jnp.int32, top.shape, 0)
    back1 = jnp.where(r < 1, pltpu.roll(prev, 1, axis=0), pltpu.roll(top, 1, axis=0))
    back2 = jnp.where(r < 2, pltpu.roll(prev, 2, axis=0), pltpu.roll(top, 2, axis=0))
    conv_sc[0:halo, :] = conv(top, back1, back2)
    o_ref[...] = (_silu(conv_sc[...]) * up).astype(o_ref.dtype)


def _gate_up(h, w_gu, cw, cb, tm=512, tn=512):
    m, d = h.shape
    nj = D_FF // tn
    vmem = 2 * (tm * d * 2 + 2 * d * tn * 4 + tm * tn * 2) + 2 * d * tn * 2 \
        + 10 * tm * tn * 4
    return pl.pallas_call(
        _gate_up_kernel,
        out_shape=jax.ShapeDtypeStruct((m, D_FF), BF16),
        grid=(nj, m // tm),
        in_specs=[pl.BlockSpec((tm, d), lambda j, i: (i, 0)),
                  pl.BlockSpec((d, tn), lambda j, i: (0, j)),
                  pl.BlockSpec((d, tn), lambda j, i: (0, j + nj)),
                  pl.BlockSpec((FFN_CONV, tn), lambda j, i: (0, j)),
                  pl.BlockSpec((1, tn), lambda j, i: (0, j))],
        out_specs=pl.BlockSpec((tm, tn), lambda j, i: (i, j)),
        scratch_shapes=[pltpu.VMEM((d, tn), BF16), pltpu.VMEM((d, tn), BF16),
                        pltpu.VMEM((SUBLANES, tn), F32), pltpu.VMEM((tm, tn), F32)],
        compiler_params=_params(("arbitrary", "arbitrary"), vmem),
        name="ffn_gate_up",
    )(h, w_gu, w_gu, cw, cb)


def _down_kernel(a_ref, w_ref, x_ref, o_ref, w_sc):
    @pl.when(pl.program_id(1) == 0)
    def _():
        w_sc[...] = w_ref[...].astype(BF16)

    o_ref[...] = x_ref[...] + jnp.dot(a_ref[...], w_sc[...], preferred_element_type=F32)


def _down(act, w, x, tm=512, tn=512):
    m, d = x.shape
    kk = act.shape[1]
    vmem = 2 * (tm * kk * 2 + kk * tn * 4 + 2 * tm * tn * 4) + kk * tn * 2 + 2 * tm * tn * 4
    return pl.pallas_call(
        _down_kernel,
        out_shape=jax.ShapeDtypeStruct((m, d), F32),
        grid=(d // tn, m // tm),
        in_specs=[pl.BlockSpec((tm, kk), lambda j, i: (i, 0)),
                  pl.BlockSpec((kk, tn), lambda j, i: (0, j)),
                  pl.BlockSpec((tm, tn), lambda j, i: (i, j))],
        out_specs=pl.BlockSpec((tm, tn), lambda j, i: (i, j)),
        scratch_shapes=[pltpu.VMEM((kk, tn), BF16)],
        compiler_params=_params(("arbitrary", "arbitrary"), vmem),
        name="ffn_down",
    )(act, w, x)


def _ple_kernel(x_ref, p_ref, nw_ref, wg_ref, wp_ref, fw_ref, o_ref, *, final):
    x = x_ref[...]
    h = _rms(x, nw_ref[...]).astype(BF16)
    gate = jax.nn.sigmoid(jnp.dot(h, wg_ref[...], preferred_element_type=F32))
    proj = jnp.dot(p_ref[...].astype(BF16), wp_ref[...], preferred_element_type=F32)
    x = x + gate * proj
    if final:
        x = _rms(x, fw_ref[...])
    o_ref[...] = x


def _ple(x, p, nw, wg, wp, fw, final, tm=256):
    m, d = x.shape
    dp = p.shape[1]
    vmem = 2 * (2 * tm * d * 4 + tm * dp * 4 + d * d * 2 + dp * d * 2) + 6 * tm * d * 4
    return pl.pallas_call(
        functools.partial(_ple_kernel, final=final),
        out_shape=jax.ShapeDtypeStruct((m, d), F32),
        grid=(m // tm,),
        in_specs=[pl.BlockSpec((tm, d), lambda i: (i, 0)),
                  pl.BlockSpec((tm, dp), lambda i: (i, 0)),
                  pl.BlockSpec((1, d), lambda i: (0, 0)),
                  pl.BlockSpec((d, d), lambda i: (0, 0)),
                  pl.BlockSpec((dp, d), lambda i: (0, 0)),
                  pl.BlockSpec((1, d), lambda i: (0, 0))],
        out_specs=pl.BlockSpec((tm, d), lambda i: (i, 0)),
        compiler_params=_params(("arbitrary",), vmem),
        name="ple_gate",
    )(x, p, nw, wg, wp, fw)


def _layer(x, p, mix_norm_w, w_in, fox_forget_bias, ssd_conv_w, ssd_conv_b, ssd_dt_bias,
           ssd_A_log, ssd_D, ssd_norm_w, w_out, ffn_norm_w, w_gate_up, ffn_conv_w,
           ffn_conv_b, w_down, ple_norm_w, w_ple_gate, w_ple_proj, final_norm_w, final):
    s = x.shape[0]
    row = lambda v: v.reshape(1, -1).astype(F32)

    w_u2 = jnp.concatenate(
        [w_in[:, _OFF_XS:_OFF_B], w_in[:, _OFF_Z:_OFF_DT], w_in[:, _OFF_B:_OFF_Z],
         w_in[:, _OFF_F:_OFF_XS], w_in[:, _OFF_DT:_IN_COLS],
         jnp.zeros((D_MODEL, LANES - FOX_HEADS - SSD_HEADS), w_in.dtype)], axis=1).astype(F32)

    h_mix = _norm(x, row(mix_norm_w))
    qkv = _proj(h_mix, w_in.astype(F32), _OFF_F, BF16, tm=1024, tn=512, name="in_proj_qkv")
    u2 = _proj(h_mix, w_u2, _U2_COLS, F32, tm=1024, tn=_U2_COLS // 3, name="in_proj_ssd")

    b_pad = jnp.zeros((1, LANES), F32).at[0, :FOX_HEADS].set(fox_forget_bias.astype(F32))
    c, aug = _fox_c(u2, b_pad)
    cT = c[:, :FOX_HEADS].T
    y_fox = _fox_attn(qkv, aug, cT[:, None, :])

    dt_raw = u2[:, _U2_SMALL + FOX_HEADS:_U2_SMALL + FOX_HEADS + SSD_HEADS]
    col = lambda v: v.reshape(-1, 1).astype(F32)
    y_ssd = _ssd(u2, dt_raw, dt_raw.T,
                 ssd_conv_w[:, :SSD_WIDTH].astype(F32), row(ssd_conv_b[:SSD_WIDTH]),
                 ssd_conv_w[:, SSD_WIDTH:].astype(F32), row(ssd_conv_b[SSD_WIDTH:]),
                 row(ssd_dt_bias), col(ssd_dt_bias), row(ssd_A_log), col(ssd_A_log),
                 row(jnp.repeat(ssd_D, HEAD_DIM)), row(ssd_norm_w))

    w_out_b = w_out.astype(BF16)
    x, h_ffn = _out_proj(x, y_fox, y_ssd, w_out_b[:FOX_WIDTH], w_out_b[FOX_WIDTH:],
                         row(ffn_norm_w))

    act = _gate_up(h_ffn, w_gate_up.astype(F32), ffn_conv_w.astype(F32), row(ffn_conv_b))
    x = _down(act, w_down.astype(F32), x)

    return _ple(x, p, row(ple_norm_w), w_ple_gate.astype(BF16), w_ple_proj.astype(BF16),
                row(final_norm_w), final)


def kernel(x, p, mix_norm_w, w_in, fox_forget_bias, ssd_conv_w, ssd_conv_b, ssd_dt_bias,
           ssd_A_log, ssd_D, ssd_norm_w, w_out, ffn_norm_w, w_gate_up, ffn_conv_w,
           ffn_conv_b, w_down, ple_norm_w, w_ple_gate, w_ple_proj, final_norm_w):
    bsz, s, d = x.shape
    depth = p.shape[0]
    outs = []
    for b in range(bsz):
        xb = x[b]
        for i in range(depth):
            xb = _layer(xb, p[i, b], mix_norm_w[i], w_in[i], fox_forget_bias[i], ssd_conv_w[i],
                        ssd_conv_b[i], ssd_dt_bias[i], ssd_A_log[i], ssd_D[i], ssd_norm_w[i],
                        w_out[i], ffn_norm_w[i], w_gate_up[i], ffn_conv_w[i], ffn_conv_b[i],
                        w_down[i], ple_norm_w[i], w_ple_gate[i], w_ple_proj[i], final_norm_w,
                        final=(i == depth - 1))
        outs.append(xb)
    return outs[0][None] if bsz == 1 else jnp.stack(outs, axis=0)
```

```python
import functools

import numpy as np
import jax
import jax.numpy as jnp
from jax import lax
from jax.experimental import pallas as pl
from jax.experimental.pallas import tpu as pltpu

F32 = jnp.float32
BF16 = jnp.bfloat16

D_MODEL = 2048
SEQ = 8192
D_PLE = 256
HEAD_DIM = 64
FOX_WIDTH = 1024
FOX_HEADS = 16
SSD_WIDTH = 1024
SSD_HEADS = 16
SSD_GROUPS = 2
SSD_STATE = 128
SSD_CONV = 4
D_FF = 5632
FFN_CONV = 3
EPS = 1e-6

_OFF_Q = 0
_OFF_F = 3 * FOX_WIDTH
_OFF_XS = _OFF_F + FOX_HEADS
_OFF_B = _OFF_XS + SSD_WIDTH
_OFF_C = _OFF_B + SSD_GROUPS * SSD_STATE
_OFF_Z = _OFF_C + SSD_GROUPS * SSD_STATE
_OFF_DT = _OFF_Z + SSD_WIDTH
_IN_COLS = _OFF_DT + SSD_HEADS

LANES = 128
SUBLANES = 8
VMEM_CAP = 60 * 1024 * 1024

_BC_WIDTH = 2 * SSD_GROUPS * SSD_STATE
_U2_COLS = 2 * SSD_WIDTH + _BC_WIDTH + LANES
_U2_SMALL = 2 * SSD_WIDTH + _BC_WIDTH


def _params(sem, vmem_bytes):
    return pltpu.CompilerParams(dimension_semantics=sem,
                                vmem_limit_bytes=int(min(VMEM_CAP, vmem_bytes)))


def _rms(xf, w):
    ms = jnp.mean(xf * xf, axis=-1, keepdims=True)
    return xf * lax.rsqrt(ms + EPS) * w


def _split3(a):
    hi = a.astype(BF16)
    r1 = a - hi.astype(F32)
    mid = r1.astype(BF16)
    lo = (r1 - mid.astype(F32)).astype(BF16)
    return hi, mid, lo


def _tri_left(tri, a):
    return sum(jnp.dot(tri, t, preferred_element_type=F32) for t in _split3(a))


def _tri_right(a, tri):
    return sum(jnp.dot(t, tri, preferred_element_type=F32) for t in _split3(a))


def _softplus(v):
    return jnp.maximum(v, 0.0) + jnp.log1p(jnp.exp(-jnp.abs(v)))


def _silu(v):
    return v * jax.nn.sigmoid(v)


def _norm_kernel(x_ref, nw_ref, h_ref):
    h_ref[...] = _rms(x_ref[...], nw_ref[...]).astype(h_ref.dtype)


def _norm(x, nw, tm=512):
    m, d = x.shape
    return pl.pallas_call(
        _norm_kernel,
        out_shape=jax.ShapeDtypeStruct((m, d), BF16),
        grid=(m // tm,),
        in_specs=[pl.BlockSpec((tm, d), lambda i: (i, 0)),
                  pl.BlockSpec((1, d), lambda i: (0, 0))],
        out_specs=pl.BlockSpec((tm, d), lambda i: (i, 0)),
        compiler_params=_params(("arbitrary",), 2 * tm * d * 6 + 4 * tm * d * 4),
        name="mix_norm",
    )(x, nw)


def _proj_kernel(h_ref, wT_ref, o_ref, w_sc):
    @pl.when(pl.program_id(1) == 0)
    def _():
        w_sc[...] = wT_ref[...].astype(BF16)

    o_ref[...] = lax.dot_general(h_ref[...], w_sc[...], (((1,), (1,)), ((), ())),
                                 preferred_element_type=F32).astype(o_ref.dtype)


def _proj(h, wT, n_cols, out_dtype, tm, tn, name):
    m, k = h.shape
    ob = jnp.dtype(out_dtype).itemsize
    vmem = 2 * (tm * k * 2 + k * tn * 4 + tm * tn * ob) + k * tn * 2 + 2 * tm * tn * 4
    return pl.pallas_call(
        _proj_kernel,
        out_shape=jax.ShapeDtypeStruct((m, n_cols), out_dtype),
        grid=(n_cols // tn, m // tm),
        in_specs=[pl.BlockSpec((tm, k), lambda j, i: (i, 0)),
                  pl.BlockSpec((tn, k), lambda j, i: (j, 0))],
        out_specs=pl.BlockSpec((tm, tn), lambda j, i: (i, j)),
        scratch_shapes=[pltpu.VMEM((tn, k), BF16)],
        compiler_params=_params(("arbitrary", "arbitrary"), vmem),
        name=name,
    )(h, wT)


_AUG_ONES = 6
_N_PAIRS = FOX_HEADS // 2


def _aug_selector():
    sel = np.zeros((4 * LANES, _N_PAIRS * LANES), np.float32)
    for pr in range(_N_PAIRS):
        for e in range(2):
            for j in range(3):
                sel[j * LANES + 2 * pr + e, pr * LANES + 3 * e + j] = -1.0
        sel[3 * LANES, pr * LANES + _AUG_ONES:pr * LANES + _AUG_ONES + 3] = 1.0
    return jnp.asarray(sel, BF16)


def _fox_c_kernel(f_ref, b_ref, sel_ref, c_ref, aug_ref, carry_sc):
    @pl.when(pl.program_id(0) == 0)
    def _():
        carry_sc[...] = jnp.zeros_like(carry_sc)

    z = f_ref[...] + b_ref[...]
    lf = jnp.minimum(z, 0.0) - jnp.log1p(jnp.exp(-jnp.abs(z)))
    t = lf.shape[0]
    row = lax.broadcasted_iota(jnp.int32, (t, t), 0)
    col = lax.broadcasted_iota(jnp.int32, (t, t), 1)
    tril = (col <= row).astype(BF16)
    cum = _tri_left(tril, lf) + carry_sc[...]
    c_ref[...] = cum
    carry_sc[...] = cum[t - 1:t, :]
    hi, mid, lo = _split3(cum)
    parts = jnp.concatenate([hi, mid, lo, jnp.ones_like(hi)], axis=1)
    aug_ref[...] = jnp.dot(parts, sel_ref[...], preferred_element_type=F32).astype(BF16)


def _fox_c(u2, b_pad, t=256):
    s = u2.shape[0]
    aug_cols = _N_PAIRS * LANES
    return pl.pallas_call(
        _fox_c_kernel,
        out_shape=(jax.ShapeDtypeStruct((s, LANES), F32),
                   jax.ShapeDtypeStruct((s, aug_cols), BF16)),
        grid=(s // t,),
        in_specs=[pl.BlockSpec((t, LANES), lambda i: (i, _U2_SMALL // LANES)),
                  pl.BlockSpec((1, LANES), lambda i: (0, 0)),
                  pl.BlockSpec((4 * LANES, aug_cols), lambda i: (0, 0))],
        out_specs=(pl.BlockSpec((t, LANES), lambda i: (i, 0)),
                   pl.BlockSpec((t, aug_cols), lambda i: (i, 0))),
        scratch_shapes=[pltpu.VMEM((1, LANES), F32)],
        compiler_params=_params(("arbitrary",), 16 * 1024 * 1024),
        name="fox_cumlogf",
    )(u2, b_pad, _aug_selector())


_V_ROWS = HEAD_DIM + 16


def _fox_attn_kernel(q_ref, k_ref, v_ref, aug_ref, cq_ref, o_ref, vT_sc, acc_sc, s0_sc, s1_sc,
                     *, t):
    qi = pl.program_id(1)
    n_kv = k_ref.shape[0] // t

    @pl.when(qi == 0)
    def _():
        pad_row = lax.broadcasted_iota(jnp.int32, (_V_ROWS - HEAD_DIM, t), 0)
        ones_rows = jnp.where(pad_row == 0, 1.0, 0.0).astype(BF16)

        def transpose_v(c, carry):
            st = pl.multiple_of(c * t, t)
            vT = v_ref[pl.ds(st, t), :].astype(F32).T.astype(BF16)
            for e in range(2):
                vT_sc[e, 0:HEAD_DIM, pl.ds(st, t)] = vT[e * HEAD_DIM:(e + 1) * HEAD_DIM, :]
                vT_sc[e, HEAD_DIM:_V_ROWS, pl.ds(st, t)] = ones_rows
            return carry
        lax.fori_loop(0, n_kv, transpose_v, 0)

    scale = HEAD_DIM ** -0.5
    qT = q_ref[...].astype(F32).T * scale
    row = lax.broadcasted_iota(jnp.int32, (LANES, t), 0)
    tk = s0_sc.shape[1]
    key_idx = lax.broadcasted_iota(jnp.int32, (tk, t), 0)
    qry_idx = lax.broadcasted_iota(jnp.int32, (tk, t), 1)

    ws = []
    for e in range(2):
        qTe = jnp.where(row // HEAD_DIM == e, qT, 0.0)
        cq_hi, cq_mid, cq_lo = (part.astype(F32) for part in _split3(cq_ref[e]))
        aug_rows = jnp.where(
            row // 3 == e, 1.0,
            jnp.where(row == _AUG_ONES, cq_hi,
                      jnp.where(row == _AUG_ONES + 1, cq_mid,
                                jnp.where(row == _AUG_ONES + 2, cq_lo, 0.0))))
        ws.append(jnp.concatenate([qTe, aug_rows], axis=0).astype(BF16))
        acc_sc[e] = jnp.zeros((_V_ROWS, t), F32)

    def scores_into(j, s_sc):
        start = pl.multiple_of(j * tk, tk)
        keys = jnp.concatenate([k_ref[pl.ds(start, tk), :], aug_ref[pl.ds(start, tk), :]],
                               axis=1)
        for e in range(2):
            s_sc[e] = jnp.dot(keys, ws[e], preferred_element_type=F32)

    def softmax_pv(j, s_sc, m_prev, masked):
        start = pl.multiple_of(j * tk, tk)
        if masked:
            visible = key_idx + start <= qry_idx + qi * t
        m_out, probs, alphas = [], [], []
        for e in range(2):
            s = s_sc[e]
            if masked:
                s = jnp.where(visible, s, -jnp.inf)
            m_new = jnp.maximum(m_prev[e], jnp.max(s, axis=0, keepdims=True))
            probs.append(jnp.exp(s - m_new).astype(BF16))
            alphas.append(jnp.exp(m_prev[e] - m_new))
            m_out.append(m_new)
        for e in range(2):
            acc_sc[e] = alphas[e] * acc_sc[e] + jnp.dot(
                vT_sc[e, :, pl.ds(start, tk)], probs[e], preferred_element_type=F32)
        return tuple(m_out)

    def two_blocks(i, m):
        scores_into(2 * i + 1, s1_sc)
        m = softmax_pv(2 * i, s0_sc, m, False)
        scores_into(2 * i + 2, s0_sc)
        return softmax_pv(2 * i + 1, s1_sc, m, False)

    def four_blocks(i, m):
        return two_blocks(2 * i + 1, two_blocks(2 * i, m))

    m_init = jnp.full((1, t), -1e30, F32)
    scores_into(0, s0_sc)
    m = lax.fori_loop(0, qi // 2, four_blocks, (m_init, m_init))
    m = lax.cond(qi % 2 == 1, lambda mm: two_blocks(qi - 1, mm), lambda mm: mm, m)
    scores_into(2 * qi + 1, s1_sc)
    m = softmax_pv(2 * qi, s0_sc, m, True)
    softmax_pv(2 * qi + 1, s1_sc, m, True)
    outs = []
    for e in range(2):
        acc = acc_sc[e]
        outs.append(acc[0:HEAD_DIM, :] / acc[HEAD_DIM:HEAD_DIM + 1, :])
    o_ref[...] = jnp.concatenate(outs, axis=0).T.astype(o_ref.dtype)


def _fox_attn(qkv, aug, cq, t=512):
    s = qkv.shape[0]
    kcol = FOX_WIDTH // LANES
    vmem = 2 * (2 * t * LANES * 2 + 3 * s * LANES * 2 + 2 * t * 4) \
        + 2 * _V_ROWS * (s * 2 + t * 4) + 10 * t * t * 4
    return pl.pallas_call(
        functools.partial(_fox_attn_kernel, t=t),
        out_shape=jax.ShapeDtypeStruct((s, FOX_WIDTH), BF16),
        grid=(_N_PAIRS, s // t),
        in_specs=[pl.BlockSpec((t, LANES), lambda pr, qi: (qi, pr)),
                  pl.BlockSpec((s, LANES), lambda pr, qi: (0, kcol + pr)),
                  pl.BlockSpec((s, LANES), lambda pr, qi: (0, 2 * kcol + pr)),
                  pl.BlockSpec((s, LANES), lambda pr, qi: (0, pr)),
                  pl.BlockSpec((2, 1, t), lambda pr, qi: (pr, 0, qi))],
        out_specs=pl.BlockSpec((t, LANES), lambda pr, qi: (qi, pr)),
        scratch_shapes=[pltpu.VMEM((2, _V_ROWS, s), BF16), pltpu.VMEM((2, _V_ROWS, t), F32),
                        pltpu.VMEM((2, t // 2, t), F32), pltpu.VMEM((2, t // 2, t), F32)],
        compiler_params=_params(("arbitrary", "arbitrary"), vmem),
        name="fox_attention",
    )(qkv, qkv, qkv, aug, cq)


def _ssd_kernel(xs_ref, z_ref, bc_ref, dt_ref, dtT_ref, cwx_ref, cbx_ref, cwbc_ref, cbbc_ref,
                dtb_ref, dtbT_ref, alog_ref, alogT_ref, dskip_ref, nw_ref, y_ref,
                xbuf, bcbuf, xtail_sc, bctail_sc, state_sc, ybuf, *, L):
    halo = SUBLANES

    @pl.when(pl.program_id(0) == 0)
    def _():
        xtail_sc[...] = jnp.zeros_like(xtail_sc)
        bctail_sc[...] = jnp.zeros_like(bctail_sc)
        state_sc[...] = jnp.zeros_like(state_sc)

    def conv_silu(u_ref, tail_sc, buf, w_ref, b_ref):
        cur = u_ref[...]
        prev = tail_sc[...]
        tail_sc[...] = cur[L - halo:L, :]
        last = SSD_CONV - 1

        def taps(c, backs):
            acc = b_ref[...] + c * w_ref[last:last + 1, :]
            for k in range(1, SSD_CONV):
                acc = acc + backs[k - 1] * w_ref[last - k:last - k + 1, :]
            return acc

        buf[...] = taps(cur, [pltpu.roll(cur, k, axis=0) for k in range(1, SSD_CONV)])
        top = cur[0:halo, :]
        r = lax.broadcasted_iota(jnp.int32, top.shape, 0)
        buf[0:halo, :] = taps(top, [
            jnp.where(r < k, pltpu.roll(prev, k, axis=0), pltpu.roll(top, k, axis=0))
            for k in range(1, SSD_CONV)])
        return _silu(buf[...])

    xc = conv_silu(xs_ref, xtail_sc, xbuf, cwx_ref, cbx_ref)
    bcc = conv_silu(bc_ref, bctail_sc, bcbuf, cwbc_ref, cbbc_ref)

    dt = _softplus(dt_ref[...] + dtb_ref[...])
    dtT = _softplus(dtT_ref[...] + dtbT_ref[...])
    a = dt * (-jnp.exp(alog_ref[...]))
    aT = dtT * (-jnp.exp(alogT_ref[...]))

    row = lax.broadcasted_iota(jnp.int32, (L, L), 0)
    col = lax.broadcasted_iota(jnp.int32, (L, L), 1)
    causal = col <= row
    a_cum = _tri_left(causal.astype(BF16), a)
    a_cumT = _tri_right(aT, (row <= col).astype(BF16))
    a_last = a_cum[L - 1:L, :]
    a_lastT = a_cumT[:, L - 1:L]
    exp_acum = jnp.exp(a_cum)
    w_endT = jnp.exp(a_lastT - a_cumT) * dtT
    chunk_decay = jnp.exp(a_last)

    n = SSD_STATE
    heads_per_group = SSD_HEADS // SSD_GROUPS
    for g in range(SSD_GROUPS):
        bg = bcc[:, g * n:(g + 1) * n]
        cg = bcc[:, SSD_GROUPS * n + g * n:SSD_GROUPS * n + (g + 1) * n]
        bgT = bg.T
        cb = lax.dot_general(cg.astype(BF16), bg.astype(BF16), (((1,), (1,)), ((), ())),
                             preferred_element_type=F32)
        for hh in range(heads_per_group):
            h = g * heads_per_group + hh
            cols = slice(h * HEAD_DIM, (h + 1) * HEAD_DIM)
            seg = a_cum[:, h:h + 1] - a_cumT[h:h + 1, :]
            decay = jnp.exp(jnp.where(causal, seg, -jnp.inf))
            mix = cb * decay * dtT[h:h + 1, :]
            xh = xc[:, cols]
            xh_b = xh.astype(BF16)
            st = state_sc[:, cols]
            y = jnp.dot(mix.astype(BF16), xh_b, preferred_element_type=F32)
            y = y + jnp.dot((cg * exp_acum[:, h:h + 1]).astype(BF16), st.astype(BF16),
                            preferred_element_type=F32)
            ybuf[:, cols] = y + dskip_ref[:, cols] * xh
            upd = jnp.dot((bgT * w_endT[h:h + 1, :]).astype(BF16), xh_b,
                          preferred_element_type=F32)
            state_sc[:, cols] = chunk_decay[:, h:h + 1] * st + upd

    y = ybuf[...] * _silu(z_ref[...])
    y_ref[...] = _rms(y, nw_ref[...]).astype(y_ref.dtype)


def _ssd(u2, dt_raw, dt_rawT, cwx, cbx, cwbc, cbbc, dtb, dtbT, alog, alogT, dskip, nw, L=256):
    s = u2.shape[0]
    w = SSD_WIDTH
    full = lambda shape: pl.BlockSpec(shape, lambda i: (0,) * len(shape))
    vmem = 2 * (2 * L * w * 4 + L * _BC_WIDTH * 4 + L * w * 2) + (2 * L + 16) * w * 4 \
        + (L + 8) * _BC_WIDTH * 4 + SSD_STATE * w * 4 + 24 * L * L * 4 + 6 * L * w * 4
    return pl.pallas_call(
        functools.partial(_ssd_kernel, L=L),
        out_shape=jax.ShapeDtypeStruct((s, w), BF16),
        grid=(s // L,),
        in_specs=[pl.BlockSpec((L, w), lambda i: (i, 0)),
                  pl.BlockSpec((L, w), lambda i: (i, 1)),
                  pl.BlockSpec((L, _BC_WIDTH), lambda i: (i, 2 * w // _BC_WIDTH)),
                  pl.BlockSpec((L, SSD_HEADS), lambda i: (i, 0)),
                  pl.BlockSpec((SSD_HEADS, L), lambda i: (0, i)),
                  full((SSD_CONV, w)), full((1, w)),
                  full((SSD_CONV, _BC_WIDTH)), full((1, _BC_WIDTH)),
                  full((1, SSD_HEADS)), full((SSD_HEADS, 1)),
                  full((1, SSD_HEADS)), full((SSD_HEADS, 1)),
                  full((1, w)), full((1, w))],
        out_specs=pl.BlockSpec((L, w), lambda i: (i, 0)),
        scratch_shapes=[pltpu.VMEM((L, w), F32),
                        pltpu.VMEM((L, _BC_WIDTH), F32),
                        pltpu.VMEM((SUBLANES, w), F32),
                        pltpu.VMEM((SUBLANES, _BC_WIDTH), F32),
                        pltpu.VMEM((SSD_STATE, w), F32),
                        pltpu.VMEM((L, w), F32)],
        compiler_params=_params(("arbitrary",), vmem),
        name="ssd_mixer",
    )(u2, u2, u2, dt_raw, dt_rawT, cwx, cbx, cwbc, cbbc, dtb, dtbT, alog, alogT, dskip, nw)


def _out_proj_kernel(x_ref, ya_ref, yb_ref, wa_ref, wb_ref, nw_ref, o_ref, h_ref):
    o = (x_ref[...]
         + jnp.dot(ya_ref[...], wa_ref[...], preferred_element_type=F32)
         + jnp.dot(yb_ref[...], wb_ref[...], preferred_element_type=F32))
    o_ref[...] = o
    h_ref[...] = _rms(o, nw_ref[...]).astype(h_ref.dtype)


def _out_proj(x, ya, yb, wa, wb, nw, tm=512):
    m, d = x.shape
    ka, kb = ya.shape[1], yb.shape[1]
    vmem = 2 * (2 * tm * d * 4 + tm * d * 2 + tm * (ka + kb) * 2 + (ka + kb) * d * 2) \
        + 4 * tm * d * 4
    return pl.pallas_call(
        _out_proj_kernel,
        out_shape=(jax.ShapeDtypeStruct((m, d), F32), jax.ShapeDtypeStruct((m, d), BF16)),
        grid=(m // tm,),
        in_specs=[pl.BlockSpec((tm, d), lambda i: (i, 0)),
                  pl.BlockSpec((tm, ka), lambda i: (i, 0)),
                  pl.BlockSpec((tm, kb), lambda i: (i, 0)),
                  pl.BlockSpec((ka, d), lambda i: (0, 0)),
                  pl.BlockSpec((kb, d), lambda i: (0, 0)),
                  pl.BlockSpec((1, d), lambda i: (0, 0))],
        out_specs=(pl.BlockSpec((tm, d), lambda i: (i, 0)),
                   pl.BlockSpec((tm, d), lambda i: (i, 0))),
        compiler_params=_params(("arbitrary",), vmem),
        name="mix_out_proj",
    )(x, ya, yb, wa, wb, nw)


def _gate_up_kernel(h_ref, wg_ref, wu_ref, cw_ref, cb_ref, o_ref, wg_sc, wu_sc, tail_sc, conv_sc):
    halo = SUBLANES
    tm = conv_sc.shape[0]

    @pl.when(pl.program_id(1) == 0)
    def _():
        wg_sc[...] = wg_ref[...].astype(BF16)
        wu_sc[...] = wu_ref[...].astype(BF16)
        tail_sc[...] = jnp.zeros_like(tail_sc)

    h = h_ref[...]
    g = jnp.dot(h, wg_sc[...], preferred_element_type=F32)
    up = jnp.dot(h, wu_sc[...], preferred_element_type=F32)

    prev = tail_sc[...]
    tail_sc[...] = g[tm - halo:tm, :]

    def conv(cur, back1, back2):
        return (cb_ref[...] + cur * cw_ref[2:3, :] + back1 * cw_ref[1:2, :]
                + back2 * cw_ref[0:1, :])

    conv_sc[...] = conv(g, pltpu.roll(g, 1, axis=0), pltpu.roll(g, 2, axis=0))
    top = g[0:halo, :]
    r = lax.broadcasted_iota(jnp.int32, top.shape, 0)
    back1 = jnp.where(r < 1, pltpu.roll(prev, 1, axis=0), pltpu.roll(top, 1, axis=0))
    back2 = jnp.where(r < 2, pltpu.roll(prev, 2, axis=0), pltpu.roll(top, 2, axis=0))
    conv_sc[0:halo, :] = conv(top, back1, back2)
    o_ref[...] = (_silu(conv_sc[...]) * up).astype(o_ref.dtype)


def _gate_up(h, w_gu, cw, cb, tm=1024, tn=512):
    m, d = h.shape
    nj = D_FF // tn
    vmem = 2 * (tm * d * 2 + 2 * d * tn * 4 + tm * tn * 2) + 2 * d * tn * 2 \
        + 10 * tm * tn * 4
    return pl.pallas_call(
        _gate_up_kernel,
        out_shape=jax.ShapeDtypeStruct((m, D_FF), BF16),
        grid=(nj, m // tm),
        in_specs=[pl.BlockSpec((tm, d), lambda j, i: (i, 0)),
                  pl.BlockSpec((d, tn), lambda j, i: (0, j)),
                  pl.BlockSpec((d, tn), lambda j, i: (0, j + nj)),
                  pl.BlockSpec((FFN_CONV, tn), lambda j, i: (0, j)),
                  pl.BlockSpec((1, tn), lambda j, i: (0, j))],
        out_specs=pl.BlockSpec((tm, tn), lambda j, i: (i, j)),
        scratch_shapes=[pltpu.VMEM((d, tn), BF16), pltpu.VMEM((d, tn), BF16),
                        pltpu.VMEM((SUBLANES, tn), F32), pltpu.VMEM((tm, tn), F32)],
        compiler_params=_params(("arbitrary", "arbitrary"), vmem),
        name="ffn_gate_up",
    )(h, w_gu, w_gu, cw, cb)


def _down_kernel(a_ref, w_ref, x_ref, o_ref, w_sc):
    @pl.when(pl.program_id(1) == 0)
    def _():
        w_sc[...] = w_ref[...].astype(BF16)

    o_ref[...] = x_ref[...] + jnp.dot(a_ref[...], w_sc[...], preferred_element_type=F32)


def _down(act, w, x, tm=512, tn=512):
    m, d = x.shape
    kk = act.shape[1]
    vmem = 2 * (tm * kk * 2 + kk * tn * 4 + 2 * tm * tn * 4) + kk * tn * 2 + 2 * tm * tn * 4
    return pl.pallas_call(
        _down_kernel,
        out_shape=jax.ShapeDtypeStruct((m, d), F32),
        grid=(d // tn, m // tm),
        in_specs=[pl.BlockSpec((tm, kk), lambda j, i: (i, 0)),
                  pl.BlockSpec((kk, tn), lambda j, i: (0, j)),
                  pl.BlockSpec((tm, tn), lambda j, i: (i, j))],
        out_specs=pl.BlockSpec((tm, tn), lambda j, i: (i, j)),
        scratch_shapes=[pltpu.VMEM((kk, tn), BF16)],
        compiler_params=_params(("arbitrary", "arbitrary"), vmem),
        name="ffn_down",
    )(act, w, x)


def _ple_kernel(x_ref, p_ref, nw_ref, wg_ref, wp_ref, fw_ref, o_ref, *, final):
    x = x_ref[...]
    h = _rms(x, nw_ref[...]).astype(BF16)
    gate = jax.nn.sigmoid(jnp.dot(h, wg_ref[...], preferred_element_type=F32))
    proj = jnp.dot(p_ref[...].astype(BF16), wp_ref[...], preferred_element_type=F32)
    x = x + gate * proj
    if final:
        x = _rms(x, fw_ref[...])
    o_ref[...] = x


def _ple(x, p, nw, wg, wp, fw, final, tm=256):
    m, d = x.shape
    dp = p.shape[1]
    vmem = 2 * (2 * tm * d * 4 + tm * dp * 4 + d * d * 2 + dp * d * 2) + 6 * tm * d * 4
    return pl.pallas_call(
        functools.partial(_ple_kernel, final=final),
        out_shape=jax.ShapeDtypeStruct((m, d), F32),
        grid=(m // tm,),
        in_specs=[pl.BlockSpec((tm, d), lambda i: (i, 0)),
                  pl.BlockSpec((tm, dp), lambda i: (i, 0)),
                  pl.BlockSpec((1, d), lambda i: (0, 0)),
                  pl.BlockSpec((d, d), lambda i: (0, 0)),
                  pl.BlockSpec((dp, d), lambda i: (0, 0)),
                  pl.BlockSpec((1, d), lambda i: (0, 0))],
        out_specs=pl.BlockSpec((tm, d), lambda i: (i, 0)),
        compiler_params=_params(("arbitrary",), vmem),
        name="ple_gate",
    )(x, p, nw, wg, wp, fw)


def _layer(x, p, mix_norm_w, w_in, fox_forget_bias, ssd_conv_w, ssd_conv_b, ssd_dt_bias,
           ssd_A_log, ssd_D, ssd_norm_w, w_out, ffn_norm_w, w_gate_up, ffn_conv_w,
           ffn_conv_b, w_down, ple_norm_w, w_ple_gate, w_ple_proj, final_norm_w, final):
    s = x.shape[0]
    row = lambda v: v.reshape(1, -1).astype(F32)

    w_inT = jnp.swapaxes(w_in, 0, 1).astype(F32)
    w_u2T = jnp.concatenate(
        [w_inT[_OFF_XS:_OFF_B], w_inT[_OFF_Z:_OFF_DT], w_inT[_OFF_B:_OFF_Z],
         w_inT[_OFF_F:_OFF_XS], w_inT[_OFF_DT:_IN_COLS],
         jnp.zeros((LANES - FOX_HEADS - SSD_HEADS, D_MODEL), F32)], axis=0)

    h_mix = _norm(x, row(mix_norm_w))
    qkv = _proj(h_mix, w_inT, _OFF_F, BF16, tm=1024, tn=512, name="in_proj_qkv")
    u2 = _proj(h_mix, w_u2T, _U2_COLS, F32, tm=1024, tn=_U2_COLS // 3, name="in_proj_ssd")

    b_pad = jnp.zeros((1, LANES), F32).at[0, :FOX_HEADS].set(fox_forget_bias.astype(F32))
    c, aug = _fox_c(u2, b_pad)
    cT = c[:, :FOX_HEADS].T
    y_fox = _fox_attn(qkv, aug, cT[:, None, :])

    dt_raw = u2[:, _U2_SMALL + FOX_HEADS:_U2_SMALL + FOX_HEADS + SSD_HEADS]
    col = lambda v: v.reshape(-1, 1).astype(F32)
    y_ssd = _ssd(u2, dt_raw, dt_raw.T,
                 ssd_conv_w[:, :SSD_WIDTH].astype(F32), row(ssd_conv_b[:SSD_WIDTH]),
                 ssd_conv_w[:, SSD_WIDTH:].astype(F32), row(ssd_conv_b[SSD_WIDTH:]),
                 row(ssd_dt_bias), col(ssd_dt_bias), row(ssd_A_log), col(ssd_A_log),
                 row(jnp.repeat(ssd_D, HEAD_DIM)), row(ssd_norm_w))

    w_out_b = w_out.astype(BF16)
    x, h_ffn = _out_proj(x, y_fox, y_ssd, w_out_b[:FOX_WIDTH], w_out_b[FOX_WIDTH:],
                         row(ffn_norm_w))

    act = _gate_up(h_ffn, w_gate_up.astype(F32), ffn_conv_w.astype(F32), row(ffn_conv_b))
    x = _down(act, w_down.astype(F32), x)

    return _ple(x, p, row(ple_norm_w), w_ple_gate.astype(BF16), w_ple_proj.astype(BF16),
                row(final_norm_w), final)


def kernel(x, p, mix_norm_w, w_in, fox_forget_bias, ssd_conv_w, ssd_conv_b, ssd_dt_bias,
           ssd_A_log, ssd_D, ssd_norm_w, w_out, ffn_norm_w, w_gate_up, ffn_conv_w,
           ffn_conv_b, w_down, ple_norm_w, w_ple_gate, w_ple_proj, final_norm_w):
    bsz, s, d = x.shape
    depth = p.shape[0]
    outs = []
    for b in range(bsz):
        xb = x[b]
        for i in range(depth):
            xb = _layer(xb, p[i, b], mix_norm_w[i], w_in[i], fox_forget_bias[i], ssd_conv_w[i],
                        ssd_conv_b[i], ssd_dt_bias[i], ssd_A_log[i], ssd_D[i], ssd_norm_w[i],
                        w_out[i], ffn_norm_w[i], w_gate_up[i], ffn_conv_w[i], ffn_conv_b[i],
                        w_down[i], ple_norm_w[i], w_ple_gate[i], w_ple_proj[i], final_norm_w,
                        final=(i == depth - 1))
        outs.append(xb)
    return outs[0][None] if bsz == 1 else jnp.stack(outs, axis=0)
```

```python
import functools

import numpy as np
import jax
import jax.numpy as jnp
from jax import lax
from jax.experimental import pallas as pl
from jax.experimental.pallas import tpu as pltpu

F32 = jnp.float32
BF16 = jnp.bfloat16

D_MODEL = 2048
SEQ = 8192
D_PLE = 256
HEAD_DIM = 64
FOX_WIDTH = 1024
FOX_HEADS = 16
SSD_WIDTH = 1024
SSD_HEADS = 16
SSD_GROUPS = 2
SSD_STATE = 128
SSD_CONV = 4
D_FF = 5632
FFN_CONV = 3
EPS = 1e-6

_OFF_Q = 0
_OFF_F = 3 * FOX_WIDTH
_OFF_XS = _OFF_F + FOX_HEADS
_OFF_B = _OFF_XS + SSD_WIDTH
_OFF_C = _OFF_B + SSD_GROUPS * SSD_STATE
_OFF_Z = _OFF_C + SSD_GROUPS * SSD_STATE
_OFF_DT = _OFF_Z + SSD_WIDTH
_IN_COLS = _OFF_DT + SSD_HEADS

LANES = 128
SUBLANES = 8
VMEM_CAP = 60 * 1024 * 1024

_BC_WIDTH = 2 * SSD_GROUPS * SSD_STATE
_U2_COLS = 2 * SSD_WIDTH + _BC_WIDTH + LANES
_U2_SMALL = 2 * SSD_WIDTH + _BC_WIDTH


def _params(sem, vmem_bytes):
    return pltpu.CompilerParams(dimension_semantics=sem,
                                vmem_limit_bytes=int(min(VMEM_CAP, vmem_bytes)))


def _rms(xf, w):
    ms = jnp.mean(xf * xf, axis=-1, keepdims=True)
    return xf * lax.rsqrt(ms + EPS) * w


def _split3(a):
    hi = a.astype(BF16)
    r1 = a - hi.astype(F32)
    mid = r1.astype(BF16)
    lo = (r1 - mid.astype(F32)).astype(BF16)
    return hi, mid, lo


def _tri_left(tri, a):
    return sum(jnp.dot(tri, t, preferred_element_type=F32) for t in _split3(a))


def _tri_right(a, tri):
    return sum(jnp.dot(t, tri, preferred_element_type=F32) for t in _split3(a))


def _softplus(v):
    return jnp.maximum(v, 0.0) + jnp.log1p(jnp.exp(-jnp.abs(v)))


def _silu(v):
    return v * jax.nn.sigmoid(v)


def _norm_kernel(x_ref, nw_ref, h_ref):
    h_ref[...] = _rms(x_ref[...], nw_ref[...]).astype(h_ref.dtype)


def _norm(x, nw, tm=512):
    m, d = x.shape
    return pl.pallas_call(
        _norm_kernel,
        out_shape=jax.ShapeDtypeStruct((m, d), BF16),
        grid=(m // tm,),
        in_specs=[pl.BlockSpec((tm, d), lambda i: (i, 0)),
                  pl.BlockSpec((1, d), lambda i: (0, 0))],
        out_specs=pl.BlockSpec((tm, d), lambda i: (i, 0)),
        compiler_params=_params(("arbitrary",), 2 * tm * d * 6 + 4 * tm * d * 4),
        name="mix_norm",
    )(x, nw)


def _proj_kernel(h_ref, wT_ref, o_ref, w_sc):
    @pl.when(pl.program_id(1) == 0)
    def _():
        w_sc[...] = wT_ref[...].astype(BF16)

    o_ref[...] = lax.dot_general(h_ref[...], w_sc[...], (((1,), (1,)), ((), ())),
                                 preferred_element_type=F32).astype(o_ref.dtype)


def _proj(h, wT, n_cols, out_dtype, tm, tn, name):
    m, k = h.shape
    ob = jnp.dtype(out_dtype).itemsize
    vmem = 2 * (tm * k * 2 + k * tn * 4 + tm * tn * ob) + k * tn * 2 + 2 * tm * tn * 4
    return pl.pallas_call(
        _proj_kernel,
        out_shape=jax.ShapeDtypeStruct((m, n_cols), out_dtype),
        grid=(n_cols // tn, m // tm),
        in_specs=[pl.BlockSpec((tm, k), lambda j, i: (i, 0)),
                  pl.BlockSpec((tn, k), lambda j, i: (j, 0))],
        out_specs=pl.BlockSpec((tm, tn), lambda j, i: (i, j)),
        scratch_shapes=[pltpu.VMEM((tn, k), BF16)],
        compiler_params=_params(("arbitrary", "arbitrary"), vmem),
        name=name,
    )(h, wT)


_AUG_ONES = 6
_N_PAIRS = FOX_HEADS // 2


def _aug_selector():
    sel = np.zeros((4 * LANES, _N_PAIRS * LANES), np.float32)
    for pr in range(_N_PAIRS):
        for e in range(2):
            for j in range(3):
                sel[j * LANES + 2 * pr + e, pr * LANES + 3 * e + j] = -1.0
        sel[3 * LANES, pr * LANES + _AUG_ONES:pr * LANES + _AUG_ONES + 3] = 1.0
    return jnp.asarray(sel, BF16)


def _fox_c_kernel(f_ref, b_ref, sel_ref, c_ref, aug_ref, carry_sc):
    @pl.when(pl.program_id(0) == 0)
    def _():
        carry_sc[...] = jnp.zeros_like(carry_sc)

    z = f_ref[...] + b_ref[...]
    lf = jnp.minimum(z, 0.0) - jnp.log1p(jnp.exp(-jnp.abs(z)))
    t = lf.shape[0]
    row = lax.broadcasted_iota(jnp.int32, (t, t), 0)
    col = lax.broadcasted_iota(jnp.int32, (t, t), 1)
    tril = (col <= row).astype(BF16)
    cum = _tri_left(tril, lf) + carry_sc[...]
    c_ref[...] = cum
    carry_sc[...] = cum[t - 1:t, :]
    hi, mid, lo = _split3(cum)
    parts = jnp.concatenate([hi, mid, lo, jnp.ones_like(hi)], axis=1)
    aug_ref[...] = jnp.dot(parts, sel_ref[...], preferred_element_type=F32).astype(BF16)


def _fox_c(u2, b_pad, t=256):
    s = u2.shape[0]
    aug_cols = _N_PAIRS * LANES
    return pl.pallas_call(
        _fox_c_kernel,
        out_shape=(jax.ShapeDtypeStruct((s, LANES), F32),
                   jax.ShapeDtypeStruct((s, aug_cols), BF16)),
        grid=(s // t,),
        in_specs=[pl.BlockSpec((t, LANES), lambda i: (i, _U2_SMALL // LANES)),
                  pl.BlockSpec((1, LANES), lambda i: (0, 0)),
                  pl.BlockSpec((4 * LANES, aug_cols), lambda i: (0, 0))],
        out_specs=(pl.BlockSpec((t, LANES), lambda i: (i, 0)),
                   pl.BlockSpec((t, aug_cols), lambda i: (i, 0))),
        scratch_shapes=[pltpu.VMEM((1, LANES), F32)],
        compiler_params=_params(("arbitrary",), 16 * 1024 * 1024),
        name="fox_cumlogf",
    )(u2, b_pad, _aug_selector())


_V_ROWS = HEAD_DIM + 16


def _fox_attn_kernel(q_ref, k_ref, v_ref, aug_ref, cq_ref, o_ref, vT_sc, acc_sc, s0_sc, s1_sc,
                     w_sc, m_sc, *, t):
    qi = pl.program_id(1)
    n_kv = k_ref.shape[0] // t

    @pl.when(qi == 0)
    def _():
        pad_row = lax.broadcasted_iota(jnp.int32, (_V_ROWS - HEAD_DIM, t), 0)
        ones_rows = jnp.where(pad_row == 0, 1.0, 0.0).astype(BF16)

        def transpose_v(c, carry):
            st = pl.multiple_of(c * t, t)
            vT = v_ref[pl.ds(st, t), :].astype(F32).T.astype(BF16)
            for e in range(2):
                vT_sc[e, 0:HEAD_DIM, pl.ds(st, t)] = vT[e * HEAD_DIM:(e + 1) * HEAD_DIM, :]
                vT_sc[e, HEAD_DIM:_V_ROWS, pl.ds(st, t)] = ones_rows
            return carry
        lax.fori_loop(0, n_kv, transpose_v, 0)

    scale = HEAD_DIM ** -0.5
    qT = q_ref[...].astype(F32).T * scale
    row = lax.broadcasted_iota(jnp.int32, (LANES, t), 0)
    tk = s0_sc.shape[1]
    key_idx = lax.broadcasted_iota(jnp.int32, (tk, t), 0)
    qry_idx = lax.broadcasted_iota(jnp.int32, (tk, t), 1)

    ws = []
    for e in range(2):
        qTe = jnp.where(row // HEAD_DIM == e, qT, 0.0)
        cq_hi, cq_mid, cq_lo = (part.astype(F32) for part in _split3(cq_ref[e]))
        aug_rows = jnp.where(
            row // 3 == e, 1.0,
            jnp.where(row == _AUG_ONES, cq_hi,
                      jnp.where(row == _AUG_ONES + 1, cq_mid,
                                jnp.where(row == _AUG_ONES + 2, cq_lo, 0.0))))
        ws.append(jnp.concatenate([qTe, aug_rows], axis=0).astype(BF16))
        acc_sc[e] = jnp.zeros((_V_ROWS, t), F32)

    def scores_into(j, s_sc):
        start = pl.multiple_of(j * tk, tk)
        keys = jnp.concatenate([k_ref[pl.ds(start, tk), :], aug_ref[pl.ds(start, tk), :]],
                               axis=1)
        for e in range(2):
            s_sc[e] = jnp.dot(keys, ws[e], preferred_element_type=F32)

    def softmax_pv(j, s_sc, m_prev, masked):
        start = pl.multiple_of(j * tk, tk)
        if masked:
            visible = key_idx + start <= qry_idx + qi * t
        m_out, probs, alphas = [], [], []
        for e in range(2):
            s = s_sc[e]
            if masked:
                s = jnp.where(visible, s, -jnp.inf)
            m_new = jnp.maximum(m_prev[e], jnp.max(s, axis=0, keepdims=True))
            probs.append(jnp.exp(s - m_new).astype(BF16))
            alphas.append(jnp.exp(m_prev[e] - m_new))
            m_out.append(m_new)
        for e in range(2):
            acc_sc[e] = alphas[e] * acc_sc[e] + jnp.dot(
                vT_sc[e, :, pl.ds(start, tk)], probs[e], preferred_element_type=F32)
        return tuple(m_out)

    def two_blocks(i, m):
        scores_into(2 * i + 1, s1_sc)
        m = softmax_pv(2 * i, s0_sc, m, False)
        scores_into(2 * i + 2, s0_sc)
        return softmax_pv(2 * i + 1, s1_sc, m, False)

    def four_blocks(i, m):
        return two_blocks(2 * i + 1, two_blocks(2 * i, m))

    m_init = jnp.full((1, t), -1e30, F32)
    scores_into(0, s0_sc)
    m = lax.fori_loop(0, qi // 2, four_blocks, (m_init, m_init))
    m = lax.cond(qi % 2 == 1, lambda mm: two_blocks(qi - 1, mm), lambda mm: mm, m)
    upper = slice(tk, t)
    last = pl.multiple_of((2 * qi + 1) * tk, tk)
    last_keys = jnp.concatenate([k_ref[pl.ds(last, tk), :], aug_ref[pl.ds(last, tk), :]], axis=1)
    for e in range(2):
        w_sc[e] = ws[e]
        s1_sc[e, :, 0:tk] = jnp.dot(last_keys, w_sc[e, :, upper], preferred_element_type=F32)
    m = softmax_pv(2 * qi, s0_sc, m, True)
    visible = (lax.broadcasted_iota(jnp.int32, (tk, tk), 0)
               <= lax.broadcasted_iota(jnp.int32, (tk, tk), 1))
    for e in range(2):
        s = jnp.where(visible, s1_sc[e, :, 0:tk], -jnp.inf)
        m_sc[e] = m[e]
        m_upper = m_sc[e, :, upper]
        m_new = jnp.maximum(m_upper, jnp.max(s, axis=0, keepdims=True))
        p = jnp.exp(s - m_new).astype(BF16)
        acc_sc[e, :, upper] = jnp.exp(m_upper - m_new) * acc_sc[e, :, upper] + jnp.dot(
            vT_sc[e, :, pl.ds(last, tk)], p, preferred_element_type=F32)
    outs = []
    for e in range(2):
        acc = acc_sc[e]
        outs.append(acc[0:HEAD_DIM, :] / acc[HEAD_DIM:HEAD_DIM + 1, :])
    o_ref[...] = jnp.concatenate(outs, axis=0).T.astype(o_ref.dtype)


def _fox_attn(qkv, aug, cq, t=512):
    s = qkv.shape[0]
    kcol = FOX_WIDTH // LANES
    vmem = 2 * (2 * t * LANES * 2 + 3 * s * LANES * 2 + 2 * t * 4) \
        + 2 * _V_ROWS * (s * 2 + t * 4) + 10 * t * t * 4
    return pl.pallas_call(
        functools.partial(_fox_attn_kernel, t=t),
        out_shape=jax.ShapeDtypeStruct((s, FOX_WIDTH), BF16),
        grid=(_N_PAIRS, s // t),
        in_specs=[pl.BlockSpec((t, LANES), lambda pr, qi: (qi, pr)),
                  pl.BlockSpec((s, LANES), lambda pr, qi: (0, kcol + pr)),
                  pl.BlockSpec((s, LANES), lambda pr, qi: (0, 2 * kcol + pr)),
                  pl.BlockSpec((s, LANES), lambda pr, qi: (0, pr)),
                  pl.BlockSpec((2, 1, t), lambda pr, qi: (pr, 0, qi))],
        out_specs=pl.BlockSpec((t, LANES), lambda pr, qi: (qi, pr)),
        scratch_shapes=[pltpu.VMEM((2, _V_ROWS, s), BF16), pltpu.VMEM((2, _V_ROWS, t), F32),
                        pltpu.VMEM((2, t // 2, t), F32), pltpu.VMEM((2, t // 2, t), F32),
                        pltpu.VMEM((2, 2 * LANES, t), BF16), pltpu.VMEM((2, 1, t), F32)],
        compiler_params=_params(("arbitrary", "arbitrary"), vmem),
        name="fox_attention",
    )(qkv, qkv, qkv, aug, cq)


def _ssd_kernel(xs_ref, z_ref, bc_ref, dt_ref, dtT_ref, cwx_ref, cbx_ref, cwbc_ref, cbbc_ref,
                dtb_ref, dtbT_ref, alog_ref, alogT_ref, dskip_ref, nw_ref, y_ref,
                xbuf, bcbuf, xtail_sc, bctail_sc, state_sc, ybuf, *, L):
    halo = SUBLANES

    @pl.when(pl.program_id(0) == 0)
    def _():
        xtail_sc[...] = jnp.zeros_like(xtail_sc)
        bctail_sc[...] = jnp.zeros_like(bctail_sc)
        state_sc[...] = jnp.zeros_like(state_sc)

    def conv_silu(u_ref, tail_sc, buf, w_ref, b_ref):
        cur = u_ref[...]
        prev = tail_sc[...]
        tail_sc[...] = cur[L - halo:L, :]
        last = SSD_CONV - 1

        def taps(c, backs):
            acc = b_ref[...] + c * w_ref[last:last + 1, :]
            for k in range(1, SSD_CONV):
                acc = acc + backs[k - 1] * w_ref[last - k:last - k + 1, :]
            return acc

        buf[...] = taps(cur, [pltpu.roll(cur, k, axis=0) for k in range(1, SSD_CONV)])
        top = cur[0:halo, :]
        r = lax.broadcasted_iota(jnp.int32, top.shape, 0)
        buf[0:halo, :] = taps(top, [
            jnp.where(r < k, pltpu.roll(prev, k, axis=0), pltpu.roll(top, k, axis=0))
            for k in range(1, SSD_CONV)])
        return _silu(buf[...])

    xc = conv_silu(xs_ref, xtail_sc, xbuf, cwx_ref, cbx_ref)
    bcc = conv_silu(bc_ref, bctail_sc, bcbuf, cwbc_ref, cbbc_ref)

    dt = _softplus(dt_ref[...] + dtb_ref[...])
    dtT = _softplus(dtT_ref[...] + dtbT_ref[...])
    a = dt * (-jnp.exp(alog_ref[...]))
    aT = dtT * (-jnp.exp(alogT_ref[...]))

    row = lax.broadcasted_iota(jnp.int32, (L, L), 0)
    col = lax.broadcasted_iota(jnp.int32, (L, L), 1)
    causal = col <= row
    a_cum = _tri_left(causal.astype(BF16), a)
    a_cumT = _tri_right(aT, (row <= col).astype(BF16))
    a_last = a_cum[L - 1:L, :]
    a_lastT = a_cumT[:, L - 1:L]
    exp_acum = jnp.exp(a_cum)
    w_endT = jnp.exp(a_lastT - a_cumT) * dtT
    chunk_decay = jnp.exp(a_last)

    n = SSD_STATE
    heads_per_group = SSD_HEADS // SSD_GROUPS
    for g in range(SSD_GROUPS):
        bg = bcc[:, g * n:(g + 1) * n]
        cg = bcc[:, SSD_GROUPS * n + g * n:SSD_GROUPS * n + (g + 1) * n]
        bgT = bg.T
        cb = lax.dot_general(cg.astype(BF16), bg.astype(BF16), (((1,), (1,)), ((), ())),
                             preferred_element_type=F32)
        for hh in range(heads_per_group):
            h = g * heads_per_group + hh
            cols = slice(h * HEAD_DIM, (h + 1) * HEAD_DIM)
            seg = a_cum[:, h:h + 1] - a_cumT[h:h + 1, :]
            decay = jnp.exp(jnp.where(causal, seg, -jnp.inf))
            mix = cb * decay * dtT[h:h + 1, :]
            xh = xc[:, cols]
            xh_b = xh.astype(BF16)
            st = state_sc[:, cols]
            y = jnp.dot(mix.astype(BF16), xh_b, preferred_element_type=F32)
            y = y + jnp.dot((cg * exp_acum[:, h:h + 1]).astype(BF16), st.astype(BF16),
                            preferred_element_type=F32)
            ybuf[:, cols] = y + dskip_ref[:, cols] * xh
            upd = jnp.dot((bgT * w_endT[h:h + 1, :]).astype(BF16), xh_b,
                          preferred_element_type=F32)
            state_sc[:, cols] = chunk_decay[:, h:h + 1] * st + upd

    y = ybuf[...] * _silu(z_ref[...])
    y_ref[...] = _rms(y, nw_ref[...]).astype(y_ref.dtype)


def _ssd(u2, dt_raw, dt_rawT, cwx, cbx, cwbc, cbbc, dtb, dtbT, alog, alogT, dskip, nw, L=256):
    s = u2.shape[0]
    w = SSD_WIDTH
    full = lambda shape: pl.BlockSpec(shape, lambda i: (0,) * len(shape))
    vmem = 2 * (2 * L * w * 4 + L * _BC_WIDTH * 4 + L * w * 2) + (2 * L + 16) * w * 4 \
        + (L + 8) * _BC_WIDTH * 4 + SSD_STATE * w * 4 + 24 * L * L * 4 + 6 * L * w * 4
    return pl.pallas_call(
        functools.partial(_ssd_kernel, L=L),
        out_shape=jax.ShapeDtypeStruct((s, w), BF16),
        grid=(s // L,),
        in_specs=[pl.BlockSpec((L, w), lambda i: (i, 0)),
                  pl.BlockSpec((L, w), lambda i: (i, 1)),
                  pl.BlockSpec((L, _BC_WIDTH), lambda i: (i, 2 * w // _BC_WIDTH)),
                  pl.BlockSpec((L, SSD_HEADS), lambda i: (i, 0)),
                  pl.BlockSpec((SSD_HEADS, L), lambda i: (0, i)),
                  full((SSD_CONV, w)), full((1, w)),
                  full((SSD_CONV, _BC_WIDTH)), full((1, _BC_WIDTH)),
                  full((1, SSD_HEADS)), full((SSD_HEADS, 1)),
                  full((1, SSD_HEADS)), full((SSD_HEADS, 1)),
                  full((1, w)), full((1, w))],
        out_specs=pl.BlockSpec((L, w), lambda i: (i, 0)),
        scratch_shapes=[pltpu.VMEM((L, w), F32),
                        pltpu.VMEM((L, _BC_WIDTH), F32),
                        pltpu.VMEM((SUBLANES, w), F32),
                        pltpu.VMEM((SUBLANES, _BC_WIDTH), F32),
                        pltpu.VMEM((SSD_STATE, w), F32),
                        pltpu.VMEM((L, w), F32)],
        compiler_params=_params(("arbitrary",), vmem),
        name="ssd_mixer",
    )(u2, u2, u2, dt_raw, dt_rawT, cwx, cbx, cwbc, cbbc, dtb, dtbT, alog, alogT, dskip, nw)


def _out_proj_kernel(x_ref, ya_ref, yb_ref, wa_ref, wb_ref, nw_ref, o_ref, h_ref):
    o = (x_ref[...]
         + jnp.dot(ya_ref[...], wa_ref[...], preferred_element_type=F32)
         + jnp.dot(yb_ref[...], wb_ref[...], preferred_element_type=F32))
    o_ref[...] = o
    h_ref[...] = _rms(o, nw_ref[...]).astype(h_ref.dtype)


def _out_proj(x, ya, yb, wa, wb, nw, tm=512):
    m, d = x.shape
    ka, kb = ya.shape[1], yb.shape[1]
    vmem = 2 * (2 * tm * d * 4 + tm * d * 2 + tm * (ka + kb) * 2 + (ka + kb) * d * 2) \
        + 4 * tm * d * 4
    return pl.pallas_call(
        _out_proj_kernel,
        out_shape=(jax.ShapeDtypeStruct((m, d), F32), jax.ShapeDtypeStruct((m, d), BF16)),
        grid=(m // tm,),
        in_specs=[pl.BlockSpec((tm, d), lambda i: (i, 0)),
                  pl.BlockSpec((tm, ka), lambda i: (i, 0)),
                  pl.BlockSpec((tm, kb), lambda i: (i, 0)),
                  pl.BlockSpec((ka, d), lambda i: (0, 0)),
                  pl.BlockSpec((kb, d), lambda i: (0, 0)),
                  pl.BlockSpec((1, d), lambda i: (0, 0))],
        out_specs=(pl.BlockSpec((tm, d), lambda i: (i, 0)),
                   pl.BlockSpec((tm, d), lambda i: (i, 0))),
        compiler_params=_params(("arbitrary",), vmem),
        name="mix_out_proj",
    )(x, ya, yb, wa, wb, nw)


def _gate_up_kernel(h_ref, wg_ref, wu_ref, cw_ref, cb_ref, o_ref, wg_sc, wu_sc, tail_sc, conv_sc):
    halo = SUBLANES
    tm = conv_sc.shape[0]

    @pl.when(pl.program_id(1) == 0)
    def _():
        wg_sc[...] = wg_ref[...].astype(BF16)
        wu_sc[...] = wu_ref[...].astype(BF16)
        tail_sc[...] = jnp.zeros_like(tail_sc)

    h = h_ref[...]
    g = jnp.dot(h, wg_sc[...], preferred_element_type=F32)
    up = jnp.dot(h, wu_sc[...], preferred_element_type=F32)

    prev = tail_sc[...]
    tail_sc[...] = g[tm - halo:tm, :]

    def conv(cur, back1, back2):
        return (cb_ref[...] + cur * cw_ref[2:3, :] + back1 * cw_ref[1:2, :]
                + back2 * cw_ref[0:1, :])

    conv_sc[...] = conv(g, pltpu.roll(g, 1, axis=0), pltpu.roll(g, 2, axis=0))
    top = g[0:halo, :]
    r = lax.broadcasted_iota(jnp.int32, top.shape, 0)
    back1 = jnp.where(r < 1, pltpu.roll(prev, 1, axis=0), pltpu.roll(top, 1, axis=0))
    back2 = jnp.where(r < 2, pltpu.roll(prev, 2, axis=0), pltpu.roll(top, 2, axis=0))
    conv_sc[0:halo, :] = conv(top, back1, back2)
    o_ref[...] = (_silu(conv_sc[...]) * up).astype(o_ref.dtype)


def _gate_up(h, w_gu, cw, cb, tm=1024, tn=512):
    m, d = h.shape
    nj = D_FF // tn
    vmem = 2 * (tm * d * 2 + 2 * d * tn * 4 + tm * tn * 2) + 2 * d * tn * 2 \
        + 10 * tm * tn * 4
    return pl.pallas_call(
        _gate_up_kernel,
        out_shape=jax.ShapeDtypeStruct((m, D_FF), BF16),
        grid=(nj, m // tm),
        in_specs=[pl.BlockSpec((tm, d), lambda j, i: (i, 0)),
                  pl.BlockSpec((d, tn), lambda j, i: (0, j)),
                  pl.BlockSpec((d, tn), lambda j, i: (0, j + nj)),
                  pl.BlockSpec((FFN_CONV, tn), lambda j, i: (0, j)),
                  pl.BlockSpec((1, tn), lambda j, i: (0, j))],
        out_specs=pl.BlockSpec((tm, tn), lambda j, i: (i, j)),
        scratch_shapes=[pltpu.VMEM((d, tn), BF16), pltpu.VMEM((d, tn), BF16),
                        pltpu.VMEM((SUBLANES, tn), F32), pltpu.VMEM((tm, tn), F32)],
        compiler_params=_params(("arbitrary", "arbitrary"), vmem),
        name="ffn_gate_up",
    )(h, w_gu, w_gu, cw, cb)


def _down_kernel(a_ref, w_ref, x_ref, o_ref, w_sc):
    @pl.when(pl.program_id(1) == 0)
    def _():
        w_sc[...] = w_ref[...].astype(BF16)

    o_ref[...] = x_ref[...] + jnp.dot(a_ref[...], w_sc[...], preferred_element_type=F32)


def _down(act, w, x, tm=512, tn=512):
    m, d = x.shape
    kk = act.shape[1]
    vmem = 2 * (tm * kk * 2 + kk * tn * 4 + 2 * tm * tn * 4) + kk * tn * 2 + 2 * tm * tn * 4
    return pl.pallas_call(
        _down_kernel,
        out_shape=jax.ShapeDtypeStruct((m, d), F32),
        grid=(d // tn, m // tm),
        in_specs=[pl.BlockSpec((tm, kk), lambda j, i: (i, 0)),
                  pl.BlockSpec((kk, tn), lambda j, i: (0, j)),
                  pl.BlockSpec((tm, tn), lambda j, i: (i, j))],
        out_specs=pl.BlockSpec((tm, tn), lambda j, i: (i, j)),
        scratch_shapes=[pltpu.VMEM((kk, tn), BF16)],
        compiler_params=_params(("arbitrary", "arbitrary"), vmem),
        name="ffn_down",
    )(act, w, x)


def _ple_kernel(x_ref, p_ref, nw_ref, wg_ref, wp_ref, fw_ref, o_ref, *, final):
    x = x_ref[...]
    h = _rms(x, nw_ref[...]).astype(BF16)
    gate = jax.nn.sigmoid(jnp.dot(h, wg_ref[...], preferred_element_type=F32))
    proj = jnp.dot(p_ref[...].astype(BF16), wp_ref[...], preferred_element_type=F32)
    x = x + gate * proj
    if final:
        x = _rms(x, fw_ref[...])
    o_ref[...] = x


def _ple(x, p, nw, wg, wp, fw, final, tm=512):
    m, d = x.shape
    dp = p.shape[1]
    vmem = 2 * (2 * tm * d * 4 + tm * dp * 4 + d * d * 2 + dp * d * 2) + 6 * tm * d * 4
    return pl.pallas_call(
        functools.partial(_ple_kernel, final=final),
        out_shape=jax.ShapeDtypeStruct((m, d), F32),
        grid=(m // tm,),
        in_specs=[pl.BlockSpec((tm, d), lambda i: (i, 0)),
                  pl.BlockSpec((tm, dp), lambda i: (i, 0)),
                  pl.BlockSpec((1, d), lambda i: (0, 0)),
                  pl.BlockSpec((d, d), lambda i: (0, 0)),
                  pl.BlockSpec((dp, d), lambda i: (0, 0)),
                  pl.BlockSpec((1, d), lambda i: (0, 0))],
        out_specs=pl.BlockSpec((tm, d), lambda i: (i, 0)),
        compiler_params=_params(("arbitrary",), vmem),
        name="ple_gate",
    )(x, p, nw, wg, wp, fw)


def _layer(x, p, mix_norm_w, w_in, fox_forget_bias, ssd_conv_w, ssd_conv_b, ssd_dt_bias,
           ssd_A_log, ssd_D, ssd_norm_w, w_out, ffn_norm_w, w_gate_up, ffn_conv_w,
           ffn_conv_b, w_down, ple_norm_w, w_ple_gate, w_ple_proj, final_norm_w, final):
    s = x.shape[0]
    row = lambda v: v.reshape(1, -1).astype(F32)

    w_inT = jnp.swapaxes(w_in, 0, 1).astype(F32)
    w_u2T = jnp.concatenate(
        [w_inT[_OFF_XS:_OFF_B], w_inT[_OFF_Z:_OFF_DT], w_inT[_OFF_B:_OFF_Z],
         w_inT[_OFF_F:_OFF_XS], w_inT[_OFF_DT:_IN_COLS],
         jnp.zeros((LANES - FOX_HEADS - SSD_HEADS, D_MODEL), F32)], axis=0)

    h_mix = _norm(x, row(mix_norm_w))
    qkv = _proj(h_mix, w_inT, _OFF_F, BF16, tm=1024, tn=512, name="in_proj_qkv")
    u2 = _proj(h_mix, w_u2T, _U2_COLS, F32, tm=1024, tn=_U2_COLS // 3, name="in_proj_ssd")

    b_pad = jnp.zeros((1, LANES), F32).at[0, :FOX_HEADS].set(fox_forget_bias.astype(F32))
    c, aug = _fox_c(u2, b_pad)
    cT = c[:, :FOX_HEADS].T
    y_fox = _fox_attn(qkv, aug, cT[:, None, :])

    dt_raw = u2[:, _U2_SMALL + FOX_HEADS:_U2_SMALL + FOX_HEADS + SSD_HEADS]
    col = lambda v: v.reshape(-1, 1).astype(F32)
    y_ssd = _ssd(u2, dt_raw, dt_raw.T,
                 ssd_conv_w[:, :SSD_WIDTH].astype(F32), row(ssd_conv_b[:SSD_WIDTH]),
                 ssd_conv_w[:, SSD_WIDTH:].astype(F32), row(ssd_conv_b[SSD_WIDTH:]),
                 row(ssd_dt_bias), col(ssd_dt_bias), row(ssd_A_log), col(ssd_A_log),
                 row(jnp.repeat(ssd_D, HEAD_DIM)), row(ssd_norm_w))

    w_out_b = w_out.astype(BF16)
    x, h_ffn = _out_proj(x, y_fox, y_ssd, w_out_b[:FOX_WIDTH], w_out_b[FOX_WIDTH:],
                         row(ffn_norm_w))

    act = _gate_up(h_ffn, w_gate_up.astype(F32), ffn_conv_w.astype(F32), row(ffn_conv_b))
    x = _down(act, w_down.astype(F32), x)

    return _ple(x, p, row(ple_norm_w), w_ple_gate.astype(BF16), w_ple_proj.astype(BF16),
                row(final_norm_w), final)


def kernel(x, p, mix_norm_w, w_in, fox_forget_bias, ssd_conv_w, ssd_conv_b, ssd_dt_bias,
           ssd_A_log, ssd_D, ssd_norm_w, w_out, ffn_norm_w, w_gate_up, ffn_conv_w,
           ffn_conv_b, w_down, ple_norm_w, w_ple_gate, w_ple_proj, final_norm_w):
    bsz, s, d = x.shape
    depth = p.shape[0]
    outs = []
    for b in range(bsz):
        xb = x[b]
        for i in range(depth):
            xb = _layer(xb, p[i, b], mix_norm_w[i], w_in[i], fox_forget_bias[i], ssd_conv_w[i],
                        ssd_conv_b[i], ssd_dt_bias[i], ssd_A_log[i], ssd_D[i], ssd_norm_w[i],
                        w_out[i], ffn_norm_w[i], w_gate_up[i], ffn_conv_w[i], ffn_conv_b[i],
                        w_down[i], ple_norm_w[i], w_ple_gate[i], w_ple_proj[i], final_norm_w,
                        final=(i == depth - 1))
        outs.append(xb)
    return outs[0][None] if bsz == 1 else jnp.stack(outs, axis=0)
```

```python
import functools

import numpy as np
import jax
import jax.numpy as jnp
from jax import lax
from jax.experimental import pallas as pl
from jax.experimental.pallas import tpu as pltpu

F32 = jnp.float32
BF16 = jnp.bfloat16

D_MODEL = 2048
SEQ = 8192
D_PLE = 256
HEAD_DIM = 64
FOX_WIDTH = 1024
FOX_HEADS = 16
SSD_WIDTH = 1024
SSD_HEADS = 16
SSD_GROUPS = 2
SSD_STATE = 128
SSD_CONV = 4
D_FF = 5632
FFN_CONV = 3
EPS = 1e-6

_OFF_Q = 0
_OFF_F = 3 * FOX_WIDTH
_OFF_XS = _OFF_F + FOX_HEADS
_OFF_B = _OFF_XS + SSD_WIDTH
_OFF_C = _OFF_B + SSD_GROUPS * SSD_STATE
_OFF_Z = _OFF_C + SSD_GROUPS * SSD_STATE
_OFF_DT = _OFF_Z + SSD_WIDTH
_IN_COLS = _OFF_DT + SSD_HEADS

LANES = 128
SUBLANES = 8
VMEM_CAP = 60 * 1024 * 1024

_BC_WIDTH = 2 * SSD_GROUPS * SSD_STATE
_U2_COLS = 2 * SSD_WIDTH + _BC_WIDTH + LANES
_U2_SMALL = 2 * SSD_WIDTH + _BC_WIDTH


def _params(sem, vmem_bytes):
    return pltpu.CompilerParams(dimension_semantics=sem,
                                vmem_limit_bytes=int(min(VMEM_CAP, vmem_bytes)))


def _rms(xf, w):
    ms = jnp.mean(xf * xf, axis=-1, keepdims=True)
    return xf * lax.rsqrt(ms + EPS) * w


def _split3(a):
    hi = a.astype(BF16)
    r1 = a - hi.astype(F32)
    mid = r1.astype(BF16)
    lo = (r1 - mid.astype(F32)).astype(BF16)
    return hi, mid, lo


def _tri_left(tri, a):
    return sum(jnp.dot(tri, t, preferred_element_type=F32) for t in _split3(a))


def _tri_right(a, tri):
    return sum(jnp.dot(t, tri, preferred_element_type=F32) for t in _split3(a))


def _softplus(v):
    return jnp.maximum(v, 0.0) + jnp.log1p(jnp.exp(-jnp.abs(v)))


def _silu(v):
    return v * jax.nn.sigmoid(v)


def _norm_kernel(x_ref, nw_ref, h_ref):
    h_ref[...] = _rms(x_ref[...], nw_ref[...]).astype(h_ref.dtype)


def _norm(x, nw, tm=512):
    m, d = x.shape
    return pl.pallas_call(
        _norm_kernel,
        out_shape=jax.ShapeDtypeStruct((m, d), BF16),
        grid=(m // tm,),
        in_specs=[pl.BlockSpec((tm, d), lambda i: (i, 0)),
                  pl.BlockSpec((1, d), lambda i: (0, 0))],
        out_specs=pl.BlockSpec((tm, d), lambda i: (i, 0)),
        compiler_params=_params(("arbitrary",), 2 * tm * d * 6 + 4 * tm * d * 4),
        name="mix_norm",
    )(x, nw)


def _proj_kernel(h_ref, wT_ref, o_ref, w_sc):
    @pl.when(pl.program_id(1) == 0)
    def _():
        w_sc[...] = wT_ref[...].astype(BF16)

    o_ref[...] = lax.dot_general(h_ref[...], w_sc[...], (((1,), (1,)), ((), ())),
                                 preferred_element_type=F32).astype(o_ref.dtype)


def _proj(h, wT, n_cols, out_dtype, tm, tn, name):
    m, k = h.shape
    ob = jnp.dtype(out_dtype).itemsize
    vmem = 2 * (tm * k * 2 + k * tn * 4 + tm * tn * ob) + k * tn * 2 + 2 * tm * tn * 4
    return pl.pallas_call(
        _proj_kernel,
        out_shape=jax.ShapeDtypeStruct((m, n_cols), out_dtype),
        grid=(n_cols // tn, m // tm),
        in_specs=[pl.BlockSpec((tm, k), lambda j, i: (i, 0)),
                  pl.BlockSpec((tn, k), lambda j, i: (j, 0))],
        out_specs=pl.BlockSpec((tm, tn), lambda j, i: (i, j)),
        scratch_shapes=[pltpu.VMEM((tn, k), BF16)],
        compiler_params=_params(("arbitrary", "arbitrary"), vmem),
        name=name,
    )(h, wT)


_AUG_ONES = 6
_N_PAIRS = FOX_HEADS // 2


def _aug_selector():
    sel = np.zeros((4 * LANES, _N_PAIRS * LANES), np.float32)
    for pr in range(_N_PAIRS):
        for e in range(2):
            for j in range(3):
                sel[j * LANES + 2 * pr + e, pr * LANES + 3 * e + j] = -1.0
        sel[3 * LANES, pr * LANES + _AUG_ONES:pr * LANES + _AUG_ONES + 3] = 1.0
    return jnp.asarray(sel, BF16)


def _fox_c_kernel(f_ref, b_ref, sel_ref, c_ref, aug_ref, carry_sc):
    @pl.when(pl.program_id(0) == 0)
    def _():
        carry_sc[...] = jnp.zeros_like(carry_sc)

    z = f_ref[...] + b_ref[...]
    lf = jnp.minimum(z, 0.0) - jnp.log1p(jnp.exp(-jnp.abs(z)))
    t = lf.shape[0]
    row = lax.broadcasted_iota(jnp.int32, (t, t), 0)
    col = lax.broadcasted_iota(jnp.int32, (t, t), 1)
    tril = (col <= row).astype(BF16)
    cum = _tri_left(tril, lf) + carry_sc[...]
    c_ref[...] = cum
    carry_sc[...] = cum[t - 1:t, :]
    hi, mid, lo = _split3(cum)
    parts = jnp.concatenate([hi, mid, lo, jnp.ones_like(hi)], axis=1)
    aug_ref[...] = jnp.dot(parts, sel_ref[...], preferred_element_type=F32).astype(BF16)


def _fox_c(u2, b_pad, t=256):
    s = u2.shape[0]
    aug_cols = _N_PAIRS * LANES
    return pl.pallas_call(
        _fox_c_kernel,
        out_shape=(jax.ShapeDtypeStruct((s, LANES), F32),
                   jax.ShapeDtypeStruct((s, aug_cols), BF16)),
        grid=(s // t,),
        in_specs=[pl.BlockSpec((t, LANES), lambda i: (i, _U2_SMALL // LANES)),
                  pl.BlockSpec((1, LANES), lambda i: (0, 0)),
                  pl.BlockSpec((4 * LANES, aug_cols), lambda i: (0, 0))],
        out_specs=(pl.BlockSpec((t, LANES), lambda i: (i, 0)),
                   pl.BlockSpec((t, aug_cols), lambda i: (i, 0))),
        scratch_shapes=[pltpu.VMEM((1, LANES), F32)],
        compiler_params=_params(("arbitrary",), 16 * 1024 * 1024),
        name="fox_cumlogf",
    )(u2, b_pad, _aug_selector())


_V_ROWS = HEAD_DIM + 16


def _fox_attn_kernel(q_ref, k_ref, v_ref, aug_ref, cq_ref, o_ref, vT_sc, acc_sc, s0_sc, s1_sc,
                     w_sc, m_sc, *, t):
    n_q = q_ref.shape[0] // t
    tk = s0_sc.shape[1]

    pad_row = lax.broadcasted_iota(jnp.int32, (_V_ROWS - HEAD_DIM, t), 0)
    ones_rows = jnp.where(pad_row == 0, 1.0, 0.0).astype(BF16)

    def transpose_v(c, carry):
        st = pl.multiple_of(c * t, t)
        vT = v_ref[pl.ds(st, t), :].astype(F32).T.astype(BF16)
        for e in range(2):
            vT_sc[e, 0:HEAD_DIM, pl.ds(st, t)] = vT[e * HEAD_DIM:(e + 1) * HEAD_DIM, :]
            vT_sc[e, HEAD_DIM:_V_ROWS, pl.ds(st, t)] = ones_rows
        return carry
    lax.fori_loop(0, n_q, transpose_v, 0)

    scale = HEAD_DIM ** -0.5
    row = lax.broadcasted_iota(jnp.int32, (LANES, t), 0)
    key_idx = lax.broadcasted_iota(jnp.int32, (tk, t), 0)
    qry_idx = lax.broadcasted_iota(jnp.int32, (tk, t), 1)
    upper = slice(tk, t)

    def build_weights(qi):
        qs = pl.multiple_of(qi * t, t)
        qT = q_ref[pl.ds(qs, t), :].astype(F32).T * scale
        for e in range(2):
            qTe = jnp.where(row // HEAD_DIM == e, qT, 0.0)
            cq_hi, cq_mid, cq_lo = (part.astype(F32)
                                    for part in _split3(cq_ref[e, :, pl.ds(qs, t)]))
            aug_rows = jnp.where(
                row // 3 == e, 1.0,
                jnp.where(row == _AUG_ONES, cq_hi,
                          jnp.where(row == _AUG_ONES + 1, cq_mid,
                                    jnp.where(row == _AUG_ONES + 2, cq_lo, 0.0))))
            w_sc[e] = jnp.concatenate([qTe, aug_rows], axis=0).astype(BF16)

    def key_block(j):
        start = pl.multiple_of(j * tk, tk)
        return jnp.concatenate([k_ref[pl.ds(start, tk), :], aug_ref[pl.ds(start, tk), :]],
                               axis=1)

    def scores_into(j, s_sc):
        keys = key_block(j)
        for e in range(2):
            s_sc[e] = jnp.dot(keys, w_sc[e], preferred_element_type=F32)

    def softmax_pv(qi, j, s_sc, m_prev, masked):
        start = pl.multiple_of(j * tk, tk)
        if masked:
            visible = key_idx + start <= qry_idx + qi * t
        m_out, probs, alphas = [], [], []
        for e in range(2):
            s = s_sc[e]
            if masked:
                s = jnp.where(visible, s, -jnp.inf)
            m_new = jnp.maximum(m_prev[e], jnp.max(s, axis=0, keepdims=True))
            probs.append(jnp.exp(s - m_new).astype(BF16))
            alphas.append(jnp.exp(m_prev[e] - m_new))
            m_out.append(m_new)
        for e in range(2):
            acc_sc[e] = alphas[e] * acc_sc[e] + jnp.dot(
                vT_sc[e, :, pl.ds(start, tk)], probs[e], preferred_element_type=F32)
        return tuple(m_out)

    def q_tile(qi, carry):
        for e in range(2):
            acc_sc[e] = jnp.zeros((_V_ROWS, t), F32)

        def two_blocks(i, m):
            scores_into(2 * i + 1, s1_sc)
            m = softmax_pv(qi, 2 * i, s0_sc, m, False)
            scores_into(2 * i + 2, s0_sc)
            return softmax_pv(qi, 2 * i + 1, s1_sc, m, False)

        def four_blocks(i, m):
            return two_blocks(2 * i + 1, two_blocks(2 * i, m))

        m_init = jnp.full((1, t), -1e30, F32)
        m = lax.fori_loop(0, qi // 2, four_blocks, (m_init, m_init))
        m = lax.cond(qi % 2 == 1, lambda mm: two_blocks(qi - 1, mm), lambda mm: mm, m)

        last = pl.multiple_of((2 * qi + 1) * tk, tk)
        last_keys = key_block(2 * qi + 1)
        for e in range(2):
            s1_sc[e, :, 0:tk] = jnp.dot(last_keys, w_sc[e, :, upper],
                                        preferred_element_type=F32)
        m = softmax_pv(qi, 2 * qi, s0_sc, m, True)

        build_weights(jnp.minimum(qi + 1, n_q - 1))
        scores_into(0, s0_sc)

        visible = (lax.broadcasted_iota(jnp.int32, (tk, tk), 0)
                   <= lax.broadcasted_iota(jnp.int32, (tk, tk), 1))
        outs = []
        for e in range(2):
            s = jnp.where(visible, s1_sc[e, :, 0:tk], -jnp.inf)
            m_sc[e] = m[e]
            m_upper = m_sc[e, :, upper]
            m_new = jnp.maximum(m_upper, jnp.max(s, axis=0, keepdims=True))
            p = jnp.exp(s - m_new).astype(BF16)
            acc_sc[e, :, upper] = jnp.exp(m_upper - m_new) * acc_sc[e, :, upper] + jnp.dot(
                vT_sc[e, :, pl.ds(last, tk)], p, preferred_element_type=F32)
            acc = acc_sc[e]
            outs.append(acc[0:HEAD_DIM, :] / acc[HEAD_DIM:HEAD_DIM + 1, :])
        o_ref[pl.ds(pl.multiple_of(qi * t, t), t), :] = jnp.concatenate(
            outs, axis=0).T.astype(o_ref.dtype)
        return carry

    build_weights(0)
    scores_into(0, s0_sc)
    lax.fori_loop(0, n_q, q_tile, 0)


def _fox_attn(qkv, aug, cq, t=512):
    s = qkv.shape[0]
    kcol = FOX_WIDTH // LANES
    vmem = 2 * (5 * s * LANES * 2 + 2 * s * 4) \
        + 2 * _V_ROWS * (s * 2 + t * 4) + 12 * t * t * 4
    return pl.pallas_call(
        functools.partial(_fox_attn_kernel, t=t),
        out_shape=jax.ShapeDtypeStruct((s, FOX_WIDTH), BF16),
        grid=(_N_PAIRS,),
        in_specs=[pl.BlockSpec((s, LANES), lambda pr: (0, pr)),
                  pl.BlockSpec((s, LANES), lambda pr: (0, kcol + pr)),
                  pl.BlockSpec((s, LANES), lambda pr: (0, 2 * kcol + pr)),
                  pl.BlockSpec((s, LANES), lambda pr: (0, pr)),
                  pl.BlockSpec((2, 1, s), lambda pr: (pr, 0, 0))],
        out_specs=pl.BlockSpec((s, LANES), lambda pr: (0, pr)),
        scratch_shapes=[pltpu.VMEM((2, _V_ROWS, s), BF16), pltpu.VMEM((2, _V_ROWS, t), F32),
                        pltpu.VMEM((2, t // 2, t), F32), pltpu.VMEM((2, t // 2, t), F32),
                        pltpu.VMEM((2, 2 * LANES, t), BF16), pltpu.VMEM((2, 1, t), F32)],
        compiler_params=_params(("arbitrary",), vmem),
        name="fox_attention",
    )(qkv, qkv, qkv, aug, cq)


def _ssd_kernel(xs_ref, z_ref, bc_ref, dt_ref, dtT_ref, cwx_ref, cbx_ref, cwbc_ref, cbbc_ref,
                dtb_ref, dtbT_ref, alog_ref, alogT_ref, dskip_ref, nw_ref, y_ref,
                xbuf, bcbuf, xtail_sc, bctail_sc, state_sc, ybuf, *, L):
    halo = SUBLANES

    @pl.when(pl.program_id(0) == 0)
    def _():
        xtail_sc[...] = jnp.zeros_like(xtail_sc)
        bctail_sc[...] = jnp.zeros_like(bctail_sc)
        state_sc[...] = jnp.zeros_like(state_sc)

    def conv_silu(u_ref, tail_sc, buf, w_ref, b_ref):
        cur = u_ref[...]
        prev = tail_sc[...]
        tail_sc[...] = cur[L - halo:L, :]
        last = SSD_CONV - 1

        def taps(c, backs):
            acc = b_ref[...] + c * w_ref[last:last + 1, :]
            for k in range(1, SSD_CONV):
                acc = acc + backs[k - 1] * w_ref[last - k:last - k + 1, :]
            return acc

        buf[...] = taps(cur, [pltpu.roll(cur, k, axis=0) for k in range(1, SSD_CONV)])
        top = cur[0:halo, :]
        r = lax.broadcasted_iota(jnp.int32, top.shape, 0)
        buf[0:halo, :] = taps(top, [
            jnp.where(r < k, pltpu.roll(prev, k, axis=0), pltpu.roll(top, k, axis=0))
            for k in range(1, SSD_CONV)])
        return _silu(buf[...])

    xc = conv_silu(xs_ref, xtail_sc, xbuf, cwx_ref, cbx_ref)
    bcc = conv_silu(bc_ref, bctail_sc, bcbuf, cwbc_ref, cbbc_ref)

    dt = _softplus(dt_ref[...] + dtb_ref[...])
    dtT = _softplus(dtT_ref[...] + dtbT_ref[...])
    a = dt * (-jnp.exp(alog_ref[...]))
    aT = dtT * (-jnp.exp(alogT_ref[...]))

    row = lax.broadcasted_iota(jnp.int32, (L, L), 0)
    col = lax.broadcasted_iota(jnp.int32, (L, L), 1)
    causal = col <= row
    a_cum = _tri_left(causal.astype(BF16), a)
    a_cumT = _tri_right(aT, (row <= col).astype(BF16))
    a_last = a_cum[L - 1:L, :]
    a_lastT = a_cumT[:, L - 1:L]
    exp_acum = jnp.exp(a_cum)
    w_endT = jnp.exp(a_lastT - a_cumT) * dtT
    chunk_decay = jnp.exp(a_last)

    n = SSD_STATE
    heads_per_group = SSD_HEADS // SSD_GROUPS
    for g in range(SSD_GROUPS):
        bg = bcc[:, g * n:(g + 1) * n]
        cg = bcc[:, SSD_GROUPS * n + g * n:SSD_GROUPS * n + (g + 1) * n]
        bgT = bg.T
        cb = lax.dot_general(cg.astype(BF16), bg.astype(BF16), (((1,), (1,)), ((), ())),
                             preferred_element_type=F32)
        for hh in range(heads_per_group):
            h = g * heads_per_group + hh
            cols = slice(h * HEAD_DIM, (h + 1) * HEAD_DIM)
            seg = a_cum[:, h:h + 1] - a_cumT[h:h + 1, :]
            decay = jnp.exp(jnp.where(causal, seg, -jnp.inf))
            mix = cb * decay * dtT[h:h + 1, :]
            xh = xc[:, cols]
            xh_b = xh.astype(BF16)
            st = state_sc[:, cols]
            y = jnp.dot(mix.astype(BF16), xh_b, preferred_element_type=F32)
            y = y + jnp.dot((cg * exp_acum[:, h:h + 1]).astype(BF16), st.astype(BF16),
                            preferred_element_type=F32)
            ybuf[:, cols] = y + dskip_ref[:, cols] * xh
            upd = jnp.dot((bgT * w_endT[h:h + 1, :]).astype(BF16), xh_b,
                          preferred_element_type=F32)
            state_sc[:, cols] = chunk_decay[:, h:h + 1] * st + upd

    y = ybuf[...] * _silu(z_ref[...])
    y_ref[...] = _rms(y, nw_ref[...]).astype(y_ref.dtype)


def _ssd(u2, dt_raw, dt_rawT, cwx, cbx, cwbc, cbbc, dtb, dtbT, alog, alogT, dskip, nw, L=256):
    s = u2.shape[0]
    w = SSD_WIDTH
    full = lambda shape: pl.BlockSpec(shape, lambda i: (0,) * len(shape))
    vmem = 2 * (2 * L * w * 4 + L * _BC_WIDTH * 4 + L * w * 2) + (2 * L + 16) * w * 4 \
        + (L + 8) * _BC_WIDTH * 4 + SSD_STATE * w * 4 + 24 * L * L * 4 + 6 * L * w * 4
    return pl.pallas_call(
        functools.partial(_ssd_kernel, L=L),
        out_shape=jax.ShapeDtypeStruct((s, w), BF16),
        grid=(s // L,),
        in_specs=[pl.BlockSpec((L, w), lambda i: (i, 0)),
                  pl.BlockSpec((L, w), lambda i: (i, 1)),
                  pl.BlockSpec((L, _BC_WIDTH), lambda i: (i, 2 * w // _BC_WIDTH)),
                  pl.BlockSpec((L, SSD_HEADS), lambda i: (i, 0)),
                  pl.BlockSpec((SSD_HEADS, L), lambda i: (0, i)),
                  full((SSD_CONV, w)), full((1, w)),
                  full((SSD_CONV, _BC_WIDTH)), full((1, _BC_WIDTH)),
                  full((1, SSD_HEADS)), full((SSD_HEADS, 1)),
                  full((1, SSD_HEADS)), full((SSD_HEADS, 1)),
                  full((1, w)), full((1, w))],
        out_specs=pl.BlockSpec((L, w), lambda i: (i, 0)),
        scratch_shapes=[pltpu.VMEM((L, w), F32),
                        pltpu.VMEM((L, _BC_WIDTH), F32),
                        pltpu.VMEM((SUBLANES, w), F32),
                        pltpu.VMEM((SUBLANES, _BC_WIDTH), F32),
                        pltpu.VMEM((SSD_STATE, w), F32),
                        pltpu.VMEM((L, w), F32)],
        compiler_params=_params(("arbitrary",), vmem),
        name="ssd_mixer",
    )(u2, u2, u2, dt_raw, dt_rawT, cwx, cbx, cwbc, cbbc, dtb, dtbT, alog, alogT, dskip, nw)


def _out_proj_kernel(x_ref, ya_ref, yb_ref, wa_ref, wb_ref, nw_ref, o_ref, h_ref):
    o = (x_ref[...]
         + jnp.dot(ya_ref[...], wa_ref[...], preferred_element_type=F32)
         + jnp.dot(yb_ref[...], wb_ref[...], preferred_element_type=F32))
    o_ref[...] = o
    h_ref[...] = _rms(o, nw_ref[...]).astype(h_ref.dtype)


def _out_proj(x, ya, yb, wa, wb, nw, tm=512):
    m, d = x.shape
    ka, kb = ya.shape[1], yb.shape[1]
    vmem = 2 * (2 * tm * d * 4 + tm * d * 2 + tm * (ka + kb) * 2 + (ka + kb) * d * 2) \
        + 4 * tm * d * 4
    return pl.pallas_call(
        _out_proj_kernel,
        out_shape=(jax.ShapeDtypeStruct((m, d), F32), jax.ShapeDtypeStruct((m, d), BF16)),
        grid=(m // tm,),
        in_specs=[pl.BlockSpec((tm, d), lambda i: (i, 0)),
                  pl.BlockSpec((tm, ka), lambda i: (i, 0)),
                  pl.BlockSpec((tm, kb), lambda i: (i, 0)),
                  pl.BlockSpec((ka, d), lambda i: (0, 0)),
                  pl.BlockSpec((kb, d), lambda i: (0, 0)),
                  pl.BlockSpec((1, d), lambda i: (0, 0))],
        out_specs=(pl.BlockSpec((tm, d), lambda i: (i, 0)),
                   pl.BlockSpec((tm, d), lambda i: (i, 0))),
        compiler_params=_params(("arbitrary",), vmem),
        name="mix_out_proj",
    )(x, ya, yb, wa, wb, nw)


def _gate_up_kernel(h_ref, wg_ref, wu_ref, cw_ref, cb_ref, o_ref, wg_sc, wu_sc, tail_sc, conv_sc):
    halo = SUBLANES
    tm = conv_sc.shape[0]

    @pl.when(pl.program_id(1) == 0)
    def _():
        wg_sc[...] = wg_ref[...].astype(BF16)
        wu_sc[...] = wu_ref[...].astype(BF16)
        tail_sc[...] = jnp.zeros_like(tail_sc)

    h = h_ref[...]
    g = jnp.dot(h, wg_sc[...], preferred_element_type=F32)
    up = jnp.dot(h, wu_sc[...], preferred_element_type=F32)

    prev = tail_sc[...]
    tail_sc[...] = g[tm - halo:tm, :]

    def conv(cur, back1, back2):
        return (cb_ref[...] + cur * cw_ref[2:3, :] + back1 * cw_ref[1:2, :]
                + back2 * cw_ref[0:1, :])

    conv_sc[...] = conv(g, pltpu.roll(g, 1, axis=0), pltpu.roll(g, 2, axis=0))
    top = g[0:halo, :]
    r = lax.broadcasted_iota(jnp.int32, top.shape, 0)
    back1 = jnp.where(r < 1, pltpu.roll(prev, 1, axis=0), pltpu.roll(top, 1, axis=0))
    back2 = jnp.where(r < 2, pltpu.roll(prev, 2, axis=0), pltpu.roll(top, 2, axis=0))
    conv_sc[0:halo, :] = conv(top, back1, back2)
    o_ref[...] = (_silu(conv_sc[...]) * up).astype(o_ref.dtype)


def _gate_up(h, w_gu, cw, cb, tm=1024, tn=512):
    m, d = h.shape
    nj = D_FF // tn
    vmem = 2 * (tm * d * 2 + 2 * d * tn * 4 + tm * tn * 2) + 2 * d * tn * 2 \
        + 10 * tm * tn * 4
    return pl.pallas_call(
        _gate_up_kernel,
        out_shape=jax.ShapeDtypeStruct((m, D_FF), BF16),
        grid=(nj, m // tm),
        in_specs=[pl.BlockSpec((tm, d), lambda j, i: (i, 0)),
                  pl.BlockSpec((d, tn), lambda j, i: (0, j)),
                  pl.BlockSpec((d, tn), lambda j, i: (0, j + nj)),
                  pl.BlockSpec((FFN_CONV, tn), lambda j, i: (0, j)),
                  pl.BlockSpec((1, tn), lambda j, i: (0, j))],
        out_specs=pl.BlockSpec((tm, tn), lambda j, i: (i, j)),
        scratch_shapes=[pltpu.VMEM((d, tn), BF16), pltpu.VMEM((d, tn), BF16),
                        pltpu.VMEM((SUBLANES, tn), F32), pltpu.VMEM((tm, tn), F32)],
        compiler_params=_params(("arbitrary", "arbitrary"), vmem),
        name="ffn_gate_up",
    )(h, w_gu, w_gu, cw, cb)


def _down_kernel(a_ref, w_ref, x_ref, o_ref, w_sc):
    @pl.when(pl.program_id(1) == 0)
    def _():
        w_sc[...] = w_ref[...].astype(BF16)

    o_ref[...] = x_ref[...] + jnp.dot(a_ref[...], w_sc[...], preferred_element_type=F32)


def _down(act, w, x, tm=512, tn=512):
    m, d = x.shape
    kk = act.shape[1]
    vmem = 2 * (tm * kk * 2 + kk * tn * 4 + 2 * tm * tn * 4) + kk * tn * 2 + 2 * tm * tn * 4
    return pl.pallas_call(
        _down_kernel,
        out_shape=jax.ShapeDtypeStruct((m, d), F32),
        grid=(d // tn, m // tm),
        in_specs=[pl.BlockSpec((tm, kk), lambda j, i: (i, 0)),
                  pl.BlockSpec((kk, tn), lambda j, i: (0, j)),
                  pl.BlockSpec((tm, tn), lambda j, i: (i, j))],
        out_specs=pl.BlockSpec((tm, tn), lambda j, i: (i, j)),
        scratch_shapes=[pltpu.VMEM((kk, tn), BF16)],
        compiler_params=_params(("arbitrary", "arbitrary"), vmem),
        name="ffn_down",
    )(act, w, x)


def _ple_kernel(x_ref, p_ref, nw_ref, wg_ref, wp_ref, fw_ref, o_ref, *, final):
    x = x_ref[...]
    h = _rms(x, nw_ref[...]).astype(BF16)
    gate = jax.nn.sigmoid(jnp.dot(h, wg_ref[...], preferred_element_type=F32))
    proj = jnp.dot(p_ref[...].astype(BF16), wp_ref[...], preferred_element_type=F32)
    x = x + gate * proj
    if final:
        x = _rms(x, fw_ref[...])
    o_ref[...] = x


def _ple(x, p, nw, wg, wp, fw, final, tm=512):
    m, d = x.shape
    dp = p.shape[1]
    vmem = 2 * (2 * tm * d * 4 + tm * dp * 4 + d * d * 2 + dp * d * 2) + 6 * tm * d * 4
    return pl.pallas_call(
        functools.partial(_ple_kernel, final=final),
        out_shape=jax.ShapeDtypeStruct((m, d), F32),
        grid=(m // tm,),
        in_specs=[pl.BlockSpec((tm, d), lambda i: (i, 0)),
                  pl.BlockSpec((tm, dp), lambda i: (i, 0)),
                  pl.BlockSpec((1, d), lambda i: (0, 0)),
                  pl.BlockSpec((d, d), lambda i: (0, 0)),
                  pl.BlockSpec((dp, d), lambda i: (0, 0)),
                  pl.BlockSpec((1, d), lambda i: (0, 0))],
        out_specs=pl.BlockSpec((tm, d), lambda i: (i, 0)),
        compiler_params=_params(("arbitrary",), vmem),
        name="ple_gate",
    )(x, p, nw, wg, wp, fw)


def _layer(x, p, mix_norm_w, w_in, fox_forget_bias, ssd_conv_w, ssd_conv_b, ssd_dt_bias,
           ssd_A_log, ssd_D, ssd_norm_w, w_out, ffn_norm_w, w_gate_up, ffn_conv_w,
           ffn_conv_b, w_down, ple_norm_w, w_ple_gate, w_ple_proj, final_norm_w, final):
    s = x.shape[0]
    row = lambda v: v.reshape(1, -1).astype(F32)

    w_inT = jnp.swapaxes(w_in, 0, 1).astype(F32)
    w_u2T = jnp.concatenate(
        [w_inT[_OFF_XS:_OFF_B], w_inT[_OFF_Z:_OFF_DT], w_inT[_OFF_B:_OFF_Z],
         w_inT[_OFF_F:_OFF_XS], w_inT[_OFF_DT:_IN_COLS],
         jnp.zeros((LANES - FOX_HEADS - SSD_HEADS, D_MODEL), F32)], axis=0)

    h_mix = _norm(x, row(mix_norm_w))
    qkv = _proj(h_mix, w_inT, _OFF_F, BF16, tm=1024, tn=512, name="in_proj_qkv")
    u2 = _proj(h_mix, w_u2T, _U2_COLS, F32, tm=1024, tn=_U2_COLS // 3, name="in_proj_ssd")

    b_pad = jnp.zeros((1, LANES), F32).at[0, :FOX_HEADS].set(fox_forget_bias.astype(F32))
    c, aug = _fox_c(u2, b_pad)
    cT = c[:, :FOX_HEADS].T
    y_fox = _fox_attn(qkv, aug, cT[:, None, :])

    dt_raw = u2[:, _U2_SMALL + FOX_HEADS:_U2_SMALL + FOX_HEADS + SSD_HEADS]
    col = lambda v: v.reshape(-1, 1).astype(F32)
    y_ssd = _ssd(u2, dt_raw, dt_raw.T,
                 ssd_conv_w[:, :SSD_WIDTH].astype(F32), row(ssd_conv_b[:SSD_WIDTH]),
                 ssd_conv_w[:, SSD_WIDTH:].astype(F32), row(ssd_conv_b[SSD_WIDTH:]),
                 row(ssd_dt_bias), col(ssd_dt_bias), row(ssd_A_log), col(ssd_A_log),
                 row(jnp.repeat(ssd_D, HEAD_DIM)), row(ssd_norm_w))

    w_out_b = w_out.astype(BF16)
    x, h_ffn = _out_proj(x, y_fox, y_ssd, w_out_b[:FOX_WIDTH], w_out_b[FOX_WIDTH:],
                         row(ffn_norm_w))

    act = _gate_up(h_ffn, w_gate_up.astype(F32), ffn_conv_w.astype(F32), row(ffn_conv_b))
    x = _down(act, w_down.astype(F32), x)

    return _ple(x, p, row(ple_norm_w), w_ple_gate.astype(BF16), w_ple_proj.astype(BF16),
                row(final_norm_w), final)


def kernel(x, p, mix_norm_w, w_in, fox_forget_bias, ssd_conv_w, ssd_conv_b, ssd_dt_bias,
           ssd_A_log, ssd_D, ssd_norm_w, w_out, ffn_norm_w, w_gate_up, ffn_conv_w,
           ffn_conv_b, w_down, ple_norm_w, w_ple_gate, w_ple_proj, final_norm_w):
    bsz, s, d = x.shape
    depth = p.shape[0]
    outs = []
    for b in range(bsz):
        xb = x[b]
        for i in range(depth):
            xb = _layer(xb, p[i, b], mix_norm_w[i], w_in[i], fox_forget_bias[i], ssd_conv_w[i],
                        ssd_conv_b[i], ssd_dt_bias[i], ssd_A_log[i], ssd_D[i], ssd_norm_w[i],
                        w_out[i], ffn_norm_w[i], w_gate_up[i], ffn_conv_w[i], ffn_conv_b[i],
                        w_down[i], ple_norm_w[i], w_ple_gate[i], w_ple_proj[i], final_norm_w,
                        final=(i == depth - 1))
        outs.append(xb)
    return outs[0][None] if bsz == 1 else jnp.stack(outs, axis=0)
```

```python
import functools

import numpy as np
import jax
import jax.numpy as jnp
from jax import lax
from jax.experimental import pallas as pl
from jax.experimental.pallas import tpu as pltpu

F32 = jnp.float32
BF16 = jnp.bfloat16

D_MODEL = 2048
SEQ = 8192
D_PLE = 256
HEAD_DIM = 64
FOX_WIDTH = 1024
FOX_HEADS = 16
SSD_WIDTH = 1024
SSD_HEADS = 16
SSD_GROUPS = 2
SSD_STATE = 128
SSD_CONV = 4
D_FF = 5632
FFN_CONV = 3
EPS = 1e-6
LOG2_E = 1.4426950408889634

_OFF_Q = 0
_OFF_F = 3 * FOX_WIDTH
_OFF_XS = _OFF_F + FOX_HEADS
_OFF_B = _OFF_XS + SSD_WIDTH
_OFF_C = _OFF_B + SSD_GROUPS * SSD_STATE
_OFF_Z = _OFF_C + SSD_GROUPS * SSD_STATE
_OFF_DT = _OFF_Z + SSD_WIDTH
_IN_COLS = _OFF_DT + SSD_HEADS

LANES = 128
SUBLANES = 8
VMEM_CAP = 60 * 1024 * 1024

_BC_WIDTH = 2 * SSD_GROUPS * SSD_STATE
_U2_COLS = 2 * SSD_WIDTH + _BC_WIDTH + LANES
_U2_SMALL = 2 * SSD_WIDTH + _BC_WIDTH


def _params(sem, vmem_bytes):
    return pltpu.CompilerParams(dimension_semantics=sem,
                                vmem_limit_bytes=int(min(VMEM_CAP, vmem_bytes)))


def _rms(xf, w):
    ms = jnp.mean(xf * xf, axis=-1, keepdims=True)
    return xf * lax.rsqrt(ms + EPS) * w


def _split3(a):
    hi = a.astype(BF16)
    r1 = a - hi.astype(F32)
    mid = r1.astype(BF16)
    lo = (r1 - mid.astype(F32)).astype(BF16)
    return hi, mid, lo


def _tri_left(tri, a):
    return sum(jnp.dot(tri, t, preferred_element_type=F32) for t in _split3(a))


def _tri_right(a, tri):
    return sum(jnp.dot(t, tri, preferred_element_type=F32) for t in _split3(a))


def _softplus(v):
    return jnp.maximum(v, 0.0) + jnp.log1p(jnp.exp(-jnp.abs(v)))


def _silu(v):
    return v * jax.nn.sigmoid(v)


def _norm_kernel(x_ref, nw_ref, h_ref):
    h_ref[...] = _rms(x_ref[...], nw_ref[...]).astype(h_ref.dtype)


def _norm(x, nw, tm=512):
    m, d = x.shape
    return pl.pallas_call(
        _norm_kernel,
        out_shape=jax.ShapeDtypeStruct((m, d), BF16),
        grid=(m // tm,),
        in_specs=[pl.BlockSpec((tm, d), lambda i: (i, 0)),
                  pl.BlockSpec((1, d), lambda i: (0, 0))],
        out_specs=pl.BlockSpec((tm, d), lambda i: (i, 0)),
        compiler_params=_params(("arbitrary",), 2 * tm * d * 6 + 4 * tm * d * 4),
        name="mix_norm",
    )(x, nw)


def _proj_kernel(h_ref, wT_ref, o_ref, w_sc):
    @pl.when(pl.program_id(1) == 0)
    def _():
        w_sc[...] = wT_ref[...].astype(BF16)

    o_ref[...] = lax.dot_general(h_ref[...], w_sc[...], (((1,), (1,)), ((), ())),
                                 preferred_element_type=F32).astype(o_ref.dtype)


def _proj(h, wT, n_cols, out_dtype, tm, tn, name):
    m, k = h.shape
    ob = jnp.dtype(out_dtype).itemsize
    vmem = 2 * (tm * k * 2 + k * tn * 4 + tm * tn * ob) + k * tn * 2 + 2 * tm * tn * 4
    return pl.pallas_call(
        _proj_kernel,
        out_shape=jax.ShapeDtypeStruct((m, n_cols), out_dtype),
        grid=(n_cols // tn, m // tm),
        in_specs=[pl.BlockSpec((tm, k), lambda j, i: (i, 0)),
                  pl.BlockSpec((tn, k), lambda j, i: (j, 0))],
        out_specs=pl.BlockSpec((tm, tn), lambda j, i: (i, j)),
        scratch_shapes=[pltpu.VMEM((tn, k), BF16)],
        compiler_params=_params(("arbitrary", "arbitrary"), vmem),
        name=name,
    )(h, wT)


_AUG_ONES = 6
_N_PAIRS = FOX_HEADS // 2


def _aug_selector():
    sel = np.zeros((4 * LANES, _N_PAIRS * LANES), np.float32)
    for pr in range(_N_PAIRS):
        for e in range(2):
            for j in range(3):
                sel[j * LANES + 2 * pr + e, pr * LANES + 3 * e + j] = -1.0
        sel[3 * LANES, pr * LANES + _AUG_ONES:pr * LANES + _AUG_ONES + 3] = 1.0
    return jnp.asarray(sel, BF16)


def _fox_c_kernel(f_ref, b_ref, sel_ref, c_ref, aug_ref, carry_sc):
    @pl.when(pl.program_id(0) == 0)
    def _():
        carry_sc[...] = jnp.zeros_like(carry_sc)

    z = f_ref[...] + b_ref[...]
    lf = jnp.minimum(z, 0.0) - jnp.log1p(jnp.exp(-jnp.abs(z)))
    t = lf.shape[0]
    row = lax.broadcasted_iota(jnp.int32, (t, t), 0)
    col = lax.broadcasted_iota(jnp.int32, (t, t), 1)
    tril = (col <= row).astype(BF16)
    cum = _tri_left(tril, lf) + carry_sc[...]
    c_ref[...] = cum
    carry_sc[...] = cum[t - 1:t, :]
    hi, mid, lo = _split3(cum * LOG2_E)
    parts = jnp.concatenate([hi, mid, lo, jnp.ones_like(hi)], axis=1)
    aug_ref[...] = jnp.dot(parts, sel_ref[...], preferred_element_type=F32).astype(BF16)


def _fox_c(u2, b_pad, t=256):
    s = u2.shape[0]
    aug_cols = _N_PAIRS * LANES
    return pl.pallas_call(
        _fox_c_kernel,
        out_shape=(jax.ShapeDtypeStruct((s, LANES), F32),
                   jax.ShapeDtypeStruct((s, aug_cols), BF16)),
        grid=(s // t,),
        in_specs=[pl.BlockSpec((t, LANES), lambda i: (i, _U2_SMALL // LANES)),
                  pl.BlockSpec((1, LANES), lambda i: (0, 0)),
                  pl.BlockSpec((4 * LANES, aug_cols), lambda i: (0, 0))],
        out_specs=(pl.BlockSpec((t, LANES), lambda i: (i, 0)),
                   pl.BlockSpec((t, aug_cols), lambda i: (i, 0))),
        scratch_shapes=[pltpu.VMEM((1, LANES), F32)],
        compiler_params=_params(("arbitrary",), 16 * 1024 * 1024),
        name="fox_cumlogf",
    )(u2, b_pad, _aug_selector())


_V_ROWS = HEAD_DIM + 16


def _fox_attn_kernel(q_ref, k_ref, v_ref, aug_ref, cq_ref, o_ref, vT_sc, acc_sc, s0_sc, s1_sc,
                     w_sc, m_sc, smax0_sc, smax1_sc, *, t):
    qi = pl.program_id(1)
    n_kv = k_ref.shape[0] // t

    @pl.when(qi == 0)
    def _():
        pad_row = lax.broadcasted_iota(jnp.int32, (_V_ROWS - HEAD_DIM, t), 0)
        ones_rows = jnp.where(pad_row == 0, 1.0, 0.0).astype(BF16)

        def transpose_v(c, carry):
            st = pl.multiple_of(c * t, t)
            vT = v_ref[pl.ds(st, t), :].astype(F32).T.astype(BF16)
            for e in range(2):
                vT_sc[e, 0:HEAD_DIM, pl.ds(st, t)] = vT[e * HEAD_DIM:(e + 1) * HEAD_DIM, :]
                vT_sc[e, HEAD_DIM:_V_ROWS, pl.ds(st, t)] = ones_rows
            return carry
        lax.fori_loop(0, n_kv, transpose_v, 0)

    scale = HEAD_DIM ** -0.5 * LOG2_E
    qT = q_ref[...].astype(F32).T * scale
    row = lax.broadcasted_iota(jnp.int32, (LANES, t), 0)
    tk = s0_sc.shape[1]
    key_idx = lax.broadcasted_iota(jnp.int32, (tk, t), 0)
    qry_idx = lax.broadcasted_iota(jnp.int32, (tk, t), 1)

    ws = []
    for e in range(2):
        qTe = jnp.where(row // HEAD_DIM == e, qT, 0.0)
        cq_hi, cq_mid, cq_lo = (part.astype(F32)
                                for part in _split3(cq_ref[e] * LOG2_E))
        aug_rows = jnp.where(
            row // 3 == e, 1.0,
            jnp.where(row == _AUG_ONES, cq_hi,
                      jnp.where(row == _AUG_ONES + 1, cq_mid,
                                jnp.where(row == _AUG_ONES + 2, cq_lo, 0.0))))
        ws.append(jnp.concatenate([qTe, aug_rows], axis=0).astype(BF16))
        acc_sc[e] = jnp.zeros((_V_ROWS, t), F32)

    def scores_into(j, s_sc, smax_sc):
        start = pl.multiple_of(j * tk, tk)
        keys = jnp.concatenate([k_ref[pl.ds(start, tk), :], aug_ref[pl.ds(start, tk), :]],
                               axis=1)
        for e in range(2):
            s = jnp.dot(keys, ws[e], preferred_element_type=F32)
            s_sc[e] = s
            smax_sc[e] = jnp.max(s, axis=0, keepdims=True)

    def softmax_pv(j, s_sc, smax_sc, m_prev, masked):
        start = pl.multiple_of(j * tk, tk)
        if masked:
            visible = key_idx + start <= qry_idx + qi * t
        m_out, probs, alphas = [], [], []
        for e in range(2):
            s = s_sc[e]
            if masked:
                s = jnp.where(visible, s, -jnp.inf)
                m_new = jnp.maximum(m_prev[e], jnp.max(s, axis=0, keepdims=True))
            else:
                m_new = jnp.maximum(m_prev[e], smax_sc[e])
            probs.append(jnp.exp2(s - m_new).astype(BF16))
            alphas.append(jnp.exp2(m_prev[e] - m_new))
            m_out.append(m_new)
        for e in range(2):
            acc_sc[e] = alphas[e] * acc_sc[e] + jnp.dot(
                vT_sc[e, :, pl.ds(start, tk)], probs[e], preferred_element_type=F32)
        return tuple(m_out)

    def two_blocks(i, m):
        scores_into(2 * i + 1, s1_sc, smax1_sc)
        m = softmax_pv(2 * i, s0_sc, smax0_sc, m, False)
        scores_into(2 * i + 2, s0_sc, smax0_sc)
        return softmax_pv(2 * i + 1, s1_sc, smax1_sc, m, False)

    def four_blocks(i, m):
        return two_blocks(2 * i + 1, two_blocks(2 * i, m))

    m_init = jnp.full((1, t), -1e30, F32)
    scores_into(0, s0_sc, smax0_sc)
    m = lax.fori_loop(0, qi // 2, four_blocks, (m_init, m_init))
    m = lax.cond(qi % 2 == 1, lambda mm: two_blocks(qi - 1, mm), lambda mm: mm, m)
    upper = slice(tk, t)
    last = pl.multiple_of((2 * qi + 1) * tk, tk)
    last_keys = jnp.concatenate([k_ref[pl.ds(last, tk), :], aug_ref[pl.ds(last, tk), :]], axis=1)
    for e in range(2):
        w_sc[e] = ws[e]
        s1_sc[e, :, 0:tk] = jnp.dot(last_keys, w_sc[e, :, upper], preferred_element_type=F32)
    m = softmax_pv(2 * qi, s0_sc, smax0_sc, m, True)
    visible = (lax.broadcasted_iota(jnp.int32, (tk, tk), 0)
               <= lax.broadcasted_iota(jnp.int32, (tk, tk), 1))
    for e in range(2):
        s = jnp.where(visible, s1_sc[e, :, 0:tk], -jnp.inf)
        m_sc[e] = m[e]
        m_upper = m_sc[e, :, upper]
        m_new = jnp.maximum(m_upper, jnp.max(s, axis=0, keepdims=True))
        p = jnp.exp2(s - m_new).astype(BF16)
        acc_sc[e, :, upper] = jnp.exp2(m_upper - m_new) * acc_sc[e, :, upper] + jnp.dot(
            vT_sc[e, :, pl.ds(last, tk)], p, preferred_element_type=F32)
    outs = []
    for e in range(2):
        acc = acc_sc[e]
        outs.append(acc[0:HEAD_DIM, :] / acc[HEAD_DIM:HEAD_DIM + 1, :])
    o_ref[...] = jnp.concatenate(outs, axis=0).T.astype(o_ref.dtype)


def _fox_attn(qkv, aug, cq, t=512):
    s = qkv.shape[0]
    kcol = FOX_WIDTH // LANES
    vmem = 2 * (2 * t * LANES * 2 + 3 * s * LANES * 2 + 2 * t * 4) \
        + 2 * _V_ROWS * (s * 2 + t * 4) + 10 * t * t * 4
    return pl.pallas_call(
        functools.partial(_fox_attn_kernel, t=t),
        out_shape=jax.ShapeDtypeStruct((s, FOX_WIDTH), BF16),
        grid=(_N_PAIRS, s // t),
        in_specs=[pl.BlockSpec((t, LANES), lambda pr, qi: (qi, pr)),
                  pl.BlockSpec((s, LANES), lambda pr, qi: (0, kcol + pr)),
                  pl.BlockSpec((s, LANES), lambda pr, qi: (0, 2 * kcol + pr)),
                  pl.BlockSpec((s, LANES), lambda pr, qi: (0, pr)),
                  pl.BlockSpec((2, 1, t), lambda pr, qi: (pr, 0, qi))],
        out_specs=pl.BlockSpec((t, LANES), lambda pr, qi: (qi, pr)),
        scratch_shapes=[pltpu.VMEM((2, _V_ROWS, s), BF16), pltpu.VMEM((2, _V_ROWS, t), F32),
                        pltpu.VMEM((2, t // 2, t), F32), pltpu.VMEM((2, t // 2, t), F32),
                        pltpu.VMEM((2, 2 * LANES, t), BF16), pltpu.VMEM((2, 1, t), F32),
                        pltpu.VMEM((2, 1, t), F32), pltpu.VMEM((2, 1, t), F32)],
        compiler_params=_params(("arbitrary", "arbitrary"), vmem),
        name="fox_attention",
    )(qkv, qkv, qkv, aug, cq)


def _ssd_kernel(xs_ref, z_ref, bc_ref, dt_ref, dtT_ref, cwx_ref, cbx_ref, cwbc_ref, cbbc_ref,
                dtb_ref, dtbT_ref, alog_ref, alogT_ref, dskip_ref, nw_ref, y_ref,
                xbuf, bcbuf, xtail_sc, bctail_sc, state_sc, ybuf, *, L):
    halo = SUBLANES

    @pl.when(pl.program_id(0) == 0)
    def _():
        xtail_sc[...] = jnp.zeros_like(xtail_sc)
        bctail_sc[...] = jnp.zeros_like(bctail_sc)
        state_sc[...] = jnp.zeros_like(state_sc)

    def conv_silu(u_ref, tail_sc, buf, w_ref, b_ref):
        cur = u_ref[...]
        prev = tail_sc[...]
        tail_sc[...] = cur[L - halo:L, :]
        last = SSD_CONV - 1

        def taps(c, backs):
            acc = b_ref[...] + c * w_ref[last:last + 1, :]
            for k in range(1, SSD_CONV):
                acc = acc + backs[k - 1] * w_ref[last - k:last - k + 1, :]
            return acc

        buf[...] = taps(cur, [pltpu.roll(cur, k, axis=0) for k in range(1, SSD_CONV)])
        top = cur[0:halo, :]
        r = lax.broadcasted_iota(jnp.int32, top.shape, 0)
        buf[0:halo, :] = taps(top, [
            jnp.where(r < k, pltpu.roll(prev, k, axis=0), pltpu.roll(top, k, axis=0))
            for k in range(1, SSD_CONV)])
        return _silu(buf[...])

    xc = conv_silu(xs_ref, xtail_sc, xbuf, cwx_ref, cbx_ref)
    bcc = conv_silu(bc_ref, bctail_sc, bcbuf, cwbc_ref, cbbc_ref)

    dt = _softplus(dt_ref[...] + dtb_ref[...])
    dtT = _softplus(dtT_ref[...] + dtbT_ref[...])
    a = dt * (-jnp.exp(alog_ref[...]))
    aT = dtT * (-jnp.exp(alogT_ref[...]))

    row = lax.broadcasted_iota(jnp.int32, (L, L), 0)
    col = lax.broadcasted_iota(jnp.int32, (L, L), 1)
    causal = col <= row
    a_cum = _tri_left(causal.astype(BF16), a)
    a_cumT = _tri_right(aT, (row <= col).astype(BF16))
    a_last = a_cum[L - 1:L, :]
    a_lastT = a_cumT[:, L - 1:L]
    exp_acum = jnp.exp(a_cum)
    w_endT = jnp.exp(a_lastT - a_cumT) * dtT
    chunk_decay = jnp.exp(a_last)

    n = SSD_STATE
    heads_per_group = SSD_HEADS // SSD_GROUPS
    for g in range(SSD_GROUPS):
        bg = bcc[:, g * n:(g + 1) * n]
        cg = bcc[:, SSD_GROUPS * n + g * n:SSD_GROUPS * n + (g + 1) * n]
        bgT = bg.T
        cb = lax.dot_general(cg.astype(BF16), bg.astype(BF16), (((1,), (1,)), ((), ())),
                             preferred_element_type=F32)
        for hh in range(heads_per_group):
            h = g * heads_per_group + hh
            cols = slice(h * HEAD_DIM, (h + 1) * HEAD_DIM)
            seg = a_cum[:, h:h + 1] - a_cumT[h:h + 1, :]
            decay = jnp.exp(jnp.where(causal, seg, -jnp.inf))
            mix = cb * decay * dtT[h:h + 1, :]
            xh = xc[:, cols]
            xh_b = xh.astype(BF16)
            st = state_sc[:, cols]
            y = jnp.dot(mix.astype(BF16), xh_b, preferred_element_type=F32)
            y = y + jnp.dot((cg * exp_acum[:, h:h + 1]).astype(BF16), st.astype(BF16),
                            preferred_element_type=F32)
            ybuf[:, cols] = y + dskip_ref[:, cols] * xh
            upd = jnp.dot((bgT * w_endT[h:h + 1, :]).astype(BF16), xh_b,
                          preferred_element_type=F32)
            state_sc[:, cols] = chunk_decay[:, h:h + 1] * st + upd

    y = ybuf[...] * _silu(z_ref[...])
    y_ref[...] = _rms(y, nw_ref[...]).astype(y_ref.dtype)


def _ssd(u2, dt_raw, dt_rawT, cwx, cbx, cwbc, cbbc, dtb, dtbT, alog, alogT, dskip, nw, L=256):
    s = u2.shape[0]
    w = SSD_WIDTH
    full = lambda shape: pl.BlockSpec(shape, lambda i: (0,) * len(shape))
    vmem = 2 * (2 * L * w * 4 + L * _BC_WIDTH * 4 + L * w * 2) + (2 * L + 16) * w * 4 \
        + (L + 8) * _BC_WIDTH * 4 + SSD_STATE * w * 4 + 24 * L * L * 4 + 6 * L * w * 4
    return pl.pallas_call(
        functools.partial(_ssd_kernel, L=L),
        out_shape=jax.ShapeDtypeStruct((s, w), BF16),
        grid=(s // L,),
        in_specs=[pl.BlockSpec((L, w), lambda i: (i, 0)),
                  pl.BlockSpec((L, w), lambda i: (i, 1)),
                  pl.BlockSpec((L, _BC_WIDTH), lambda i: (i, 2 * w // _BC_WIDTH)),
                  pl.BlockSpec((L, SSD_HEADS), lambda i: (i, 0)),
                  pl.BlockSpec((SSD_HEADS, L), lambda i: (0, i)),
                  full((SSD_CONV, w)), full((1, w)),
                  full((SSD_CONV, _BC_WIDTH)), full((1, _BC_WIDTH)),
                  full((1, SSD_HEADS)), full((SSD_HEADS, 1)),
                  full((1, SSD_HEADS)), full((SSD_HEADS, 1)),
                  full((1, w)), full((1, w))],
        out_specs=pl.BlockSpec((L, w), lambda i: (i, 0)),
        scratch_shapes=[pltpu.VMEM((L, w), F32),
                        pltpu.VMEM((L, _BC_WIDTH), F32),
                        pltpu.VMEM((SUBLANES, w), F32),
                        pltpu.VMEM((SUBLANES, _BC_WIDTH), F32),
                        pltpu.VMEM((SSD_STATE, w), F32),
                        pltpu.VMEM((L, w), F32)],
        compiler_params=_params(("arbitrary",), vmem),
        name="ssd_mixer",
    )(u2, u2, u2, dt_raw, dt_rawT, cwx, cbx, cwbc, cbbc, dtb, dtbT, alog, alogT, dskip, nw)


def _out_proj_kernel(x_ref, ya_ref, yb_ref, wa_ref, wb_ref, nw_ref, o_ref, h_ref):
    o = (x_ref[...]
         + jnp.dot(ya_ref[...], wa_ref[...], preferred_element_type=F32)
         + jnp.dot(yb_ref[...], wb_ref[...], preferred_element_type=F32))
    o_ref[...] = o
    h_ref[...] = _rms(o, nw_ref[...]).astype(h_ref.dtype)


def _out_proj(x, ya, yb, wa, wb, nw, tm=512):
    m, d = x.shape
    ka, kb = ya.shape[1], yb.shape[1]
    vmem = 2 * (2 * tm * d * 4 + tm * d * 2 + tm * (ka + kb) * 2 + (ka + kb) * d * 2) \
        + 4 * tm * d * 4
    return pl.pallas_call(
        _out_proj_kernel,
        out_shape=(jax.ShapeDtypeStruct((m, d), F32), jax.ShapeDtypeStruct((m, d), BF16)),
        grid=(m // tm,),
        in_specs=[pl.BlockSpec((tm, d), lambda i: (i, 0)),
                  pl.BlockSpec((tm, ka), lambda i: (i, 0)),
                  pl.BlockSpec((tm, kb), lambda i: (i, 0)),
                  pl.BlockSpec((ka, d), lambda i: (0, 0)),
                  pl.BlockSpec((kb, d), lambda i: (0, 0)),
                  pl.BlockSpec((1, d), lambda i: (0, 0))],
        out_specs=(pl.BlockSpec((tm, d), lambda i: (i, 0)),
                   pl.BlockSpec((tm, d), lambda i: (i, 0))),
        compiler_params=_params(("arbitrary",), vmem),
        name="mix_out_proj",
    )(x, ya, yb, wa, wb, nw)


def _gate_up_kernel(h_ref, wg_ref, wu_ref, cw_ref, cb_ref, o_ref, wg_sc, wu_sc, tail_sc, conv_sc):
    halo = SUBLANES
    tm = conv_sc.shape[0]

    @pl.when(pl.program_id(1) == 0)
    def _():
        wg_sc[...] = wg_ref[...].astype(BF16)
        wu_sc[...] = wu_ref[...].astype(BF16)
        tail_sc[...] = jnp.zeros_like(tail_sc)

    h = h_ref[...]
    g = jnp.dot(h, wg_sc[...], preferred_element_type=F32)
    up = jnp.dot(h, wu_sc[...], preferred_element_type=F32)

    prev = tail_sc[...]
    tail_sc[...] = g[tm - halo:tm, :]

    def conv(cur, back1, back2):
        return (cb_ref[...] + cur * cw_ref[2:3, :] + back1 * cw_ref[1:2, :]
                + back2 * cw_ref[0:1, :])

    conv_sc[...] = conv(g, pltpu.roll(g, 1, axis=0), pltpu.roll(g, 2, axis=0))
    top = g[0:halo, :]
    r = lax.broadcasted_iota(jnp.int32, top.shape, 0)
    back1 = jnp.where(r < 1, pltpu.roll(prev, 1, axis=0), pltpu.roll(top, 1, axis=0))
    back2 = jnp.where(r < 2, pltpu.roll(prev, 2, axis=0), pltpu.roll(top, 2, axis=0))
    conv_sc[0:halo, :] = conv(top, back1, back2)
    o_ref[...] = (_silu(conv_sc[...]) * up).astype(o_ref.dtype)


def _gate_up(h, w_gu, cw, cb, tm=1024, tn=512):
    m, d = h.shape
    nj = D_FF // tn
    vmem = 2 * (tm * d * 2 + 2 * d * tn * 4 + tm * tn * 2) + 2 * d * tn * 2 \
        + 10 * tm * tn * 4
    return pl.pallas_call(
        _gate_up_kernel,
        out_shape=jax.ShapeDtypeStruct((m, D_FF), BF16),
        grid=(nj, m // tm),
        in_specs=[pl.BlockSpec((tm, d), lambda j, i: (i, 0)),
                  pl.BlockSpec((d, tn), lambda j, i: (0, j)),
                  pl.BlockSpec((d, tn), lambda j, i: (0, j + nj)),
                  pl.BlockSpec((FFN_CONV, tn), lambda j, i: (0, j)),
                  pl.BlockSpec((1, tn), lambda j, i: (0, j))],
        out_specs=pl.BlockSpec((tm, tn), lambda j, i: (i, j)),
        scratch_shapes=[pltpu.VMEM((d, tn), BF16), pltpu.VMEM((d, tn), BF16),
                        pltpu.VMEM((SUBLANES, tn), F32), pltpu.VMEM((tm, tn), F32)],
        compiler_params=_params(("arbitrary", "arbitrary"), vmem),
        name="ffn_gate_up",
    )(h, w_gu, w_gu, cw, cb)


def _down_kernel(a_ref, w_ref, x_ref, o_ref, w_sc):
    @pl.when(pl.program_id(1) == 0)
    def _():
        w_sc[...] = w_ref[...].astype(BF16)

    o_ref[...] = x_ref[...] + jnp.dot(a_ref[...], w_sc[...], preferred_element_type=F32)


def _down(act, w, x, tm=512, tn=512):
    m, d = x.shape
    kk = act.shape[1]
    vmem = 2 * (tm * kk * 2 + kk * tn * 4 + 2 * tm * tn * 4) + kk * tn * 2 + 2 * tm * tn * 4
    return pl.pallas_call(
        _down_kernel,
        out_shape=jax.ShapeDtypeStruct((m, d), F32),
        grid=(d // tn, m // tm),
        in_specs=[pl.BlockSpec((tm, kk), lambda j, i: (i, 0)),
                  pl.BlockSpec((kk, tn), lambda j, i: (0, j)),
                  pl.BlockSpec((tm, tn), lambda j, i: (i, j))],
        out_specs=pl.BlockSpec((tm, tn), lambda j, i: (i, j)),
        scratch_shapes=[pltpu.VMEM((kk, tn), BF16)],
        compiler_params=_params(("arbitrary", "arbitrary"), vmem),
        name="ffn_down",
    )(act, w, x)


def _ple_kernel(x_ref, p_ref, nw_ref, wg_ref, wp_ref, fw_ref, o_ref, *, final):
    x = x_ref[...]
    h = _rms(x, nw_ref[...]).astype(BF16)
    gate = jax.nn.sigmoid(jnp.dot(h, wg_ref[...], preferred_element_type=F32))
    proj = jnp.dot(p_ref[...].astype(BF16), wp_ref[...], preferred_element_type=F32)
    x = x + gate * proj
    if final:
        x = _rms(x, fw_ref[...])
    o_ref[...] = x


def _ple(x, p, nw, wg, wp, fw, final, tm=512):
    m, d = x.shape
    dp = p.shape[1]
    vmem = 2 * (2 * tm * d * 4 + tm * dp * 4 + d * d * 2 + dp * d * 2) + 6 * tm * d * 4
    return pl.pallas_call(
        functools.partial(_ple_kernel, final=final),
        out_shape=jax.ShapeDtypeStruct((m, d), F32),
        grid=(m // tm,),
        in_specs=[pl.BlockSpec((tm, d), lambda i: (i, 0)),
                  pl.BlockSpec((tm, dp), lambda i: (i, 0)),
                  pl.BlockSpec((1, d), lambda i: (0, 0)),
                  pl.BlockSpec((d, d), lambda i: (0, 0)),
                  pl.BlockSpec((dp, d), lambda i: (0, 0)),
                  pl.BlockSpec((1, d), lambda i: (0, 0))],
        out_specs=pl.BlockSpec((tm, d), lambda i: (i, 0)),
        compiler_params=_params(("arbitrary",), vmem),
        name="ple_gate",
    )(x, p, nw, wg, wp, fw)


def _layer(x, p, mix_norm_w, w_in, fox_forget_bias, ssd_conv_w, ssd_conv_b, ssd_dt_bias,
           ssd_A_log, ssd_D, ssd_norm_w, w_out, ffn_norm_w, w_gate_up, ffn_conv_w,
           ffn_conv_b, w_down, ple_norm_w, w_ple_gate, w_ple_proj, final_norm_w, final):
    s = x.shape[0]
    row = lambda v: v.reshape(1, -1).astype(F32)

    w_inT = jnp.swapaxes(w_in, 0, 1).astype(F32)
    w_u2T = jnp.concatenate(
        [w_inT[_OFF_XS:_OFF_B], w_inT[_OFF_Z:_OFF_DT], w_inT[_OFF_B:_OFF_Z],
         w_inT[_OFF_F:_OFF_XS], w_inT[_OFF_DT:_IN_COLS],
         jnp.zeros((LANES - FOX_HEADS - SSD_HEADS, D_MODEL), F32)], axis=0)

    h_mix = _norm(x, row(mix_norm_w))
    qkv = _proj(h_mix, w_inT, _OFF_F, BF16, tm=1024, tn=512, name="in_proj_qkv")
    u2 = _proj(h_mix, w_u2T, _U2_COLS, F32, tm=1024, tn=_U2_COLS // 3, name="in_proj_ssd")

    b_pad = jnp.zeros((1, LANES), F32).at[0, :FOX_HEADS].set(fox_forget_bias.astype(F32))
    c, aug = _fox_c(u2, b_pad)
    cT = c[:, :FOX_HEADS].T
    y_fox = _fox_attn(qkv, aug, cT[:, None, :])

    dt_raw = u2[:, _U2_SMALL + FOX_HEADS:_U2_SMALL + FOX_HEADS + SSD_HEADS]
    col = lambda v: v.reshape(-1, 1).astype(F32)
    y_ssd = _ssd(u2, dt_raw, dt_raw.T,
                 ssd_conv_w[:, :SSD_WIDTH].astype(F32), row(ssd_conv_b[:SSD_WIDTH]),
                 ssd_conv_w[:, SSD_WIDTH:].astype(F32), row(ssd_conv_b[SSD_WIDTH:]),
                 row(ssd_dt_bias), col(ssd_dt_bias), row(ssd_A_log), col(ssd_A_log),
                 row(jnp.repeat(ssd_D, HEAD_DIM)), row(ssd_norm_w))

    w_out_b = w_out.astype(BF16)
    x, h_ffn = _out_proj(x, y_fox, y_ssd, w_out_b[:FOX_WIDTH], w_out_b[FOX_WIDTH:],
                         row(ffn_norm_w))

    act = _gate_up(h_ffn, w_gate_up.astype(F32), ffn_conv_w.astype(F32), row(ffn_conv_b))
    x = _down(act, w_down.astype(F32), x)

    return _ple(x, p, row(ple_norm_w), w_ple_gate.astype(BF16), w_ple_proj.astype(BF16),
                row(final_norm_w), final)


def kernel(x, p, mix_norm_w, w_in, fox_forget_bias, ssd_conv_w, ssd_conv_b, ssd_dt_bias,
           ssd_A_log, ssd_D, ssd_norm_w, w_out, ffn_norm_w, w_gate_up, ffn_conv_w,
           ffn_conv_b, w_down, ple_norm_w, w_ple_gate, w_ple_proj, final_norm_w):
    bsz, s, d = x.shape
    depth = p.shape[0]
    outs = []
    for b in range(bsz):
        xb = x[b]
        for i in range(depth):
            xb = _layer(xb, p[i, b], mix_norm_w[i], w_in[i], fox_forget_bias[i], ssd_conv_w[i],
                        ssd_conv_b[i], ssd_dt_bias[i], ssd_A_log[i], ssd_D[i], ssd_norm_w[i],
                        w_out[i], ffn_norm_w[i], w_gate_up[i], ffn_conv_w[i], ffn_conv_b[i],
                        w_down[i], ple_norm_w[i], w_ple_gate[i], w_ple_proj[i], final_norm_w,
                        final=(i == depth - 1))
        outs.append(xb)
    return outs[0][None] if bsz == 1 else jnp.stack(outs, axis=0)
```

```python
import functools

import numpy as np
import jax
import jax.numpy as jnp
from jax import lax
from jax.experimental import pallas as pl
from jax.experimental.pallas import tpu as pltpu

F32 = jnp.float32
BF16 = jnp.bfloat16

D_MODEL = 2048
SEQ = 8192
D_PLE = 256
HEAD_DIM = 64
FOX_WIDTH = 1024
FOX_HEADS = 16
SSD_WIDTH = 1024
SSD_HEADS = 16
SSD_GROUPS = 2
SSD_STATE = 128
SSD_CONV = 4
D_FF = 5632
FFN_CONV = 3
EPS = 1e-6
LOG2_E = 1.4426950408889634

_OFF_Q = 0
_OFF_F = 3 * FOX_WIDTH
_OFF_XS = _OFF_F + FOX_HEADS
_OFF_B = _OFF_XS + SSD_WIDTH
_OFF_C = _OFF_B + SSD_GROUPS * SSD_STATE
_OFF_Z = _OFF_C + SSD_GROUPS * SSD_STATE
_OFF_DT = _OFF_Z + SSD_WIDTH
_IN_COLS = _OFF_DT + SSD_HEADS

LANES = 128
SUBLANES = 8
VMEM_CAP = 60 * 1024 * 1024

_BC_WIDTH = 2 * SSD_GROUPS * SSD_STATE
_U2_COLS = 2 * SSD_WIDTH + _BC_WIDTH + LANES
_U2_SMALL = 2 * SSD_WIDTH + _BC_WIDTH


def _params(sem, vmem_bytes):
    return pltpu.CompilerParams(dimension_semantics=sem,
                                vmem_limit_bytes=int(min(VMEM_CAP, vmem_bytes)))


def _rms(xf, w):
    ms = jnp.mean(xf * xf, axis=-1, keepdims=True)
    return xf * lax.rsqrt(ms + EPS) * w


def _split3(a):
    hi = a.astype(BF16)
    r1 = a - hi.astype(F32)
    mid = r1.astype(BF16)
    lo = (r1 - mid.astype(F32)).astype(BF16)
    return hi, mid, lo


def _tri_left(tri, a):
    return sum(jnp.dot(tri, t, preferred_element_type=F32) for t in _split3(a))


def _tri_right(a, tri):
    return sum(jnp.dot(t, tri, preferred_element_type=F32) for t in _split3(a))


def _softplus(v):
    return jnp.maximum(v, 0.0) + jnp.log1p(jnp.exp(-jnp.abs(v)))


def _silu(v):
    return v * jax.nn.sigmoid(v)


def _norm_kernel(x_ref, nw_ref, h_ref):
    h_ref[...] = _rms(x_ref[...], nw_ref[...]).astype(h_ref.dtype)


def _norm(x, nw, tm=512):
    m, d = x.shape
    return pl.pallas_call(
        _norm_kernel,
        out_shape=jax.ShapeDtypeStruct((m, d), BF16),
        grid=(m // tm,),
        in_specs=[pl.BlockSpec((tm, d), lambda i: (i, 0)),
                  pl.BlockSpec((1, d), lambda i: (0, 0))],
        out_specs=pl.BlockSpec((tm, d), lambda i: (i, 0)),
        compiler_params=_params(("arbitrary",), 2 * tm * d * 6 + 4 * tm * d * 4),
        name="mix_norm",
    )(x, nw)


def _proj_kernel(h_ref, wT_ref, o_ref, w_sc):
    @pl.when(pl.program_id(1) == 0)
    def _():
        w_sc[...] = wT_ref[...].astype(BF16)

    o_ref[...] = lax.dot_general(h_ref[...], w_sc[...], (((1,), (1,)), ((), ())),
                                 preferred_element_type=F32).astype(o_ref.dtype)


def _proj(h, wT, n_cols, out_dtype, tm, tn, name):
    m, k = h.shape
    ob = jnp.dtype(out_dtype).itemsize
    vmem = 2 * (tm * k * 2 + k * tn * 4 + tm * tn * ob) + k * tn * 2 + 2 * tm * tn * 4
    return pl.pallas_call(
        _proj_kernel,
        out_shape=jax.ShapeDtypeStruct((m, n_cols), out_dtype),
        grid=(n_cols // tn, m // tm),
        in_specs=[pl.BlockSpec((tm, k), lambda j, i: (i, 0)),
                  pl.BlockSpec((tn, k), lambda j, i: (j, 0))],
        out_specs=pl.BlockSpec((tm, tn), lambda j, i: (i, j)),
        scratch_shapes=[pltpu.VMEM((tn, k), BF16)],
        compiler_params=_params(("arbitrary", "arbitrary"), vmem),
        name=name,
    )(h, wT)


_AUG_ONES = 6
_N_PAIRS = FOX_HEADS // 2


def _aug_selector():
    sel = np.zeros((4 * LANES, _N_PAIRS * LANES), np.float32)
    for pr in range(_N_PAIRS):
        for e in range(2):
            for j in range(3):
                sel[j * LANES + 2 * pr + e, pr * LANES + 3 * e + j] = -1.0
        sel[3 * LANES, pr * LANES + _AUG_ONES:pr * LANES + _AUG_ONES + 3] = 1.0
    return jnp.asarray(sel, BF16)


def _fox_c_kernel(f_ref, b_ref, sel_ref, c_ref, aug_ref, carry_sc):
    @pl.when(pl.program_id(0) == 0)
    def _():
        carry_sc[...] = jnp.zeros_like(carry_sc)

    z = f_ref[...] + b_ref[...]
    lf = jnp.minimum(z, 0.0) - jnp.log1p(jnp.exp(-jnp.abs(z)))
    t = lf.shape[0]
    row = lax.broadcasted_iota(jnp.int32, (t, t), 0)
    col = lax.broadcasted_iota(jnp.int32, (t, t), 1)
    tril = (col <= row).astype(BF16)
    cum = _tri_left(tril, lf) + carry_sc[...]
    c_ref[...] = cum
    carry_sc[...] = cum[t - 1:t, :]
    hi, mid, lo = _split3(cum * LOG2_E)
    parts = jnp.concatenate([hi, mid, lo, jnp.ones_like(hi)], axis=1)
    aug_ref[...] = jnp.dot(parts, sel_ref[...], preferred_element_type=F32).astype(BF16)


def _fox_c(u2, b_pad, t=256):
    s = u2.shape[0]
    aug_cols = _N_PAIRS * LANES
    return pl.pallas_call(
        _fox_c_kernel,
        out_shape=(jax.ShapeDtypeStruct((s, LANES), F32),
                   jax.ShapeDtypeStruct((s, aug_cols), BF16)),
        grid=(s // t,),
        in_specs=[pl.BlockSpec((t, LANES), lambda i: (i, _U2_SMALL // LANES)),
                  pl.BlockSpec((1, LANES), lambda i: (0, 0)),
                  pl.BlockSpec((4 * LANES, aug_cols), lambda i: (0, 0))],
        out_specs=(pl.BlockSpec((t, LANES), lambda i: (i, 0)),
                   pl.BlockSpec((t, aug_cols), lambda i: (i, 0))),
        scratch_shapes=[pltpu.VMEM((1, LANES), F32)],
        compiler_params=_params(("arbitrary",), 16 * 1024 * 1024),
        name="fox_cumlogf",
    )(u2, b_pad, _aug_selector())


_V_ROWS = HEAD_DIM + 16


def _fox_attn_kernel(q_ref, k_ref, v_ref, aug_ref, cq_ref, o_ref, vT_sc, acc_sc, s0_sc, s1_sc,
                     w_sc, m_sc, smax0_sc, smax1_sc, *, t):
    qi = pl.program_id(1)
    n_kv = k_ref.shape[0] // t

    @pl.when(qi == 0)
    def _():
        pad_row = lax.broadcasted_iota(jnp.int32, (_V_ROWS - HEAD_DIM, t), 0)
        ones_rows = jnp.where(pad_row == 0, 1.0, 0.0).astype(BF16)

        def transpose_v(c, carry):
            st = pl.multiple_of(c * t, t)
            vT = v_ref[pl.ds(st, t), :].astype(F32).T.astype(BF16)
            for e in range(2):
                vT_sc[e, 0:HEAD_DIM, pl.ds(st, t)] = vT[e * HEAD_DIM:(e + 1) * HEAD_DIM, :]
                vT_sc[e, HEAD_DIM:_V_ROWS, pl.ds(st, t)] = ones_rows
            return carry
        lax.fori_loop(0, n_kv, transpose_v, 0)

    scale = HEAD_DIM ** -0.5 * LOG2_E
    qT = q_ref[...].astype(F32).T * scale
    row = lax.broadcasted_iota(jnp.int32, (LANES, t), 0)
    tk = s0_sc.shape[1]
    key_idx = lax.broadcasted_iota(jnp.int32, (tk, t), 0)
    qry_idx = lax.broadcasted_iota(jnp.int32, (tk, t), 1)

    ws = []
    for e in range(2):
        qTe = jnp.where(row // HEAD_DIM == e, qT, 0.0)
        cq_hi, cq_mid, cq_lo = (part.astype(F32)
                                for part in _split3(cq_ref[e] * LOG2_E))
        top = lax.broadcasted_iota(jnp.int32, (2 * SUBLANES, t), 0)
        aug_rows = jnp.where(
            top // 3 == e, 1.0,
            jnp.where(top == _AUG_ONES, cq_hi,
                      jnp.where(top == _AUG_ONES + 1, cq_mid,
                                jnp.where(top == _AUG_ONES + 2, cq_lo, 0.0))))
        unused = jnp.zeros((LANES - 2 * SUBLANES, t), F32)
        ws.append(jnp.concatenate([qTe, aug_rows, unused], axis=0).astype(BF16))
        acc_sc[e] = jnp.zeros((_V_ROWS, t), F32)

    def scores_into(j, s_sc, smax_sc):
        start = pl.multiple_of(j * tk, tk)
        keys = jnp.concatenate([k_ref[pl.ds(start, tk), :], aug_ref[pl.ds(start, tk), :]],
                               axis=1)
        for e in range(2):
            s = jnp.dot(keys, ws[e], preferred_element_type=F32)
            s_sc[e] = s
            smax_sc[e] = jnp.max(s, axis=0, keepdims=True)

    def softmax_pv(j, s_sc, smax_sc, m_prev, masked):
        start = pl.multiple_of(j * tk, tk)
        if masked:
            visible = key_idx + start <= qry_idx + qi * t
        m_out, probs, alphas = [], [], []
        for e in range(2):
            s = s_sc[e]
            if masked:
                s = jnp.where(visible, s, -jnp.inf)
                m_new = jnp.maximum(m_prev[e], jnp.max(s, axis=0, keepdims=True))
            else:
                m_new = jnp.maximum(m_prev[e], smax_sc[e])
            probs.append(jnp.exp2(s - m_new).astype(BF16))
            alphas.append(jnp.exp2(m_prev[e] - m_new))
            m_out.append(m_new)
        for e in range(2):
            acc_sc[e] = alphas[e] * acc_sc[e] + jnp.dot(
                vT_sc[e, :, pl.ds(start, tk)], probs[e], preferred_element_type=F32)
        return tuple(m_out)

    def two_blocks(i, m):
        scores_into(2 * i + 1, s1_sc, smax1_sc)
        m = softmax_pv(2 * i, s0_sc, smax0_sc, m, False)
        scores_into(2 * i + 2, s0_sc, smax0_sc)
        return softmax_pv(2 * i + 1, s1_sc, smax1_sc, m, False)

    def eight_blocks(i, m):
        for r in range(4):
            m = two_blocks(4 * i + r, m)
        return m

    m_init = jnp.full((1, t), -1e30, F32)
    scores_into(0, s0_sc, smax0_sc)
    m = lax.fori_loop(0, qi // 4, eight_blocks, (m_init, m_init))
    m = lax.fori_loop(4 * (qi // 4), qi, two_blocks, m)
    upper = slice(tk, t)
    last = pl.multiple_of((2 * qi + 1) * tk, tk)
    last_keys = jnp.concatenate([k_ref[pl.ds(last, tk), :], aug_ref[pl.ds(last, tk), :]], axis=1)
    for e in range(2):
        w_sc[e] = ws[e]
        s1_sc[e, :, 0:tk] = jnp.dot(last_keys, w_sc[e, :, upper], preferred_element_type=F32)
    m = softmax_pv(2 * qi, s0_sc, smax0_sc, m, True)
    visible = (lax.broadcasted_iota(jnp.int32, (tk, tk), 0)
               <= lax.broadcasted_iota(jnp.int32, (tk, tk), 1))
    for e in range(2):
        s = jnp.where(visible, s1_sc[e, :, 0:tk], -jnp.inf)
        m_sc[e] = m[e]
        m_upper = m_sc[e, :, upper]
        m_new = jnp.maximum(m_upper, jnp.max(s, axis=0, keepdims=True))
        p = jnp.exp2(s - m_new).astype(BF16)
        acc_sc[e, :, upper] = jnp.exp2(m_upper - m_new) * acc_sc[e, :, upper] + jnp.dot(
            vT_sc[e, :, pl.ds(last, tk)], p, preferred_element_type=F32)
    outs = []
    for e in range(2):
        acc = acc_sc[e]
        outs.append(acc[0:HEAD_DIM, :] / acc[HEAD_DIM:HEAD_DIM + 1, :])
    o_ref[...] = jnp.concatenate(outs, axis=0).T.astype(o_ref.dtype)


def _fox_attn(qkv, aug, cq, t=512):
    s = qkv.shape[0]
    kcol = FOX_WIDTH // LANES
    vmem = 2 * (2 * t * LANES * 2 + 3 * s * LANES * 2 + 2 * t * 4) \
        + 2 * _V_ROWS * (s * 2 + t * 4) + 10 * t * t * 4
    return pl.pallas_call(
        functools.partial(_fox_attn_kernel, t=t),
        out_shape=jax.ShapeDtypeStruct((s, FOX_WIDTH), BF16),
        grid=(_N_PAIRS, s // t),
        in_specs=[pl.BlockSpec((t, LANES), lambda pr, qi: (qi, pr)),
                  pl.BlockSpec((s, LANES), lambda pr, qi: (0, kcol + pr)),
                  pl.BlockSpec((s, LANES), lambda pr, qi: (0, 2 * kcol + pr)),
                  pl.BlockSpec((s, LANES), lambda pr, qi: (0, pr)),
                  pl.BlockSpec((2, 1, t), lambda pr, qi: (pr, 0, qi))],
        out_specs=pl.BlockSpec((t, LANES), lambda pr, qi: (qi, pr)),
        scratch_shapes=[pltpu.VMEM((2, _V_ROWS, s), BF16), pltpu.VMEM((2, _V_ROWS, t), F32),
                        pltpu.VMEM((2, t // 2, t), F32), pltpu.VMEM((2, t // 2, t), F32),
                        pltpu.VMEM((2, 2 * LANES, t), BF16), pltpu.VMEM((2, 1, t), F32),
                        pltpu.VMEM((2, 1, t), F32), pltpu.VMEM((2, 1, t), F32)],
        compiler_params=_params(("arbitrary", "arbitrary"), vmem),
        name="fox_attention",
    )(qkv, qkv, qkv, aug, cq)


def _ssd_kernel(xs_ref, z_ref, bc_ref, dt_ref, dtT_ref, cwx_ref, cbx_ref, cwbc_ref, cbbc_ref,
                dtb_ref, dtbT_ref, alog_ref, alogT_ref, dskip_ref, nw_ref, y_ref,
                xbuf, bcbuf, xtail_sc, bctail_sc, state_sc, ybuf, *, L):
    halo = SUBLANES

    @pl.when(pl.program_id(0) == 0)
    def _():
        xtail_sc[...] = jnp.zeros_like(xtail_sc)
        bctail_sc[...] = jnp.zeros_like(bctail_sc)
        state_sc[...] = jnp.zeros_like(state_sc)

    def conv_silu(u_ref, tail_sc, buf, w_ref, b_ref):
        cur = u_ref[...]
        prev = tail_sc[...]
        tail_sc[...] = cur[L - halo:L, :]
        last = SSD_CONV - 1

        def taps(c, backs):
            acc = b_ref[...] + c * w_ref[last:last + 1, :]
            for k in range(1, SSD_CONV):
                acc = acc + backs[k - 1] * w_ref[last - k:last - k + 1, :]
            return acc

        buf[...] = taps(cur, [pltpu.roll(cur, k, axis=0) for k in range(1, SSD_CONV)])
        top = cur[0:halo, :]
        r = lax.broadcasted_iota(jnp.int32, top.shape, 0)
        buf[0:halo, :] = taps(top, [
            jnp.where(r < k, pltpu.roll(prev, k, axis=0), pltpu.roll(top, k, axis=0))
            for k in range(1, SSD_CONV)])
        return _silu(buf[...])

    xc = conv_silu(xs_ref, xtail_sc, xbuf, cwx_ref, cbx_ref)
    bcc = conv_silu(bc_ref, bctail_sc, bcbuf, cwbc_ref, cbbc_ref)

    dt = _softplus(dt_ref[...] + dtb_ref[...])
    dtT = _softplus(dtT_ref[...] + dtbT_ref[...])
    a = dt * (-jnp.exp(alog_ref[...]))
    aT = dtT * (-jnp.exp(alogT_ref[...]))

    row = lax.broadcasted_iota(jnp.int32, (L, L), 0)
    col = lax.broadcasted_iota(jnp.int32, (L, L), 1)
    causal = col <= row
    a_cum = _tri_left(causal.astype(BF16), a)
    a_cumT = _tri_right(aT, (row <= col).astype(BF16))
    a_last = a_cum[L - 1:L, :]
    a_lastT = a_cumT[:, L - 1:L]
    exp_acum = jnp.exp(a_cum)
    w_endT = jnp.exp(a_lastT - a_cumT) * dtT
    chunk_decay = jnp.exp(a_last)

    n = SSD_STATE
    heads_per_group = SSD_HEADS // SSD_GROUPS
    for g in range(SSD_GROUPS):
        bg = bcc[:, g * n:(g + 1) * n]
        cg = bcc[:, SSD_GROUPS * n + g * n:SSD_GROUPS * n + (g + 1) * n]
        bgT = bg.T
        cb = lax.dot_general(cg.astype(BF16), bg.astype(BF16), (((1,), (1,)), ((), ())),
                             preferred_element_type=F32)
        for hh in range(heads_per_group):
            h = g * heads_per_group + hh
            cols = slice(h * HEAD_DIM, (h + 1) * HEAD_DIM)
            seg = a_cum[:, h:h + 1] - a_cumT[h:h + 1, :]
            decay = jnp.exp(jnp.where(causal, seg, -jnp.inf))
            mix = cb * decay * dtT[h:h + 1, :]
            xh = xc[:, cols]
            xh_b = xh.astype(BF16)
            st = state_sc[:, cols]
            y = jnp.dot(mix.astype(BF16), xh_b, preferred_element_type=F32)
            y = y + jnp.dot((cg * exp_acum[:, h:h + 1]).astype(BF16), st.astype(BF16),
                            preferred_element_type=F32)
            ybuf[:, cols] = y + dskip_ref[:, cols] * xh
            upd = jnp.dot((bgT * w_endT[h:h + 1, :]).astype(BF16), xh_b,
                          preferred_element_type=F32)
            state_sc[:, cols] = chunk_decay[:, h:h + 1] * st + upd

    y = ybuf[...] * _silu(z_ref[...])
    y_ref[...] = _rms(y, nw_ref[...]).astype(y_ref.dtype)


def _ssd(u2, dt_raw, dt_rawT, cwx, cbx, cwbc, cbbc, dtb, dtbT, alog, alogT, dskip, nw, L=256):
    s = u2.shape[0]
    w = SSD_WIDTH
    full = lambda shape: pl.BlockSpec(shape, lambda i: (0,) * len(shape))
    vmem = 2 * (2 * L * w * 4 + L * _BC_WIDTH * 4 + L * w * 2) + (2 * L + 16) * w * 4 \
        + (L + 8) * _BC_WIDTH * 4 + SSD_STATE * w * 4 + 24 * L * L * 4 + 6 * L * w * 4
    return pl.pallas_call(
        functools.partial(_ssd_kernel, L=L),
        out_shape=jax.ShapeDtypeStruct((s, w), BF16),
        grid=(s // L,),
        in_specs=[pl.BlockSpec((L, w), lambda i: (i, 0)),
                  pl.BlockSpec((L, w), lambda i: (i, 1)),
                  pl.BlockSpec((L, _BC_WIDTH), lambda i: (i, 2 * w // _BC_WIDTH)),
                  pl.BlockSpec((L, SSD_HEADS), lambda i: (i, 0)),
                  pl.BlockSpec((SSD_HEADS, L), lambda i: (0, i)),
                  full((SSD_CONV, w)), full((1, w)),
                  full((SSD_CONV, _BC_WIDTH)), full((1, _BC_WIDTH)),
                  full((1, SSD_HEADS)), full((SSD_HEADS, 1)),
                  full((1, SSD_HEADS)), full((SSD_HEADS, 1)),
                  full((1, w)), full((1, w))],
        out_specs=pl.BlockSpec((L, w), lambda i: (i, 0)),
        scratch_shapes=[pltpu.VMEM((L, w), F32),
                        pltpu.VMEM((L, _BC_WIDTH), F32),
                        pltpu.VMEM((SUBLANES, w), F32),
                        pltpu.VMEM((SUBLANES, _BC_WIDTH), F32),
                        pltpu.VMEM((SSD_STATE, w), F32),
                        pltpu.VMEM((L, w), F32)],
        compiler_params=_params(("arbitrary",), vmem),
        name="ssd_mixer",
    )(u2, u2, u2, dt_raw, dt_rawT, cwx, cbx, cwbc, cbbc, dtb, dtbT, alog, alogT, dskip, nw)


def _out_proj_kernel(x_ref, ya_ref, yb_ref, wa_ref, wb_ref, nw_ref, o_ref, h_ref):
    o = (x_ref[...]
         + jnp.dot(ya_ref[...], wa_ref[...], preferred_element_type=F32)
         + jnp.dot(yb_ref[...], wb_ref[...], preferred_element_type=F32))
    o_ref[...] = o
    h_ref[...] = _rms(o, nw_ref[...]).astype(h_ref.dtype)


def _out_proj(x, ya, yb, wa, wb, nw, tm=512):
    m, d = x.shape
    ka, kb = ya.shape[1], yb.shape[1]
    vmem = 2 * (2 * tm * d * 4 + tm * d * 2 + tm * (ka + kb) * 2 + (ka + kb) * d * 2) \
        + 4 * tm * d * 4
    return pl.pallas_call(
        _out_proj_kernel,
        out_shape=(jax.ShapeDtypeStruct((m, d), F32), jax.ShapeDtypeStruct((m, d), BF16)),
        grid=(m // tm,),
        in_specs=[pl.BlockSpec((tm, d), lambda i: (i, 0)),
                  pl.BlockSpec((tm, ka), lambda i: (i, 0)),
                  pl.BlockSpec((tm, kb), lambda i: (i, 0)),
                  pl.BlockSpec((ka, d), lambda i: (0, 0)),
                  pl.BlockSpec((kb, d), lambda i: (0, 0)),
                  pl.BlockSpec((1, d), lambda i: (0, 0))],
        out_specs=(pl.BlockSpec((tm, d), lambda i: (i, 0)),
                   pl.BlockSpec((tm, d), lambda i: (i, 0))),
        compiler_params=_params(("arbitrary",), vmem),
        name="mix_out_proj",
    )(x, ya, yb, wa, wb, nw)


def _gate_up_kernel(h_ref, wg_ref, wu_ref, cw_ref, cb_ref, o_ref, wg_sc, wu_sc, tail_sc, conv_sc):
    halo = SUBLANES
    tm = conv_sc.shape[0]

    @pl.when(pl.program_id(1) == 0)
    def _():
        wg_sc[...] = wg_ref[...].astype(BF16)
        wu_sc[...] = wu_ref[...].astype(BF16)
        tail_sc[...] = jnp.zeros_like(tail_sc)

    h = h_ref[...]
    g = jnp.dot(h, wg_sc[...], preferred_element_type=F32)
    up = jnp.dot(h, wu_sc[...], preferred_element_type=F32)

    prev = tail_sc[...]
    tail_sc[...] = g[tm - halo:tm, :]

    def conv(cur, back1, back2):
        return (cb_ref[...] + cur * cw_ref[2:3, :] + back1 * cw_ref[1:2, :]
                + back2 * cw_ref[0:1, :])

    conv_sc[...] = conv(g, pltpu.roll(g, 1, axis=0), pltpu.roll(g, 2, axis=0))
    top = g[0:halo, :]
    r = lax.broadcasted_iota(jnp.int32, top.shape, 0)
    back1 = jnp.where(r < 1, pltpu.roll(prev, 1, axis=0), pltpu.roll(top, 1, axis=0))
    back2 = jnp.where(r < 2, pltpu.roll(prev, 2, axis=0), pltpu.roll(top, 2, axis=0))
    conv_sc[0:halo, :] = conv(top, back1, back2)
    o_ref[...] = (_silu(conv_sc[...]) * up).astype(o_ref.dtype)


def _gate_up(h, w_gu, cw, cb, tm=1024, tn=512):
    m, d = h.shape
    nj = D_FF // tn
    vmem = 2 * (tm * d * 2 + 2 * d * tn * 4 + tm * tn * 2) + 2 * d * tn * 2 \
        + 10 * tm * tn * 4
    return pl.pallas_call(
        _gate_up_kernel,
        out_shape=jax.ShapeDtypeStruct((m, D_FF), BF16),
        grid=(nj, m // tm),
        in_specs=[pl.BlockSpec((tm, d), lambda j, i: (i, 0)),
                  pl.BlockSpec((d, tn), lambda j, i: (0, j)),
                  pl.BlockSpec((d, tn), lambda j, i: (0, j + nj)),
                  pl.BlockSpec((FFN_CONV, tn), lambda j, i: (0, j)),
                  pl.BlockSpec((1, tn), lambda j, i: (0, j))],
        out_specs=pl.BlockSpec((tm, tn), lambda j, i: (i, j)),
        scratch_shapes=[pltpu.VMEM((d, tn), BF16), pltpu.VMEM((d, tn), BF16),
                        pltpu.VMEM((SUBLANES, tn), F32), pltpu.VMEM((tm, tn), F32)],
        compiler_params=_params(("arbitrary", "arbitrary"), vmem),
        name="ffn_gate_up",
    )(h, w_gu, w_gu, cw, cb)


def _down_kernel(a_ref, w_ref, x_ref, o_ref, w_sc):
    @pl.when(pl.program_id(1) == 0)
    def _():
        w_sc[...] = w_ref[...].astype(BF16)

    o_ref[...] = x_ref[...] + jnp.dot(a_ref[...], w_sc[...], preferred_element_type=F32)


def _down(act, w, x, tm=512, tn=512):
    m, d = x.shape
    kk = act.shape[1]
    vmem = 2 * (tm * kk * 2 + kk * tn * 4 + 2 * tm * tn * 4) + kk * tn * 2 + 2 * tm * tn * 4
    return pl.pallas_call(
        _down_kernel,
        out_shape=jax.ShapeDtypeStruct((m, d), F32),
        grid=(d // tn, m // tm),
        in_specs=[pl.BlockSpec((tm, kk), lambda j, i: (i, 0)),
                  pl.BlockSpec((kk, tn), lambda j, i: (0, j)),
                  pl.BlockSpec((tm, tn), lambda j, i: (i, j))],
        out_specs=pl.BlockSpec((tm, tn), lambda j, i: (i, j)),
        scratch_shapes=[pltpu.VMEM((kk, tn), BF16)],
        compiler_params=_params(("arbitrary", "arbitrary"), vmem),
        name="ffn_down",
    )(act, w, x)


def _ple_kernel(x_ref, p_ref, nw_ref, wg_ref, wp_ref, fw_ref, o_ref, *, final):
    x = x_ref[...]
    h = _rms(x, nw_ref[...]).astype(BF16)
    gate = jax.nn.sigmoid(jnp.dot(h, wg_ref[...], preferred_element_type=F32))
    proj = jnp.dot(p_ref[...].astype(BF16), wp_ref[...], preferred_element_type=F32)
    x = x + gate * proj
    if final:
        x = _rms(x, fw_ref[...])
    o_ref[...] = x


def _ple(x, p, nw, wg, wp, fw, final, tm=512):
    m, d = x.shape
    dp = p.shape[1]
    vmem = 2 * (2 * tm * d * 4 + tm * dp * 4 + d * d * 2 + dp * d * 2) + 6 * tm * d * 4
    return pl.pallas_call(
        functools.partial(_ple_kernel, final=final),
        out_shape=jax.ShapeDtypeStruct((m, d), F32),
        grid=(m // tm,),
        in_specs=[pl.BlockSpec((tm, d), lambda i: (i, 0)),
                  pl.BlockSpec((tm, dp), lambda i: (i, 0)),
                  pl.BlockSpec((1, d), lambda i: (0, 0)),
                  pl.BlockSpec((d, d), lambda i: (0, 0)),
                  pl.BlockSpec((dp, d), lambda i: (0, 0)),
                  pl.BlockSpec((1, d), lambda i: (0, 0))],
        out_specs=pl.BlockSpec((tm, d), lambda i: (i, 0)),
        compiler_params=_params(("arbitrary",), vmem),
        name="ple_gate",
    )(x, p, nw, wg, wp, fw)


def _layer(x, p, mix_norm_w, w_in, fox_forget_bias, ssd_conv_w, ssd_conv_b, ssd_dt_bias,
           ssd_A_log, ssd_D, ssd_norm_w, w_out, ffn_norm_w, w_gate_up, ffn_conv_w,
           ffn_conv_b, w_down, ple_norm_w, w_ple_gate, w_ple_proj, final_norm_w, final):
    s = x.shape[0]
    row = lambda v: v.reshape(1, -1).astype(F32)

    w_inT = jnp.swapaxes(w_in, 0, 1).astype(F32)
    w_u2T = jnp.concatenate(
        [w_inT[_OFF_XS:_OFF_B], w_inT[_OFF_Z:_OFF_DT], w_inT[_OFF_B:_OFF_Z],
         w_inT[_OFF_F:_OFF_XS], w_inT[_OFF_DT:_IN_COLS],
         jnp.zeros((LANES - FOX_HEADS - SSD_HEADS, D_MODEL), F32)], axis=0)

    h_mix = _norm(x, row(mix_norm_w))
    qkv = _proj(h_mix, w_inT, _OFF_F, BF16, tm=1024, tn=512, name="in_proj_qkv")
    u2 = _proj(h_mix, w_u2T, _U2_COLS, F32, tm=1024, tn=_U2_COLS // 3, name="in_proj_ssd")

    b_pad = jnp.zeros((1, LANES), F32).at[0, :FOX_HEADS].set(fox_forget_bias.astype(F32))
    c, aug = _fox_c(u2, b_pad)
    cT = c[:, :FOX_HEADS].T
    y_fox = _fox_attn(qkv, aug, cT[:, None, :])

    dt_raw = u2[:, _U2_SMALL + FOX_HEADS:_U2_SMALL + FOX_HEADS + SSD_HEADS]
    col = lambda v: v.reshape(-1, 1).astype(F32)
    y_ssd = _ssd(u2, dt_raw, dt_raw.T,
                 ssd_conv_w[:, :SSD_WIDTH].astype(F32), row(ssd_conv_b[:SSD_WIDTH]),
                 ssd_conv_w[:, SSD_WIDTH:].astype(F32), row(ssd_conv_b[SSD_WIDTH:]),
                 row(ssd_dt_bias), col(ssd_dt_bias), row(ssd_A_log), col(ssd_A_log),
                 row(jnp.repeat(ssd_D, HEAD_DIM)), row(ssd_norm_w))

    w_out_b = w_out.astype(BF16)
    x, h_ffn = _out_proj(x, y_fox, y_ssd, w_out_b[:FOX_WIDTH], w_out_b[FOX_WIDTH:],
                         row(ffn_norm_w))

    act = _gate_up(h_ffn, w_gate_up.astype(F32), ffn_conv_w.astype(F32), row(ffn_conv_b))
    x = _down(act, w_down.astype(F32), x)

    return _ple(x, p, row(ple_norm_w), w_ple_gate.astype(BF16), w_ple_proj.astype(BF16),
                row(final_norm_w), final)


def kernel(x, p, mix_norm_w, w_in, fox_forget_bias, ssd_conv_w, ssd_conv_b, ssd_dt_bias,
           ssd_A_log, ssd_D, ssd_norm_w, w_out, ffn_norm_w, w_gate_up, ffn_conv_w,
           ffn_conv_b, w_down, ple_norm_w, w_ple_gate, w_ple_proj, final_norm_w):
    bsz, s, d = x.shape
    depth = p.shape[0]
    outs = []
    for b in range(bsz):
        xb = x[b]
        for i in range(depth):
            xb = _layer(xb, p[i, b], mix_norm_w[i], w_in[i], fox_forget_bias[i], ssd_conv_w[i],
                        ssd_conv_b[i], ssd_dt_bias[i], ssd_A_log[i], ssd_D[i], ssd_norm_w[i],
                        w_out[i], ffn_norm_w[i], w_gate_up[i], ffn_conv_w[i], ffn_conv_b[i],
                        w_down[i], ple_norm_w[i], w_ple_gate[i], w_ple_proj[i], final_norm_w,
                        final=(i == depth - 1))
        outs.append(xb)
    return outs[0][None] if bsz == 1 else jnp.stack(outs, axis=0)
```

```python
import functools

import numpy as np
import jax
import jax.numpy as jnp
from jax import lax
from jax.experimental import pallas as pl
from jax.experimental.pallas import tpu as pltpu

F32 = jnp.float32
BF16 = jnp.bfloat16

D_MODEL = 2048
SEQ = 8192
D_PLE = 256
HEAD_DIM = 64
FOX_WIDTH = 1024
FOX_HEADS = 16
SSD_WIDTH = 1024
SSD_HEADS = 16
SSD_GROUPS = 2
SSD_STATE = 128
SSD_CONV = 4
D_FF = 5632
FFN_CONV = 3
EPS = 1e-6
LOG2_E = 1.4426950408889634

_OFF_Q = 0
_OFF_F = 3 * FOX_WIDTH
_OFF_XS = _OFF_F + FOX_HEADS
_OFF_B = _OFF_XS + SSD_WIDTH
_OFF_C = _OFF_B + SSD_GROUPS * SSD_STATE
_OFF_Z = _OFF_C + SSD_GROUPS * SSD_STATE
_OFF_DT = _OFF_Z + SSD_WIDTH
_IN_COLS = _OFF_DT + SSD_HEADS

LANES = 128
SUBLANES = 8
VMEM_CAP = 60 * 1024 * 1024

_BC_WIDTH = 2 * SSD_GROUPS * SSD_STATE
_U2_COLS = 2 * SSD_WIDTH + _BC_WIDTH + LANES
_U2_SMALL = 2 * SSD_WIDTH + _BC_WIDTH


def _params(sem, vmem_bytes):
    return pltpu.CompilerParams(dimension_semantics=sem,
                                vmem_limit_bytes=int(min(VMEM_CAP, vmem_bytes)))


def _rms(xf, w):
    ms = jnp.mean(xf * xf, axis=-1, keepdims=True)
    return xf * lax.rsqrt(ms + EPS) * w


def _split3(a):
    hi = a.astype(BF16)
    r1 = a - hi.astype(F32)
    mid = r1.astype(BF16)
    lo = (r1 - mid.astype(F32)).astype(BF16)
    return hi, mid, lo


def _tri_left(tri, a):
    return sum(jnp.dot(tri, t, preferred_element_type=F32) for t in _split3(a))


def _tri_right(a, tri):
    return sum(jnp.dot(t, tri, preferred_element_type=F32) for t in _split3(a))


def _softplus(v):
    return jnp.maximum(v, 0.0) + jnp.log1p(jnp.exp(-jnp.abs(v)))


def _silu(v):
    return v * jax.nn.sigmoid(v)


def _norm_kernel(x_ref, nw_ref, h_ref):
    h_ref[...] = _rms(x_ref[...], nw_ref[...]).astype(h_ref.dtype)


def _norm(x, nw, tm=512):
    m, d = x.shape
    return pl.pallas_call(
        _norm_kernel,
        out_shape=jax.ShapeDtypeStruct((m, d), BF16),
        grid=(m // tm,),
        in_specs=[pl.BlockSpec((tm, d), lambda i: (i, 0)),
                  pl.BlockSpec((1, d), lambda i: (0, 0))],
        out_specs=pl.BlockSpec((tm, d), lambda i: (i, 0)),
        compiler_params=_params(("arbitrary",), 2 * tm * d * 6 + 4 * tm * d * 4),
        name="mix_norm",
    )(x, nw)


def _proj_kernel(h_ref, wT_ref, o_ref, w_sc):
    @pl.when(pl.program_id(1) == 0)
    def _():
        w_sc[...] = wT_ref[...].astype(BF16)

    o_ref[...] = lax.dot_general(h_ref[...], w_sc[...], (((1,), (1,)), ((), ())),
                                 preferred_element_type=F32).astype(o_ref.dtype)


def _proj(h, wT, n_cols, out_dtype, tm, tn, name):
    m, k = h.shape
    ob = jnp.dtype(out_dtype).itemsize
    vmem = 2 * (tm * k * 2 + k * tn * 4 + tm * tn * ob) + k * tn * 2 + 2 * tm * tn * 4
    return pl.pallas_call(
        _proj_kernel,
        out_shape=jax.ShapeDtypeStruct((m, n_cols), out_dtype),
        grid=(n_cols // tn, m // tm),
        in_specs=[pl.BlockSpec((tm, k), lambda j, i: (i, 0)),
                  pl.BlockSpec((tn, k), lambda j, i: (j, 0))],
        out_specs=pl.BlockSpec((tm, tn), lambda j, i: (i, j)),
        scratch_shapes=[pltpu.VMEM((tn, k), BF16)],
        compiler_params=_params(("arbitrary", "arbitrary"), vmem),
        name=name,
    )(h, wT)


_AUG_ONES = 6
_N_PAIRS = FOX_HEADS // 2


def _aug_selector():
    sel = np.zeros((4 * LANES, _N_PAIRS * LANES), np.float32)
    for pr in range(_N_PAIRS):
        for e in range(2):
            for j in range(3):
                sel[j * LANES + 2 * pr + e, pr * LANES + 3 * e + j] = -1.0
        sel[3 * LANES, pr * LANES + _AUG_ONES:pr * LANES + _AUG_ONES + 3] = 1.0
    return jnp.asarray(sel, BF16)


def _fox_c_kernel(f_ref, b_ref, sel_ref, c_ref, aug_ref, carry_sc):
    @pl.when(pl.program_id(0) == 0)
    def _():
        carry_sc[...] = jnp.zeros_like(carry_sc)

    z = f_ref[...] + b_ref[...]
    lf = jnp.minimum(z, 0.0) - jnp.log1p(jnp.exp(-jnp.abs(z)))
    t = lf.shape[0]
    row = lax.broadcasted_iota(jnp.int32, (t, t), 0)
    col = lax.broadcasted_iota(jnp.int32, (t, t), 1)
    tril = (col <= row).astype(BF16)
    cum = _tri_left(tril, lf) + carry_sc[...]
    c_ref[...] = cum
    carry_sc[...] = cum[t - 1:t, :]
    hi, mid, lo = _split3(cum * LOG2_E)
    parts = jnp.concatenate([hi, mid, lo, jnp.ones_like(hi)], axis=1)
    aug_ref[...] = jnp.dot(parts, sel_ref[...], preferred_element_type=F32).astype(BF16)


def _fox_c(u2, b_pad, t=256):
    s = u2.shape[0]
    aug_cols = _N_PAIRS * LANES
    return pl.pallas_call(
        _fox_c_kernel,
        out_shape=(jax.ShapeDtypeStruct((s, LANES), F32),
                   jax.ShapeDtypeStruct((s, aug_cols), BF16)),
        grid=(s // t,),
        in_specs=[pl.BlockSpec((t, LANES), lambda i: (i, _U2_SMALL // LANES)),
                  pl.BlockSpec((1, LANES), lambda i: (0, 0)),
                  pl.BlockSpec((4 * LANES, aug_cols), lambda i: (0, 0))],
        out_specs=(pl.BlockSpec((t, LANES), lambda i: (i, 0)),
                   pl.BlockSpec((t, aug_cols), lambda i: (i, 0))),
        scratch_shapes=[pltpu.VMEM((1, LANES), F32)],
        compiler_params=_params(("arbitrary",), 16 * 1024 * 1024),
        name="fox_cumlogf",
    )(u2, b_pad, _aug_selector())


_V_ROWS = HEAD_DIM + 16


def _fox_attn_kernel(q_ref, k_ref, v_ref, aug_ref, cq_ref, o_ref, vT_sc, acc_sc, s0_sc, s1_sc,
                     w_sc, m_sc, smax0_sc, smax1_sc, *, t):
    qi = pl.program_id(1)
    n_kv = k_ref.shape[0] // t

    @pl.when(qi == 0)
    def _():
        pad_row = lax.broadcasted_iota(jnp.int32, (_V_ROWS - HEAD_DIM, t), 0)
        ones_rows = jnp.where(pad_row == 0, 1.0, 0.0).astype(BF16)

        def transpose_v(c, carry):
            st = pl.multiple_of(c * t, t)
            vT = v_ref[pl.ds(st, t), :].astype(F32).T.astype(BF16)
            for e in range(2):
                vT_sc[e, 0:HEAD_DIM, pl.ds(st, t)] = vT[e * HEAD_DIM:(e + 1) * HEAD_DIM, :]
                vT_sc[e, HEAD_DIM:_V_ROWS, pl.ds(st, t)] = ones_rows
            return carry
        lax.fori_loop(0, n_kv, transpose_v, 0)

    scale = HEAD_DIM ** -0.5 * LOG2_E
    qT = q_ref[...].astype(F32).T * scale
    row = lax.broadcasted_iota(jnp.int32, (LANES, t), 0)
    tk = s0_sc.shape[1]
    key_idx = lax.broadcasted_iota(jnp.int32, (tk, t), 0)
    qry_idx = lax.broadcasted_iota(jnp.int32, (tk, t), 1)

    ws = []
    for e in range(2):
        qTe = jnp.where(row // HEAD_DIM == e, qT, 0.0)
        cq_hi, cq_mid, cq_lo = (part.astype(F32)
                                for part in _split3(cq_ref[e] * LOG2_E))
        top = lax.broadcasted_iota(jnp.int32, (2 * SUBLANES, t), 0)
        aug_rows = jnp.where(
            top // 3 == e, 1.0,
            jnp.where(top == _AUG_ONES, cq_hi,
                      jnp.where(top == _AUG_ONES + 1, cq_mid,
                                jnp.where(top == _AUG_ONES + 2, cq_lo, 0.0))))
        unused = jnp.zeros((LANES - 2 * SUBLANES, t), F32)
        ws.append(jnp.concatenate([qTe, aug_rows, unused], axis=0).astype(BF16))
        acc_sc[e] = jnp.zeros((_V_ROWS, t), F32)

    def scores_into(j, s_sc, smax_sc):
        start = pl.multiple_of(j * tk, tk)
        keys = jnp.concatenate([k_ref[pl.ds(start, tk), :], aug_ref[pl.ds(start, tk), :]],
                               axis=1)
        for e in range(2):
            s = jnp.dot(keys, ws[e], preferred_element_type=F32)
            s_sc[e] = s
            smax_sc[e] = jnp.max(s, axis=0, keepdims=True)

    def softmax_pv(j, s_sc, smax_sc, m_prev, masked):
        start = pl.multiple_of(j * tk, tk)
        if masked:
            visible = key_idx + start <= qry_idx + qi * t
        m_out, probs, alphas = [], [], []
        for e in range(2):
            s = s_sc[e]
            if masked:
                s = jnp.where(visible, s, -jnp.inf)
                m_new = jnp.maximum(m_prev[e], jnp.max(s, axis=0, keepdims=True))
            else:
                m_new = jnp.maximum(m_prev[e], smax_sc[e])
            probs.append(jnp.exp2(s - m_new).astype(BF16))
            alphas.append(jnp.exp2(m_prev[e] - m_new))
            m_out.append(m_new)
        for e in range(2):
            acc_sc[e] = alphas[e] * acc_sc[e] + jnp.dot(
                vT_sc[e, :, pl.ds(start, tk)], probs[e], preferred_element_type=F32)
        return tuple(m_out)

    def two_blocks(i, m):
        scores_into(2 * i + 1, s1_sc, smax1_sc)
        m = softmax_pv(2 * i, s0_sc, smax0_sc, m, False)
        scores_into(2 * i + 2, s0_sc, smax0_sc)
        return softmax_pv(2 * i + 1, s1_sc, smax1_sc, m, False)

    def eight_blocks(i, m):
        for r in range(4):
            m = two_blocks(4 * i + r, m)
        return m

    m_init = jnp.full((1, t), -1e30, F32)
    scores_into(0, s0_sc, smax0_sc)
    m = lax.fori_loop(0, qi // 4, eight_blocks, (m_init, m_init))
    m = lax.fori_loop(4 * (qi // 4), qi, two_blocks, m)
    upper = slice(tk, t)
    last = pl.multiple_of((2 * qi + 1) * tk, tk)
    last_keys = jnp.concatenate([k_ref[pl.ds(last, tk), :], aug_ref[pl.ds(last, tk), :]], axis=1)
    for e in range(2):
        w_sc[e] = ws[e]
        s1_sc[e, :, 0:tk] = jnp.dot(last_keys, w_sc[e, :, upper], preferred_element_type=F32)
    m = softmax_pv(2 * qi, s0_sc, smax0_sc, m, True)
    visible = (lax.broadcasted_iota(jnp.int32, (tk, tk), 0)
               <= lax.broadcasted_iota(jnp.int32, (tk, tk), 1))
    for e in range(2):
        s = jnp.where(visible, s1_sc[e, :, 0:tk], -jnp.inf)
        m_sc[e] = m[e]
        m_upper = m_sc[e, :, upper]
        m_new = jnp.maximum(m_upper, jnp.max(s, axis=0, keepdims=True))
        p = jnp.exp2(s - m_new).astype(BF16)
        acc_sc[e, :, upper] = jnp.exp2(m_upper - m_new) * acc_sc[e, :, upper] + jnp.dot(
            vT_sc[e, :, pl.ds(last, tk)], p, preferred_element_type=F32)
    outs = []
    for e in range(2):
        acc = acc_sc[e]
        outs.append(acc[0:HEAD_DIM, :] / acc[HEAD_DIM:HEAD_DIM + 1, :])
    o_ref[...] = jnp.concatenate(outs, axis=0).T.astype(o_ref.dtype)


def _fox_attn(qkv, aug, cq, t=512):
    s = qkv.shape[0]
    kcol = FOX_WIDTH // LANES
    vmem = 2 * (2 * t * LANES * 2 + 3 * s * LANES * 2 + 2 * t * 4) \
        + 2 * _V_ROWS * (s * 2 + t * 4) + 10 * t * t * 4
    return pl.pallas_call(
        functools.partial(_fox_attn_kernel, t=t),
        out_shape=jax.ShapeDtypeStruct((s, FOX_WIDTH), BF16),
        grid=(_N_PAIRS, s // t),
        in_specs=[pl.BlockSpec((t, LANES), lambda pr, qi: (qi, pr)),
                  pl.BlockSpec((s, LANES), lambda pr, qi: (0, kcol + pr)),
                  pl.BlockSpec((s, LANES), lambda pr, qi: (0, 2 * kcol + pr)),
                  pl.BlockSpec((s, LANES), lambda pr, qi: (0, pr)),
                  pl.BlockSpec((2, 1, t), lambda pr, qi: (pr, 0, qi))],
        out_specs=pl.BlockSpec((t, LANES), lambda pr, qi: (qi, pr)),
        scratch_shapes=[pltpu.VMEM((2, _V_ROWS, s), BF16), pltpu.VMEM((2, _V_ROWS, t), F32),
                        pltpu.VMEM((2, t // 2, t), F32), pltpu.VMEM((2, t // 2, t), F32),
                        pltpu.VMEM((2, 2 * LANES, t), BF16), pltpu.VMEM((2, 1, t), F32),
                        pltpu.VMEM((2, 1, t), F32), pltpu.VMEM((2, 1, t), F32)],
        compiler_params=_params(("arbitrary", "arbitrary"), vmem),
        name="fox_attention",
    )(qkv, qkv, qkv, aug, cq)


def _ssd_kernel(xs_ref, z_ref, bc_ref, dt_ref, dtT_ref, cwx_ref, cbx_ref, cwbc_ref, cbbc_ref,
                dtb_ref, dtbT_ref, alog_ref, alogT_ref, dskip_ref, nw_ref, y_ref,
                xbuf, bcbuf, xtail_sc, bctail_sc, state_sc, ybuf, *, L):
    halo = SUBLANES

    @pl.when(pl.program_id(0) == 0)
    def _():
        xtail_sc[...] = jnp.zeros_like(xtail_sc)
        bctail_sc[...] = jnp.zeros_like(bctail_sc)
        state_sc[...] = jnp.zeros_like(state_sc)

    def conv_silu(u_ref, tail_sc, buf, w_ref, b_ref):
        cur = u_ref[...]
        prev = tail_sc[...]
        tail_sc[...] = cur[L - halo:L, :]
        last = SSD_CONV - 1

        def taps(c, backs):
            acc = b_ref[...] + c * w_ref[last:last + 1, :]
            for k in range(1, SSD_CONV):
                acc = acc + backs[k - 1] * w_ref[last - k:last - k + 1, :]
            return acc

        buf[...] = taps(cur, [pltpu.roll(cur, k, axis=0) for k in range(1, SSD_CONV)])
        top = cur[0:halo, :]
        r = lax.broadcasted_iota(jnp.int32, top.shape, 0)
        buf[0:halo, :] = taps(top, [
            jnp.where(r < k, pltpu.roll(prev, k, axis=0), pltpu.roll(top, k, axis=0))
            for k in range(1, SSD_CONV)])
        return _silu(buf[...])

    xc = conv_silu(xs_ref, xtail_sc, xbuf, cwx_ref, cbx_ref)
    bcc = conv_silu(bc_ref, bctail_sc, bcbuf, cwbc_ref, cbbc_ref)

    dt = _softplus(dt_ref[...] + dtb_ref[...])
    dtT = _softplus(dtT_ref[...] + dtbT_ref[...])
    a = dt * (-jnp.exp(alog_ref[...]) * LOG2_E)
    aT = dtT * (-jnp.exp(alogT_ref[...]) * LOG2_E)

    row = lax.broadcasted_iota(jnp.int32, (L, L), 0)
    col = lax.broadcasted_iota(jnp.int32, (L, L), 1)
    causal = col <= row
    a_cum = _tri_left(causal.astype(BF16), a)
    a_cumT = _tri_right(aT, (row <= col).astype(BF16))
    a_last = a_cum[L - 1:L, :]
    a_lastT = a_cumT[:, L - 1:L]
    exp_acum = jnp.exp2(a_cum)
    w_endT = jnp.exp2(a_lastT - a_cumT) * dtT
    chunk_decay = jnp.exp2(a_last)
    src_T = a_cumT - jnp.log2(dtT)

    n = SSD_STATE
    heads_per_group = SSD_HEADS // SSD_GROUPS
    first_half = lax.broadcasted_iota(jnp.int32, (1, LANES), 1) < HEAD_DIM
    for g in range(SSD_GROUPS):
        bg = bcc[:, g * n:(g + 1) * n]
        cg = bcc[:, SSD_GROUPS * n + g * n:SSD_GROUPS * n + (g + 1) * n]
        bgT = bg.T
        cb = lax.dot_general(cg.astype(BF16), bg.astype(BF16), (((1,), (1,)), ((), ())),
                             preferred_element_type=F32)
        for pair in range(heads_per_group // 2):
            h0 = g * heads_per_group + 2 * pair
            cols = slice(h0 * HEAD_DIM, (h0 + 2) * HEAD_DIM)
            x_pair = xc[:, cols]
            x_pair_b = x_pair.astype(BF16)
            st = state_sc[:, cols]
            st_b = st.astype(BF16)
            ys, upds = [], []
            for h in (h0, h0 + 1):
                seg = a_cum[:, h:h + 1] - src_T[h:h + 1, :]
                mix = cb * jnp.exp2(jnp.where(causal, seg, -jnp.inf))
                y = jnp.dot(mix.astype(BF16), x_pair_b, preferred_element_type=F32)
                ys.append(y + jnp.dot((cg * exp_acum[:, h:h + 1]).astype(BF16), st_b,
                                      preferred_element_type=F32))
                upds.append(jnp.dot((bgT * w_endT[h:h + 1, :]).astype(BF16), x_pair_b,
                                    preferred_element_type=F32))
            ybuf[:, cols] = jnp.where(first_half, ys[0], ys[1]) + dskip_ref[:, cols] * x_pair
            keep = jnp.where(first_half, chunk_decay[:, h0:h0 + 1], chunk_decay[:, h0 + 1:h0 + 2])
            state_sc[:, cols] = keep * st + jnp.where(first_half, upds[0], upds[1])

    y = ybuf[...] * _silu(z_ref[...])
    y_ref[...] = _rms(y, nw_ref[...]).astype(y_ref.dtype)


def _ssd(u2, dt_raw, dt_rawT, cwx, cbx, cwbc, cbbc, dtb, dtbT, alog, alogT, dskip, nw, L=256):
    s = u2.shape[0]
    w = SSD_WIDTH
    full = lambda shape: pl.BlockSpec(shape, lambda i: (0,) * len(shape))
    vmem = 2 * (2 * L * w * 4 + L * _BC_WIDTH * 4 + L * w * 2) + (2 * L + 16) * w * 4 \
        + (L + 8) * _BC_WIDTH * 4 + SSD_STATE * w * 4 + 24 * L * L * 4 + 6 * L * w * 4
    return pl.pallas_call(
        functools.partial(_ssd_kernel, L=L),
        out_shape=jax.ShapeDtypeStruct((s, w), BF16),
        grid=(s // L,),
        in_specs=[pl.BlockSpec((L, w), lambda i: (i, 0)),
                  pl.BlockSpec((L, w), lambda i: (i, 1)),
                  pl.BlockSpec((L, _BC_WIDTH), lambda i: (i, 2 * w // _BC_WIDTH)),
                  pl.BlockSpec((L, SSD_HEADS), lambda i: (i, 0)),
                  pl.BlockSpec((SSD_HEADS, L), lambda i: (0, i)),
                  full((SSD_CONV, w)), full((1, w)),
                  full((SSD_CONV, _BC_WIDTH)), full((1, _BC_WIDTH)),
                  full((1, SSD_HEADS)), full((SSD_HEADS, 1)),
                  full((1, SSD_HEADS)), full((SSD_HEADS, 1)),
                  full((1, w)), full((1, w))],
        out_specs=pl.BlockSpec((L, w), lambda i: (i, 0)),
        scratch_shapes=[pltpu.VMEM((L, w), F32),
                        pltpu.VMEM((L, _BC_WIDTH), F32),
                        pltpu.VMEM((SUBLANES, w), F32),
                        pltpu.VMEM((SUBLANES, _BC_WIDTH), F32),
                        pltpu.VMEM((SSD_STATE, w), F32),
                        pltpu.VMEM((L, w), F32)],
        compiler_params=_params(("arbitrary",), vmem),
        name="ssd_mixer",
    )(u2, u2, u2, dt_raw, dt_rawT, cwx, cbx, cwbc, cbbc, dtb, dtbT, alog, alogT, dskip, nw)


def _out_proj_kernel(x_ref, ya_ref, yb_ref, wa_ref, wb_ref, nw_ref, o_ref, h_ref):
    o = (x_ref[...]
         + jnp.dot(ya_ref[...], wa_ref[...], preferred_element_type=F32)
         + jnp.dot(yb_ref[...], wb_ref[...], preferred_element_type=F32))
    o_ref[...] = o
    h_ref[...] = _rms(o, nw_ref[...]).astype(h_ref.dtype)


def _out_proj(x, ya, yb, wa, wb, nw, tm=512):
    m, d = x.shape
    ka, kb = ya.shape[1], yb.shape[1]
    vmem = 2 * (2 * tm * d * 4 + tm * d * 2 + tm * (ka + kb) * 2 + (ka + kb) * d * 2) \
        + 4 * tm * d * 4
    return pl.pallas_call(
        _out_proj_kernel,
        out_shape=(jax.ShapeDtypeStruct((m, d), F32), jax.ShapeDtypeStruct((m, d), BF16)),
        grid=(m // tm,),
        in_specs=[pl.BlockSpec((tm, d), lambda i: (i, 0)),
                  pl.BlockSpec((tm, ka), lambda i: (i, 0)),
                  pl.BlockSpec((tm, kb), lambda i: (i, 0)),
                  pl.BlockSpec((ka, d), lambda i: (0, 0)),
                  pl.BlockSpec((kb, d), lambda i: (0, 0)),
                  pl.BlockSpec((1, d), lambda i: (0, 0))],
        out_specs=(pl.BlockSpec((tm, d), lambda i: (i, 0)),
                   pl.BlockSpec((tm, d), lambda i: (i, 0))),
        compiler_params=_params(("arbitrary",), vmem),
        name="mix_out_proj",
    )(x, ya, yb, wa, wb, nw)


def _gate_up_kernel(h_ref, wg_ref, wu_ref, cw_ref, cb_ref, o_ref, wg_sc, wu_sc, tail_sc, conv_sc):
    halo = SUBLANES
    tm = conv_sc.shape[0]

    @pl.when(pl.program_id(1) == 0)
    def _():
        wg_sc[...] = wg_ref[...].astype(BF16)
        wu_sc[...] = wu_ref[...].astype(BF16)
        tail_sc[...] = jnp.zeros_like(tail_sc)

    h = h_ref[...]
    g = jnp.dot(h, wg_sc[...], preferred_element_type=F32)
    up = jnp.dot(h, wu_sc[...], preferred_element_type=F32)

    prev = tail_sc[...]
    tail_sc[...] = g[tm - halo:tm, :]

    def conv(cur, back1, back2):
        return (cb_ref[...] + cur * cw_ref[2:3, :] + back1 * cw_ref[1:2, :]
                + back2 * cw_ref[0:1, :])

    conv_sc[...] = conv(g, pltpu.roll(g, 1, axis=0), pltpu.roll(g, 2, axis=0))
    top = g[0:halo, :]
    r = lax.broadcasted_iota(jnp.int32, top.shape, 0)
    back1 = jnp.where(r < 1, pltpu.roll(prev, 1, axis=0), pltpu.roll(top, 1, axis=0))
    back2 = jnp.where(r < 2, pltpu.roll(prev, 2, axis=0), pltpu.roll(top, 2, axis=0))
    conv_sc[0:halo, :] = conv(top, back1, back2)
    o_ref[...] = (_silu(conv_sc[...]) * up).astype(o_ref.dtype)


def _gate_up(h, w_gu, cw, cb, tm=1024, tn=512):
    m, d = h.shape
    nj = D_FF // tn
    vmem = 2 * (tm * d * 2 + 2 * d * tn * 4 + tm * tn * 2) + 2 * d * tn * 2 \
        + 10 * tm * tn * 4
    return pl.pallas_call(
        _gate_up_kernel,
        out_shape=jax.ShapeDtypeStruct((m, D_FF), BF16),
        grid=(nj, m // tm),
        in_specs=[pl.BlockSpec((tm, d), lambda j, i: (i, 0)),
                  pl.BlockSpec((d, tn), lambda j, i: (0, j)),
                  pl.BlockSpec((d, tn), lambda j, i: (0, j + nj)),
                  pl.BlockSpec((FFN_CONV, tn), lambda j, i: (0, j)),
                  pl.BlockSpec((1, tn), lambda j, i: (0, j))],
        out_specs=pl.BlockSpec((tm, tn), lambda j, i: (i, j)),
        scratch_shapes=[pltpu.VMEM((d, tn), BF16), pltpu.VMEM((d, tn), BF16),
                        pltpu.VMEM((SUBLANES, tn), F32), pltpu.VMEM((tm, tn), F32)],
        compiler_params=_params(("arbitrary", "arbitrary"), vmem),
        name="ffn_gate_up",
    )(h, w_gu, w_gu, cw, cb)


def _down_kernel(a_ref, w_ref, x_ref, o_ref, w_sc):
    @pl.when(pl.program_id(1) == 0)
    def _():
        w_sc[...] = w_ref[...].astype(BF16)

    o_ref[...] = x_ref[...] + jnp.dot(a_ref[...], w_sc[...], preferred_element_type=F32)


def _down(act, w, x, tm=512, tn=512):
    m, d = x.shape
    kk = act.shape[1]
    vmem = 2 * (tm * kk * 2 + kk * tn * 4 + 2 * tm * tn * 4) + kk * tn * 2 + 2 * tm * tn * 4
    return pl.pallas_call(
        _down_kernel,
        out_shape=jax.ShapeDtypeStruct((m, d), F32),
        grid=(d // tn, m // tm),
        in_specs=[pl.BlockSpec((tm, kk), lambda j, i: (i, 0)),
                  pl.BlockSpec((kk, tn), lambda j, i: (0, j)),
                  pl.BlockSpec((tm, tn), lambda j, i: (i, j))],
        out_specs=pl.BlockSpec((tm, tn), lambda j, i: (i, j)),
        scratch_shapes=[pltpu.VMEM((kk, tn), BF16)],
        compiler_params=_params(("arbitrary", "arbitrary"), vmem),
        name="ffn_down",
    )(act, w, x)


def _ple_kernel(x_ref, p_ref, nw_ref, wg_ref, wp_ref, fw_ref, o_ref, *, final):
    x = x_ref[...]
    h = _rms(x, nw_ref[...]).astype(BF16)
    gate = jax.nn.sigmoid(jnp.dot(h, wg_ref[...], preferred_element_type=F32))
    proj = jnp.dot(p_ref[...].astype(BF16), wp_ref[...], preferred_element_type=F32)
    x = x + gate * proj
    if final:
        x = _rms(x, fw_ref[...])
    o_ref[...] = x


def _ple(x, p, nw, wg, wp, fw, final, tm=512):
    m, d = x.shape
    dp = p.shape[1]
    vmem = 2 * (2 * tm * d * 4 + tm * dp * 4 + d * d * 2 + dp * d * 2) + 6 * tm * d * 4
    return pl.pallas_call(
        functools.partial(_ple_kernel, final=final),
        out_shape=jax.ShapeDtypeStruct((m, d), F32),
        grid=(m // tm,),
        in_specs=[pl.BlockSpec((tm, d), lambda i: (i, 0)),
                  pl.BlockSpec((tm, dp), lambda i: (i, 0)),
                  pl.BlockSpec((1, d), lambda i: (0, 0)),
                  pl.BlockSpec((d, d), lambda i: (0, 0)),
                  pl.BlockSpec((dp, d), lambda i: (0, 0)),
                  pl.BlockSpec((1, d), lambda i: (0, 0))],
        out_specs=pl.BlockSpec((tm, d), lambda i: (i, 0)),
        compiler_params=_params(("arbitrary",), vmem),
        name="ple_gate",
    )(x, p, nw, wg, wp, fw)


def _layer(x, p, mix_norm_w, w_in, fox_forget_bias, ssd_conv_w, ssd_conv_b, ssd_dt_bias,
           ssd_A_log, ssd_D, ssd_norm_w, w_out, ffn_norm_w, w_gate_up, ffn_conv_w,
           ffn_conv_b, w_down, ple_norm_w, w_ple_gate, w_ple_proj, final_norm_w, final):
    s = x.shape[0]
    row = lambda v: v.reshape(1, -1).astype(F32)

    w_inT = jnp.swapaxes(w_in, 0, 1).astype(F32)
    w_u2T = jnp.concatenate(
        [w_inT[_OFF_XS:_OFF_B], w_inT[_OFF_Z:_OFF_DT], w_inT[_OFF_B:_OFF_Z],
         w_inT[_OFF_F:_OFF_XS], w_inT[_OFF_DT:_IN_COLS],
         jnp.zeros((LANES - FOX_HEADS - SSD_HEADS, D_MODEL), F32)], axis=0)

    h_mix = _norm(x, row(mix_norm_w))
    qkv = _proj(h_mix, w_inT, _OFF_F, BF16, tm=1024, tn=512, name="in_proj_qkv")
    u2 = _proj(h_mix, w_u2T, _U2_COLS, F32, tm=1024, tn=_U2_COLS // 3, name="in_proj_ssd")

    b_pad = jnp.zeros((1, LANES), F32).at[0, :FOX_HEADS].set(fox_forget_bias.astype(F32))
    c, aug = _fox_c(u2, b_pad)
    cT = c[:, :FOX_HEADS].T
    y_fox = _fox_attn(qkv, aug, cT[:, None, :])

    dt_raw = u2[:, _U2_SMALL + FOX_HEADS:_U2_SMALL + FOX_HEADS + SSD_HEADS]
    col = lambda v: v.reshape(-1, 1).astype(F32)
    y_ssd = _ssd(u2, dt_raw, dt_raw.T,
                 ssd_conv_w[:, :SSD_WIDTH].astype(F32), row(ssd_conv_b[:SSD_WIDTH]),
                 ssd_conv_w[:, SSD_WIDTH:].astype(F32), row(ssd_conv_b[SSD_WIDTH:]),
                 row(ssd_dt_bias), col(ssd_dt_bias), row(ssd_A_log), col(ssd_A_log),
                 row(jnp.repeat(ssd_D, HEAD_DIM)), row(ssd_norm_w))

    w_out_b = w_out.astype(BF16)
    x, h_ffn = _out_proj(x, y_fox, y_ssd, w_out_b[:FOX_WIDTH], w_out_b[FOX_WIDTH:],
                         row(ffn_norm_w))

    act = _gate_up(h_ffn, w_gate_up.astype(F32), ffn_conv_w.astype(F32), row(ffn_conv_b))
    x = _down(act, w_down.astype(F32), x)

    return _ple(x, p, row(ple_norm_w), w_ple_gate.astype(BF16), w_ple_proj.astype(BF16),
                row(final_norm_w), final)


def kernel(x, p, mix_norm_w, w_in, fox_forget_bias, ssd_conv_w, ssd_conv_b, ssd_dt_bias,
           ssd_A_log, ssd_D, ssd_norm_w, w_out, ffn_norm_w, w_gate_up, ffn_conv_w,
           ffn_conv_b, w_down, ple_norm_w, w_ple_gate, w_ple_proj, final_norm_w):
    bsz, s, d = x.shape
    depth = p.shape[0]
    outs = []
    for b in range(bsz):
        xb = x[b]
        for i in range(depth):
            xb = _layer(xb, p[i, b], mix_norm_w[i], w_in[i], fox_forget_bias[i], ssd_conv_w[i],
                        ssd_conv_b[i], ssd_dt_bias[i], ssd_A_log[i], ssd_D[i], ssd_norm_w[i],
                        w_out[i], ffn_norm_w[i], w_gate_up[i], ffn_conv_w[i], ffn_conv_b[i],
                        w_down[i], ple_norm_w[i], w_ple_gate[i], w_ple_proj[i], final_norm_w,
                        final=(i == depth - 1))
        outs.append(xb)
    return outs[0][None] if bsz == 1 else jnp.stack(outs, axis=0)
```

```python
import functools

import numpy as np
import jax
import jax.numpy as jnp
from jax import lax
from jax.experimental import pallas as pl
from jax.experimental.pallas import tpu as pltpu

F32 = jnp.float32
BF16 = jnp.bfloat16

D_MODEL = 2048
SEQ = 8192
D_PLE = 256
HEAD_DIM = 64
FOX_WIDTH = 1024
FOX_HEADS = 16
SSD_WIDTH = 1024
SSD_HEADS = 16
SSD_GROUPS = 2
SSD_STATE = 128
SSD_CONV = 4
D_FF = 5632
FFN_CONV = 3
EPS = 1e-6
LOG2_E = 1.4426950408889634

_OFF_Q = 0
_OFF_F = 3 * FOX_WIDTH
_OFF_XS = _OFF_F + FOX_HEADS
_OFF_B = _OFF_XS + SSD_WIDTH
_OFF_C = _OFF_B + SSD_GROUPS * SSD_STATE
_OFF_Z = _OFF_C + SSD_GROUPS * SSD_STATE
_OFF_DT = _OFF_Z + SSD_WIDTH
_IN_COLS = _OFF_DT + SSD_HEADS

LANES = 128
SUBLANES = 8
VMEM_CAP = 60 * 1024 * 1024

_BC_WIDTH = 2 * SSD_GROUPS * SSD_STATE
_U2_COLS = 2 * SSD_WIDTH + _BC_WIDTH + LANES
_U2_SMALL = 2 * SSD_WIDTH + _BC_WIDTH


def _params(sem, vmem_bytes, claim_all=False):
    limit = VMEM_CAP if claim_all else min(VMEM_CAP, vmem_bytes)
    return pltpu.CompilerParams(dimension_semantics=sem, vmem_limit_bytes=int(limit))


def _rms(xf, w):
    ms = jnp.mean(xf * xf, axis=-1, keepdims=True)
    return xf * lax.rsqrt(ms + EPS) * w


def _split3(a):
    hi = a.astype(BF16)
    r1 = a - hi.astype(F32)
    mid = r1.astype(BF16)
    lo = (r1 - mid.astype(F32)).astype(BF16)
    return hi, mid, lo


def _tri_left(tri, a):
    return sum(jnp.dot(tri, t, preferred_element_type=F32) for t in _split3(a))


def _tri_right(a, tri):
    return sum(jnp.dot(t, tri, preferred_element_type=F32) for t in _split3(a))


def _softplus(v):
    return jnp.maximum(v, 0.0) + jnp.log1p(jnp.exp(-jnp.abs(v)))


def _silu(v):
    return v * jax.nn.sigmoid(v)


def _norm_kernel(x_ref, nw_ref, h_ref):
    h_ref[...] = _rms(x_ref[...], nw_ref[...]).astype(h_ref.dtype)


def _norm(x, nw, tm=512):
    m, d = x.shape
    return pl.pallas_call(
        _norm_kernel,
        out_shape=jax.ShapeDtypeStruct((m, d), BF16),
        grid=(m // tm,),
        in_specs=[pl.BlockSpec((tm, d), lambda i: (i, 0)),
                  pl.BlockSpec((1, d), lambda i: (0, 0))],
        out_specs=pl.BlockSpec((tm, d), lambda i: (i, 0)),
        compiler_params=_params(("arbitrary",), 2 * tm * d * 6 + 4 * tm * d * 4),
        name="mix_norm",
    )(x, nw)


def _proj_kernel(h_ref, wT_ref, o_ref, w_sc):
    @pl.when(pl.program_id(1) == 0)
    def _():
        w_sc[...] = wT_ref[...].astype(BF16)

    o_ref[...] = lax.dot_general(h_ref[...], w_sc[...], (((1,), (1,)), ((), ())),
                                 preferred_element_type=F32).astype(o_ref.dtype)


def _proj(h, wT, n_cols, out_dtype, tm, tn, name):
    m, k = h.shape
    ob = jnp.dtype(out_dtype).itemsize
    vmem = 2 * (tm * k * 2 + k * tn * 4 + tm * tn * ob) + k * tn * 2 + 2 * tm * tn * 4
    return pl.pallas_call(
        _proj_kernel,
        out_shape=jax.ShapeDtypeStruct((m, n_cols), out_dtype),
        grid=(n_cols // tn, m // tm),
        in_specs=[pl.BlockSpec((tm, k), lambda j, i: (i, 0)),
                  pl.BlockSpec((tn, k), lambda j, i: (j, 0))],
        out_specs=pl.BlockSpec((tm, tn), lambda j, i: (i, j)),
        scratch_shapes=[pltpu.VMEM((tn, k), BF16)],
        compiler_params=_params(("arbitrary", "arbitrary"), vmem),
        name=name,
    )(h, wT)


_AUG_ONES = 6
_N_PAIRS = FOX_HEADS // 2


def _aug_selector():
    sel = np.zeros((4 * LANES, _N_PAIRS * LANES), np.float32)
    for pr in range(_N_PAIRS):
        for e in range(2):
            for j in range(3):
                sel[j * LANES + 2 * pr + e, pr * LANES + 3 * e + j] = -1.0
        sel[3 * LANES, pr * LANES + _AUG_ONES:pr * LANES + _AUG_ONES + 3] = 1.0
    return jnp.asarray(sel, BF16)


def _fox_c_kernel(f_ref, b_ref, sel_ref, c_ref, aug_ref, carry_sc):
    @pl.when(pl.program_id(0) == 0)
    def _():
        carry_sc[...] = jnp.zeros_like(carry_sc)

    z = f_ref[...] + b_ref[...]
    lf = jnp.minimum(z, 0.0) - jnp.log1p(jnp.exp(-jnp.abs(z)))
    t = lf.shape[0]
    row = lax.broadcasted_iota(jnp.int32, (t, t), 0)
    col = lax.broadcasted_iota(jnp.int32, (t, t), 1)
    tril = (col <= row).astype(BF16)
    cum = _tri_left(tril, lf) + carry_sc[...]
    c_ref[...] = cum
    carry_sc[...] = cum[t - 1:t, :]
    hi, mid, lo = _split3(cum * LOG2_E)
    parts = jnp.concatenate([hi, mid, lo, jnp.ones_like(hi)], axis=1)
    aug_ref[...] = jnp.dot(parts, sel_ref[...], preferred_element_type=F32).astype(BF16)


def _fox_c(u2, b_pad, t=256):
    s = u2.shape[0]
    aug_cols = _N_PAIRS * LANES
    return pl.pallas_call(
        _fox_c_kernel,
        out_shape=(jax.ShapeDtypeStruct((s, LANES), F32),
                   jax.ShapeDtypeStruct((s, aug_cols), BF16)),
        grid=(s // t,),
        in_specs=[pl.BlockSpec((t, LANES), lambda i: (i, _U2_SMALL // LANES)),
                  pl.BlockSpec((1, LANES), lambda i: (0, 0)),
                  pl.BlockSpec((4 * LANES, aug_cols), lambda i: (0, 0))],
        out_specs=(pl.BlockSpec((t, LANES), lambda i: (i, 0)),
                   pl.BlockSpec((t, aug_cols), lambda i: (i, 0))),
        scratch_shapes=[pltpu.VMEM((1, LANES), F32)],
        compiler_params=_params(("arbitrary",), 16 * 1024 * 1024, claim_all=True),
        name="fox_cumlogf",
    )(u2, b_pad, _aug_selector())


_V_ROWS = HEAD_DIM + 16


def _fox_attn_kernel(q_ref, k_ref, v_ref, aug_ref, cq_ref, o_ref, vT_sc, acc_sc, s0_sc, s1_sc,
                     w_sc, m_sc, smax0_sc, smax1_sc, *, t):
    qi = pl.program_id(1)
    n_kv = k_ref.shape[0] // t

    @pl.when(qi == 0)
    def _():
        pad_row = lax.broadcasted_iota(jnp.int32, (_V_ROWS - HEAD_DIM, t), 0)
        ones_rows = jnp.where(pad_row == 0, 1.0, 0.0).astype(BF16)

        def transpose_v(c, carry):
            st = pl.multiple_of(c * t, t)
            vT = v_ref[pl.ds(st, t), :].astype(F32).T.astype(BF16)
            for e in range(2):
                vT_sc[e, 0:HEAD_DIM, pl.ds(st, t)] = vT[e * HEAD_DIM:(e + 1) * HEAD_DIM, :]
                vT_sc[e, HEAD_DIM:_V_ROWS, pl.ds(st, t)] = ones_rows
            return carry
        lax.fori_loop(0, n_kv, transpose_v, 0)

    scale = HEAD_DIM ** -0.5 * LOG2_E
    qT = q_ref[...].astype(F32).T * scale
    row = lax.broadcasted_iota(jnp.int32, (LANES, t), 0)
    tk = s0_sc.shape[1]
    key_idx = lax.broadcasted_iota(jnp.int32, (tk, t), 0)
    qry_idx = lax.broadcasted_iota(jnp.int32, (tk, t), 1)

    ws = []
    for e in range(2):
        qTe = jnp.where(row // HEAD_DIM == e, qT, 0.0)
        cq_hi, cq_mid, cq_lo = (part.astype(F32)
                                for part in _split3(cq_ref[e] * LOG2_E))
        top = lax.broadcasted_iota(jnp.int32, (2 * SUBLANES, t), 0)
        aug_rows = jnp.where(
            top // 3 == e, 1.0,
            jnp.where(top == _AUG_ONES, cq_hi,
                      jnp.where(top == _AUG_ONES + 1, cq_mid,
                                jnp.where(top == _AUG_ONES + 2, cq_lo, 0.0))))
        unused = jnp.zeros((LANES - 2 * SUBLANES, t), F32)
        ws.append(jnp.concatenate([qTe, aug_rows, unused], axis=0).astype(BF16))
        acc_sc[e] = jnp.zeros((_V_ROWS, t), F32)

    def scores_into(j, s_sc, smax_sc):
        start = pl.multiple_of(j * tk, tk)
        keys = jnp.concatenate([k_ref[pl.ds(start, tk), :], aug_ref[pl.ds(start, tk), :]],
                               axis=1)
        for e in range(2):
            s = jnp.dot(keys, ws[e], preferred_element_type=F32)
            s_sc[e] = s
            smax_sc[e] = jnp.max(s, axis=0, keepdims=True)

    def softmax_pv(j, s_sc, smax_sc, m_prev, masked):
        start = pl.multiple_of(j * tk, tk)
        if masked:
            visible = key_idx + start <= qry_idx + qi * t
        m_out, probs, alphas = [], [], []
        for e in range(2):
            s = s_sc[e]
            if masked:
                s = jnp.where(visible, s, -jnp.inf)
                m_new = jnp.maximum(m_prev[e], jnp.max(s, axis=0, keepdims=True))
            else:
                m_new = jnp.maximum(m_prev[e], smax_sc[e])
            probs.append(jnp.exp2(s - m_new).astype(BF16))
            alphas.append(jnp.exp2(m_prev[e] - m_new))
            m_out.append(m_new)
        for e in range(2):
            acc_sc[e] = alphas[e] * acc_sc[e] + jnp.dot(
                vT_sc[e, :, pl.ds(start, tk)], probs[e], preferred_element_type=F32)
        return tuple(m_out)

    def two_blocks(i, m):
        scores_into(2 * i + 1, s1_sc, smax1_sc)
        m = softmax_pv(2 * i, s0_sc, smax0_sc, m, False)
        scores_into(2 * i + 2, s0_sc, smax0_sc)
        return softmax_pv(2 * i + 1, s1_sc, smax1_sc, m, False)

    def eight_blocks(i, m):
        for r in range(4):
            m = two_blocks(4 * i + r, m)
        return m

    m_init = jnp.full((1, t), -1e30, F32)
    scores_into(0, s0_sc, smax0_sc)
    m = lax.fori_loop(0, qi // 4, eight_blocks, (m_init, m_init))
    m = lax.fori_loop(4 * (qi // 4), qi, two_blocks, m)
    upper = slice(tk, t)
    last = pl.multiple_of((2 * qi + 1) * tk, tk)
    last_keys = jnp.concatenate([k_ref[pl.ds(last, tk), :], aug_ref[pl.ds(last, tk), :]], axis=1)
    for e in range(2):
        w_sc[e] = ws[e]
        s1_sc[e, :, 0:tk] = jnp.dot(last_keys, w_sc[e, :, upper], preferred_element_type=F32)
    m = softmax_pv(2 * qi, s0_sc, smax0_sc, m, True)
    visible = (lax.broadcasted_iota(jnp.int32, (tk, tk), 0)
               <= lax.broadcasted_iota(jnp.int32, (tk, tk), 1))
    for e in range(2):
        s = jnp.where(visible, s1_sc[e, :, 0:tk], -jnp.inf)
        m_sc[e] = m[e]
        m_upper = m_sc[e, :, upper]
        m_new = jnp.maximum(m_upper, jnp.max(s, axis=0, keepdims=True))
        p = jnp.exp2(s - m_new).astype(BF16)
        acc_sc[e, :, upper] = jnp.exp2(m_upper - m_new) * acc_sc[e, :, upper] + jnp.dot(
            vT_sc[e, :, pl.ds(last, tk)], p, preferred_element_type=F32)
    outs = []
    for e in range(2):
        acc = acc_sc[e]
        outs.append(acc[0:HEAD_DIM, :] / acc[HEAD_DIM:HEAD_DIM + 1, :])
    o_ref[...] = jnp.concatenate(outs, axis=0).T.astype(o_ref.dtype)


def _fox_attn(qkv, aug, cq, t=512):
    s = qkv.shape[0]
    kcol = FOX_WIDTH // LANES
    vmem = 2 * (2 * t * LANES * 2 + 3 * s * LANES * 2 + 2 * t * 4) \
        + 2 * _V_ROWS * (s * 2 + t * 4) + 10 * t * t * 4
    return pl.pallas_call(
        functools.partial(_fox_attn_kernel, t=t),
        out_shape=jax.ShapeDtypeStruct((s, FOX_WIDTH), BF16),
        grid=(_N_PAIRS, s // t),
        in_specs=[pl.BlockSpec((t, LANES), lambda pr, qi: (qi, pr)),
                  pl.BlockSpec((s, LANES), lambda pr, qi: (0, kcol + pr)),
                  pl.BlockSpec((s, LANES), lambda pr, qi: (0, 2 * kcol + pr)),
                  pl.BlockSpec((s, LANES), lambda pr, qi: (0, pr)),
                  pl.BlockSpec((2, 1, t), lambda pr, qi: (pr, 0, qi))],
        out_specs=pl.BlockSpec((t, LANES), lambda pr, qi: (qi, pr)),
        scratch_shapes=[pltpu.VMEM((2, _V_ROWS, s), BF16), pltpu.VMEM((2, _V_ROWS, t), F32),
                        pltpu.VMEM((2, t // 2, t), F32), pltpu.VMEM((2, t // 2, t), F32),
                        pltpu.VMEM((2, 2 * LANES, t), BF16), pltpu.VMEM((2, 1, t), F32),
                        pltpu.VMEM((2, 1, t), F32), pltpu.VMEM((2, 1, t), F32)],
        compiler_params=_params(("arbitrary", "arbitrary"), vmem, claim_all=True),
        name="fox_attention",
    )(qkv, qkv, qkv, aug, cq)


def _ssd_kernel(xs_ref, z_ref, bc_ref, dt_ref, dtT_ref, cwx_ref, cbx_ref, cwbc_ref, cbbc_ref,
                dtb_ref, dtbT_ref, alog_ref, alogT_ref, dskip_ref, nw_ref, y_ref,
                xbuf, bcbuf, xtail_sc, bctail_sc, state_sc, ybuf, *, L):
    halo = SUBLANES

    @pl.when(pl.program_id(0) == 0)
    def _():
        xtail_sc[...] = jnp.zeros_like(xtail_sc)
        bctail_sc[...] = jnp.zeros_like(bctail_sc)
        state_sc[...] = jnp.zeros_like(state_sc)

    def conv_silu(u_ref, tail_sc, buf, w_ref, b_ref):
        cur = u_ref[...]
        prev = tail_sc[...]
        tail_sc[...] = cur[L - halo:L, :]
        last = SSD_CONV - 1

        def taps(c, backs):
            acc = b_ref[...] + c * w_ref[last:last + 1, :]
            for k in range(1, SSD_CONV):
                acc = acc + backs[k - 1] * w_ref[last - k:last - k + 1, :]
            return acc

        buf[...] = taps(cur, [pltpu.roll(cur, k, axis=0) for k in range(1, SSD_CONV)])
        top = cur[0:halo, :]
        r = lax.broadcasted_iota(jnp.int32, top.shape, 0)
        buf[0:halo, :] = taps(top, [
            jnp.where(r < k, pltpu.roll(prev, k, axis=0), pltpu.roll(top, k, axis=0))
            for k in range(1, SSD_CONV)])
        return _silu(buf[...])

    xc = conv_silu(xs_ref, xtail_sc, xbuf, cwx_ref, cbx_ref)
    bcc = conv_silu(bc_ref, bctail_sc, bcbuf, cwbc_ref, cbbc_ref)

    dt = _softplus(dt_ref[...] + dtb_ref[...])
    dtT = _softplus(dtT_ref[...] + dtbT_ref[...])
    a = dt * (-jnp.exp(alog_ref[...]) * LOG2_E)
    aT = dtT * (-jnp.exp(alogT_ref[...]) * LOG2_E)

    row = lax.broadcasted_iota(jnp.int32, (L, L), 0)
    col = lax.broadcasted_iota(jnp.int32, (L, L), 1)
    causal = col <= row
    a_cum = _tri_left(causal.astype(BF16), a)
    a_cumT = _tri_right(aT, (row <= col).astype(BF16))
    a_last = a_cum[L - 1:L, :]
    a_lastT = a_cumT[:, L - 1:L]
    exp_acum = jnp.exp2(a_cum)
    w_endT = jnp.exp2(a_lastT - a_cumT) * dtT
    chunk_decay = jnp.exp2(a_last)
    src_T = a_cumT - jnp.log2(dtT)

    n = SSD_STATE
    heads_per_group = SSD_HEADS // SSD_GROUPS
    first_half = lax.broadcasted_iota(jnp.int32, (1, LANES), 1) < HEAD_DIM
    for g in range(SSD_GROUPS):
        bg = bcc[:, g * n:(g + 1) * n]
        cg = bcc[:, SSD_GROUPS * n + g * n:SSD_GROUPS * n + (g + 1) * n]
        bgT = bg.T
        cb = lax.dot_general(cg.astype(BF16), bg.astype(BF16), (((1,), (1,)), ((), ())),
                             preferred_element_type=F32)
        for pair in range(heads_per_group // 2):
            h0 = g * heads_per_group + 2 * pair
            cols = slice(h0 * HEAD_DIM, (h0 + 2) * HEAD_DIM)
            x_pair = xc[:, cols]
            x_pair_b = x_pair.astype(BF16)
            st = state_sc[:, cols]
            st_b = st.astype(BF16)
            ys, upds = [], []
            for h in (h0, h0 + 1):
                seg = a_cum[:, h:h + 1] - src_T[h:h + 1, :]
                mix = cb * jnp.exp2(jnp.where(causal, seg, -jnp.inf))
                y = jnp.dot(mix.astype(BF16), x_pair_b, preferred_element_type=F32)
                ys.append(y + jnp.dot((cg * exp_acum[:, h:h + 1]).astype(BF16), st_b,
                                      preferred_element_type=F32))
                upds.append(jnp.dot((bgT * w_endT[h:h + 1, :]).astype(BF16), x_pair_b,
                                    preferred_element_type=F32))
            ybuf[:, cols] = jnp.where(first_half, ys[0], ys[1]) + dskip_ref[:, cols] * x_pair
            keep = jnp.where(first_half, chunk_decay[:, h0:h0 + 1], chunk_decay[:, h0 + 1:h0 + 2])
            state_sc[:, cols] = keep * st + jnp.where(first_half, upds[0], upds[1])

    y = ybuf[...] * _silu(z_ref[...])
    y_ref[...] = _rms(y, nw_ref[...]).astype(y_ref.dtype)


def _ssd(u2, dt_raw, dt_rawT, cwx, cbx, cwbc, cbbc, dtb, dtbT, alog, alogT, dskip, nw, L=256):
    s = u2.shape[0]
    w = SSD_WIDTH
    full = lambda shape: pl.BlockSpec(shape, lambda i: (0,) * len(shape))
    vmem = 2 * (2 * L * w * 4 + L * _BC_WIDTH * 4 + L * w * 2) + (2 * L + 16) * w * 4 \
        + (L + 8) * _BC_WIDTH * 4 + SSD_STATE * w * 4 + 24 * L * L * 4 + 6 * L * w * 4
    return pl.pallas_call(
        functools.partial(_ssd_kernel, L=L),
        out_shape=jax.ShapeDtypeStruct((s, w), BF16),
        grid=(s // L,),
        in_specs=[pl.BlockSpec((L, w), lambda i: (i, 0)),
                  pl.BlockSpec((L, w), lambda i: (i, 1)),
                  pl.BlockSpec((L, _BC_WIDTH), lambda i: (i, 2 * w // _BC_WIDTH)),
                  pl.BlockSpec((L, SSD_HEADS), lambda i: (i, 0)),
                  pl.BlockSpec((SSD_HEADS, L), lambda i: (0, i)),
                  full((SSD_CONV, w)), full((1, w)),
                  full((SSD_CONV, _BC_WIDTH)), full((1, _BC_WIDTH)),
                  full((1, SSD_HEADS)), full((SSD_HEADS, 1)),
                  full((1, SSD_HEADS)), full((SSD_HEADS, 1)),
                  full((1, w)), full((1, w))],
        out_specs=pl.BlockSpec((L, w), lambda i: (i, 0)),
        scratch_shapes=[pltpu.VMEM((L, w), F32),
                        pltpu.VMEM((L, _BC_WIDTH), F32),
                        pltpu.VMEM((SUBLANES, w), F32),
                        pltpu.VMEM((SUBLANES, _BC_WIDTH), F32),
                        pltpu.VMEM((SSD_STATE, w), F32),
                        pltpu.VMEM((L, w), F32)],
        compiler_params=_params(("arbitrary",), vmem, claim_all=True),
        name="ssd_mixer",
    )(u2, u2, u2, dt_raw, dt_rawT, cwx, cbx, cwbc, cbbc, dtb, dtbT, alog, alogT, dskip, nw)


def _out_proj_kernel(x_ref, ya_ref, yb_ref, w_ref, nw_ref, o_ref, h_ref, w_sc):
    @pl.when(pl.program_id(0) == 0)
    def _():
        w_sc[...] = w_ref[...].astype(BF16)

    ka = ya_ref.shape[1]
    o = (x_ref[...]
         + jnp.dot(ya_ref[...], w_sc[0:ka, :], preferred_element_type=F32)
         + jnp.dot(yb_ref[...], w_sc[ka:, :], preferred_element_type=F32))
    o_ref[...] = o
    h_ref[...] = _rms(o, nw_ref[...]).astype(h_ref.dtype)


def _out_proj(x, ya, yb, w, nw, tm=512):
    m, d = x.shape
    ka, kb = ya.shape[1], yb.shape[1]
    vmem = 2 * (2 * tm * d * 4 + tm * d * 2 + tm * (ka + kb) * 2) + (ka + kb) * d * 6 \
        + 4 * tm * d * 4
    return pl.pallas_call(
        _out_proj_kernel,
        out_shape=(jax.ShapeDtypeStruct((m, d), F32), jax.ShapeDtypeStruct((m, d), BF16)),
        grid=(m // tm,),
        in_specs=[pl.BlockSpec((tm, d), lambda i: (i, 0)),
                  pl.BlockSpec((tm, ka), lambda i: (i, 0)),
                  pl.BlockSpec((tm, kb), lambda i: (i, 0)),
                  pl.BlockSpec((ka + kb, d), lambda i: (0, 0), pipeline_mode=pl.Buffered(1)),
                  pl.BlockSpec((1, d), lambda i: (0, 0))],
        out_specs=(pl.BlockSpec((tm, d), lambda i: (i, 0)),
                   pl.BlockSpec((tm, d), lambda i: (i, 0))),
        scratch_shapes=[pltpu.VMEM((ka + kb, d), BF16)],
        compiler_params=_params(("arbitrary",), vmem),
        name="mix_out_proj",
    )(x, ya, yb, w, nw)


def _gate_up_kernel(h_ref, wg_ref, wu_ref, cw_ref, cb_ref, o_ref, wg_sc, wu_sc, tail_sc, conv_sc):
    halo = SUBLANES
    tm = conv_sc.shape[0]

    @pl.when(pl.program_id(1) == 0)
    def _():
        wg_sc[...] = wg_ref[...].astype(BF16)
        wu_sc[...] = wu_ref[...].astype(BF16)
        tail_sc[...] = jnp.zeros_like(tail_sc)

    h = h_ref[...]
    g = jnp.dot(h, wg_sc[...], preferred_element_type=F32)
    up = jnp.dot(h, wu_sc[...], preferred_element_type=F32)

    prev = tail_sc[...]
    tail_sc[...] = g[tm - halo:tm, :]

    def conv(cur, back1, back2):
        return (cb_ref[...] + cur * cw_ref[2:3, :] + back1 * cw_ref[1:2, :]
                + back2 * cw_ref[0:1, :])

    conv_sc[...] = conv(g, pltpu.roll(g, 1, axis=0), pltpu.roll(g, 2, axis=0))
    top = g[0:halo, :]
    r = lax.broadcasted_iota(jnp.int32, top.shape, 0)
    back1 = jnp.where(r < 1, pltpu.roll(prev, 1, axis=0), pltpu.roll(top, 1, axis=0))
    back2 = jnp.where(r < 2, pltpu.roll(prev, 2, axis=0), pltpu.roll(top, 2, axis=0))
    conv_sc[0:halo, :] = conv(top, back1, back2)
    o_ref[...] = (_silu(conv_sc[...]) * up).astype(o_ref.dtype)


def _gate_up(h, w_gu, cw, cb, tm=1024, tn=512):
    m, d = h.shape
    nj = D_FF // tn
    vmem = 2 * (tm * d * 2 + 2 * d * tn * 4 + tm * tn * 2) + 2 * d * tn * 2 \
        + 10 * tm * tn * 4
    return pl.pallas_call(
        _gate_up_kernel,
        out_shape=jax.ShapeDtypeStruct((m, D_FF), BF16),
        grid=(nj, m // tm),
        in_specs=[pl.BlockSpec((tm, d), lambda j, i: (i, 0)),
                  pl.BlockSpec((d, tn), lambda j, i: (0, j)),
                  pl.BlockSpec((d, tn), lambda j, i: (0, j + nj)),
                  pl.BlockSpec((FFN_CONV, tn), lambda j, i: (0, j)),
                  pl.BlockSpec((1, tn), lambda j, i: (0, j))],
        out_specs=pl.BlockSpec((tm, tn), lambda j, i: (i, j)),
        scratch_shapes=[pltpu.VMEM((d, tn), BF16), pltpu.VMEM((d, tn), BF16),
                        pltpu.VMEM((SUBLANES, tn), F32), pltpu.VMEM((tm, tn), F32)],
        compiler_params=_params(("arbitrary", "arbitrary"), vmem),
        name="ffn_gate_up",
    )(h, w_gu, w_gu, cw, cb)


def _down_kernel(a_ref, w_ref, x_ref, o_ref, w_sc):
    @pl.when(pl.program_id(1) == 0)
    def _():
        w_sc[...] = w_ref[...].astype(BF16)

    o_ref[...] = x_ref[...] + jnp.dot(a_ref[...], w_sc[...], preferred_element_type=F32)


def _down(act, w, x, tm=512, tn=512):
    m, d = x.shape
    kk = act.shape[1]
    vmem = 2 * (tm * kk * 2 + kk * tn * 4 + 2 * tm * tn * 4) + kk * tn * 2 + 2 * tm * tn * 4
    return pl.pallas_call(
        _down_kernel,
        out_shape=jax.ShapeDtypeStruct((m, d), F32),
        grid=(d // tn, m // tm),
        in_specs=[pl.BlockSpec((tm, kk), lambda j, i: (i, 0)),
                  pl.BlockSpec((kk, tn), lambda j, i: (0, j)),
                  pl.BlockSpec((tm, tn), lambda j, i: (i, j))],
        out_specs=pl.BlockSpec((tm, tn), lambda j, i: (i, j)),
        scratch_shapes=[pltpu.VMEM((kk, tn), BF16)],
        compiler_params=_params(("arbitrary", "arbitrary"), vmem),
        name="ffn_down",
    )(act, w, x)


def _ple_kernel(x_ref, p_ref, nw_ref, wg_ref, wp_ref, fw_ref, o_ref, wg_sc, wp_sc, *, final):
    @pl.when(pl.program_id(0) == 0)
    def _():
        wg_sc[...] = wg_ref[...].astype(BF16)
        wp_sc[...] = wp_ref[...].astype(BF16)

    x = x_ref[...]
    h = _rms(x, nw_ref[...]).astype(BF16)
    gate = jax.nn.sigmoid(jnp.dot(h, wg_sc[...], preferred_element_type=F32))
    proj = jnp.dot(p_ref[...].astype(BF16), wp_sc[...], preferred_element_type=F32)
    x = x + gate * proj
    if final:
        x = _rms(x, fw_ref[...])
    o_ref[...] = x


def _ple(x, p, nw, wg, wp, fw, final, tm=512):
    m, d = x.shape
    dp = p.shape[1]
    vmem = 2 * (2 * tm * d * 4 + tm * dp * 4) + (d + dp) * d * 6 + 6 * tm * d * 4
    resident = lambda shape: pl.BlockSpec(shape, lambda i: (0, 0), pipeline_mode=pl.Buffered(1))
    return pl.pallas_call(
        functools.partial(_ple_kernel, final=final),
        out_shape=jax.ShapeDtypeStruct((m, d), F32),
        grid=(m // tm,),
        in_specs=[pl.BlockSpec((tm, d), lambda i: (i, 0)),
                  pl.BlockSpec((tm, dp), lambda i: (i, 0)),
                  pl.BlockSpec((1, d), lambda i: (0, 0)),
                  resident((d, d)),
                  resident((dp, d)),
                  pl.BlockSpec((1, d), lambda i: (0, 0))],
        out_specs=pl.BlockSpec((tm, d), lambda i: (i, 0)),
        scratch_shapes=[pltpu.VMEM((d, d), BF16), pltpu.VMEM((dp, d), BF16)],
        compiler_params=_params(("arbitrary",), vmem),
        name="ple_gate",
    )(x, p, nw, wg, wp, fw)


def _layer(x, p, mix_norm_w, w_in, fox_forget_bias, ssd_conv_w, ssd_conv_b, ssd_dt_bias,
           ssd_A_log, ssd_D, ssd_norm_w, w_out, ffn_norm_w, w_gate_up, ffn_conv_w,
           ffn_conv_b, w_down, ple_norm_w, w_ple_gate, w_ple_proj, final_norm_w, final):
    s = x.shape[0]
    row = lambda v: v.reshape(1, -1).astype(F32)

    w_inT = jnp.swapaxes(w_in, 0, 1).astype(F32)
    w_u2T = jnp.concatenate(
        [w_inT[_OFF_XS:_OFF_B], w_inT[_OFF_Z:_OFF_DT], w_inT[_OFF_B:_OFF_Z],
         w_inT[_OFF_F:_OFF_XS], w_inT[_OFF_DT:_IN_COLS],
         jnp.zeros((LANES - FOX_HEADS - SSD_HEADS, D_MODEL), F32)], axis=0)

    h_mix = _norm(x, row(mix_norm_w))
    qkv = _proj(h_mix, w_inT, _OFF_F, BF16, tm=1024, tn=512, name="in_proj_qkv")
    u2 = _proj(h_mix, w_u2T, _U2_COLS, F32, tm=1024, tn=_U2_COLS // 3, name="in_proj_ssd")

    b_pad = jnp.zeros((1, LANES), F32).at[0, :FOX_HEADS].set(fox_forget_bias.astype(F32))
    c, aug = _fox_c(u2, b_pad)
    cT = c[:, :FOX_HEADS].T
    y_fox = _fox_attn(qkv, aug, cT[:, None, :])

    dt_raw = u2[:, _U2_SMALL + FOX_HEADS:_U2_SMALL + FOX_HEADS + SSD_HEADS]
    col = lambda v: v.reshape(-1, 1).astype(F32)
    y_ssd = _ssd(u2, dt_raw, dt_raw.T,
                 ssd_conv_w[:, :SSD_WIDTH].astype(F32), row(ssd_conv_b[:SSD_WIDTH]),
                 ssd_conv_w[:, SSD_WIDTH:].astype(F32), row(ssd_conv_b[SSD_WIDTH:]),
                 row(ssd_dt_bias), col(ssd_dt_bias), row(ssd_A_log), col(ssd_A_log),
                 row(jnp.repeat(ssd_D, HEAD_DIM)), row(ssd_norm_w))

    x, h_ffn = _out_proj(x, y_fox, y_ssd, w_out.astype(F32), row(ffn_norm_w))

    act = _gate_up(h_ffn, w_gate_up.astype(F32), ffn_conv_w.astype(F32), row(ffn_conv_b))
    x = _down(act, w_down.astype(F32), x)

    return _ple(x, p, row(ple_norm_w), w_ple_gate.astype(F32), w_ple_proj.astype(F32),
                row(final_norm_w), final)


def kernel(x, p, mix_norm_w, w_in, fox_forget_bias, ssd_conv_w, ssd_conv_b, ssd_dt_bias,
           ssd_A_log, ssd_D, ssd_norm_w, w_out, ffn_norm_w, w_gate_up, ffn_conv_w,
           ffn_conv_b, w_down, ple_norm_w, w_ple_gate, w_ple_proj, final_norm_w):
    bsz, s, d = x.shape
    depth = p.shape[0]
    outs = []
    for b in range(bsz):
        xb = x[b]
        for i in range(depth):
            xb = _layer(xb, p[i, b], mix_norm_w[i], w_in[i], fox_forget_bias[i], ssd_conv_w[i],
                        ssd_conv_b[i], ssd_dt_bias[i], ssd_A_log[i], ssd_D[i], ssd_norm_w[i],
                        w_out[i], ffn_norm_w[i], w_gate_up[i], ffn_conv_w[i], ffn_conv_b[i],
                        w_down[i], ple_norm_w[i], w_ple_gate[i], w_ple_proj[i], final_norm_w,
                        final=(i == depth - 1))
        outs.append(xb)
    return outs[0][None] if bsz == 1 else jnp.stack(outs, axis=0)
```

```python
import functools

import numpy as np
import jax
import jax.numpy as jnp
from jax import lax
from jax.experimental import pallas as pl
from jax.experimental.pallas import tpu as pltpu

F32 = jnp.float32
BF16 = jnp.bfloat16

D_MODEL = 2048
SEQ = 8192
D_PLE = 256
HEAD_DIM = 64
FOX_WIDTH = 1024
FOX_HEADS = 16
SSD_WIDTH = 1024
SSD_HEADS = 16
SSD_GROUPS = 2
SSD_STATE = 128
SSD_CONV = 4
D_FF = 5632
FFN_CONV = 3
EPS = 1e-6
LOG2_E = 1.4426950408889634

_OFF_Q = 0
_OFF_F = 3 * FOX_WIDTH
_OFF_XS = _OFF_F + FOX_HEADS
_OFF_B = _OFF_XS + SSD_WIDTH
_OFF_C = _OFF_B + SSD_GROUPS * SSD_STATE
_OFF_Z = _OFF_C + SSD_GROUPS * SSD_STATE
_OFF_DT = _OFF_Z + SSD_WIDTH
_IN_COLS = _OFF_DT + SSD_HEADS

LANES = 128
SUBLANES = 8
VMEM_CAP = 60 * 1024 * 1024

_BC_WIDTH = 2 * SSD_GROUPS * SSD_STATE
_U2_COLS = 2 * SSD_WIDTH + _BC_WIDTH + LANES
_U2_SMALL = 2 * SSD_WIDTH + _BC_WIDTH


def _params(sem, vmem_bytes, claim_all=False):
    limit = VMEM_CAP if claim_all else min(VMEM_CAP, vmem_bytes)
    return pltpu.CompilerParams(dimension_semantics=sem, vmem_limit_bytes=int(limit))


def _rms(xf, w):
    ms = jnp.mean(xf * xf, axis=-1, keepdims=True)
    return xf * lax.rsqrt(ms + EPS) * w


def _split3(a):
    hi = a.astype(BF16)
    r1 = a - hi.astype(F32)
    mid = r1.astype(BF16)
    lo = (r1 - mid.astype(F32)).astype(BF16)
    return hi, mid, lo


def _tri_left(tri, a):
    return sum(jnp.dot(tri, t, preferred_element_type=F32) for t in _split3(a))


def _tri_right(a, tri):
    return sum(jnp.dot(t, tri, preferred_element_type=F32) for t in _split3(a))


def _softplus(v):
    return jnp.maximum(v, 0.0) + jnp.log1p(jnp.exp(-jnp.abs(v)))


def _silu(v):
    return v * jax.nn.sigmoid(v)


def _norm_kernel(x_ref, nw_ref, h_ref):
    h_ref[...] = _rms(x_ref[...], nw_ref[...]).astype(h_ref.dtype)


def _norm(x, nw, tm=512):
    m, d = x.shape
    return pl.pallas_call(
        _norm_kernel,
        out_shape=jax.ShapeDtypeStruct((m, d), BF16),
        grid=(m // tm,),
        in_specs=[pl.BlockSpec((tm, d), lambda i: (i, 0)),
                  pl.BlockSpec((1, d), lambda i: (0, 0))],
        out_specs=pl.BlockSpec((tm, d), lambda i: (i, 0)),
        compiler_params=_params(("arbitrary",), 2 * tm * d * 6 + 4 * tm * d * 4),
        name="mix_norm",
    )(x, nw)


def _proj_kernel(h_ref, wT_ref, o_ref, w_sc):
    @pl.when(pl.program_id(1) == 0)
    def _():
        w_sc[...] = wT_ref[...].astype(BF16)

    o_ref[...] = lax.dot_general(h_ref[...], w_sc[...], (((1,), (1,)), ((), ())),
                                 preferred_element_type=F32).astype(o_ref.dtype)


def _proj(h, wT, n_cols, out_dtype, tm, tn, name):
    m, k = h.shape
    ob = jnp.dtype(out_dtype).itemsize
    vmem = 2 * (tm * k * 2 + k * tn * 4 + tm * tn * ob) + k * tn * 2 + 2 * tm * tn * 4
    return pl.pallas_call(
        _proj_kernel,
        out_shape=jax.ShapeDtypeStruct((m, n_cols), out_dtype),
        grid=(n_cols // tn, m // tm),
        in_specs=[pl.BlockSpec((tm, k), lambda j, i: (i, 0)),
                  pl.BlockSpec((tn, k), lambda j, i: (j, 0))],
        out_specs=pl.BlockSpec((tm, tn), lambda j, i: (i, j)),
        scratch_shapes=[pltpu.VMEM((tn, k), BF16)],
        compiler_params=_params(("arbitrary", "arbitrary"), vmem),
        name=name,
    )(h, wT)


_AUG_ONES = 6
_N_PAIRS = FOX_HEADS // 2


def _aug_selector():
    sel = np.zeros((LANES, _N_PAIRS * LANES), np.float32)
    for pr in range(_N_PAIRS):
        for e in range(2):
            for j in range(3):
                sel[j * FOX_HEADS + 2 * pr + e, pr * LANES + 3 * e + j] = -1.0
        sel[3 * FOX_HEADS, pr * LANES + _AUG_ONES:pr * LANES + _AUG_ONES + 3] = 1.0
    return jnp.asarray(sel, BF16)


def _fox_c_kernel(f_ref, b_ref, sel_ref, c_ref, aug_ref, carry_sc):
    @pl.when(pl.program_id(0) == 0)
    def _():
        carry_sc[...] = jnp.zeros_like(carry_sc)

    z = f_ref[...] + b_ref[...]
    lf = jnp.minimum(z, 0.0) - jnp.log1p(jnp.exp(-jnp.abs(z)))
    t = lf.shape[0]
    row = lax.broadcasted_iota(jnp.int32, (t, t), 0)
    col = lax.broadcasted_iota(jnp.int32, (t, t), 1)
    tril = (col <= row).astype(BF16)
    cum = _tri_left(tril, lf) + carry_sc[...]
    c_ref[...] = cum
    carry_sc[...] = cum[t - 1:t, :]
    hi, mid, lo = _split3(cum * LOG2_E)
    lane = lax.broadcasted_iota(jnp.int32, cum.shape, 1)
    h = FOX_HEADS
    parts = jnp.where(
        lane < h, hi.astype(F32),
        jnp.where(lane < 2 * h, pltpu.roll(mid.astype(F32), h, axis=1),
                  jnp.where(lane < 3 * h, pltpu.roll(lo.astype(F32), 2 * h, axis=1),
                            jnp.where(lane == 3 * h, 1.0, 0.0))))
    aug_ref[...] = jnp.dot(parts.astype(BF16), sel_ref[...],
                           preferred_element_type=F32).astype(BF16)


def _fox_c(u2, b_pad, t=256):
    s = u2.shape[0]
    aug_cols = _N_PAIRS * LANES
    return pl.pallas_call(
        _fox_c_kernel,
        out_shape=(jax.ShapeDtypeStruct((s, LANES), F32),
                   jax.ShapeDtypeStruct((s, aug_cols), BF16)),
        grid=(s // t,),
        in_specs=[pl.BlockSpec((t, LANES), lambda i: (i, _U2_SMALL // LANES)),
                  pl.BlockSpec((1, LANES), lambda i: (0, 0)),
                  pl.BlockSpec((LANES, aug_cols), lambda i: (0, 0))],
        out_specs=(pl.BlockSpec((t, LANES), lambda i: (i, 0)),
                   pl.BlockSpec((t, aug_cols), lambda i: (i, 0))),
        scratch_shapes=[pltpu.VMEM((1, LANES), F32)],
        compiler_params=_params(("arbitrary",), 16 * 1024 * 1024, claim_all=True),
        name="fox_cumlogf",
    )(u2, b_pad, _aug_selector())


_V_ROWS = HEAD_DIM + 16


def _fox_attn_kernel(q_ref, k_ref, v_ref, aug_ref, cq_ref, o_ref, vT_sc, acc_sc, s0_sc, s1_sc,
                     w_sc, m_sc, smax0_sc, smax1_sc, *, t):
    qi = pl.program_id(1)
    n_kv = k_ref.shape[0] // t

    @pl.when(qi == 0)
    def _():
        pad_row = lax.broadcasted_iota(jnp.int32, (_V_ROWS - HEAD_DIM, t), 0)
        ones_rows = jnp.where(pad_row == 0, 1.0, 0.0).astype(BF16)

        def transpose_v(c, carry):
            st = pl.multiple_of(c * t, t)
            vT = v_ref[pl.ds(st, t), :].astype(F32).T.astype(BF16)
            for e in range(2):
                vT_sc[e, 0:HEAD_DIM, pl.ds(st, t)] = vT[e * HEAD_DIM:(e + 1) * HEAD_DIM, :]
                vT_sc[e, HEAD_DIM:_V_ROWS, pl.ds(st, t)] = ones_rows
            return carry
        lax.fori_loop(0, n_kv, transpose_v, 0)

    scale = HEAD_DIM ** -0.5 * LOG2_E
    qT = q_ref[...].astype(F32).T * scale
    row = lax.broadcasted_iota(jnp.int32, (LANES, t), 0)
    tk = s0_sc.shape[1]
    key_idx = lax.broadcasted_iota(jnp.int32, (tk, t), 0)
    qry_idx = lax.broadcasted_iota(jnp.int32, (tk, t), 1)

    ws = []
    for e in range(2):
        qTe = jnp.where(row // HEAD_DIM == e, qT, 0.0)
        cq_hi, cq_mid, cq_lo = (part.astype(F32)
                                for part in _split3(cq_ref[e] * LOG2_E))
        top = lax.broadcasted_iota(jnp.int32, (2 * SUBLANES, t), 0)
        aug_rows = jnp.where(
            top // 3 == e, 1.0,
            jnp.where(top == _AUG_ONES, cq_hi,
                      jnp.where(top == _AUG_ONES + 1, cq_mid,
                                jnp.where(top == _AUG_ONES + 2, cq_lo, 0.0))))
        unused = jnp.zeros((LANES - 2 * SUBLANES, t), F32)
        ws.append(jnp.concatenate([qTe, aug_rows, unused], axis=0).astype(BF16))
        acc_sc[e] = jnp.zeros((_V_ROWS, t), F32)

    def scores_into(j, s_sc, smax_sc):
        start = pl.multiple_of(j * tk, tk)
        keys = jnp.concatenate([k_ref[pl.ds(start, tk), :], aug_ref[pl.ds(start, tk), :]],
                               axis=1)
        for e in range(2):
            s = jnp.dot(keys, ws[e], preferred_element_type=F32)
            s_sc[e] = s
            smax_sc[e] = jnp.max(s, axis=0, keepdims=True)

    def softmax_pv(j, s_sc, smax_sc, m_prev, masked):
        start = pl.multiple_of(j * tk, tk)
        if masked:
            visible = key_idx + start <= qry_idx + qi * t
        m_out, probs, alphas = [], [], []
        for e in range(2):
            s = s_sc[e]
            if masked:
                s = jnp.where(visible, s, -jnp.inf)
                m_new = jnp.maximum(m_prev[e], jnp.max(s, axis=0, keepdims=True))
            else:
                m_new = jnp.maximum(m_prev[e], smax_sc[e])
            probs.append(jnp.exp2(s - m_new).astype(BF16))
            alphas.append(jnp.exp2(m_prev[e] - m_new))
            m_out.append(m_new)
        for e in range(2):
            acc_sc[e] = alphas[e] * acc_sc[e] + jnp.dot(
                vT_sc[e, :, pl.ds(start, tk)], probs[e], preferred_element_type=F32)
        return tuple(m_out)

    def two_blocks(i, m):
        scores_into(2 * i + 1, s1_sc, smax1_sc)
        m = softmax_pv(2 * i, s0_sc, smax0_sc, m, False)
        scores_into(2 * i + 2, s0_sc, smax0_sc)
        return softmax_pv(2 * i + 1, s1_sc, smax1_sc, m, False)

    def unrolled(n):
        def trip(i, m):
            for r in range(n):
                m = two_blocks(n * i + r, m)
            return m
        return trip

    m_init = jnp.full((1, t), -1e30, F32)
    scores_into(0, s0_sc, smax0_sc)
    fours, rest = qi // 4, qi % 4
    m = lax.fori_loop(0, fours, unrolled(4), (m_init, m_init))
    m = lax.fori_loop(2 * fours, 2 * fours + rest // 2, unrolled(2), m)
    m = lax.fori_loop(4 * fours + 2 * (rest // 2), qi, two_blocks, m)
    upper = slice(tk, t)
    last = pl.multiple_of((2 * qi + 1) * tk, tk)
    last_keys = jnp.concatenate([k_ref[pl.ds(last, tk), :], aug_ref[pl.ds(last, tk), :]], axis=1)
    for e in range(2):
        w_sc[e] = ws[e]
        s1_sc[e, :, 0:tk] = jnp.dot(last_keys, w_sc[e, :, upper], preferred_element_type=F32)
    m = softmax_pv(2 * qi, s0_sc, smax0_sc, m, True)
    visible = (lax.broadcasted_iota(jnp.int32, (tk, tk), 0)
               <= lax.broadcasted_iota(jnp.int32, (tk, tk), 1))
    for e in range(2):
        s = jnp.where(visible, s1_sc[e, :, 0:tk], -jnp.inf)
        m_sc[e] = m[e]
        m_upper = m_sc[e, :, upper]
        m_new = jnp.maximum(m_upper, jnp.max(s, axis=0, keepdims=True))
        p = jnp.exp2(s - m_new).astype(BF16)
        acc_sc[e, :, upper] = jnp.exp2(m_upper - m_new) * acc_sc[e, :, upper] + jnp.dot(
            vT_sc[e, :, pl.ds(last, tk)], p, preferred_element_type=F32)
    outs = []
    for e in range(2):
        acc = acc_sc[e]
        outs.append(acc[0:HEAD_DIM, :] / acc[HEAD_DIM:HEAD_DIM + 1, :])
    o_ref[...] = jnp.concatenate(outs, axis=0).T.astype(o_ref.dtype)


def _fox_attn(qkv, aug, cq, t=512):
    s = qkv.shape[0]
    kcol = FOX_WIDTH // LANES
    vmem = 2 * (2 * t * LANES * 2 + 3 * s * LANES * 2 + 2 * t * 4) \
        + 2 * _V_ROWS * (s * 2 + t * 4) + 10 * t * t * 4
    return pl.pallas_call(
        functools.partial(_fox_attn_kernel, t=t),
        out_shape=jax.ShapeDtypeStruct((s, FOX_WIDTH), BF16),
        grid=(_N_PAIRS, s // t),
        in_specs=[pl.BlockSpec((t, LANES), lambda pr, qi: (qi, pr)),
                  pl.BlockSpec((s, LANES), lambda pr, qi: (0, kcol + pr)),
                  pl.BlockSpec((s, LANES), lambda pr, qi: (0, 2 * kcol + pr)),
                  pl.BlockSpec((s, LANES), lambda pr, qi: (0, pr)),
                  pl.BlockSpec((2, 1, t), lambda pr, qi: (pr, 0, qi))],
        out_specs=pl.BlockSpec((t, LANES), lambda pr, qi: (qi, pr)),
        scratch_shapes=[pltpu.VMEM((2, _V_ROWS, s), BF16), pltpu.VMEM((2, _V_ROWS, t), F32),
                        pltpu.VMEM((2, t // 2, t), F32), pltpu.VMEM((2, t // 2, t), F32),
                        pltpu.VMEM((2, 2 * LANES, t), BF16), pltpu.VMEM((2, 1, t), F32),
                        pltpu.VMEM((2, 1, t), F32), pltpu.VMEM((2, 1, t), F32)],
        compiler_params=_params(("arbitrary", "arbitrary"), vmem, claim_all=True),
        name="fox_attention",
    )(qkv, qkv, qkv, aug, cq)


def _ssd_kernel(xs_ref, z_ref, bc_ref, dt_ref, dtT_ref, cwx_ref, cbx_ref, cwbc_ref, cbbc_ref,
                dtb_ref, dtbT_ref, alog_ref, alogT_ref, dskip_ref, nw_ref, y_ref,
                xbuf, bcbuf, xtail_sc, bctail_sc, state_sc, ybuf, *, L):
    halo = SUBLANES

    @pl.when(pl.program_id(0) == 0)
    def _():
        xtail_sc[...] = jnp.zeros_like(xtail_sc)
        bctail_sc[...] = jnp.zeros_like(bctail_sc)
        state_sc[...] = jnp.zeros_like(state_sc)

    def conv_silu(u_ref, tail_sc, buf, w_ref, b_ref):
        cur = u_ref[...]
        prev = tail_sc[...]
        tail_sc[...] = cur[L - halo:L, :]
        last = SSD_CONV - 1

        def taps(c, backs):
            acc = b_ref[...] + c * w_ref[last:last + 1, :]
            for k in range(1, SSD_CONV):
                acc = acc + backs[k - 1] * w_ref[last - k:last - k + 1, :]
            return acc

        buf[...] = taps(cur, [pltpu.roll(cur, k, axis=0) for k in range(1, SSD_CONV)])
        top = cur[0:halo, :]
        r = lax.broadcasted_iota(jnp.int32, top.shape, 0)
        buf[0:halo, :] = taps(top, [
            jnp.where(r < k, pltpu.roll(prev, k, axis=0), pltpu.roll(top, k, axis=0))
            for k in range(1, SSD_CONV)])
        return _silu(buf[...])

    xc = conv_silu(xs_ref, xtail_sc, xbuf, cwx_ref, cbx_ref)
    bcc = conv_silu(bc_ref, bctail_sc, bcbuf, cwbc_ref, cbbc_ref)

    dt = _softplus(dt_ref[...] + dtb_ref[...])
    dtT = _softplus(dtT_ref[...] + dtbT_ref[...])
    a = dt * (-jnp.exp(alog_ref[...]) * LOG2_E)
    aT = dtT * (-jnp.exp(alogT_ref[...]) * LOG2_E)

    row = lax.broadcasted_iota(jnp.int32, (L, L), 0)
    col = lax.broadcasted_iota(jnp.int32, (L, L), 1)
    causal = col <= row
    a_cum = _tri_left(causal.astype(BF16), a)
    a_cumT = _tri_right(aT, (row <= col).astype(BF16))
    a_last = a_cum[L - 1:L, :]
    a_lastT = a_cumT[:, L - 1:L]
    exp_acum = jnp.exp2(a_cum)
    w_endT = jnp.exp2(a_lastT - a_cumT) * dtT
    chunk_decay = jnp.exp2(a_last)
    src_T = a_cumT - jnp.log2(dtT)

    n = SSD_STATE
    heads_per_group = SSD_HEADS // SSD_GROUPS
    first_half = lax.broadcasted_iota(jnp.int32, (1, LANES), 1) < HEAD_DIM
    for g in range(SSD_GROUPS):
        bg = bcc[:, g * n:(g + 1) * n]
        cg = bcc[:, SSD_GROUPS * n + g * n:SSD_GROUPS * n + (g + 1) * n]
        bgT = bg.T
        cb = lax.dot_general(cg.astype(BF16), bg.astype(BF16), (((1,), (1,)), ((), ())),
                             preferred_element_type=F32)
        for pair in range(heads_per_group // 2):
            h0 = g * heads_per_group + 2 * pair
            cols = slice(h0 * HEAD_DIM, (h0 + 2) * HEAD_DIM)
            x_pair = xc[:, cols]
            x_pair_b = x_pair.astype(BF16)
            st = state_sc[:, cols]
            st_b = st.astype(BF16)
            ys, upds = [], []
            for h in (h0, h0 + 1):
                seg = a_cum[:, h:h + 1] - src_T[h:h + 1, :]
                mix = cb * jnp.exp2(jnp.where(causal, seg, -jnp.inf))
                y = jnp.dot(mix.astype(BF16), x_pair_b, preferred_element_type=F32)
                ys.append(y + jnp.dot((cg * exp_acum[:, h:h + 1]).astype(BF16), st_b,
                                      preferred_element_type=F32))
                upds.append(jnp.dot((bgT * w_endT[h:h + 1, :]).astype(BF16), x_pair_b,
                                    preferred_element_type=F32))
            ybuf[:, cols] = jnp.where(first_half, ys[0], ys[1]) + dskip_ref[:, cols] * x_pair
            keep = jnp.where(first_half, chunk_decay[:, h0:h0 + 1], chunk_decay[:, h0 + 1:h0 + 2])
            state_sc[:, cols] = keep * st + jnp.where(first_half, upds[0], upds[1])

    y = ybuf[...] * _silu(z_ref[...])
    y_ref[...] = _rms(y, nw_ref[...]).astype(y_ref.dtype)


def _ssd(u2, dt_raw, dt_rawT, cwx, cbx, cwbc, cbbc, dtb, dtbT, alog, alogT, dskip, nw, L=256):
    s = u2.shape[0]
    w = SSD_WIDTH
    full = lambda shape: pl.BlockSpec(shape, lambda i: (0,) * len(shape))
    vmem = 2 * (2 * L * w * 4 + L * _BC_WIDTH * 4 + L * w * 2) + (2 * L + 16) * w * 4 \
        + (L + 8) * _BC_WIDTH * 4 + SSD_STATE * w * 4 + 24 * L * L * 4 + 6 * L * w * 4
    return pl.pallas_call(
        functools.partial(_ssd_kernel, L=L),
        out_shape=jax.ShapeDtypeStruct((s, w), BF16),
        grid=(s // L,),
        in_specs=[pl.BlockSpec((L, w), lambda i: (i, 0)),
                  pl.BlockSpec((L, w), lambda i: (i, 1)),
                  pl.BlockSpec((L, _BC_WIDTH), lambda i: (i, 2 * w // _BC_WIDTH)),
                  pl.BlockSpec((L, SSD_HEADS), lambda i: (i, 0)),
                  pl.BlockSpec((SSD_HEADS, L), lambda i: (0, i)),
                  full((SSD_CONV, w)), full((1, w)),
                  full((SSD_CONV, _BC_WIDTH)), full((1, _BC_WIDTH)),
                  full((1, SSD_HEADS)), full((SSD_HEADS, 1)),
                  full((1, SSD_HEADS)), full((SSD_HEADS, 1)),
                  full((1, w)), full((1, w))],
        out_specs=pl.BlockSpec((L, w), lambda i: (i, 0)),
        scratch_shapes=[pltpu.VMEM((L, w), F32),
                        pltpu.VMEM((L, _BC_WIDTH), F32),
                        pltpu.VMEM((SUBLANES, w), F32),
                        pltpu.VMEM((SUBLANES, _BC_WIDTH), F32),
                        pltpu.VMEM((SSD_STATE, w), F32),
                        pltpu.VMEM((L, w), F32)],
        compiler_params=_params(("arbitrary",), vmem, claim_all=True),
        name="ssd_mixer",
    )(u2, u2, u2, dt_raw, dt_rawT, cwx, cbx, cwbc, cbbc, dtb, dtbT, alog, alogT, dskip, nw)


def _out_proj_kernel(x_ref, ya_ref, yb_ref, w_ref, nw_ref, o_ref, h_ref, w_sc):
    @pl.when(pl.program_id(0) == 0)
    def _():
        w_sc[...] = w_ref[...].astype(BF16)

    ka = ya_ref.shape[1]
    o = (x_ref[...]
         + jnp.dot(ya_ref[...], w_sc[0:ka, :], preferred_element_type=F32)
         + jnp.dot(yb_ref[...], w_sc[ka:, :], preferred_element_type=F32))
    o_ref[...] = o
    h_ref[...] = _rms(o, nw_ref[...]).astype(h_ref.dtype)


def _out_proj(x, ya, yb, w, nw, tm=512):
    m, d = x.shape
    ka, kb = ya.shape[1], yb.shape[1]
    vmem = 2 * (2 * tm * d * 4 + tm * d * 2 + tm * (ka + kb) * 2) + (ka + kb) * d * 6 \
        + 4 * tm * d * 4
    return pl.pallas_call(
        _out_proj_kernel,
        out_shape=(jax.ShapeDtypeStruct((m, d), F32), jax.ShapeDtypeStruct((m, d), BF16)),
        grid=(m // tm,),
        in_specs=[pl.BlockSpec((tm, d), lambda i: (i, 0)),
                  pl.BlockSpec((tm, ka), lambda i: (i, 0)),
                  pl.BlockSpec((tm, kb), lambda i: (i, 0)),
                  pl.BlockSpec((ka + kb, d), lambda i: (0, 0), pipeline_mode=pl.Buffered(1)),
                  pl.BlockSpec((1, d), lambda i: (0, 0))],
        out_specs=(pl.BlockSpec((tm, d), lambda i: (i, 0)),
                   pl.BlockSpec((tm, d), lambda i: (i, 0))),
        scratch_shapes=[pltpu.VMEM((ka + kb, d), BF16)],
        compiler_params=_params(("arbitrary",), vmem),
        name="mix_out_proj",
    )(x, ya, yb, w, nw)


def _gate_up_kernel(h_ref, wg_ref, wu_ref, cw_ref, cb_ref, o_ref, wg_sc, wu_sc, tail_sc, conv_sc):
    halo = SUBLANES
    tm = conv_sc.shape[0]

    @pl.when(pl.program_id(1) == 0)
    def _():
        wg_sc[...] = wg_ref[...].astype(BF16)
        wu_sc[...] = wu_ref[...].astype(BF16)
        tail_sc[...] = jnp.zeros_like(tail_sc)

    h = h_ref[...]
    g = jnp.dot(h, wg_sc[...], preferred_element_type=F32)
    up = jnp.dot(h, wu_sc[...], preferred_element_type=F32)

    prev = tail_sc[...]
    tail_sc[...] = g[tm - halo:tm, :]

    def conv(cur, back1, back2):
        return (cb_ref[...] + cur * cw_ref[2:3, :] + back1 * cw_ref[1:2, :]
                + back2 * cw_ref[0:1, :])

    conv_sc[...] = conv(g, pltpu.roll(g, 1, axis=0), pltpu.roll(g, 2, axis=0))
    top = g[0:halo, :]
    r = lax.broadcasted_iota(jnp.int32, top.shape, 0)
    back1 = jnp.where(r < 1, pltpu.roll(prev, 1, axis=0), pltpu.roll(top, 1, axis=0))
    back2 = jnp.where(r < 2, pltpu.roll(prev, 2, axis=0), pltpu.roll(top, 2, axis=0))
    conv_sc[0:halo, :] = conv(top, back1, back2)
    o_ref[...] = (_silu(conv_sc[...]) * up).astype(o_ref.dtype)


def _gate_up(h, w_gu, cw, cb, tm=1024, tn=512):
    m, d = h.shape
    nj = D_FF // tn
    vmem = 2 * (tm * d * 2 + 2 * d * tn * 4 + tm * tn * 2) + 2 * d * tn * 2 \
        + 10 * tm * tn * 4
    return pl.pallas_call(
        _gate_up_kernel,
        out_shape=jax.ShapeDtypeStruct((m, D_FF), BF16),
        grid=(nj, m // tm),
        in_specs=[pl.BlockSpec((tm, d), lambda j, i: (i, 0)),
                  pl.BlockSpec((d, tn), lambda j, i: (0, j)),
                  pl.BlockSpec((d, tn), lambda j, i: (0, j + nj)),
                  pl.BlockSpec((FFN_CONV, tn), lambda j, i: (0, j)),
                  pl.BlockSpec((1, tn), lambda j, i: (0, j))],
        out_specs=pl.BlockSpec((tm, tn), lambda j, i: (i, j)),
        scratch_shapes=[pltpu.VMEM((d, tn), BF16), pltpu.VMEM((d, tn), BF16),
                        pltpu.VMEM((SUBLANES, tn), F32), pltpu.VMEM((tm, tn), F32)],
        compiler_params=_params(("arbitrary", "arbitrary"), vmem),
        name="ffn_gate_up",
    )(h, w_gu, w_gu, cw, cb)


def _down_kernel(a_ref, w_ref, x_ref, o_ref, w_sc):
    @pl.when(pl.program_id(1) == 0)
    def _():
        w_sc[...] = w_ref[...].astype(BF16)

    o_ref[...] = x_ref[...] + jnp.dot(a_ref[...], w_sc[...], preferred_element_type=F32)


def _down(act, w, x, tm=512, tn=512):
    m, d = x.shape
    kk = act.shape[1]
    vmem = 2 * (tm * kk * 2 + kk * tn * 4 + 2 * tm * tn * 4) + kk * tn * 2 + 2 * tm * tn * 4
    return pl.pallas_call(
        _down_kernel,
        out_shape=jax.ShapeDtypeStruct((m, d), F32),
        grid=(d // tn, m // tm),
        in_specs=[pl.BlockSpec((tm, kk), lambda j, i: (i, 0)),
                  pl.BlockSpec((kk, tn), lambda j, i: (0, j)),
                  pl.BlockSpec((tm, tn), lambda j, i: (i, j))],
        out_specs=pl.BlockSpec((tm, tn), lambda j, i: (i, j)),
        scratch_shapes=[pltpu.VMEM((kk, tn), BF16)],
        compiler_params=_params(("arbitrary", "arbitrary"), vmem),
        name="ffn_down",
    )(act, w, x)


def _ple_kernel(x_ref, p_ref, nw_ref, wg_ref, wp_ref, fw_ref, o_ref, wg_sc, wp_sc, *, final):
    @pl.when(pl.program_id(0) == 0)
    def _():
        wg_sc[...] = wg_ref[...].astype(BF16)
        wp_sc[...] = wp_ref[...].astype(BF16)

    x = x_ref[...]
    h = _rms(x, nw_ref[...]).astype(BF16)
    gate = jax.nn.sigmoid(jnp.dot(h, wg_sc[...], preferred_element_type=F32))
    proj = jnp.dot(p_ref[...].astype(BF16), wp_sc[...], preferred_element_type=F32)
    x = x + gate * proj
    if final:
        x = _rms(x, fw_ref[...])
    o_ref[...] = x


def _ple(x, p, nw, wg, wp, fw, final, tm=512):
    m, d = x.shape
    dp = p.shape[1]
    vmem = 2 * (2 * tm * d * 4 + tm * dp * 4) + (d + dp) * d * 6 + 6 * tm * d * 4
    resident = lambda shape: pl.BlockSpec(shape, lambda i: (0, 0), pipeline_mode=pl.Buffered(1))
    return pl.pallas_call(
        functools.partial(_ple_kernel, final=final),
        out_shape=jax.ShapeDtypeStruct((m, d), F32),
        grid=(m // tm,),
        in_specs=[pl.BlockSpec((tm, d), lambda i: (i, 0)),
                  pl.BlockSpec((tm, dp), lambda i: (i, 0)),
                  pl.BlockSpec((1, d), lambda i: (0, 0)),
                  resident((d, d)),
                  resident((dp, d)),
                  pl.BlockSpec((1, d), lambda i: (0, 0))],
        out_specs=pl.BlockSpec((tm, d), lambda i: (i, 0)),
        scratch_shapes=[pltpu.VMEM((d, d), BF16), pltpu.VMEM((dp, d), BF16)],
        compiler_params=_params(("arbitrary",), vmem),
        name="ple_gate",
    )(x, p, nw, wg, wp, fw)


def _layer(x, p, mix_norm_w, w_in, fox_forget_bias, ssd_conv_w, ssd_conv_b, ssd_dt_bias,
           ssd_A_log, ssd_D, ssd_norm_w, w_out, ffn_norm_w, w_gate_up, ffn_conv_w,
           ffn_conv_b, w_down, ple_norm_w, w_ple_gate, w_ple_proj, final_norm_w, final):
    s = x.shape[0]
    row = lambda v: v.reshape(1, -1).astype(F32)

    w_inT = jnp.swapaxes(w_in, 0, 1).astype(F32)
    w_u2T = jnp.concatenate(
        [w_inT[_OFF_XS:_OFF_B], w_inT[_OFF_Z:_OFF_DT], w_inT[_OFF_B:_OFF_Z],
         w_inT[_OFF_F:_OFF_XS], w_inT[_OFF_DT:_IN_COLS],
         jnp.zeros((LANES - FOX_HEADS - SSD_HEADS, D_MODEL), F32)], axis=0)

    h_mix = _norm(x, row(mix_norm_w))
    qkv = _proj(h_mix, w_inT, _OFF_F, BF16, tm=1024, tn=512, name="in_proj_qkv")
    u2 = _proj(h_mix, w_u2T, _U2_COLS, F32, tm=1024, tn=_U2_COLS // 3, name="in_proj_ssd")

    b_pad = jnp.zeros((1, LANES), F32).at[0, :FOX_HEADS].set(fox_forget_bias.astype(F32))
    c, aug = _fox_c(u2, b_pad)
    cT = c[:, :FOX_HEADS].T
    y_fox = _fox_attn(qkv, aug, cT[:, None, :])

    dt_raw = u2[:, _U2_SMALL + FOX_HEADS:_U2_SMALL + FOX_HEADS + SSD_HEADS]
    col = lambda v: v.reshape(-1, 1).astype(F32)
    y_ssd = _ssd(u2, dt_raw, dt_raw.T,
                 ssd_conv_w[:, :SSD_WIDTH].astype(F32), row(ssd_conv_b[:SSD_WIDTH]),
                 ssd_conv_w[:, SSD_WIDTH:].astype(F32), row(ssd_conv_b[SSD_WIDTH:]),
                 row(ssd_dt_bias), col(ssd_dt_bias), row(ssd_A_log), col(ssd_A_log),
                 row(jnp.repeat(ssd_D, HEAD_DIM)), row(ssd_norm_w))

    x, h_ffn = _out_proj(x, y_fox, y_ssd, w_out.astype(F32), row(ffn_norm_w))

    act = _gate_up(h_ffn, w_gate_up.astype(F32), ffn_conv_w.astype(F32), row(ffn_conv_b))
    x = _down(act, w_down.astype(F32), x)

    return _ple(x, p, row(ple_norm_w), w_ple_gate.astype(F32), w_ple_proj.astype(F32),
                row(final_norm_w), final)


def kernel(x, p, mix_norm_w, w_in, fox_forget_bias, ssd_conv_w, ssd_conv_b, ssd_dt_bias,
           ssd_A_log, ssd_D, ssd_norm_w, w_out, ffn_norm_w, w_gate_up, ffn_conv_w,
           ffn_conv_b, w_down, ple_norm_w, w_ple_gate, w_ple_proj, final_norm_w):
    bsz, s, d = x.shape
    depth = p.shape[0]
    outs = []
    for b in range(bsz):
        xb = x[b]
        for i in range(depth):
            xb = _layer(xb, p[i, b], mix_norm_w[i], w_in[i], fox_forget_bias[i], ssd_conv_w[i],
                        ssd_conv_b[i], ssd_dt_bias[i], ssd_A_log[i], ssd_D[i], ssd_norm_w[i],
                        w_out[i], ffn_norm_w[i], w_gate_up[i], ffn_conv_w[i], ffn_conv_b[i],
                        w_down[i], ple_norm_w[i], w_ple_gate[i], w_ple_proj[i], final_norm_w,
                        final=(i == depth - 1))
        outs.append(xb)
    return outs[0][None] if bsz == 1 else jnp.stack(outs, axis=0)
```

```python
import functools

import numpy as np
import jax
import jax.numpy as jnp
from jax import lax
from jax.experimental import pallas as pl
from jax.experimental.pallas import tpu as pltpu

F32 = jnp.float32
BF16 = jnp.bfloat16

D_MODEL = 2048
SEQ = 8192
D_PLE = 256
HEAD_DIM = 64
FOX_WIDTH = 1024
FOX_HEADS = 16
SSD_WIDTH = 1024
SSD_HEADS = 16
SSD_GROUPS = 2
SSD_STATE = 128
SSD_CONV = 4
D_FF = 5632
FFN_CONV = 3
EPS = 1e-6
LOG2_E = 1.4426950408889634

_OFF_Q = 0
_OFF_F = 3 * FOX_WIDTH
_OFF_XS = _OFF_F + FOX_HEADS
_OFF_B = _OFF_XS + SSD_WIDTH
_OFF_C = _OFF_B + SSD_GROUPS * SSD_STATE
_OFF_Z = _OFF_C + SSD_GROUPS * SSD_STATE
_OFF_DT = _OFF_Z + SSD_WIDTH
_IN_COLS = _OFF_DT + SSD_HEADS

LANES = 128
SUBLANES = 8
VMEM_CAP = 60 * 1024 * 1024

_BC_WIDTH = 2 * SSD_GROUPS * SSD_STATE
_U2_COLS = 2 * SSD_WIDTH + _BC_WIDTH


def _params(sem, vmem_bytes, claim_all=False):
    limit = VMEM_CAP if claim_all else min(VMEM_CAP, vmem_bytes)
    return pltpu.CompilerParams(dimension_semantics=sem, vmem_limit_bytes=int(limit))


def _rms(xf, w):
    ms = jnp.mean(xf * xf, axis=-1, keepdims=True)
    return xf * lax.rsqrt(ms + EPS) * w


def _split3(a):
    hi = a.astype(BF16)
    r1 = a - hi.astype(F32)
    mid = r1.astype(BF16)
    lo = (r1 - mid.astype(F32)).astype(BF16)
    return hi, mid, lo


def _tri_left(tri, a):
    return sum(jnp.dot(tri, t, preferred_element_type=F32) for t in _split3(a))


def _tri_right(a, tri):
    return sum(jnp.dot(t, tri, preferred_element_type=F32) for t in _split3(a))


def _softplus(v):
    return jnp.maximum(v, 0.0) + jnp.log1p(jnp.exp(-jnp.abs(v)))


def _silu(v):
    return v * jax.nn.sigmoid(v)


def _norm_kernel(x_ref, nw_ref, wT_ref, h_ref, small_ref):
    h = _rms(x_ref[...], nw_ref[...]).astype(h_ref.dtype)
    h_ref[...] = h
    small_ref[...] = lax.dot_general(h, wT_ref[...].astype(BF16), (((1,), (1,)), ((), ())),
                                     preferred_element_type=F32)


def _norm(x, nw, w_smallT, tm=512):
    m, d = x.shape
    n_small = w_smallT.shape[0]
    vmem = 2 * (tm * d * 6 + n_small * d * 4 + tm * n_small * 4) + 4 * tm * d * 4
    return pl.pallas_call(
        _norm_kernel,
        out_shape=(jax.ShapeDtypeStruct((m, d), BF16),
                   jax.ShapeDtypeStruct((m, n_small), F32)),
        grid=(m // tm,),
        in_specs=[pl.BlockSpec((tm, d), lambda i: (i, 0)),
                  pl.BlockSpec((1, d), lambda i: (0, 0)),
                  pl.BlockSpec((n_small, d), lambda i: (0, 0))],
        out_specs=(pl.BlockSpec((tm, d), lambda i: (i, 0)),
                   pl.BlockSpec((tm, n_small), lambda i: (i, 0))),
        compiler_params=_params(("arbitrary",), vmem),
        name="mix_norm",
    )(x, nw, w_smallT)


def _proj_kernel(h_ref, wT_ref, o_ref, w_sc):
    @pl.when(pl.program_id(1) == 0)
    def _():
        w_sc[...] = wT_ref[...].astype(BF16)

    o_ref[...] = lax.dot_general(h_ref[...], w_sc[...], (((1,), (1,)), ((), ())),
                                 preferred_element_type=F32).astype(o_ref.dtype)


def _proj(h, wT, n_cols, out_dtype, tm, tn, name):
    m, k = h.shape
    ob = jnp.dtype(out_dtype).itemsize
    vmem = 2 * (tm * k * 2 + k * tn * 4 + tm * tn * ob) + k * tn * 2 + 2 * tm * tn * 4
    return pl.pallas_call(
        _proj_kernel,
        out_shape=jax.ShapeDtypeStruct((m, n_cols), out_dtype),
        grid=(n_cols // tn, m // tm),
        in_specs=[pl.BlockSpec((tm, k), lambda j, i: (i, 0)),
                  pl.BlockSpec((tn, k), lambda j, i: (j, 0))],
        out_specs=pl.BlockSpec((tm, tn), lambda j, i: (i, j)),
        scratch_shapes=[pltpu.VMEM((tn, k), BF16)],
        compiler_params=_params(("arbitrary", "arbitrary"), vmem),
        name=name,
    )(h, wT)


_AUG_ONES = 6
_N_PAIRS = FOX_HEADS // 2


def _aug_selector():
    sel = np.zeros((LANES, _N_PAIRS * LANES), np.float32)
    for pr in range(_N_PAIRS):
        for e in range(2):
            for j in range(3):
                sel[j * FOX_HEADS + 2 * pr + e, pr * LANES + 3 * e + j] = -1.0
        sel[3 * FOX_HEADS, pr * LANES + _AUG_ONES:pr * LANES + _AUG_ONES + 3] = 1.0
    return jnp.asarray(sel, BF16)


def _fox_c_kernel(f_ref, b_ref, sel_ref, c_ref, aug_ref, carry_sc):
    @pl.when(pl.program_id(0) == 0)
    def _():
        carry_sc[...] = jnp.zeros_like(carry_sc)

    z = f_ref[...] + b_ref[...]
    lf = jnp.minimum(z, 0.0) - jnp.log1p(jnp.exp(-jnp.abs(z)))
    t = lf.shape[0]
    row = lax.broadcasted_iota(jnp.int32, (t, t), 0)
    col = lax.broadcasted_iota(jnp.int32, (t, t), 1)
    tril = (col <= row).astype(BF16)
    cum = _tri_left(tril, lf) + carry_sc[...]
    c_ref[...] = cum
    carry_sc[...] = cum[t - 1:t, :]
    hi, mid, lo = _split3(cum * LOG2_E)
    lane = lax.broadcasted_iota(jnp.int32, cum.shape, 1)
    h = FOX_HEADS
    parts = jnp.where(
        lane < h, hi.astype(F32),
        jnp.where(lane < 2 * h, pltpu.roll(mid.astype(F32), h, axis=1),
                  jnp.where(lane < 3 * h, pltpu.roll(lo.astype(F32), 2 * h, axis=1),
                            jnp.where(lane == 3 * h, 1.0, 0.0))))
    aug_ref[...] = jnp.dot(parts.astype(BF16), sel_ref[...],
                           preferred_element_type=F32).astype(BF16)


def _fox_c(small, b_pad, t=256):
    s = small.shape[0]
    aug_cols = _N_PAIRS * LANES
    return pl.pallas_call(
        _fox_c_kernel,
        out_shape=(jax.ShapeDtypeStruct((s, LANES), F32),
                   jax.ShapeDtypeStruct((s, aug_cols), BF16)),
        grid=(s // t,),
        in_specs=[pl.BlockSpec((t, LANES), lambda i: (i, 0)),
                  pl.BlockSpec((1, LANES), lambda i: (0, 0)),
                  pl.BlockSpec((LANES, aug_cols), lambda i: (0, 0))],
        out_specs=(pl.BlockSpec((t, LANES), lambda i: (i, 0)),
                   pl.BlockSpec((t, aug_cols), lambda i: (i, 0))),
        scratch_shapes=[pltpu.VMEM((1, LANES), F32)],
        compiler_params=_params(("arbitrary",), 16 * 1024 * 1024, claim_all=True),
        name="fox_cumlogf",
    )(small, b_pad, _aug_selector())


_V_ROWS = HEAD_DIM + 16


def _fox_attn_kernel(q_ref, k_ref, v_ref, aug_ref, cq_ref, o_ref, vT_sc, acc_sc, s0_sc, s1_sc,
                     w_sc, m_sc, smax0_sc, smax1_sc, *, t):
    qi = pl.program_id(1)
    n_kv = k_ref.shape[0] // t
    n_heads = q_ref.shape[1] // HEAD_DIM

    @pl.when(qi == 0)
    def _():
        pad_row = lax.broadcasted_iota(jnp.int32, (_V_ROWS - HEAD_DIM, t), 0)
        ones_rows = jnp.where(pad_row == 0, 1.0, 0.0).astype(BF16)

        def transpose_v(c, carry):
            st = pl.multiple_of(c * t, t)
            vT = v_ref[pl.ds(st, t), :].astype(F32).T.astype(BF16)
            for e in range(n_heads):
                vT_sc[e, 0:HEAD_DIM, pl.ds(st, t)] = vT[e * HEAD_DIM:(e + 1) * HEAD_DIM, :]
                vT_sc[e, HEAD_DIM:_V_ROWS, pl.ds(st, t)] = ones_rows
            return carry
        lax.fori_loop(0, n_kv, transpose_v, 0)

    scale = HEAD_DIM ** -0.5 * LOG2_E
    qT = q_ref[...].astype(F32).T * scale
    row = lax.broadcasted_iota(jnp.int32, (LANES, t), 0)
    tk = s0_sc.shape[1]
    key_idx = lax.broadcasted_iota(jnp.int32, (tk, t), 0)
    qry_idx = lax.broadcasted_iota(jnp.int32, (tk, t), 1)

    ws = []
    for e in range(n_heads):
        pair, side = divmod(e, 2)
        qTe = jnp.where(row // HEAD_DIM == side, qT[pair * LANES:(pair + 1) * LANES, :], 0.0)
        cq_hi, cq_mid, cq_lo = (part.astype(F32)
                                for part in _split3(cq_ref[e] * LOG2_E))
        top = lax.broadcasted_iota(jnp.int32, (2 * SUBLANES, t), 0)
        aug_rows = jnp.where(
            top // 3 == side, 1.0,
            jnp.where(top == _AUG_ONES, cq_hi,
                      jnp.where(top == _AUG_ONES + 1, cq_mid,
                                jnp.where(top == _AUG_ONES + 2, cq_lo, 0.0))))
        unused = jnp.zeros((LANES - 2 * SUBLANES, t), F32)
        ws.append(jnp.concatenate([qTe, aug_rows, unused], axis=0).astype(BF16))
        acc_sc[e] = jnp.zeros((_V_ROWS, t), F32)

    def key_blocks(j):
        start = pl.multiple_of(j * tk, tk)
        return [jnp.concatenate([k_ref[pl.ds(start, tk), pr * LANES:(pr + 1) * LANES],
                                 aug_ref[pl.ds(start, tk), pr * LANES:(pr + 1) * LANES]], axis=1)
                for pr in range(n_heads // 2)]

    def scores_into(j, s_sc, smax_sc):
        keys = key_blocks(j)
        for e in range(n_heads):
            s = jnp.dot(keys[e // 2], ws[e], preferred_element_type=F32)
            s_sc[e] = s
            smax_sc[e] = jnp.max(s, axis=0, keepdims=True)

    def softmax_pv(j, s_sc, smax_sc, m_prev, masked):
        start = pl.multiple_of(j * tk, tk)
        if masked:
            visible = key_idx + start <= qry_idx + qi * t
        m_out, probs, alphas = [], [], []
        for e in range(n_heads):
            s = s_sc[e]
            if masked:
                s = jnp.where(visible, s, -jnp.inf)
                m_new = jnp.maximum(m_prev[e], jnp.max(s, axis=0, keepdims=True))
            else:
                m_new = jnp.maximum(m_prev[e], smax_sc[e])
            probs.append(jnp.exp2(s - m_new).astype(BF16))
            alphas.append(jnp.exp2(m_prev[e] - m_new))
            m_out.append(m_new)
        for e in range(n_heads):
            acc_sc[e] = alphas[e] * acc_sc[e] + jnp.dot(
                vT_sc[e, :, pl.ds(start, tk)], probs[e], preferred_element_type=F32)
        return tuple(m_out)

    def two_blocks(i, m):
        scores_into(2 * i + 1, s1_sc, smax1_sc)
        m = softmax_pv(2 * i, s0_sc, smax0_sc, m, False)
        scores_into(2 * i + 2, s0_sc, smax0_sc)
        return softmax_pv(2 * i + 1, s1_sc, smax1_sc, m, False)

    def unrolled(n):
        def trip(i, m):
            for r in range(n):
                m = two_blocks(n * i + r, m)
            return m
        return trip

    m_init = jnp.full((1, t), -1e30, F32)
    scores_into(0, s0_sc, smax0_sc)
    fours, rest = qi // 4, qi % 4
    m = lax.fori_loop(0, fours, unrolled(4), (m_init,) * n_heads)
    m = lax.fori_loop(2 * fours, 2 * fours + rest // 2, unrolled(2), m)
    m = lax.fori_loop(4 * fours + 2 * (rest // 2), qi, two_blocks, m)
    upper = slice(tk, t)
    last = pl.multiple_of((2 * qi + 1) * tk, tk)
    last_keys = key_blocks(2 * qi + 1)
    for e in range(n_heads):
        w_sc[e] = ws[e]
        s1_sc[e, :, 0:tk] = jnp.dot(last_keys[e // 2], w_sc[e, :, upper],
                                    preferred_element_type=F32)
    m = softmax_pv(2 * qi, s0_sc, smax0_sc, m, True)
    visible = (lax.broadcasted_iota(jnp.int32, (tk, tk), 0)
               <= lax.broadcasted_iota(jnp.int32, (tk, tk), 1))
    for e in range(n_heads):
        s = jnp.where(visible, s1_sc[e, :, 0:tk], -jnp.inf)
        m_sc[e] = m[e]
        m_upper = m_sc[e, :, upper]
        m_new = jnp.maximum(m_upper, jnp.max(s, axis=0, keepdims=True))
        p = jnp.exp2(s - m_new).astype(BF16)
        acc_sc[e, :, upper] = jnp.exp2(m_upper - m_new) * acc_sc[e, :, upper] + jnp.dot(
            vT_sc[e, :, pl.ds(last, tk)], p, preferred_element_type=F32)
    outs = []
    for e in range(n_heads):
        acc = acc_sc[e]
        outs.append(acc[0:HEAD_DIM, :] / acc[HEAD_DIM:HEAD_DIM + 1, :])
    o_ref[...] = jnp.concatenate(outs, axis=0).T.astype(o_ref.dtype)


def _fox_attn(qkv, aug, cq, t=512, heads=4):
    s = qkv.shape[0]
    w = heads * HEAD_DIM
    kcol = FOX_WIDTH // w
    vmem = 2 * (2 * t * w * 2 + 3 * s * w * 2 + heads * t * 4) \
        + heads * _V_ROWS * (s * 2 + t * 4) + 5 * heads * t * t * 4
    return pl.pallas_call(
        functools.partial(_fox_attn_kernel, t=t),
        out_shape=jax.ShapeDtypeStruct((s, FOX_WIDTH), BF16),
        grid=(FOX_HEADS // heads, s // t),
        in_specs=[pl.BlockSpec((t, w), lambda g, qi: (qi, g)),
                  pl.BlockSpec((s, w), lambda g, qi: (0, kcol + g)),
                  pl.BlockSpec((s, w), lambda g, qi: (0, 2 * kcol + g)),
                  pl.BlockSpec((s, w), lambda g, qi: (0, g)),
                  pl.BlockSpec((heads, 1, t), lambda g, qi: (g, 0, qi))],
        out_specs=pl.BlockSpec((t, w), lambda g, qi: (qi, g)),
        scratch_shapes=[pltpu.VMEM((heads, _V_ROWS, s), BF16),
                        pltpu.VMEM((heads, _V_ROWS, t), F32),
                        pltpu.VMEM((heads, t // 2, t), F32), pltpu.VMEM((heads, t // 2, t), F32),
                        pltpu.VMEM((heads, 2 * LANES, t), BF16), pltpu.VMEM((heads, 1, t), F32),
                        pltpu.VMEM((heads, 1, t), F32), pltpu.VMEM((heads, 1, t), F32)],
        compiler_params=_params(("arbitrary", "arbitrary"), vmem, claim_all=True),
        name="fox_attention",
    )(qkv, qkv, qkv, aug, cq)


def _ssd_kernel(xs_ref, z_ref, bc_ref, dt_ref, dtT_ref, cwx_ref, cbx_ref, cwbc_ref, cbbc_ref,
                dtb_ref, dtbT_ref, alog_ref, alogT_ref, dskip_ref, nw_ref, y_ref,
                xbuf, bcbuf, xtail_sc, bctail_sc, state_sc, ybuf, *, L):
    halo = SUBLANES

    @pl.when(pl.program_id(0) == 0)
    def _():
        xtail_sc[...] = jnp.zeros_like(xtail_sc)
        bctail_sc[...] = jnp.zeros_like(bctail_sc)
        state_sc[...] = jnp.zeros_like(state_sc)

    def conv_silu(u_ref, tail_sc, buf, w_ref, b_ref):
        cur = u_ref[...]
        prev = tail_sc[...]
        tail_sc[...] = cur[L - halo:L, :]
        last = SSD_CONV - 1

        def taps(c, backs):
            acc = b_ref[...] + c * w_ref[last:last + 1, :]
            for k in range(1, SSD_CONV):
                acc = acc + backs[k - 1] * w_ref[last - k:last - k + 1, :]
            return acc

        buf[...] = taps(cur, [pltpu.roll(cur, k, axis=0) for k in range(1, SSD_CONV)])
        top = cur[0:halo, :]
        r = lax.broadcasted_iota(jnp.int32, top.shape, 0)
        buf[0:halo, :] = taps(top, [
            jnp.where(r < k, pltpu.roll(prev, k, axis=0), pltpu.roll(top, k, axis=0))
            for k in range(1, SSD_CONV)])
        return _silu(buf[...])

    xc = conv_silu(xs_ref, xtail_sc, xbuf, cwx_ref, cbx_ref)
    bcc = conv_silu(bc_ref, bctail_sc, bcbuf, cwbc_ref, cbbc_ref)

    dt = _softplus(dt_ref[...] + dtb_ref[...])
    dtT = _softplus(dtT_ref[...] + dtbT_ref[...])
    a = dt * (-jnp.exp(alog_ref[...]) * LOG2_E)
    aT = dtT * (-jnp.exp(alogT_ref[...]) * LOG2_E)

    row = lax.broadcasted_iota(jnp.int32, (L, L), 0)
    col = lax.broadcasted_iota(jnp.int32, (L, L), 1)
    causal = col <= row
    a_cum = _tri_left(causal.astype(BF16), a)
    a_cumT = _tri_right(aT, (row <= col).astype(BF16))
    a_last = a_cum[L - 1:L, :]
    a_lastT = a_cumT[:, L - 1:L]
    exp_acum = jnp.exp2(a_cum)
    w_endT = jnp.exp2(a_lastT - a_cumT) * dtT
    chunk_decay = jnp.exp2(a_last)
    src_T = a_cumT - jnp.log2(dtT)

    n = SSD_STATE
    heads_per_group = SSD_HEADS // SSD_GROUPS
    first_half = lax.broadcasted_iota(jnp.int32, (1, LANES), 1) < HEAD_DIM
    for g in range(SSD_GROUPS):
        bg = bcc[:, g * n:(g + 1) * n]
        cg = bcc[:, SSD_GROUPS * n + g * n:SSD_GROUPS * n + (g + 1) * n]
        bgT = bg.T
        cb = lax.dot_general(cg.astype(BF16), bg.astype(BF16), (((1,), (1,)), ((), ())),
                             preferred_element_type=F32)
        for pair in range(heads_per_group // 2):
            h0 = g * heads_per_group + 2 * pair
            cols = slice(h0 * HEAD_DIM, (h0 + 2) * HEAD_DIM)
            x_pair = xc[:, cols]
            x_pair_b = x_pair.astype(BF16)
            st = state_sc[:, cols]
            st_b = st.astype(BF16)
            ys, upds = [], []
            for h in (h0, h0 + 1):
                seg = a_cum[:, h:h + 1] - src_T[h:h + 1, :]
                mix = cb * jnp.exp2(jnp.where(causal, seg, -jnp.inf))
                y = jnp.dot(mix.astype(BF16), x_pair_b, preferred_element_type=F32)
                ys.append(y + jnp.dot((cg * exp_acum[:, h:h + 1]).astype(BF16), st_b,
                                      preferred_element_type=F32))
                upds.append(jnp.dot((bgT * w_endT[h:h + 1, :]).astype(BF16), x_pair_b,
                                    preferred_element_type=F32))
            ybuf[:, cols] = jnp.where(first_half, ys[0], ys[1]) + dskip_ref[:, cols] * x_pair
            keep = jnp.where(first_half, chunk_decay[:, h0:h0 + 1], chunk_decay[:, h0 + 1:h0 + 2])
            state_sc[:, cols] = keep * st + jnp.where(first_half, upds[0], upds[1])

    y = ybuf[...] * _silu(z_ref[...])
    y_ref[...] = _rms(y, nw_ref[...]).astype(y_ref.dtype)


def _ssd(u2, dt_raw, dt_rawT, cwx, cbx, cwbc, cbbc, dtb, dtbT, alog, alogT, dskip, nw, L=256):
    s = u2.shape[0]
    w = SSD_WIDTH
    full = lambda shape: pl.BlockSpec(shape, lambda i: (0,) * len(shape))
    vmem = 2 * (2 * L * w * 4 + L * _BC_WIDTH * 4 + L * w * 2) + (2 * L + 16) * w * 4 \
        + (L + 8) * _BC_WIDTH * 4 + SSD_STATE * w * 4 + 24 * L * L * 4 + 6 * L * w * 4
    return pl.pallas_call(
        functools.partial(_ssd_kernel, L=L),
        out_shape=jax.ShapeDtypeStruct((s, w), BF16),
        grid=(s // L,),
        in_specs=[pl.BlockSpec((L, w), lambda i: (i, 0)),
                  pl.BlockSpec((L, w), lambda i: (i, 1)),
                  pl.BlockSpec((L, _BC_WIDTH), lambda i: (i, 2 * w // _BC_WIDTH)),
                  pl.BlockSpec((L, SSD_HEADS), lambda i: (i, 0)),
                  pl.BlockSpec((SSD_HEADS, L), lambda i: (0, i)),
                  full((SSD_CONV, w)), full((1, w)),
                  full((SSD_CONV, _BC_WIDTH)), full((1, _BC_WIDTH)),
                  full((1, SSD_HEADS)), full((SSD_HEADS, 1)),
                  full((1, SSD_HEADS)), full((SSD_HEADS, 1)),
                  full((1, w)), full((1, w))],
        out_specs=pl.BlockSpec((L, w), lambda i: (i, 0)),
        scratch_shapes=[pltpu.VMEM((L, w), F32),
                        pltpu.VMEM((L, _BC_WIDTH), F32),
                        pltpu.VMEM((SUBLANES, w), F32),
                        pltpu.VMEM((SUBLANES, _BC_WIDTH), F32),
                        pltpu.VMEM((SSD_STATE, w), F32),
                        pltpu.VMEM((L, w), F32)],
        compiler_params=_params(("arbitrary",), vmem, claim_all=True),
        name="ssd_mixer",
    )(u2, u2, u2, dt_raw, dt_rawT, cwx, cbx, cwbc, cbbc, dtb, dtbT, alog, alogT, dskip, nw)


def _out_proj_kernel(x_ref, ya_ref, yb_ref, w_ref, nw_ref, o_ref, h_ref, w_sc):
    @pl.when(pl.program_id(0) == 0)
    def _():
        w_sc[...] = w_ref[...].astype(BF16)

    ka = ya_ref.shape[1]
    o = (x_ref[...]
         + jnp.dot(ya_ref[...], w_sc[0:ka, :], preferred_element_type=F32)
         + jnp.dot(yb_ref[...], w_sc[ka:, :], preferred_element_type=F32))
    o_ref[...] = o
    h_ref[...] = _rms(o, nw_ref[...]).astype(h_ref.dtype)


def _out_proj(x, ya, yb, w, nw, tm=512):
    m, d = x.shape
    ka, kb = ya.shape[1], yb.shape[1]
    vmem = 2 * (2 * tm * d * 4 + tm * d * 2 + tm * (ka + kb) * 2) + (ka + kb) * d * 6 \
        + 4 * tm * d * 4
    return pl.pallas_call(
        _out_proj_kernel,
        out_shape=(jax.ShapeDtypeStruct((m, d), F32), jax.ShapeDtypeStruct((m, d), BF16)),
        grid=(m // tm,),
        in_specs=[pl.BlockSpec((tm, d), lambda i: (i, 0)),
                  pl.BlockSpec((tm, ka), lambda i: (i, 0)),
                  pl.BlockSpec((tm, kb), lambda i: (i, 0)),
                  pl.BlockSpec((ka + kb, d), lambda i: (0, 0), pipeline_mode=pl.Buffered(1)),
                  pl.BlockSpec((1, d), lambda i: (0, 0))],
        out_specs=(pl.BlockSpec((tm, d), lambda i: (i, 0)),
                   pl.BlockSpec((tm, d), lambda i: (i, 0))),
        scratch_shapes=[pltpu.VMEM((ka + kb, d), BF16)],
        compiler_params=_params(("arbitrary",), vmem),
        name="mix_out_proj",
    )(x, ya, yb, w, nw)


def _gate_up_kernel(h_ref, wg_ref, wu_ref, cw_ref, cb_ref, o_ref, wg_sc, wu_sc, tail_sc, conv_sc):
    halo = SUBLANES
    tm = conv_sc.shape[0]

    @pl.when(pl.program_id(1) == 0)
    def _():
        wg_sc[...] = wg_ref[...].astype(BF16)
        wu_sc[...] = wu_ref[...].astype(BF16)
        tail_sc[...] = jnp.zeros_like(tail_sc)

    h = h_ref[...]
    g = jnp.dot(h, wg_sc[...], preferred_element_type=F32)
    up = jnp.dot(h, wu_sc[...], preferred_element_type=F32)

    prev = tail_sc[...]
    tail_sc[...] = g[tm - halo:tm, :]

    def conv(cur, back1, back2):
        return (cb_ref[...] + cur * cw_ref[2:3, :] + back1 * cw_ref[1:2, :]
                + back2 * cw_ref[0:1, :])

    conv_sc[...] = conv(g, pltpu.roll(g, 1, axis=0), pltpu.roll(g, 2, axis=0))
    top = g[0:halo, :]
    r = lax.broadcasted_iota(jnp.int32, top.shape, 0)
    back1 = jnp.where(r < 1, pltpu.roll(prev, 1, axis=0), pltpu.roll(top, 1, axis=0))
    back2 = jnp.where(r < 2, pltpu.roll(prev, 2, axis=0), pltpu.roll(top, 2, axis=0))
    conv_sc[0:halo, :] = conv(top, back1, back2)
    o_ref[...] = (_silu(conv_sc[...]) * up).astype(o_ref.dtype)


def _gate_up(h, w_gu, cw, cb, tm=1024, tn=512):
    m, d = h.shape
    nj = D_FF // tn
    vmem = 2 * (tm * d * 2 + 2 * d * tn * 4 + tm * tn * 2) + 2 * d * tn * 2 \
        + 10 * tm * tn * 4
    return pl.pallas_call(
        _gate_up_kernel,
        out_shape=jax.ShapeDtypeStruct((m, D_FF), BF16),
        grid=(nj, m // tm),
        in_specs=[pl.BlockSpec((tm, d), lambda j, i: (i, 0)),
                  pl.BlockSpec((d, tn), lambda j, i: (0, j)),
                  pl.BlockSpec((d, tn), lambda j, i: (0, j + nj)),
                  pl.BlockSpec((FFN_CONV, tn), lambda j, i: (0, j)),
                  pl.BlockSpec((1, tn), lambda j, i: (0, j))],
        out_specs=pl.BlockSpec((tm, tn), lambda j, i: (i, j)),
        scratch_shapes=[pltpu.VMEM((d, tn), BF16), pltpu.VMEM((d, tn), BF16),
                        pltpu.VMEM((SUBLANES, tn), F32), pltpu.VMEM((tm, tn), F32)],
        compiler_params=_params(("arbitrary", "arbitrary"), vmem),
        name="ffn_gate_up",
    )(h, w_gu, w_gu, cw, cb)


def _down_kernel(a_ref, w_ref, x_ref, o_ref, w_sc):
    @pl.when(pl.program_id(1) == 0)
    def _():
        w_sc[...] = w_ref[...].astype(BF16)

    o_ref[...] = x_ref[...] + jnp.dot(a_ref[...], w_sc[...], preferred_element_type=F32)


def _down(act, w, x, tm=512, tn=512):
    m, d = x.shape
    kk = act.shape[1]
    vmem = 2 * (tm * kk * 2 + kk * tn * 4 + 2 * tm * tn * 4) + kk * tn * 2 + 2 * tm * tn * 4
    return pl.pallas_call(
        _down_kernel,
        out_shape=jax.ShapeDtypeStruct((m, d), F32),
        grid=(d // tn, m // tm),
        in_specs=[pl.BlockSpec((tm, kk), lambda j, i: (i, 0)),
                  pl.BlockSpec((kk, tn), lambda j, i: (0, j)),
                  pl.BlockSpec((tm, tn), lambda j, i: (i, j))],
        out_specs=pl.BlockSpec((tm, tn), lambda j, i: (i, j)),
        scratch_shapes=[pltpu.VMEM((kk, tn), BF16)],
        compiler_params=_params(("arbitrary", "arbitrary"), vmem),
        name="ffn_down",
    )(act, w, x)


def _ple_kernel(x_ref, p_ref, nw_ref, wg_ref, wp_ref, fw_ref, o_ref, wg_sc, wp_sc, *, final):
    @pl.when(pl.program_id(0) == 0)
    def _():
        wg_sc[...] = wg_ref[...].astype(BF16)
        wp_sc[...] = wp_ref[...].astype(BF16)

    x = x_ref[...]
    h = _rms(x, nw_ref[...]).astype(BF16)
    gate = jax.nn.sigmoid(jnp.dot(h, wg_sc[...], preferred_element_type=F32))
    proj = jnp.dot(p_ref[...].astype(BF16), wp_sc[...], preferred_element_type=F32)
    x = x + gate * proj
    if final:
        x = _rms(x, fw_ref[...])
    o_ref[...] = x


def _ple(x, p, nw, wg, wp, fw, final, tm=512):
    m, d = x.shape
    dp = p.shape[1]
    vmem = 2 * (2 * tm * d * 4 + tm * dp * 4) + (d + dp) * d * 6 + 6 * tm * d * 4
    resident = lambda shape: pl.BlockSpec(shape, lambda i: (0, 0), pipeline_mode=pl.Buffered(1))
    return pl.pallas_call(
        functools.partial(_ple_kernel, final=final),
        out_shape=jax.ShapeDtypeStruct((m, d), F32),
        grid=(m // tm,),
        in_specs=[pl.BlockSpec((tm, d), lambda i: (i, 0)),
                  pl.BlockSpec((tm, dp), lambda i: (i, 0)),
                  pl.BlockSpec((1, d), lambda i: (0, 0)),
                  resident((d, d)),
                  resident((dp, d)),
                  pl.BlockSpec((1, d), lambda i: (0, 0))],
        out_specs=pl.BlockSpec((tm, d), lambda i: (i, 0)),
        scratch_shapes=[pltpu.VMEM((d, d), BF16), pltpu.VMEM((dp, d), BF16)],
        compiler_params=_params(("arbitrary",), vmem),
        name="ple_gate",
    )(x, p, nw, wg, wp, fw)


def _layer(x, p, mix_norm_w, w_in, fox_forget_bias, ssd_conv_w, ssd_conv_b, ssd_dt_bias,
           ssd_A_log, ssd_D, ssd_norm_w, w_out, ffn_norm_w, w_gate_up, ffn_conv_w,
           ffn_conv_b, w_down, ple_norm_w, w_ple_gate, w_ple_proj, final_norm_w, final):
    s = x.shape[0]
    row = lambda v: v.reshape(1, -1).astype(F32)

    w_inT = jnp.swapaxes(w_in, 0, 1).astype(F32)
    w_u2T = jnp.concatenate(
        [w_inT[_OFF_XS:_OFF_B], w_inT[_OFF_Z:_OFF_DT], w_inT[_OFF_B:_OFF_Z]], axis=0)
    w_smallT = jnp.concatenate(
        [w_inT[_OFF_F:_OFF_XS], w_inT[_OFF_DT:_IN_COLS],
         jnp.zeros((LANES - FOX_HEADS - SSD_HEADS, D_MODEL), F32)], axis=0)

    h_mix, small = _norm(x, row(mix_norm_w), w_smallT)
    qkv = _proj(h_mix, w_inT, _OFF_F, BF16, tm=1024, tn=512, name="in_proj_qkv")
    u2 = _proj(h_mix, w_u2T, _U2_COLS, F32, tm=1024, tn=_U2_COLS // 2, name="in_proj_ssd")

    b_pad = jnp.zeros((1, LANES), F32).at[0, :FOX_HEADS].set(fox_forget_bias.astype(F32))
    c, aug = _fox_c(small, b_pad)
    cT = c[:, :FOX_HEADS].T
    y_fox = _fox_attn(qkv, aug, cT[:, None, :])

    dt_raw = small[:, FOX_HEADS:FOX_HEADS + SSD_HEADS]
    col = lambda v: v.reshape(-1, 1).astype(F32)
    y_ssd = _ssd(u2, dt_raw, dt_raw.T,
                 ssd_conv_w[:, :SSD_WIDTH].astype(F32), row(ssd_conv_b[:SSD_WIDTH]),
                 ssd_conv_w[:, SSD_WIDTH:].astype(F32), row(ssd_conv_b[SSD_WIDTH:]),
                 row(ssd_dt_bias), col(ssd_dt_bias), row(ssd_A_log), col(ssd_A_log),
                 row(jnp.repeat(ssd_D, HEAD_DIM)), row(ssd_norm_w))

    x, h_ffn = _out_proj(x, y_fox, y_ssd, w_out.astype(F32), row(ffn_norm_w))

    act = _gate_up(h_ffn, w_gate_up.astype(F32), ffn_conv_w.astype(F32), row(ffn_conv_b))
    x = _down(act, w_down.astype(F32), x)

    return _ple(x, p, row(ple_norm_w), w_ple_gate.astype(F32), w_ple_proj.astype(F32),
                row(final_norm_w), final)


def kernel(x, p, mix_norm_w, w_in, fox_forget_bias, ssd_conv_w, ssd_conv_b, ssd_dt_bias,
           ssd_A_log, ssd_D, ssd_norm_w, w_out, ffn_norm_w, w_gate_up, ffn_conv_w,
           ffn_conv_b, w_down, ple_norm_w, w_ple_gate, w_ple_proj, final_norm_w):
    bsz, s, d = x.shape
    depth = p.shape[0]
    outs = []
    for b in range(bsz):
        xb = x[b]
        for i in range(depth):
            xb = _layer(xb, p[i, b], mix_norm_w[i], w_in[i], fox_forget_bias[i], ssd_conv_w[i],
                        ssd_conv_b[i], ssd_dt_bias[i], ssd_A_log[i], ssd_D[i], ssd_norm_w[i],
                        w_out[i], ffn_norm_w[i], w_gate_up[i], ffn_conv_w[i], ffn_conv_b[i],
                        w_down[i], ple_norm_w[i], w_ple_gate[i], w_ple_proj[i], final_norm_w,
                        final=(i == depth - 1))
        outs.append(xb)
    return outs[0][None] if bsz == 1 else jnp.stack(outs, axis=0)
```

```python
import functools

import numpy as np
import jax
import jax.numpy as jnp
from jax import lax
from jax.experimental import pallas as pl
from jax.experimental.pallas import tpu as pltpu

F32 = jnp.float32
BF16 = jnp.bfloat16

D_MODEL = 2048
HEAD_DIM = 64
FOX_WIDTH = 1024
FOX_HEADS = 16
SSD_WIDTH = 1024
SSD_HEADS = 16
SSD_GROUPS = 2
SSD_STATE = 128
SSD_CONV = 4
D_FF = 5632
FFN_CONV = 3
EPS = 1e-6
LOG2_E = 1.4426950408889634

_OFF_Q = 0
_OFF_F = 3 * FOX_WIDTH
_OFF_XS = _OFF_F + FOX_HEADS
_OFF_B = _OFF_XS + SSD_WIDTH
_OFF_C = _OFF_B + SSD_GROUPS * SSD_STATE
_OFF_Z = _OFF_C + SSD_GROUPS * SSD_STATE
_OFF_DT = _OFF_Z + SSD_WIDTH
_IN_COLS = _OFF_DT + SSD_HEADS

LANES = 128
SUBLANES = 8
VMEM_CAP = 60 * 1024 * 1024

_BC_WIDTH = 2 * SSD_GROUPS * SSD_STATE
_U2_COLS = 2 * SSD_WIDTH + _BC_WIDTH


def _params(sem, vmem_bytes, claim_all=False):
    limit = VMEM_CAP if claim_all else min(VMEM_CAP, vmem_bytes)
    return pltpu.CompilerParams(dimension_semantics=sem, vmem_limit_bytes=int(limit))


def _rms(xf, w):
    ms = jnp.mean(xf * xf, axis=-1, keepdims=True)
    return xf * lax.rsqrt(ms + EPS) * w


def _split3(a):
    hi = a.astype(BF16)
    r1 = a - hi.astype(F32)
    mid = r1.astype(BF16)
    lo = (r1 - mid.astype(F32)).astype(BF16)
    return hi, mid, lo


def _tri_left(tri, a):
    return sum(jnp.dot(tri, t, preferred_element_type=F32) for t in _split3(a))


def _tri_right(a, tri):
    return sum(jnp.dot(t, tri, preferred_element_type=F32) for t in _split3(a))


def _softplus(v):
    return jnp.maximum(v, 0.0) + jnp.log1p(jnp.exp(-jnp.abs(v)))


def _silu(v):
    return v / (1.0 + jnp.exp2(v * (-LOG2_E)))


def _norm_kernel(x_ref, nw_ref, wT_ref, h_ref, small_ref):
    h = _rms(x_ref[...], nw_ref[...]).astype(h_ref.dtype)
    h_ref[...] = h
    small_ref[...] = lax.dot_general(h, wT_ref[...].astype(BF16), (((1,), (1,)), ((), ())),
                                     preferred_element_type=F32)


def _norm(x, nw, w_smallT, tm=512):
    m, d = x.shape
    n_small = w_smallT.shape[0]
    vmem = 2 * (tm * d * 6 + n_small * d * 4 + tm * n_small * 4) + 4 * tm * d * 4
    return pl.pallas_call(
        _norm_kernel,
        out_shape=(jax.ShapeDtypeStruct((m, d), BF16),
                   jax.ShapeDtypeStruct((m, n_small), F32)),
        grid=(m // tm,),
        in_specs=[pl.BlockSpec((tm, d), lambda i: (i, 0)),
                  pl.BlockSpec((1, d), lambda i: (0, 0)),
                  pl.BlockSpec((n_small, d), lambda i: (0, 0))],
        out_specs=(pl.BlockSpec((tm, d), lambda i: (i, 0)),
                   pl.BlockSpec((tm, n_small), lambda i: (i, 0))),
        compiler_params=_params(("arbitrary",), vmem),
        name="mix_norm",
    )(x, nw, w_smallT)


def _proj_kernel(h_ref, wT_ref, o_ref, w_sc, *, transposed):
    @pl.when(pl.program_id(1) == 0)
    def _():
        w_sc[...] = wT_ref[...].astype(BF16)

    lhs, rhs = (w_sc[...], h_ref[...]) if transposed else (h_ref[...], w_sc[...])
    o_ref[...] = lax.dot_general(lhs, rhs, (((1,), (1,)), ((), ())),
                                 preferred_element_type=F32).astype(o_ref.dtype)


def _proj(h, wT, n_cols, out_dtype, tm, tn, name, first_col=0, transposed=False):
    m, k = h.shape
    ob = jnp.dtype(out_dtype).itemsize
    j0 = first_col // tn
    vmem = 2 * (tm * k * 2 + k * tn * 4 + tm * tn * ob) + k * tn * 2 + 2 * tm * tn * 4
    return pl.pallas_call(
        functools.partial(_proj_kernel, transposed=transposed),
        out_shape=jax.ShapeDtypeStruct((n_cols, m) if transposed else (m, n_cols), out_dtype),
        grid=(n_cols // tn, m // tm),
        in_specs=[pl.BlockSpec((tm, k), lambda j, i: (i, 0)),
                  pl.BlockSpec((tn, k), lambda j, i: (j0 + j, 0))],
        out_specs=(pl.BlockSpec((tn, tm), lambda j, i: (j, i)) if transposed
                   else pl.BlockSpec((tm, tn), lambda j, i: (i, j))),
        scratch_shapes=[pltpu.VMEM((tn, k), BF16)],
        compiler_params=_params(("arbitrary", "arbitrary"), vmem),
        name=name,
    )(h, wT)


_AUG_ONES = 6
_N_PAIRS = FOX_HEADS // 2


def _aug_selector():
    sel = np.zeros((LANES, _N_PAIRS * LANES), np.float32)
    for pr in range(_N_PAIRS):
        for e in range(2):
            for j in range(3):
                sel[j * FOX_HEADS + 2 * pr + e, pr * LANES + 3 * e + j] = -1.0
        sel[3 * FOX_HEADS, pr * LANES + _AUG_ONES:pr * LANES + _AUG_ONES + 3] = 1.0
    return jnp.asarray(sel, BF16)


def _fox_c_kernel(f_ref, b_ref, sel_ref, c_ref, aug_ref, carry_sc):
    @pl.when(pl.program_id(0) == 0)
    def _():
        carry_sc[...] = jnp.zeros_like(carry_sc)

    sub = _CUM_ROWS
    row = lax.broadcasted_iota(jnp.int32, (sub, sub), 0)
    col = lax.broadcasted_iota(jnp.int32, (sub, sub), 1)
    tril = (col <= row).astype(BF16)
    lane = lax.broadcasted_iota(jnp.int32, (sub, LANES), 1)
    h = FOX_HEADS
    carry = carry_sc[...]
    for r in range(f_ref.shape[0] // sub):
        rows = slice(r * sub, (r + 1) * sub)
        z = f_ref[rows, :] + b_ref[...]
        lf = jnp.minimum(z, 0.0) - jnp.log1p(jnp.exp(-jnp.abs(z)))
        cum = _tri_left(tril, lf) + carry
        c_ref[rows, :] = cum
        carry = cum[sub - 1:sub, :]
        hi, mid, lo = _split3(cum * LOG2_E)
        parts = jnp.where(
            lane < h, hi.astype(F32),
            jnp.where(lane < 2 * h, pltpu.roll(mid.astype(F32), h, axis=1),
                      jnp.where(lane < 3 * h, pltpu.roll(lo.astype(F32), 2 * h, axis=1),
                                jnp.where(lane == 3 * h, 1.0, 0.0))))
        aug_ref[rows, :] = jnp.dot(parts.astype(BF16), sel_ref[...],
                                   preferred_element_type=F32).astype(BF16)
    carry_sc[...] = carry


_CUM_ROWS = 256


def _fox_c(small, b_pad, t=4 * _CUM_ROWS):
    s = small.shape[0]
    aug_cols = _N_PAIRS * LANES
    return pl.pallas_call(
        _fox_c_kernel,
        out_shape=(jax.ShapeDtypeStruct((s, LANES), F32),
                   jax.ShapeDtypeStruct((s, aug_cols), BF16)),
        grid=(s // t,),
        in_specs=[pl.BlockSpec((t, LANES), lambda i: (i, 0)),
                  pl.BlockSpec((1, LANES), lambda i: (0, 0)),
                  pl.BlockSpec((LANES, aug_cols), lambda i: (0, 0))],
        out_specs=(pl.BlockSpec((t, LANES), lambda i: (i, 0)),
                   pl.BlockSpec((t, aug_cols), lambda i: (i, 0))),
        scratch_shapes=[pltpu.VMEM((1, LANES), F32)],
        compiler_params=_params(("arbitrary",), 16 * 1024 * 1024, claim_all=True),
        name="fox_cumlogf",
    )(small, b_pad, _aug_selector())


_V_ROWS = HEAD_DIM + 16


def _fox_attn_kernel(q_ref, k_ref, vT_ref, aug_ref, cq_ref, o_ref, vT_sc, acc_sc, s0_sc, s1_sc,
                     w_sc, m_sc, smax0_sc, smax1_sc, *, t):
    qi = pl.program_id(1)
    n_kv = k_ref.shape[0] // t
    n_heads = q_ref.shape[1] // HEAD_DIM

    @pl.when(qi == 0)
    def _():
        pad_row = lax.broadcasted_iota(jnp.int32, (_V_ROWS - HEAD_DIM, t), 0)
        ones_rows = jnp.where(pad_row == 0, 1.0, 0.0).astype(BF16)

        def place_v(c, carry):
            st = pl.multiple_of(c * t, t)
            for e in range(n_heads):
                vT_sc[e, 0:HEAD_DIM, pl.ds(st, t)] = vT_ref[e * HEAD_DIM:(e + 1) * HEAD_DIM,
                                                            pl.ds(st, t)]
                vT_sc[e, HEAD_DIM:_V_ROWS, pl.ds(st, t)] = ones_rows
            return carry
        lax.fori_loop(0, n_kv, place_v, 0)

    scale = HEAD_DIM ** -0.5 * LOG2_E
    qT = q_ref[...].astype(F32).T * scale
    row = lax.broadcasted_iota(jnp.int32, (LANES, t), 0)
    tk = s0_sc.shape[1]
    key_idx = lax.broadcasted_iota(jnp.int32, (tk, t), 0)
    qry_idx = lax.broadcasted_iota(jnp.int32, (tk, t), 1)

    ws = []
    for e in range(n_heads):
        pair, side = divmod(e, 2)
        qTe = jnp.where(row // HEAD_DIM == side, qT[pair * LANES:(pair + 1) * LANES, :], 0.0)
        cq_hi, cq_mid, cq_lo = (part.astype(F32)
                                for part in _split3(cq_ref[e] * LOG2_E))
        top = lax.broadcasted_iota(jnp.int32, (2 * SUBLANES, t), 0)
        aug_rows = jnp.where(
            top // 3 == side, 1.0,
            jnp.where(top == _AUG_ONES, cq_hi,
                      jnp.where(top == _AUG_ONES + 1, cq_mid,
                                jnp.where(top == _AUG_ONES + 2, cq_lo, 0.0))))
        unused = jnp.zeros((LANES - 2 * SUBLANES, t), F32)
        ws.append(jnp.concatenate([qTe, aug_rows, unused], axis=0).astype(BF16))
        acc_sc[e] = jnp.zeros((_V_ROWS, t), F32)

    def key_blocks(j):
        start = pl.multiple_of(j * tk, tk)
        return [jnp.concatenate([k_ref[pl.ds(start, tk), pr * LANES:(pr + 1) * LANES],
                                 aug_ref[pl.ds(start, tk), pr * LANES:(pr + 1) * LANES]], axis=1)
                for pr in range(n_heads // 2)]

    def scores_into(j, s_sc, smax_sc):
        keys = key_blocks(j)
        for e in range(n_heads):
            s = jnp.dot(keys[e // 2], ws[e], preferred_element_type=F32)
            s_sc[e] = s
            smax_sc[e] = jnp.max(s, axis=0, keepdims=True)

    def softmax_pv(j, s_sc, smax_sc, m_prev, masked):
        start = pl.multiple_of(j * tk, tk)
        if masked:
            visible = key_idx + start <= qry_idx + qi * t
        m_out, probs, alphas = [], [], []
        for e in range(n_heads):
            s = s_sc[e]
            if masked:
                s = jnp.where(visible, s, -jnp.inf)
                m_new = jnp.maximum(m_prev[e], jnp.max(s, axis=0, keepdims=True))
            else:
                m_new = jnp.maximum(m_prev[e], smax_sc[e])
            probs.append(jnp.exp2(s - m_new).astype(BF16))
            alphas.append(jnp.exp2(m_prev[e] - m_new))
            m_out.append(m_new)
        for e in range(n_heads):
            acc_sc[e] = alphas[e] * acc_sc[e] + jnp.dot(
                vT_sc[e, :, pl.ds(start, tk)], probs[e], preferred_element_type=F32)
        return tuple(m_out)

    def two_blocks(i, m):
        scores_into(2 * i + 1, s1_sc, smax1_sc)
        m = softmax_pv(2 * i, s0_sc, smax0_sc, m, False)
        scores_into(2 * i + 2, s0_sc, smax0_sc)
        return softmax_pv(2 * i + 1, s1_sc, smax1_sc, m, False)

    def unrolled(n):
        def trip(i, m):
            for r in range(n):
                m = two_blocks(n * i + r, m)
            return m
        return trip

    m_init = jnp.full((1, t), -1e30, F32)
    scores_into(0, s0_sc, smax0_sc)
    fours, rest = qi // 4, qi % 4
    m = lax.fori_loop(0, fours, unrolled(4), (m_init,) * n_heads)
    m = lax.fori_loop(2 * fours, 2 * fours + rest // 2, unrolled(2), m)
    m = lax.fori_loop(4 * fours + 2 * (rest // 2), qi, two_blocks, m)
    upper = slice(tk, t)
    last = pl.multiple_of((2 * qi + 1) * tk, tk)
    last_keys = key_blocks(2 * qi + 1)
    for e in range(n_heads):
        w_sc[e] = ws[e]
        s1_sc[e, :, 0:tk] = jnp.dot(last_keys[e // 2], w_sc[e, :, upper],
                                    preferred_element_type=F32)
    m = softmax_pv(2 * qi, s0_sc, smax0_sc, m, True)
    visible = (lax.broadcasted_iota(jnp.int32, (tk, tk), 0)
               <= lax.broadcasted_iota(jnp.int32, (tk, tk), 1))
    for e in range(n_heads):
        s = jnp.where(visible, s1_sc[e, :, 0:tk], -jnp.inf)
        m_sc[e] = m[e]
        m_upper = m_sc[e, :, upper]
        m_new = jnp.maximum(m_upper, jnp.max(s, axis=0, keepdims=True))
        p = jnp.exp2(s - m_new).astype(BF16)
        acc_sc[e, :, upper] = jnp.exp2(m_upper - m_new) * acc_sc[e, :, upper] + jnp.dot(
            vT_sc[e, :, pl.ds(last, tk)], p, preferred_element_type=F32)
    outs = []
    for e in range(n_heads):
        acc = acc_sc[e]
        outs.append(acc[0:HEAD_DIM, :] / acc[HEAD_DIM:HEAD_DIM + 1, :])
    o_ref[...] = jnp.concatenate(outs, axis=0).T.astype(o_ref.dtype)


def _fox_attn(qk, vT, aug, cq, t=512, heads=4):
    s = qk.shape[0]
    w = heads * HEAD_DIM
    kcol = FOX_WIDTH // w
    vmem = 2 * (2 * t * w * 2 + 3 * s * w * 2 + heads * t * 4) \
        + heads * _V_ROWS * (s * 2 + t * 4) + 5 * heads * t * t * 4
    return pl.pallas_call(
        functools.partial(_fox_attn_kernel, t=t),
        out_shape=jax.ShapeDtypeStruct((s, FOX_WIDTH), BF16),
        grid=(FOX_HEADS // heads, s // t),
        in_specs=[pl.BlockSpec((t, w), lambda g, qi: (qi, g)),
                  pl.BlockSpec((s, w), lambda g, qi: (0, kcol + g)),
                  pl.BlockSpec((w, s), lambda g, qi: (g, 0)),
                  pl.BlockSpec((s, w), lambda g, qi: (0, g)),
                  pl.BlockSpec((heads, 1, t), lambda g, qi: (g, 0, qi))],
        out_specs=pl.BlockSpec((t, w), lambda g, qi: (qi, g)),
        scratch_shapes=[pltpu.VMEM((heads, _V_ROWS, s), BF16),
                        pltpu.VMEM((heads, _V_ROWS, t), F32),
                        pltpu.VMEM((heads, t // 2, t), F32), pltpu.VMEM((heads, t // 2, t), F32),
                        pltpu.VMEM((heads, 2 * LANES, t), BF16), pltpu.VMEM((heads, 1, t), F32),
                        pltpu.VMEM((heads, 1, t), F32), pltpu.VMEM((heads, 1, t), F32)],
        compiler_params=_params(("arbitrary", "arbitrary"), vmem, claim_all=True),
        name="fox_attention",
    )(qk, qk, vT, aug, cq)


def _ssd_kernel(xs_ref, z_ref, bc_ref, dt_ref, dtT_ref, cwx_ref, cbx_ref, cwbc_ref, cbbc_ref,
                dtb_ref, dtbT_ref, alog_ref, alogT_ref, dskip_ref, nw_ref, y_ref,
                xbuf, bcbuf, xtail_sc, bctail_sc, state_sc, ybuf, *, L):
    halo = SUBLANES

    @pl.when(pl.program_id(0) == 0)
    def _():
        xtail_sc[...] = jnp.zeros_like(xtail_sc)
        bctail_sc[...] = jnp.zeros_like(bctail_sc)
        state_sc[...] = jnp.zeros_like(state_sc)

    def conv_silu(u_ref, tail_sc, buf, w_ref, b_ref):
        cur = u_ref[...]
        prev = tail_sc[...]
        tail_sc[...] = cur[L - halo:L, :]
        last = SSD_CONV - 1

        def taps(c, backs):
            acc = b_ref[...] + c * w_ref[last:last + 1, :]
            for k in range(1, SSD_CONV):
                acc = acc + backs[k - 1] * w_ref[last - k:last - k + 1, :]
            return acc

        buf[...] = taps(cur, [pltpu.roll(cur, k, axis=0) for k in range(1, SSD_CONV)])
        top = cur[0:halo, :]
        r = lax.broadcasted_iota(jnp.int32, top.shape, 0)
        buf[0:halo, :] = taps(top, [
            jnp.where(r < k, pltpu.roll(prev, k, axis=0), pltpu.roll(top, k, axis=0))
            for k in range(1, SSD_CONV)])
        return _silu(buf[...])

    xc = conv_silu(xs_ref, xtail_sc, xbuf, cwx_ref, cbx_ref)
    bcc = conv_silu(bc_ref, bctail_sc, bcbuf, cwbc_ref, cbbc_ref)

    dt = _softplus(dt_ref[...] + dtb_ref[...])
    dtT = _softplus(dtT_ref[...] + dtbT_ref[...])
    a = dt * (-jnp.exp(alog_ref[...]) * LOG2_E)
    aT = dtT * (-jnp.exp(alogT_ref[...]) * LOG2_E)

    row = lax.broadcasted_iota(jnp.int32, (L, L), 0)
    col = lax.broadcasted_iota(jnp.int32, (L, L), 1)
    causal = col <= row
    a_cum = _tri_left(causal.astype(BF16), a)
    a_cumT = _tri_right(aT, (row <= col).astype(BF16))
    a_last = a_cum[L - 1:L, :]
    a_lastT = a_cumT[:, L - 1:L]
    exp_acum = jnp.exp2(a_cum)
    w_endT = jnp.exp2(a_lastT - a_cumT) * dtT
    chunk_decay = jnp.exp2(a_last)
    src_T = a_cumT - jnp.log2(dtT)

    n = SSD_STATE
    heads_per_group = SSD_HEADS // SSD_GROUPS
    first_half = lax.broadcasted_iota(jnp.int32, (1, LANES), 1) < HEAD_DIM
    for g in range(SSD_GROUPS):
        bg = bcc[:, g * n:(g + 1) * n]
        cg = bcc[:, SSD_GROUPS * n + g * n:SSD_GROUPS * n + (g + 1) * n]
        bgT = bg.T
        cb = lax.dot_general(cg.astype(BF16), bg.astype(BF16), (((1,), (1,)), ((), ())),
                             preferred_element_type=F32)
        for pair in range(heads_per_group // 2):
            h0 = g * heads_per_group + 2 * pair
            cols = slice(h0 * HEAD_DIM, (h0 + 2) * HEAD_DIM)
            x_pair = xc[:, cols]
            x_pair_b = x_pair.astype(BF16)
            st = state_sc[:, cols]
            st_b = st.astype(BF16)
            ys, upds = [], []
            for h in (h0, h0 + 1):
                seg = a_cum[:, h:h + 1] - src_T[h:h + 1, :]
                mix = cb * jnp.exp2(jnp.where(causal, seg, -jnp.inf))
                y = jnp.dot(mix.astype(BF16), x_pair_b, preferred_element_type=F32)
                ys.append(y + jnp.dot((cg * exp_acum[:, h:h + 1]).astype(BF16), st_b,
                                      preferred_element_type=F32))
                upds.append(jnp.dot((bgT * w_endT[h:h + 1, :]).astype(BF16), x_pair_b,
                                    preferred_element_type=F32))
            ybuf[:, cols] = jnp.where(first_half, ys[0], ys[1]) + dskip_ref[:, cols] * x_pair
            keep = jnp.where(first_half, chunk_decay[:, h0:h0 + 1], chunk_decay[:, h0 + 1:h0 + 2])
            state_sc[:, cols] = keep * st + jnp.where(first_half, upds[0], upds[1])

    y = ybuf[...] * _silu(z_ref[...])
    y_ref[...] = _rms(y, nw_ref[...]).astype(y_ref.dtype)


def _ssd(u2, dt_raw, dt_rawT, cwx, cbx, cwbc, cbbc, dtb, dtbT, alog, alogT, dskip, nw, L=256):
    s = u2.shape[0]
    w = SSD_WIDTH
    full = lambda shape: pl.BlockSpec(shape, lambda i: (0,) * len(shape))
    vmem = 2 * (2 * L * w * 4 + L * _BC_WIDTH * 4 + L * w * 2) + (2 * L + 16) * w * 4 \
        + (L + 8) * _BC_WIDTH * 4 + SSD_STATE * w * 4 + 24 * L * L * 4 + 6 * L * w * 4
    return pl.pallas_call(
        functools.partial(_ssd_kernel, L=L),
        out_shape=jax.ShapeDtypeStruct((s, w), BF16),
        grid=(s // L,),
        in_specs=[pl.BlockSpec((L, w), lambda i: (i, 0)),
                  pl.BlockSpec((L, w), lambda i: (i, 1)),
                  pl.BlockSpec((L, _BC_WIDTH), lambda i: (i, 2 * w // _BC_WIDTH)),
                  pl.BlockSpec((L, SSD_HEADS), lambda i: (i, 0)),
                  pl.BlockSpec((SSD_HEADS, L), lambda i: (0, i)),
                  full((SSD_CONV, w)), full((1, w)),
                  full((SSD_CONV, _BC_WIDTH)), full((1, _BC_WIDTH)),
                  full((1, SSD_HEADS)), full((SSD_HEADS, 1)),
                  full((1, SSD_HEADS)), full((SSD_HEADS, 1)),
                  full((1, w)), full((1, w))],
        out_specs=pl.BlockSpec((L, w), lambda i: (i, 0)),
        scratch_shapes=[pltpu.VMEM((L, w), F32),
                        pltpu.VMEM((L, _BC_WIDTH), F32),
                        pltpu.VMEM((SUBLANES, w), F32),
                        pltpu.VMEM((SUBLANES, _BC_WIDTH), F32),
                        pltpu.VMEM((SSD_STATE, w), F32),
                        pltpu.VMEM((L, w), F32)],
        compiler_params=_params(("arbitrary",), vmem, claim_all=True),
        name="ssd_mixer",
    )(u2, u2, u2, dt_raw, dt_rawT, cwx, cbx, cwbc, cbbc, dtb, dtbT, alog, alogT, dskip, nw)


def _out_proj_kernel(x_ref, ya_ref, yb_ref, w_ref, nw_ref, o_ref, h_ref, w_sc):
    @pl.when(pl.program_id(0) == 0)
    def _():
        w_sc[...] = w_ref[...].astype(BF16)

    ka = ya_ref.shape[1]
    o = (x_ref[...]
         + jnp.dot(ya_ref[...], w_sc[0:ka, :], preferred_element_type=F32)
         + jnp.dot(yb_ref[...], w_sc[ka:, :], preferred_element_type=F32))
    o_ref[...] = o
    h_ref[...] = _rms(o, nw_ref[...]).astype(h_ref.dtype)


def _out_proj(x, ya, yb, w, nw, tm=512):
    m, d = x.shape
    ka, kb = ya.shape[1], yb.shape[1]
    vmem = 2 * (2 * tm * d * 4 + tm * d * 2 + tm * (ka + kb) * 2) + (ka + kb) * d * 6 \
        + 4 * tm * d * 4
    return pl.pallas_call(
        _out_proj_kernel,
        out_shape=(jax.ShapeDtypeStruct((m, d), F32), jax.ShapeDtypeStruct((m, d), BF16)),
        grid=(m // tm,),
        in_specs=[pl.BlockSpec((tm, d), lambda i: (i, 0)),
                  pl.BlockSpec((tm, ka), lambda i: (i, 0)),
                  pl.BlockSpec((tm, kb), lambda i: (i, 0)),
                  pl.BlockSpec((ka + kb, d), lambda i: (0, 0), pipeline_mode=pl.Buffered(1)),
                  pl.BlockSpec((1, d), lambda i: (0, 0))],
        out_specs=(pl.BlockSpec((tm, d), lambda i: (i, 0)),
                   pl.BlockSpec((tm, d), lambda i: (i, 0))),
        scratch_shapes=[pltpu.VMEM((ka + kb, d), BF16)],
        compiler_params=_params(("arbitrary",), vmem),
        name="mix_out_proj",
    )(x, ya, yb, w, nw)


def _gate_up_kernel(h_ref, wg_ref, wu_ref, cw_ref, cb_ref, o_ref, wg_sc, wu_sc, tail_sc, conv_sc):
    halo = SUBLANES
    tm = conv_sc.shape[0]

    @pl.when(pl.program_id(1) == 0)
    def _():
        wg_sc[...] = wg_ref[...].astype(BF16)
        wu_sc[...] = wu_ref[...].astype(BF16)
        tail_sc[...] = jnp.zeros_like(tail_sc)

    h = h_ref[...]
    g = jnp.dot(h, wg_sc[...], preferred_element_type=F32)
    up = jnp.dot(h, wu_sc[...], preferred_element_type=F32)

    prev = tail_sc[...]
    tail_sc[...] = g[tm - halo:tm, :]

    def conv(cur, back1, back2):
        return (cb_ref[...] + cur * cw_ref[2:3, :] + back1 * cw_ref[1:2, :]
                + back2 * cw_ref[0:1, :])

    conv_sc[...] = conv(g, pltpu.roll(g, 1, axis=0), pltpu.roll(g, 2, axis=0))
    top = g[0:halo, :]
    r = lax.broadcasted_iota(jnp.int32, top.shape, 0)
    back1 = jnp.where(r < 1, pltpu.roll(prev, 1, axis=0), pltpu.roll(top, 1, axis=0))
    back2 = jnp.where(r < 2, pltpu.roll(prev, 2, axis=0), pltpu.roll(top, 2, axis=0))
    conv_sc[0:halo, :] = conv(top, back1, back2)
    o_ref[...] = (_silu(conv_sc[...]) * up).astype(o_ref.dtype)


def _gate_up(h, w_gu, cw, cb, tm=1024, tn=512):
    m, d = h.shape
    nj = D_FF // tn
    vmem = 2 * (tm * d * 2 + 2 * d * tn * 4 + tm * tn * 2) + 2 * d * tn * 2 \
        + 10 * tm * tn * 4
    return pl.pallas_call(
        _gate_up_kernel,
        out_shape=jax.ShapeDtypeStruct((m, D_FF), BF16),
        grid=(nj, m // tm),
        in_specs=[pl.BlockSpec((tm, d), lambda j, i: (i, 0)),
                  pl.BlockSpec((d, tn), lambda j, i: (0, j)),
                  pl.BlockSpec((d, tn), lambda j, i: (0, j + nj)),
                  pl.BlockSpec((FFN_CONV, tn), lambda j, i: (0, j)),
                  pl.BlockSpec((1, tn), lambda j, i: (0, j))],
        out_specs=pl.BlockSpec((tm, tn), lambda j, i: (i, j)),
        scratch_shapes=[pltpu.VMEM((d, tn), BF16), pltpu.VMEM((d, tn), BF16),
                        pltpu.VMEM((SUBLANES, tn), F32), pltpu.VMEM((tm, tn), F32)],
        compiler_params=_params(("arbitrary", "arbitrary"), vmem),
        name="ffn_gate_up",
    )(h, w_gu, w_gu, cw, cb)


def _down_kernel(a_ref, w_ref, x_ref, o_ref, w_sc):
    @pl.when(pl.program_id(1) == 0)
    def _():
        w_sc[...] = w_ref[...].astype(BF16)

    o_ref[...] = x_ref[...] + jnp.dot(a_ref[...], w_sc[...], preferred_element_type=F32)


def _down(act, w, x, tm=512, tn=512):
    m, d = x.shape
    kk = act.shape[1]
    vmem = 2 * (tm * kk * 2 + kk * tn * 4 + 2 * tm * tn * 4) + kk * tn * 2 + 2 * tm * tn * 4
    return pl.pallas_call(
        _down_kernel,
        out_shape=jax.ShapeDtypeStruct((m, d), F32),
        grid=(d // tn, m // tm),
        in_specs=[pl.BlockSpec((tm, kk), lambda j, i: (i, 0)),
                  pl.BlockSpec((kk, tn), lambda j, i: (0, j)),
                  pl.BlockSpec((tm, tn), lambda j, i: (i, j))],
        out_specs=pl.BlockSpec((tm, tn), lambda j, i: (i, j)),
        scratch_shapes=[pltpu.VMEM((kk, tn), BF16)],
        compiler_params=_params(("arbitrary", "arbitrary"), vmem),
        name="ffn_down",
    )(act, w, x)


def _ple_kernel(x_ref, p_ref, nw_ref, wg_ref, wp_ref, fw_ref, o_ref, wg_sc, wp_sc, *, final):
    @pl.when(pl.program_id(0) == 0)
    def _():
        wg_sc[...] = wg_ref[...].astype(BF16)
        wp_sc[...] = wp_ref[...].astype(BF16)

    x = x_ref[...]
    h = _rms(x, nw_ref[...]).astype(BF16)
    gate = jax.nn.sigmoid(jnp.dot(h, wg_sc[...], preferred_element_type=F32))
    proj = jnp.dot(p_ref[...].astype(BF16), wp_sc[...], preferred_element_type=F32)
    x = x + gate * proj
    if final:
        x = _rms(x, fw_ref[...])
    o_ref[...] = x


def _ple(x, p, nw, wg, wp, fw, final, tm=512):
    m, d = x.shape
    dp = p.shape[1]
    vmem = 2 * (2 * tm * d * 4 + tm * dp * 4) + (d + dp) * d * 6 + 6 * tm * d * 4
    resident = lambda shape: pl.BlockSpec(shape, lambda i: (0, 0), pipeline_mode=pl.Buffered(1))
    return pl.pallas_call(
        functools.partial(_ple_kernel, final=final),
        out_shape=jax.ShapeDtypeStruct((m, d), F32),
        grid=(m // tm,),
        in_specs=[pl.BlockSpec((tm, d), lambda i: (i, 0)),
                  pl.BlockSpec((tm, dp), lambda i: (i, 0)),
                  pl.BlockSpec((1, d), lambda i: (0, 0)),
                  resident((d, d)),
                  resident((dp, d)),
                  pl.BlockSpec((1, d), lambda i: (0, 0))],
        out_specs=pl.BlockSpec((tm, d), lambda i: (i, 0)),
        scratch_shapes=[pltpu.VMEM((d, d), BF16), pltpu.VMEM((dp, d), BF16)],
        compiler_params=_params(("arbitrary",), vmem),
        name="ple_gate",
    )(x, p, nw, wg, wp, fw)


def _layer(x, p, mix_norm_w, w_in, fox_forget_bias, ssd_conv_w, ssd_conv_b, ssd_dt_bias,
           ssd_A_log, ssd_D, ssd_norm_w, w_out, ffn_norm_w, w_gate_up, ffn_conv_w,
           ffn_conv_b, w_down, ple_norm_w, w_ple_gate, w_ple_proj, final_norm_w, final):
    row = lambda v: v.reshape(1, -1).astype(F32)

    w_inT = jnp.swapaxes(w_in, 0, 1).astype(F32)
    w_u2T = jnp.concatenate(
        [w_inT[_OFF_XS:_OFF_B], w_inT[_OFF_Z:_OFF_DT], w_inT[_OFF_B:_OFF_Z]], axis=0)
    w_smallT = jnp.concatenate(
        [w_inT[_OFF_F:_OFF_XS], w_inT[_OFF_DT:_IN_COLS],
         jnp.zeros((LANES - FOX_HEADS - SSD_HEADS, D_MODEL), F32)], axis=0)

    h_mix, small = _norm(x, row(mix_norm_w), w_smallT)
    qk = _proj(h_mix, w_inT, 2 * FOX_WIDTH, BF16, tm=1024, tn=512, name="in_proj_qk")
    vT = _proj(h_mix, w_inT, FOX_WIDTH, BF16, tm=1024, tn=512, name="in_proj_v",
               first_col=2 * FOX_WIDTH, transposed=True)
    u2 = _proj(h_mix, w_u2T, _U2_COLS, F32, tm=1024, tn=_U2_COLS // 2, name="in_proj_ssd")

    b_pad = jnp.zeros((1, LANES), F32).at[0, :FOX_HEADS].set(fox_forget_bias.astype(F32))
    c, aug = _fox_c(small, b_pad)
    cT = c[:, :FOX_HEADS].T
    y_fox = _fox_attn(qk, vT, aug, cT[:, None, :])

    dt_raw = small[:, FOX_HEADS:FOX_HEADS + SSD_HEADS]
    col = lambda v: v.reshape(-1, 1).astype(F32)
    y_ssd = _ssd(u2, dt_raw, dt_raw.T,
                 ssd_conv_w[:, :SSD_WIDTH].astype(F32), row(ssd_conv_b[:SSD_WIDTH]),
                 ssd_conv_w[:, SSD_WIDTH:].astype(F32), row(ssd_conv_b[SSD_WIDTH:]),
                 row(ssd_dt_bias), col(ssd_dt_bias), row(ssd_A_log), col(ssd_A_log),
                 row(jnp.repeat(ssd_D, HEAD_DIM)), row(ssd_norm_w))

    x, h_ffn = _out_proj(x, y_fox, y_ssd, w_out.astype(F32), row(ffn_norm_w))

    act = _gate_up(h_ffn, w_gate_up.astype(F32), ffn_conv_w.astype(F32), row(ffn_conv_b))
    x = _down(act, w_down.astype(F32), x)

    return _ple(x, p, row(ple_norm_w), w_ple_gate.astype(F32), w_ple_proj.astype(F32),
                row(final_norm_w), final)


def kernel(x, p, mix_norm_w, w_in, fox_forget_bias, ssd_conv_w, ssd_conv_b, ssd_dt_bias,
           ssd_A_log, ssd_D, ssd_norm_w, w_out, ffn_norm_w, w_gate_up, ffn_conv_w,
           ffn_conv_b, w_down, ple_norm_w, w_ple_gate, w_ple_proj, final_norm_w):
    bsz = x.shape[0]
    depth = p.shape[0]
    outs = []
    for b in range(bsz):
        xb = x[b]
        for i in range(depth):
            xb = _layer(xb, p[i, b], mix_norm_w[i], w_in[i], fox_forget_bias[i], ssd_conv_w[i],
                        ssd_conv_b[i], ssd_dt_bias[i], ssd_A_log[i], ssd_D[i], ssd_norm_w[i],
                        w_out[i], ffn_norm_w[i], w_gate_up[i], ffn_conv_w[i], ffn_conv_b[i],
                        w_down[i], ple_norm_w[i], w_ple_gate[i], w_ple_proj[i], final_norm_w,
                        final=(i == depth - 1))
        outs.append(xb)
    return outs[0][None] if bsz == 1 else jnp.stack(outs, axis=0)
```

```python
import functools

import numpy as np
import jax
import jax.numpy as jnp
from jax import lax
from jax.experimental import pallas as pl
from jax.experimental.pallas import tpu as pltpu

F32 = jnp.float32
BF16 = jnp.bfloat16

D_MODEL = 2048
HEAD_DIM = 64
FOX_WIDTH = 1024
FOX_HEADS = 16
SSD_WIDTH = 1024
SSD_HEADS = 16
SSD_GROUPS = 2
SSD_STATE = 128
SSD_CONV = 4
D_FF = 5632
FFN_CONV = 3
EPS = 1e-6
LOG2_E = 1.4426950408889634

_OFF_Q = 0
_OFF_F = 3 * FOX_WIDTH
_OFF_XS = _OFF_F + FOX_HEADS
_OFF_B = _OFF_XS + SSD_WIDTH
_OFF_C = _OFF_B + SSD_GROUPS * SSD_STATE
_OFF_Z = _OFF_C + SSD_GROUPS * SSD_STATE
_OFF_DT = _OFF_Z + SSD_WIDTH
_IN_COLS = _OFF_DT + SSD_HEADS

LANES = 128
SUBLANES = 8
VMEM_CAP = 60 * 1024 * 1024

_BC_WIDTH = 2 * SSD_GROUPS * SSD_STATE
_U2_COLS = 2 * SSD_WIDTH + _BC_WIDTH


def _params(sem, vmem_bytes, claim_all=False):
    limit = VMEM_CAP if claim_all else min(VMEM_CAP, vmem_bytes)
    return pltpu.CompilerParams(dimension_semantics=sem, vmem_limit_bytes=int(limit))


def _rms(xf, w):
    ms = jnp.mean(xf * xf, axis=-1, keepdims=True)
    return xf * lax.rsqrt(ms + EPS) * w


def _split3(a):
    hi = a.astype(BF16)
    r1 = a - hi.astype(F32)
    mid = r1.astype(BF16)
    lo = (r1 - mid.astype(F32)).astype(BF16)
    return hi, mid, lo


def _tri_left(tri, a):
    return sum(jnp.dot(tri, t, preferred_element_type=F32) for t in _split3(a))


def _tri_right(a, tri):
    return sum(jnp.dot(t, tri, preferred_element_type=F32) for t in _split3(a))


def _softplus(v):
    return jnp.maximum(v, 0.0) + jnp.log1p(jnp.exp(-jnp.abs(v)))


def _silu(v):
    return v / (1.0 + jnp.exp2(v * (-LOG2_E)))


def _norm_kernel(x_ref, nw_ref, wT_ref, h_ref, small_ref):
    h = _rms(x_ref[...], nw_ref[...]).astype(h_ref.dtype)
    h_ref[...] = h
    small_ref[...] = lax.dot_general(h, wT_ref[...].astype(BF16), (((1,), (1,)), ((), ())),
                                     preferred_element_type=F32)


def _norm(x, nw, w_smallT, tm=512):
    m, d = x.shape
    n_small = w_smallT.shape[0]
    vmem = 2 * (tm * d * 6 + n_small * d * 4 + tm * n_small * 4) + 4 * tm * d * 4
    return pl.pallas_call(
        _norm_kernel,
        out_shape=(jax.ShapeDtypeStruct((m, d), BF16),
                   jax.ShapeDtypeStruct((m, n_small), F32)),
        grid=(m // tm,),
        in_specs=[pl.BlockSpec((tm, d), lambda i: (i, 0)),
                  pl.BlockSpec((1, d), lambda i: (0, 0)),
                  pl.BlockSpec((n_small, d), lambda i: (0, 0))],
        out_specs=(pl.BlockSpec((tm, d), lambda i: (i, 0)),
                   pl.BlockSpec((tm, n_small), lambda i: (i, 0))),
        compiler_params=_params(("arbitrary",), vmem),
        name="mix_norm",
    )(x, nw, w_smallT)


def _proj_kernel(h_ref, wT_ref, o_ref, w_sc, *, transposed):
    @pl.when(pl.program_id(1) == 0)
    def _():
        w_sc[...] = wT_ref[...].astype(BF16)

    lhs, rhs = (w_sc[...], h_ref[...]) if transposed else (h_ref[...], w_sc[...])
    o_ref[...] = lax.dot_general(lhs, rhs, (((1,), (1,)), ((), ())),
                                 preferred_element_type=F32).astype(o_ref.dtype)


def _proj(h, wT, n_cols, out_dtype, tm, tn, name, first_col=0, transposed=False):
    m, k = h.shape
    ob = jnp.dtype(out_dtype).itemsize
    j0 = first_col // tn
    vmem = 2 * (tm * k * 2 + k * tn * 4 + tm * tn * ob) + k * tn * 2 + 2 * tm * tn * 4
    return pl.pallas_call(
        functools.partial(_proj_kernel, transposed=transposed),
        out_shape=jax.ShapeDtypeStruct((n_cols, m) if transposed else (m, n_cols), out_dtype),
        grid=(n_cols // tn, m // tm),
        in_specs=[pl.BlockSpec((tm, k), lambda j, i: (i, 0)),
                  pl.BlockSpec((tn, k), lambda j, i: (j0 + j, 0))],
        out_specs=(pl.BlockSpec((tn, tm), lambda j, i: (j, i)) if transposed
                   else pl.BlockSpec((tm, tn), lambda j, i: (i, j))),
        scratch_shapes=[pltpu.VMEM((tn, k), BF16)],
        compiler_params=_params(("arbitrary", "arbitrary"), vmem),
        name=name,
    )(h, wT)


_AUG_ONES = 6
_N_PAIRS = FOX_HEADS // 2


def _aug_selector():
    sel = np.zeros((LANES, _N_PAIRS * LANES), np.float32)
    for pr in range(_N_PAIRS):
        for e in range(2):
            for j in range(3):
                sel[j * FOX_HEADS + 2 * pr + e, pr * LANES + 3 * e + j] = -1.0
        sel[3 * FOX_HEADS, pr * LANES + _AUG_ONES:pr * LANES + _AUG_ONES + 3] = 1.0
    return jnp.asarray(sel, BF16)


def _fox_c_kernel(f_ref, b_ref, sel_ref, c_ref, aug_ref, carry_sc):
    @pl.when(pl.program_id(0) == 0)
    def _():
        carry_sc[...] = jnp.zeros_like(carry_sc)

    sub = _CUM_ROWS
    row = lax.broadcasted_iota(jnp.int32, (sub, sub), 0)
    col = lax.broadcasted_iota(jnp.int32, (sub, sub), 1)
    tril = (col <= row).astype(BF16)
    lane = lax.broadcasted_iota(jnp.int32, (sub, LANES), 1)
    h = FOX_HEADS
    carry = carry_sc[...]
    for r in range(f_ref.shape[0] // sub):
        rows = slice(r * sub, (r + 1) * sub)
        z = f_ref[rows, :] + b_ref[...]
        lf = jnp.minimum(z, 0.0) - jnp.log1p(jnp.exp(-jnp.abs(z)))
        cum = _tri_left(tril, lf) + carry
        c_ref[rows, :] = cum
        carry = cum[sub - 1:sub, :]
        hi, mid, lo = _split3(cum * LOG2_E)
        parts = jnp.where(
            lane < h, hi.astype(F32),
            jnp.where(lane < 2 * h, pltpu.roll(mid.astype(F32), h, axis=1),
                      jnp.where(lane < 3 * h, pltpu.roll(lo.astype(F32), 2 * h, axis=1),
                                jnp.where(lane == 3 * h, 1.0, 0.0))))
        aug_ref[rows, :] = jnp.dot(parts.astype(BF16), sel_ref[...],
                                   preferred_element_type=F32).astype(BF16)
    carry_sc[...] = carry


_CUM_ROWS = 256


def _fox_c(small, b_pad, t=4 * _CUM_ROWS):
    s = small.shape[0]
    aug_cols = _N_PAIRS * LANES
    return pl.pallas_call(
        _fox_c_kernel,
        out_shape=(jax.ShapeDtypeStruct((s, LANES), F32),
                   jax.ShapeDtypeStruct((s, aug_cols), BF16)),
        grid=(s // t,),
        in_specs=[pl.BlockSpec((t, LANES), lambda i: (i, 0)),
                  pl.BlockSpec((1, LANES), lambda i: (0, 0)),
                  pl.BlockSpec((LANES, aug_cols), lambda i: (0, 0))],
        out_specs=(pl.BlockSpec((t, LANES), lambda i: (i, 0)),
                   pl.BlockSpec((t, aug_cols), lambda i: (i, 0))),
        scratch_shapes=[pltpu.VMEM((1, LANES), F32)],
        compiler_params=_params(("arbitrary",), 16 * 1024 * 1024, claim_all=True),
        name="fox_cumlogf",
    )(small, b_pad, _aug_selector())


_V_ROWS = HEAD_DIM + 16


def _fox_attn_kernel(q_ref, k_ref, vT_ref, aug_ref, cq_ref, o_ref, vT_sc, acc_sc, s0_sc, s1_sc,
                     w_sc, m_sc, smax0_sc, smax1_sc, *, t):
    qi = pl.program_id(1)
    n_kv = k_ref.shape[0] // t
    n_heads = k_ref.shape[1] // HEAD_DIM

    @pl.when(qi == 0)
    def _():
        pad_row = lax.broadcasted_iota(jnp.int32, (_V_ROWS - HEAD_DIM, t), 0)
        ones_rows = jnp.where(pad_row == 0, 1.0, 0.0).astype(BF16)

        def place_v(c, carry):
            st = pl.multiple_of(c * t, t)
            for e in range(n_heads):
                vT_sc[e, 0:HEAD_DIM, pl.ds(st, t)] = vT_ref[e * HEAD_DIM:(e + 1) * HEAD_DIM,
                                                            pl.ds(st, t)]
                vT_sc[e, HEAD_DIM:_V_ROWS, pl.ds(st, t)] = ones_rows
            return carry
        lax.fori_loop(0, n_kv, place_v, 0)

    scale = HEAD_DIM ** -0.5 * LOG2_E
    qT = q_ref[...].astype(F32) * scale
    row = lax.broadcasted_iota(jnp.int32, (LANES, t), 0)
    tk = s0_sc.shape[1]
    key_idx = lax.broadcasted_iota(jnp.int32, (tk, t), 0)
    qry_idx = lax.broadcasted_iota(jnp.int32, (tk, t), 1)

    ws = []
    for e in range(n_heads):
        pair, side = divmod(e, 2)
        qTe = jnp.where(row // HEAD_DIM == side, qT[pair * LANES:(pair + 1) * LANES, :], 0.0)
        cq_hi, cq_mid, cq_lo = (part.astype(F32)
                                for part in _split3(cq_ref[e] * LOG2_E))
        top = lax.broadcasted_iota(jnp.int32, (2 * SUBLANES, t), 0)
        aug_rows = jnp.where(
            top // 3 == side, 1.0,
            jnp.where(top == _AUG_ONES, cq_hi,
                      jnp.where(top == _AUG_ONES + 1, cq_mid,
                                jnp.where(top == _AUG_ONES + 2, cq_lo, 0.0))))
        unused = jnp.zeros((LANES - 2 * SUBLANES, t), F32)
        ws.append(jnp.concatenate([qTe, aug_rows, unused], axis=0).astype(BF16))
        acc_sc[e] = jnp.zeros((_V_ROWS, t), F32)

    def key_blocks(j):
        start = pl.multiple_of(j * tk, tk)
        return [jnp.concatenate([k_ref[pl.ds(start, tk), pr * LANES:(pr + 1) * LANES],
                                 aug_ref[pl.ds(start, tk), pr * LANES:(pr + 1) * LANES]], axis=1)
                for pr in range(n_heads // 2)]

    def scores_into(j, s_sc, smax_sc):
        keys = key_blocks(j)
        for e in range(n_heads):
            s = jnp.dot(keys[e // 2], ws[e], preferred_element_type=F32)
            s_sc[e] = s
            smax_sc[e] = jnp.max(s, axis=0, keepdims=True)

    def softmax_pv(j, s_sc, smax_sc, m_prev, masked):
        start = pl.multiple_of(j * tk, tk)
        if masked:
            visible = key_idx + start <= qry_idx + qi * t
        m_out, probs, alphas = [], [], []
        for e in range(n_heads):
            s = s_sc[e]
            if masked:
                s = jnp.where(visible, s, -jnp.inf)
                m_new = jnp.maximum(m_prev[e], jnp.max(s, axis=0, keepdims=True))
            else:
                m_new = jnp.maximum(m_prev[e], smax_sc[e])
            probs.append(jnp.exp2(s - m_new).astype(BF16))
            alphas.append(jnp.exp2(m_prev[e] - m_new))
            m_out.append(m_new)
        for e in range(n_heads):
            acc_sc[e] = alphas[e] * acc_sc[e] + jnp.dot(
                vT_sc[e, :, pl.ds(start, tk)], probs[e], preferred_element_type=F32)
        return tuple(m_out)

    def two_blocks(i, m):
        scores_into(2 * i + 1, s1_sc, smax1_sc)
        m = softmax_pv(2 * i, s0_sc, smax0_sc, m, False)
        scores_into(2 * i + 2, s0_sc, smax0_sc)
        return softmax_pv(2 * i + 1, s1_sc, smax1_sc, m, False)

    def unrolled(n):
        def trip(i, m):
            for r in range(n):
                m = two_blocks(n * i + r, m)
            return m
        return trip

    m_init = jnp.full((1, t), -1e30, F32)
    scores_into(0, s0_sc, smax0_sc)
    fours, rest = qi // 4, qi % 4
    m = lax.fori_loop(0, fours, unrolled(4), (m_init,) * n_heads)
    m = lax.fori_loop(2 * fours, 2 * fours + rest // 2, unrolled(2), m)
    m = lax.fori_loop(4 * fours + 2 * (rest // 2), qi, two_blocks, m)
    upper = slice(tk, t)
    last = pl.multiple_of((2 * qi + 1) * tk, tk)
    last_keys = key_blocks(2 * qi + 1)
    for e in range(n_heads):
        w_sc[e] = ws[e]
        s1_sc[e, :, 0:tk] = jnp.dot(last_keys[e // 2], w_sc[e, :, upper],
                                    preferred_element_type=F32)
    m = softmax_pv(2 * qi, s0_sc, smax0_sc, m, True)
    visible = (lax.broadcasted_iota(jnp.int32, (tk, tk), 0)
               <= lax.broadcasted_iota(jnp.int32, (tk, tk), 1))
    for e in range(n_heads):
        s = jnp.where(visible, s1_sc[e, :, 0:tk], -jnp.inf)
        m_sc[e] = m[e]
        m_upper = m_sc[e, :, upper]
        m_new = jnp.maximum(m_upper, jnp.max(s, axis=0, keepdims=True))
        p = jnp.exp2(s - m_new).astype(BF16)
        acc_sc[e, :, upper] = jnp.exp2(m_upper - m_new) * acc_sc[e, :, upper] + jnp.dot(
            vT_sc[e, :, pl.ds(last, tk)], p, preferred_element_type=F32)
    outs = []
    for e in range(n_heads):
        acc = acc_sc[e]
        outs.append(acc[0:HEAD_DIM, :] / acc[HEAD_DIM:HEAD_DIM + 1, :])
    o_ref[...] = jnp.concatenate(outs, axis=0).T.astype(o_ref.dtype)


def _fox_attn(qT, k, vT, aug, cq, t=512, heads=4):
    s = k.shape[0]
    w = heads * HEAD_DIM
    vmem = 2 * (2 * t * w * 2 + 3 * s * w * 2 + heads * t * 4) \
        + heads * _V_ROWS * (s * 2 + t * 4) + 5 * heads * t * t * 4
    return pl.pallas_call(
        functools.partial(_fox_attn_kernel, t=t),
        out_shape=jax.ShapeDtypeStruct((s, FOX_WIDTH), BF16),
        grid=(FOX_HEADS // heads, s // t),
        in_specs=[pl.BlockSpec((w, t), lambda g, qi: (g, qi)),
                  pl.BlockSpec((s, w), lambda g, qi: (0, g)),
                  pl.BlockSpec((w, s), lambda g, qi: (g, 0)),
                  pl.BlockSpec((s, w), lambda g, qi: (0, g)),
                  pl.BlockSpec((heads, 1, t), lambda g, qi: (g, 0, qi))],
        out_specs=pl.BlockSpec((t, w), lambda g, qi: (qi, g)),
        scratch_shapes=[pltpu.VMEM((heads, _V_ROWS, s), BF16),
                        pltpu.VMEM((heads, _V_ROWS, t), F32),
                        pltpu.VMEM((heads, t // 2, t), F32), pltpu.VMEM((heads, t // 2, t), F32),
                        pltpu.VMEM((heads, 2 * LANES, t), BF16), pltpu.VMEM((heads, 1, t), F32),
                        pltpu.VMEM((heads, 1, t), F32), pltpu.VMEM((heads, 1, t), F32)],
        compiler_params=_params(("arbitrary", "arbitrary"), vmem, claim_all=True),
        name="fox_attention",
    )(qT, k, vT, aug, cq)


def _ssd_kernel(xs_ref, z_ref, bc_ref, dt_ref, dtT_ref, cwx_ref, cbx_ref, cwbc_ref, cbbc_ref,
                dtb_ref, dtbT_ref, alog_ref, alogT_ref, dskip_ref, nw_ref, y_ref,
                xbuf, bcbuf, xtail_sc, bctail_sc, state_sc, ybuf, *, L):
    halo = SUBLANES

    @pl.when(pl.program_id(0) == 0)
    def _():
        xtail_sc[...] = jnp.zeros_like(xtail_sc)
        bctail_sc[...] = jnp.zeros_like(bctail_sc)
        state_sc[...] = jnp.zeros_like(state_sc)

    def conv_silu(u_ref, tail_sc, buf, w_ref, b_ref):
        cur = u_ref[...]
        prev = tail_sc[...]
        tail_sc[...] = cur[L - halo:L, :]
        last = SSD_CONV - 1

        def taps(c, backs):
            acc = b_ref[...] + c * w_ref[last:last + 1, :]
            for k in range(1, SSD_CONV):
                acc = acc + backs[k - 1] * w_ref[last - k:last - k + 1, :]
            return acc

        buf[...] = taps(cur, [pltpu.roll(cur, k, axis=0) for k in range(1, SSD_CONV)])
        top = cur[0:halo, :]
        r = lax.broadcasted_iota(jnp.int32, top.shape, 0)
        buf[0:halo, :] = taps(top, [
            jnp.where(r < k, pltpu.roll(prev, k, axis=0), pltpu.roll(top, k, axis=0))
            for k in range(1, SSD_CONV)])
        return _silu(buf[...])

    xc = conv_silu(xs_ref, xtail_sc, xbuf, cwx_ref, cbx_ref)
    bcc = conv_silu(bc_ref, bctail_sc, bcbuf, cwbc_ref, cbbc_ref)

    dt = _softplus(dt_ref[...] + dtb_ref[...])
    dtT = _softplus(dtT_ref[...] + dtbT_ref[...])
    a = dt * (-jnp.exp(alog_ref[...]) * LOG2_E)
    aT = dtT * (-jnp.exp(alogT_ref[...]) * LOG2_E)

    row = lax.broadcasted_iota(jnp.int32, (L, L), 0)
    col = lax.broadcasted_iota(jnp.int32, (L, L), 1)
    causal = col <= row
    a_cum = _tri_left(causal.astype(BF16), a)
    a_cumT = _tri_right(aT, (row <= col).astype(BF16))
    a_last = a_cum[L - 1:L, :]
    a_lastT = a_cumT[:, L - 1:L]
    exp_acum = jnp.exp2(a_cum)
    w_endT = jnp.exp2(a_lastT - a_cumT) * dtT
    chunk_decay = jnp.exp2(a_last)
    src_T = a_cumT - jnp.log2(dtT)

    n = SSD_STATE
    heads_per_group = SSD_HEADS // SSD_GROUPS
    first_half = lax.broadcasted_iota(jnp.int32, (1, LANES), 1) < HEAD_DIM
    for g in range(SSD_GROUPS):
        bg = bcc[:, g * n:(g + 1) * n]
        cg = bcc[:, SSD_GROUPS * n + g * n:SSD_GROUPS * n + (g + 1) * n]
        bgT = bg.T
        cb = lax.dot_general(cg.astype(BF16), bg.astype(BF16), (((1,), (1,)), ((), ())),
                             preferred_element_type=F32)
        for pair in range(heads_per_group // 2):
            h0 = g * heads_per_group + 2 * pair
            cols = slice(h0 * HEAD_DIM, (h0 + 2) * HEAD_DIM)
            x_pair = xc[:, cols]
            x_pair_b = x_pair.astype(BF16)
            st = state_sc[:, cols]
            st_b = st.astype(BF16)
            ys, upds = [], []
            for h in (h0, h0 + 1):
                seg = a_cum[:, h:h + 1] - src_T[h:h + 1, :]
                mix = cb * jnp.exp2(jnp.where(causal, seg, -jnp.inf))
                y = jnp.dot(mix.astype(BF16), x_pair_b, preferred_element_type=F32)
                ys.append(y + jnp.dot((cg * exp_acum[:, h:h + 1]).astype(BF16), st_b,
                                      preferred_element_type=F32))
                upds.append(jnp.dot((bgT * w_endT[h:h + 1, :]).astype(BF16), x_pair_b,
                                    preferred_element_type=F32))
            ybuf[:, cols] = jnp.where(first_half, ys[0], ys[1]) + dskip_ref[:, cols] * x_pair
            keep = jnp.where(first_half, chunk_decay[:, h0:h0 + 1], chunk_decay[:, h0 + 1:h0 + 2])
            state_sc[:, cols] = keep * st + jnp.where(first_half, upds[0], upds[1])

    y = ybuf[...] * _silu(z_ref[...])
    y_ref[...] = _rms(y, nw_ref[...]).astype(y_ref.dtype)


def _ssd(u2, dt_raw, dt_rawT, cwx, cbx, cwbc, cbbc, dtb, dtbT, alog, alogT, dskip, nw, L=256):
    s = u2.shape[0]
    w = SSD_WIDTH
    full = lambda shape: pl.BlockSpec(shape, lambda i: (0,) * len(shape))
    vmem = 2 * (2 * L * w * 4 + L * _BC_WIDTH * 4 + L * w * 2) + (2 * L + 16) * w * 4 \
        + (L + 8) * _BC_WIDTH * 4 + SSD_STATE * w * 4 + 24 * L * L * 4 + 6 * L * w * 4
    return pl.pallas_call(
        functools.partial(_ssd_kernel, L=L),
        out_shape=jax.ShapeDtypeStruct((s, w), BF16),
        grid=(s // L,),
        in_specs=[pl.BlockSpec((L, w), lambda i: (i, 0)),
                  pl.BlockSpec((L, w), lambda i: (i, 1)),
                  pl.BlockSpec((L, _BC_WIDTH), lambda i: (i, 2 * w // _BC_WIDTH)),
                  pl.BlockSpec((L, SSD_HEADS), lambda i: (i, 0)),
                  pl.BlockSpec((SSD_HEADS, L), lambda i: (0, i)),
                  full((SSD_CONV, w)), full((1, w)),
                  full((SSD_CONV, _BC_WIDTH)), full((1, _BC_WIDTH)),
                  full((1, SSD_HEADS)), full((SSD_HEADS, 1)),
                  full((1, SSD_HEADS)), full((SSD_HEADS, 1)),
                  full((1, w)), full((1, w))],
        out_specs=pl.BlockSpec((L, w), lambda i: (i, 0)),
        scratch_shapes=[pltpu.VMEM((L, w), F32),
                        pltpu.VMEM((L, _BC_WIDTH), F32),
                        pltpu.VMEM((SUBLANES, w), F32),
                        pltpu.VMEM((SUBLANES, _BC_WIDTH), F32),
                        pltpu.VMEM((SSD_STATE, w), F32),
                        pltpu.VMEM((L, w), F32)],
        compiler_params=_params(("arbitrary",), vmem, claim_all=True),
        name="ssd_mixer",
    )(u2, u2, u2, dt_raw, dt_rawT, cwx, cbx, cwbc, cbbc, dtb, dtbT, alog, alogT, dskip, nw)


def _out_proj_kernel(x_ref, ya_ref, yb_ref, w_ref, nw_ref, o_ref, h_ref, w_sc):
    @pl.when(pl.program_id(0) == 0)
    def _():
        w_sc[...] = w_ref[...].astype(BF16)

    ka = ya_ref.shape[1]
    o = (x_ref[...]
         + jnp.dot(ya_ref[...], w_sc[0:ka, :], preferred_element_type=F32)
         + jnp.dot(yb_ref[...], w_sc[ka:, :], preferred_element_type=F32))
    o_ref[...] = o
    h_ref[...] = _rms(o, nw_ref[...]).astype(h_ref.dtype)


def _out_proj(x, ya, yb, w, nw, tm=512):
    m, d = x.shape
    ka, kb = ya.shape[1], yb.shape[1]
    vmem = 2 * (2 * tm * d * 4 + tm * d * 2 + tm * (ka + kb) * 2) + (ka + kb) * d * 6 \
        + 4 * tm * d * 4
    return pl.pallas_call(
        _out_proj_kernel,
        out_shape=(jax.ShapeDtypeStruct((m, d), F32), jax.ShapeDtypeStruct((m, d), BF16)),
        grid=(m // tm,),
        in_specs=[pl.BlockSpec((tm, d), lambda i: (i, 0)),
                  pl.BlockSpec((tm, ka), lambda i: (i, 0)),
                  pl.BlockSpec((tm, kb), lambda i: (i, 0)),
                  pl.BlockSpec((ka + kb, d), lambda i: (0, 0), pipeline_mode=pl.Buffered(1)),
                  pl.BlockSpec((1, d), lambda i: (0, 0))],
        out_specs=(pl.BlockSpec((tm, d), lambda i: (i, 0)),
                   pl.BlockSpec((tm, d), lambda i: (i, 0))),
        scratch_shapes=[pltpu.VMEM((ka + kb, d), BF16)],
        compiler_params=_params(("arbitrary",), vmem),
        name="mix_out_proj",
    )(x, ya, yb, w, nw)


def _gate_up_kernel(h_ref, wg_ref, wu_ref, cw_ref, cb_ref, o_ref, wg_sc, wu_sc, tail_sc, conv_sc):
    halo = SUBLANES
    tm = conv_sc.shape[0]

    @pl.when(pl.program_id(1) == 0)
    def _():
        wg_sc[...] = wg_ref[...].astype(BF16)
        wu_sc[...] = wu_ref[...].astype(BF16)
        tail_sc[...] = jnp.zeros_like(tail_sc)

    h = h_ref[...]
    g = jnp.dot(h, wg_sc[...], preferred_element_type=F32)
    up = jnp.dot(h, wu_sc[...], preferred_element_type=F32)

    prev = tail_sc[...]
    tail_sc[...] = g[tm - halo:tm, :]

    def conv(cur, back1, back2):
        return (cb_ref[...] + cur * cw_ref[2:3, :] + back1 * cw_ref[1:2, :]
                + back2 * cw_ref[0:1, :])

    conv_sc[...] = conv(g, pltpu.roll(g, 1, axis=0), pltpu.roll(g, 2, axis=0))
    top = g[0:halo, :]
    r = lax.broadcasted_iota(jnp.int32, top.shape, 0)
    back1 = jnp.where(r < 1, pltpu.roll(prev, 1, axis=0), pltpu.roll(top, 1, axis=0))
    back2 = jnp.where(r < 2, pltpu.roll(prev, 2, axis=0), pltpu.roll(top, 2, axis=0))
    conv_sc[0:halo, :] = conv(top, back1, back2)
    o_ref[...] = (_silu(conv_sc[...]) * up).astype(o_ref.dtype)


def _gate_up(h, w_gu, cw, cb, tm=1024, tn=512):
    m, d = h.shape
    nj = D_FF // tn
    vmem = 2 * (tm * d * 2 + 2 * d * tn * 4 + tm * tn * 2) + 2 * d * tn * 2 \
        + 10 * tm * tn * 4
    return pl.pallas_call(
        _gate_up_kernel,
        out_shape=jax.ShapeDtypeStruct((m, D_FF), BF16),
        grid=(nj, m // tm),
        in_specs=[pl.BlockSpec((tm, d), lambda j, i: (i, 0)),
                  pl.BlockSpec((d, tn), lambda j, i: (0, j)),
                  pl.BlockSpec((d, tn), lambda j, i: (0, j + nj)),
                  pl.BlockSpec((FFN_CONV, tn), lambda j, i: (0, j)),
                  pl.BlockSpec((1, tn), lambda j, i: (0, j))],
        out_specs=pl.BlockSpec((tm, tn), lambda j, i: (i, j)),
        scratch_shapes=[pltpu.VMEM((d, tn), BF16), pltpu.VMEM((d, tn), BF16),
                        pltpu.VMEM((SUBLANES, tn), F32), pltpu.VMEM((tm, tn), F32)],
        compiler_params=_params(("arbitrary", "arbitrary"), vmem),
        name="ffn_gate_up",
    )(h, w_gu, w_gu, cw, cb)


def _down_kernel(a_ref, w_ref, x_ref, o_ref, w_sc):
    @pl.when(pl.program_id(1) == 0)
    def _():
        w_sc[...] = w_ref[...].astype(BF16)

    o_ref[...] = x_ref[...] + jnp.dot(a_ref[...], w_sc[...], preferred_element_type=F32)


def _down(act, w, x, tm=512, tn=512):
    m, d = x.shape
    kk = act.shape[1]
    vmem = 2 * (tm * kk * 2 + kk * tn * 4 + 2 * tm * tn * 4) + kk * tn * 2 + 2 * tm * tn * 4
    return pl.pallas_call(
        _down_kernel,
        out_shape=jax.ShapeDtypeStruct((m, d), F32),
        grid=(d // tn, m // tm),
        in_specs=[pl.BlockSpec((tm, kk), lambda j, i: (i, 0)),
                  pl.BlockSpec((kk, tn), lambda j, i: (0, j)),
                  pl.BlockSpec((tm, tn), lambda j, i: (i, j))],
        out_specs=pl.BlockSpec((tm, tn), lambda j, i: (i, j)),
        scratch_shapes=[pltpu.VMEM((kk, tn), BF16)],
        compiler_params=_params(("arbitrary", "arbitrary"), vmem),
        name="ffn_down",
    )(act, w, x)


def _ple_kernel(x_ref, p_ref, nw_ref, wg_ref, wp_ref, fw_ref, o_ref, wg_sc, wp_sc, *, final):
    @pl.when(pl.program_id(0) == 0)
    def _():
        wg_sc[...] = wg_ref[...].astype(BF16)
        wp_sc[...] = wp_ref[...].astype(BF16)

    x = x_ref[...]
    h = _rms(x, nw_ref[...]).astype(BF16)
    gate = jax.nn.sigmoid(jnp.dot(h, wg_sc[...], preferred_element_type=F32))
    proj = jnp.dot(p_ref[...].astype(BF16), wp_sc[...], preferred_element_type=F32)
    x = x + gate * proj
    if final:
        x = _rms(x, fw_ref[...])
    o_ref[...] = x


def _ple(x, p, nw, wg, wp, fw, final, tm=512):
    m, d = x.shape
    dp = p.shape[1]
    vmem = 2 * (2 * tm * d * 4 + tm * dp * 4) + (d + dp) * d * 6 + 6 * tm * d * 4
    resident = lambda shape: pl.BlockSpec(shape, lambda i: (0, 0), pipeline_mode=pl.Buffered(1))
    return pl.pallas_call(
        functools.partial(_ple_kernel, final=final),
        out_shape=jax.ShapeDtypeStruct((m, d), F32),
        grid=(m // tm,),
        in_specs=[pl.BlockSpec((tm, d), lambda i: (i, 0)),
                  pl.BlockSpec((tm, dp), lambda i: (i, 0)),
                  pl.BlockSpec((1, d), lambda i: (0, 0)),
                  resident((d, d)),
                  resident((dp, d)),
                  pl.BlockSpec((1, d), lambda i: (0, 0))],
        out_specs=pl.BlockSpec((tm, d), lambda i: (i, 0)),
        scratch_shapes=[pltpu.VMEM((d, d), BF16), pltpu.VMEM((dp, d), BF16)],
        compiler_params=_params(("arbitrary",), vmem),
        name="ple_gate",
    )(x, p, nw, wg, wp, fw)


def _layer(x, p, mix_norm_w, w_in, fox_forget_bias, ssd_conv_w, ssd_conv_b, ssd_dt_bias,
           ssd_A_log, ssd_D, ssd_norm_w, w_out, ffn_norm_w, w_gate_up, ffn_conv_w,
           ffn_conv_b, w_down, ple_norm_w, w_ple_gate, w_ple_proj, final_norm_w, final):
    row = lambda v: v.reshape(1, -1).astype(F32)

    w_inT = jnp.swapaxes(w_in, 0, 1).astype(F32)
    w_u2T = jnp.concatenate(
        [w_inT[_OFF_XS:_OFF_B], w_inT[_OFF_Z:_OFF_DT], w_inT[_OFF_B:_OFF_Z]], axis=0)
    w_smallT = jnp.concatenate(
        [w_inT[_OFF_F:_OFF_XS], w_inT[_OFF_DT:_IN_COLS],
         jnp.zeros((LANES - FOX_HEADS - SSD_HEADS, D_MODEL), F32)], axis=0)

    h_mix, small = _norm(x, row(mix_norm_w), w_smallT)
    qT = _proj(h_mix, w_inT, FOX_WIDTH, BF16, tm=1024, tn=512, name="in_proj_q",
               transposed=True)
    k = _proj(h_mix, w_inT, FOX_WIDTH, BF16, tm=1024, tn=512, name="in_proj_k",
              first_col=FOX_WIDTH)
    vT = _proj(h_mix, w_inT, FOX_WIDTH, BF16, tm=1024, tn=512, name="in_proj_v",
               first_col=2 * FOX_WIDTH, transposed=True)
    u2 = _proj(h_mix, w_u2T, _U2_COLS, F32, tm=1024, tn=_U2_COLS // 2, name="in_proj_ssd")

    b_pad = jnp.zeros((1, LANES), F32).at[0, :FOX_HEADS].set(fox_forget_bias.astype(F32))
    c, aug = _fox_c(small, b_pad)
    cT = c[:, :FOX_HEADS].T
    y_fox = _fox_attn(qT, k, vT, aug, cT[:, None, :])

    dt_raw = small[:, FOX_HEADS:FOX_HEADS + SSD_HEADS]
    col = lambda v: v.reshape(-1, 1).astype(F32)
    y_ssd = _ssd(u2, dt_raw, dt_raw.T,
                 ssd_conv_w[:, :SSD_WIDTH].astype(F32), row(ssd_conv_b[:SSD_WIDTH]),
                 ssd_conv_w[:, SSD_WIDTH:].astype(F32), row(ssd_conv_b[SSD_WIDTH:]),
                 row(ssd_dt_bias), col(ssd_dt_bias), row(ssd_A_log), col(ssd_A_log),
                 row(jnp.repeat(ssd_D, HEAD_DIM)), row(ssd_norm_w))

    x, h_ffn = _out_proj(x, y_fox, y_ssd, w_out.astype(F32), row(ffn_norm_w))

    act = _gate_up(h_ffn, w_gate_up.astype(F32), ffn_conv_w.astype(F32), row(ffn_conv_b))
    x = _down(act, w_down.astype(F32), x)

    return _ple(x, p, row(ple_norm_w), w_ple_gate.astype(F32), w_ple_proj.astype(F32),
                row(final_norm_w), final)


def kernel(x, p, mix_norm_w, w_in, fox_forget_bias, ssd_conv_w, ssd_conv_b, ssd_dt_bias,
           ssd_A_log, ssd_D, ssd_norm_w, w_out, ffn_norm_w, w_gate_up, ffn_conv_w,
           ffn_conv_b, w_down, ple_norm_w, w_ple_gate, w_ple_proj, final_norm_w):
    bsz = x.shape[0]
    depth = p.shape[0]
    outs = []
    for b in range(bsz):
        xb = x[b]
        for i in range(depth):
            xb = _layer(xb, p[i, b], mix_norm_w[i], w_in[i], fox_forget_bias[i], ssd_conv_w[i],
                        ssd_conv_b[i], ssd_dt_bias[i], ssd_A_log[i], ssd_D[i], ssd_norm_w[i],
                        w_out[i], ffn_norm_w[i], w_gate_up[i], ffn_conv_w[i], ffn_conv_b[i],
                        w_down[i], ple_norm_w[i], w_ple_gate[i], w_ple_proj[i], final_norm_w,
                        final=(i == depth - 1))
        outs.append(xb)
    return outs[0][None] if bsz == 1 else jnp.stack(outs, axis=0)
```

```python
import functools

import numpy as np
import jax
import jax.numpy as jnp
from jax import lax
from jax.experimental import pallas as pl
from jax.experimental.pallas import tpu as pltpu

F32 = jnp.float32
BF16 = jnp.bfloat16

D_MODEL = 2048
HEAD_DIM = 64
FOX_WIDTH = 1024
FOX_HEADS = 16
SSD_WIDTH = 1024
SSD_HEADS = 16
SSD_GROUPS = 2
SSD_STATE = 128
SSD_CONV = 4
D_FF = 5632
FFN_CONV = 3
EPS = 1e-6
LOG2_E = 1.4426950408889634

_OFF_Q = 0
_OFF_F = 3 * FOX_WIDTH
_OFF_XS = _OFF_F + FOX_HEADS
_OFF_B = _OFF_XS + SSD_WIDTH
_OFF_C = _OFF_B + SSD_GROUPS * SSD_STATE
_OFF_Z = _OFF_C + SSD_GROUPS * SSD_STATE
_OFF_DT = _OFF_Z + SSD_WIDTH
_IN_COLS = _OFF_DT + SSD_HEADS

LANES = 128
SUBLANES = 8
VMEM_CAP = 60 * 1024 * 1024

_BC_WIDTH = 2 * SSD_GROUPS * SSD_STATE
_U2_COLS = 2 * SSD_WIDTH + _BC_WIDTH


def _params(sem, vmem_bytes, claim_all=False):
    limit = VMEM_CAP if claim_all else min(VMEM_CAP, vmem_bytes)
    return pltpu.CompilerParams(dimension_semantics=sem, vmem_limit_bytes=int(limit))


def _rms(xf, w):
    ms = jnp.mean(xf * xf, axis=-1, keepdims=True)
    return xf * lax.rsqrt(ms + EPS) * w


def _split3(a):
    hi = a.astype(BF16)
    r1 = a - hi.astype(F32)
    mid = r1.astype(BF16)
    lo = (r1 - mid.astype(F32)).astype(BF16)
    return hi, mid, lo


def _tri_left(tri, a):
    return sum(jnp.dot(tri, t, preferred_element_type=F32) for t in _split3(a))


def _tri_right(a, tri):
    return sum(jnp.dot(t, tri, preferred_element_type=F32) for t in _split3(a))


def _softplus(v):
    return jnp.maximum(v, 0.0) + jnp.log1p(jnp.exp(-jnp.abs(v)))


def _silu(v):
    return v / (1.0 + jnp.exp2(v * (-LOG2_E)))


def _norm_kernel(x_ref, nw_ref, wT_ref, h_ref, small_ref):
    h = _rms(x_ref[...], nw_ref[...]).astype(h_ref.dtype)
    h_ref[...] = h
    small_ref[...] = lax.dot_general(h, wT_ref[...].astype(BF16), (((1,), (1,)), ((), ())),
                                     preferred_element_type=F32)


def _norm(x, nw, w_smallT, tm=512):
    m, d = x.shape
    n_small = w_smallT.shape[0]
    vmem = 2 * (tm * d * 6 + n_small * d * 4 + tm * n_small * 4) + 4 * tm * d * 4
    return pl.pallas_call(
        _norm_kernel,
        out_shape=(jax.ShapeDtypeStruct((m, d), BF16),
                   jax.ShapeDtypeStruct((m, n_small), F32)),
        grid=(m // tm,),
        in_specs=[pl.BlockSpec((tm, d), lambda i: (i, 0)),
                  pl.BlockSpec((1, d), lambda i: (0, 0)),
                  pl.BlockSpec((n_small, d), lambda i: (0, 0))],
        out_specs=(pl.BlockSpec((tm, d), lambda i: (i, 0)),
                   pl.BlockSpec((tm, n_small), lambda i: (i, 0))),
        compiler_params=_params(("arbitrary",), vmem),
        name="mix_norm",
    )(x, nw, w_smallT)


def _proj_kernel(h_ref, wT_ref, o_ref, w_sc, *, transposed):
    @pl.when(pl.program_id(1) == 0)
    def _():
        w_sc[...] = wT_ref[...].astype(BF16)

    lhs, rhs = (w_sc[...], h_ref[...]) if transposed else (h_ref[...], w_sc[...])
    o_ref[...] = lax.dot_general(lhs, rhs, (((1,), (1,)), ((), ())),
                                 preferred_element_type=F32).astype(o_ref.dtype)


def _proj(h, wT, n_cols, out_dtype, tm, tn, name, first_col=0, transposed=False):
    m, k = h.shape
    ob = jnp.dtype(out_dtype).itemsize
    j0 = first_col // tn
    vmem = 2 * (tm * k * 2 + k * tn * 4 + tm * tn * ob) + k * tn * 2 + 2 * tm * tn * 4
    return pl.pallas_call(
        functools.partial(_proj_kernel, transposed=transposed),
        out_shape=jax.ShapeDtypeStruct((n_cols, m) if transposed else (m, n_cols), out_dtype),
        grid=(n_cols // tn, m // tm),
        in_specs=[pl.BlockSpec((tm, k), lambda j, i: (i, 0)),
                  pl.BlockSpec((tn, k), lambda j, i: (j0 + j, 0))],
        out_specs=(pl.BlockSpec((tn, tm), lambda j, i: (j, i)) if transposed
                   else pl.BlockSpec((tm, tn), lambda j, i: (i, j))),
        scratch_shapes=[pltpu.VMEM((tn, k), BF16)],
        compiler_params=_params(("arbitrary", "arbitrary"), vmem),
        name=name,
    )(h, wT)


_AUG_ONES = 6
_N_PAIRS = FOX_HEADS // 2


def _aug_selector():
    sel = np.zeros((LANES, _N_PAIRS * LANES), np.float32)
    for pr in range(_N_PAIRS):
        for e in range(2):
            for j in range(3):
                sel[j * FOX_HEADS + 2 * pr + e, pr * LANES + 3 * e + j] = -1.0
        sel[3 * FOX_HEADS, pr * LANES + _AUG_ONES:pr * LANES + _AUG_ONES + 3] = 1.0
    return jnp.asarray(sel, BF16)


def _fox_c_kernel(f_ref, b_ref, sel_ref, c_ref, aug_ref, carry_sc):
    @pl.when(pl.program_id(0) == 0)
    def _():
        carry_sc[...] = jnp.zeros_like(carry_sc)

    sub = _CUM_ROWS
    row = lax.broadcasted_iota(jnp.int32, (sub, sub), 0)
    col = lax.broadcasted_iota(jnp.int32, (sub, sub), 1)
    tril = (col <= row).astype(BF16)
    lane = lax.broadcasted_iota(jnp.int32, (sub, LANES), 1)
    h = FOX_HEADS
    carry = carry_sc[...]
    for r in range(f_ref.shape[0] // sub):
        rows = slice(r * sub, (r + 1) * sub)
        z = f_ref[rows, :] + b_ref[...]
        lf = jnp.minimum(z, 0.0) - jnp.log1p(jnp.exp(-jnp.abs(z)))
        cum = _tri_left(tril, lf) + carry
        c_ref[rows, :] = cum
        carry = cum[sub - 1:sub, :]
        hi, mid, lo = _split3(cum * LOG2_E)
        parts = jnp.where(
            lane < h, hi.astype(F32),
            jnp.where(lane < 2 * h, pltpu.roll(mid.astype(F32), h, axis=1),
                      jnp.where(lane < 3 * h, pltpu.roll(lo.astype(F32), 2 * h, axis=1),
                                jnp.where(lane == 3 * h, 1.0, 0.0))))
        aug_ref[rows, :] = jnp.dot(parts.astype(BF16), sel_ref[...],
                                   preferred_element_type=F32).astype(BF16)
    carry_sc[...] = carry


_CUM_ROWS = 256


def _fox_c(small, b_pad, t=4 * _CUM_ROWS):
    s = small.shape[0]
    aug_cols = _N_PAIRS * LANES
    return pl.pallas_call(
        _fox_c_kernel,
        out_shape=(jax.ShapeDtypeStruct((s, LANES), F32),
                   jax.ShapeDtypeStruct((s, aug_cols), BF16)),
        grid=(s // t,),
        in_specs=[pl.BlockSpec((t, LANES), lambda i: (i, 0)),
                  pl.BlockSpec((1, LANES), lambda i: (0, 0)),
                  pl.BlockSpec((LANES, aug_cols), lambda i: (0, 0))],
        out_specs=(pl.BlockSpec((t, LANES), lambda i: (i, 0)),
                   pl.BlockSpec((t, aug_cols), lambda i: (i, 0))),
        scratch_shapes=[pltpu.VMEM((1, LANES), F32)],
        compiler_params=_params(("arbitrary",), 16 * 1024 * 1024, claim_all=True),
        name="fox_cumlogf",
    )(small, b_pad, _aug_selector())


_V_ROWS = HEAD_DIM + 16


def _fox_attn_kernel(q_ref, k_ref, vT_ref, aug_ref, cq_ref, o_ref, vT_sc, acc_sc, s0_sc, s1_sc,
                     w_sc, m_sc, smax0_sc, smax1_sc, *, t):
    qi = pl.program_id(1)
    n_kv = k_ref.shape[0] // t
    n_heads = q_ref.shape[1] // HEAD_DIM

    @pl.when(qi == 0)
    def _():
        pad_row = lax.broadcasted_iota(jnp.int32, (_V_ROWS - HEAD_DIM, t), 0)
        ones_rows = jnp.where(pad_row == 0, 1.0, 0.0).astype(BF16)

        def place_v(c, carry):
            st = pl.multiple_of(c * t, t)
            for e in range(n_heads):
                vT_sc[e, 0:HEAD_DIM, pl.ds(st, t)] = vT_ref[e * HEAD_DIM:(e + 1) * HEAD_DIM,
                                                            pl.ds(st, t)]
                vT_sc[e, HEAD_DIM:_V_ROWS, pl.ds(st, t)] = ones_rows
            return carry
        lax.fori_loop(0, n_kv, place_v, 0)

    scale = HEAD_DIM ** -0.5 * LOG2_E
    qT = q_ref[...].astype(F32).T * scale
    row = lax.broadcasted_iota(jnp.int32, (LANES, t), 0)
    tk = s0_sc.shape[1]
    key_idx = lax.broadcasted_iota(jnp.int32, (tk, t), 0)
    qry_idx = lax.broadcasted_iota(jnp.int32, (tk, t), 1)

    ws = []
    for e in range(n_heads):
        pair, side = divmod(e, 2)
        qTe = jnp.where(row // HEAD_DIM == side, qT[pair * LANES:(pair + 1) * LANES, :], 0.0)
        cq_hi, cq_mid, cq_lo = (part.astype(F32)
                                for part in _split3(cq_ref[e] * LOG2_E))
        top = lax.broadcasted_iota(jnp.int32, (2 * SUBLANES, t), 0)
        aug_rows = jnp.where(
            top // 3 == side, 1.0,
            jnp.where(top == _AUG_ONES, cq_hi,
                      jnp.where(top == _AUG_ONES + 1, cq_mid,
                                jnp.where(top == _AUG_ONES + 2, cq_lo, 0.0))))
        unused = jnp.zeros((LANES - 2 * SUBLANES, t), F32)
        ws.append(jnp.concatenate([qTe, aug_rows, unused], axis=0).astype(BF16))
        acc_sc[e] = jnp.zeros((_V_ROWS, t), F32)

    def key_blocks(j):
        start = pl.multiple_of(j * tk, tk)
        return [jnp.concatenate([k_ref[pl.ds(start, tk), pr * LANES:(pr + 1) * LANES],
                                 aug_ref[pl.ds(start, tk), pr * LANES:(pr + 1) * LANES]], axis=1)
                for pr in range(n_heads // 2)]

    def scores_into(j, s_sc, smax_sc):
        keys = key_blocks(j)
        for e in range(n_heads):
            s = jnp.dot(keys[e // 2], ws[e], preferred_element_type=F32)
            s_sc[e] = s
            smax_sc[e] = jnp.max(s, axis=0, keepdims=True)

    def softmax_pv(j, s_sc, smax_sc, m_prev, masked):
        start = pl.multiple_of(j * tk, tk)
        if masked:
            visible = key_idx + start <= qry_idx + qi * t
        m_out, probs, alphas = [], [], []
        for e in range(n_heads):
            s = s_sc[e]
            if masked:
                s = jnp.where(visible, s, -jnp.inf)
                m_new = jnp.maximum(m_prev[e], jnp.max(s, axis=0, keepdims=True))
            else:
                m_new = jnp.maximum(m_prev[e], smax_sc[e])
            probs.append(jnp.exp2(s - m_new).astype(BF16))
            alphas.append(jnp.exp2(m_prev[e] - m_new))
            m_out.append(m_new)
        for e in range(n_heads):
            acc_sc[e] = alphas[e] * acc_sc[e] + jnp.dot(
                vT_sc[e, :, pl.ds(start, tk)], probs[e], preferred_element_type=F32)
        return tuple(m_out)

    def two_blocks(i, m):
        scores_into(2 * i + 1, s1_sc, smax1_sc)
        m = softmax_pv(2 * i, s0_sc, smax0_sc, m, False)
        scores_into(2 * i + 2, s0_sc, smax0_sc)
        return softmax_pv(2 * i + 1, s1_sc, smax1_sc, m, False)

    def unrolled(n):
        def trip(i, m):
            for r in range(n):
                m = two_blocks(n * i + r, m)
            return m
        return trip

    m_init = jnp.full((1, t), -1e30, F32)
    scores_into(0, s0_sc, smax0_sc)
    fours, rest = qi // 4, qi % 4
    m = lax.fori_loop(0, fours, unrolled(4), (m_init,) * n_heads)
    m = lax.fori_loop(2 * fours, 2 * fours + rest // 2, unrolled(2), m)
    m = lax.fori_loop(4 * fours + 2 * (rest // 2), qi, two_blocks, m)
    upper = slice(tk, t)
    last = pl.multiple_of((2 * qi + 1) * tk, tk)
    last_keys = key_blocks(2 * qi + 1)
    for e in range(n_heads):
        w_sc[e] = ws[e]
        s1_sc[e, :, 0:tk] = jnp.dot(last_keys[e // 2], w_sc[e, :, upper],
                                    preferred_element_type=F32)
    m = softmax_pv(2 * qi, s0_sc, smax0_sc, m, True)
    visible = (lax.broadcasted_iota(jnp.int32, (tk, tk), 0)
               <= lax.broadcasted_iota(jnp.int32, (tk, tk), 1))
    for e in range(n_heads):
        s = jnp.where(visible, s1_sc[e, :, 0:tk], -jnp.inf)
        m_sc[e] = m[e]
        m_upper = m_sc[e, :, upper]
        m_new = jnp.maximum(m_upper, jnp.max(s, axis=0, keepdims=True))
        p = jnp.exp2(s - m_new).astype(BF16)
        acc_sc[e, :, upper] = jnp.exp2(m_upper - m_new) * acc_sc[e, :, upper] + jnp.dot(
            vT_sc[e, :, pl.ds(last, tk)], p, preferred_element_type=F32)
    outs = []
    for e in range(n_heads):
        acc = acc_sc[e]
        outs.append(acc[0:HEAD_DIM, :] / acc[HEAD_DIM:HEAD_DIM + 1, :])
    o_ref[...] = jnp.concatenate(outs, axis=0).T.astype(o_ref.dtype)


def _fox_attn(qk, vT, aug, cq, t=512, heads=4):
    s = qk.shape[0]
    w = heads * HEAD_DIM
    kcol = FOX_WIDTH // w
    vmem = 2 * (2 * t * w * 2 + 3 * s * w * 2 + heads * t * 4) \
        + heads * _V_ROWS * (s * 2 + t * 4) + 5 * heads * t * t * 4
    return pl.pallas_call(
        functools.partial(_fox_attn_kernel, t=t),
        out_shape=jax.ShapeDtypeStruct((s, FOX_WIDTH), BF16),
        grid=(FOX_HEADS // heads, s // t),
        in_specs=[pl.BlockSpec((t, w), lambda g, qi: (qi, g)),
                  pl.BlockSpec((s, w), lambda g, qi: (0, kcol + g)),
                  pl.BlockSpec((w, s), lambda g, qi: (g, 0)),
                  pl.BlockSpec((s, w), lambda g, qi: (0, g)),
                  pl.BlockSpec((heads, 1, t), lambda g, qi: (g, 0, qi))],
        out_specs=pl.BlockSpec((t, w), lambda g, qi: (qi, g)),
        scratch_shapes=[pltpu.VMEM((heads, _V_ROWS, s), BF16),
                        pltpu.VMEM((heads, _V_ROWS, t), F32),
                        pltpu.VMEM((heads, t // 2, t), F32), pltpu.VMEM((heads, t // 2, t), F32),
                        pltpu.VMEM((heads, 2 * LANES, t), BF16), pltpu.VMEM((heads, 1, t), F32),
                        pltpu.VMEM((heads, 1, t), F32), pltpu.VMEM((heads, 1, t), F32)],
        compiler_params=_params(("arbitrary", "arbitrary"), vmem, claim_all=True),
        name="fox_attention",
    )(qk, qk, vT, aug, cq)


def _ssd_kernel(xs_ref, z_ref, bc_ref, dt_ref, dtT_ref, cwx_ref, cbx_ref, cwbc_ref, cbbc_ref,
                dtb_ref, dtbT_ref, alog_ref, alogT_ref, dskip_ref, nw_ref, y_ref,
                xbuf, bcbuf, xtail_sc, bctail_sc, state_sc, ybuf, *, L):
    halo = SUBLANES

    @pl.when(pl.program_id(0) == 0)
    def _():
        xtail_sc[...] = jnp.zeros_like(xtail_sc)
        bctail_sc[...] = jnp.zeros_like(bctail_sc)
        state_sc[...] = jnp.zeros_like(state_sc)

    def conv_silu(u_ref, tail_sc, buf, w_ref, b_ref):
        cur = u_ref[...]
        prev = tail_sc[...]
        tail_sc[...] = cur[L - halo:L, :]
        last = SSD_CONV - 1

        def taps(c, backs):
            acc = b_ref[...] + c * w_ref[last:last + 1, :]
            for k in range(1, SSD_CONV):
                acc = acc + backs[k - 1] * w_ref[last - k:last - k + 1, :]
            return acc

        buf[...] = taps(cur, [pltpu.roll(cur, k, axis=0) for k in range(1, SSD_CONV)])
        top = cur[0:halo, :]
        r = lax.broadcasted_iota(jnp.int32, top.shape, 0)
        buf[0:halo, :] = taps(top, [
            jnp.where(r < k, pltpu.roll(prev, k, axis=0), pltpu.roll(top, k, axis=0))
            for k in range(1, SSD_CONV)])
        return _silu(buf[...])

    xc = conv_silu(xs_ref, xtail_sc, xbuf, cwx_ref, cbx_ref)
    bcc = conv_silu(bc_ref, bctail_sc, bcbuf, cwbc_ref, cbbc_ref)

    dt = _softplus(dt_ref[...] + dtb_ref[...])
    dtT = _softplus(dtT_ref[...] + dtbT_ref[...])
    a = dt * (-jnp.exp(alog_ref[...]) * LOG2_E)
    aT = dtT * (-jnp.exp(alogT_ref[...]) * LOG2_E)

    row = lax.broadcasted_iota(jnp.int32, (L, L), 0)
    col = lax.broadcasted_iota(jnp.int32, (L, L), 1)
    causal = col <= row
    a_cum = _tri_left(causal.astype(BF16), a)
    a_cumT = _tri_right(aT, (row <= col).astype(BF16))
    a_last = a_cum[L - 1:L, :]
    a_lastT = a_cumT[:, L - 1:L]
    exp_acum = jnp.exp2(a_cum)
    w_endT = jnp.exp2(a_lastT - a_cumT) * dtT
    chunk_decay = jnp.exp2(a_last)
    src_T = a_cumT - jnp.log2(dtT)

    n = SSD_STATE
    heads_per_group = SSD_HEADS // SSD_GROUPS
    first_half = lax.broadcasted_iota(jnp.int32, (1, LANES), 1) < HEAD_DIM
    for g in range(SSD_GROUPS):
        bg = bcc[:, g * n:(g + 1) * n]
        cg = bcc[:, SSD_GROUPS * n + g * n:SSD_GROUPS * n + (g + 1) * n]
        bgT = bg.T
        cb = lax.dot_general(cg.astype(BF16), bg.astype(BF16), (((1,), (1,)), ((), ())),
                             preferred_element_type=F32)
        for pair in range(heads_per_group // 2):
            h0 = g * heads_per_group + 2 * pair
            cols = slice(h0 * HEAD_DIM, (h0 + 2) * HEAD_DIM)
            x_pair = xc[:, cols]
            x_pair_b = x_pair.astype(BF16)
            st = state_sc[:, cols]
            st_b = st.astype(BF16)
            ys, upds = [], []
            for h in (h0, h0 + 1):
                seg = a_cum[:, h:h + 1] - src_T[h:h + 1, :]
                mix = cb * jnp.exp2(jnp.where(causal, seg, -jnp.inf))
                y = jnp.dot(mix.astype(BF16), x_pair_b, preferred_element_type=F32)
                ys.append(y + jnp.dot((cg * exp_acum[:, h:h + 1]).astype(BF16), st_b,
                                      preferred_element_type=F32))
                upds.append(jnp.dot((bgT * w_endT[h:h + 1, :]).astype(BF16), x_pair_b,
                                    preferred_element_type=F32))
            ybuf[:, cols] = jnp.where(first_half, ys[0], ys[1]) + dskip_ref[:, cols] * x_pair
            keep = jnp.where(first_half, chunk_decay[:, h0:h0 + 1], chunk_decay[:, h0 + 1:h0 + 2])
            state_sc[:, cols] = keep * st + jnp.where(first_half, upds[0], upds[1])

    y = ybuf[...] * _silu(z_ref[...])
    y_ref[...] = _rms(y, nw_ref[...]).astype(y_ref.dtype)


def _ssd(u2, dt_raw, dt_rawT, cwx, cbx, cwbc, cbbc, dtb, dtbT, alog, alogT, dskip, nw, L=256):
    s = u2.shape[0]
    w = SSD_WIDTH
    full = lambda shape: pl.BlockSpec(shape, lambda i: (0,) * len(shape))
    vmem = 2 * (2 * L * w * 4 + L * _BC_WIDTH * 4 + L * w * 2) + (2 * L + 16) * w * 4 \
        + (L + 8) * _BC_WIDTH * 4 + SSD_STATE * w * 4 + 24 * L * L * 4 + 6 * L * w * 4
    return pl.pallas_call(
        functools.partial(_ssd_kernel, L=L),
        out_shape=jax.ShapeDtypeStruct((s, w), BF16),
        grid=(s // L,),
        in_specs=[pl.BlockSpec((L, w), lambda i: (i, 0)),
                  pl.BlockSpec((L, w), lambda i: (i, 1)),
                  pl.BlockSpec((L, _BC_WIDTH), lambda i: (i, 2 * w // _BC_WIDTH)),
                  pl.BlockSpec((L, SSD_HEADS), lambda i: (i, 0)),
                  pl.BlockSpec((SSD_HEADS, L), lambda i: (0, i)),
                  full((SSD_CONV, w)), full((1, w)),
                  full((SSD_CONV, _BC_WIDTH)), full((1, _BC_WIDTH)),
                  full((1, SSD_HEADS)), full((SSD_HEADS, 1)),
                  full((1, SSD_HEADS)), full((SSD_HEADS, 1)),
                  full((1, w)), full((1, w))],
        out_specs=pl.BlockSpec((L, w), lambda i: (i, 0)),
        scratch_shapes=[pltpu.VMEM((L, w), F32),
                        pltpu.VMEM((L, _BC_WIDTH), F32),
                        pltpu.VMEM((SUBLANES, w), F32),
                        pltpu.VMEM((SUBLANES, _BC_WIDTH), F32),
                        pltpu.VMEM((SSD_STATE, w), F32),
                        pltpu.VMEM((L, w), F32)],
        compiler_params=_params(("arbitrary",), vmem, claim_all=True),
        name="ssd_mixer",
    )(u2, u2, u2, dt_raw, dt_rawT, cwx, cbx, cwbc, cbbc, dtb, dtbT, alog, alogT, dskip, nw)


def _out_proj_kernel(x_ref, ya_ref, yb_ref, w_ref, nw_ref, o_ref, h_ref, w_sc):
    @pl.when(pl.program_id(0) == 0)
    def _():
        w_sc[...] = w_ref[...].astype(BF16)

    ka = ya_ref.shape[1]
    o = (x_ref[...]
         + jnp.dot(ya_ref[...], w_sc[0:ka, :], preferred_element_type=F32)
         + jnp.dot(yb_ref[...], w_sc[ka:, :], preferred_element_type=F32))
    o_ref[...] = o
    h_ref[...] = _rms(o, nw_ref[...]).astype(h_ref.dtype)


def _out_proj(x, ya, yb, w, nw, tm=512):
    m, d = x.shape
    ka, kb = ya.shape[1], yb.shape[1]
    vmem = 2 * (2 * tm * d * 4 + tm * d * 2 + tm * (ka + kb) * 2) + (ka + kb) * d * 6 \
        + 4 * tm * d * 4
    return pl.pallas_call(
        _out_proj_kernel,
        out_shape=(jax.ShapeDtypeStruct((m, d), F32), jax.ShapeDtypeStruct((m, d), BF16)),
        grid=(m // tm,),
        in_specs=[pl.BlockSpec((tm, d), lambda i: (i, 0)),
                  pl.BlockSpec((tm, ka), lambda i: (i, 0)),
                  pl.BlockSpec((tm, kb), lambda i: (i, 0)),
                  pl.BlockSpec((ka + kb, d), lambda i: (0, 0), pipeline_mode=pl.Buffered(1)),
                  pl.BlockSpec((1, d), lambda i: (0, 0))],
        out_specs=(pl.BlockSpec((tm, d), lambda i: (i, 0)),
                   pl.BlockSpec((tm, d), lambda i: (i, 0))),
        scratch_shapes=[pltpu.VMEM((ka + kb, d), BF16)],
        compiler_params=_params(("arbitrary",), vmem),
        name="mix_out_proj",
    )(x, ya, yb, w, nw)


def _gate_up_kernel(h_ref, wg_ref, wu_ref, cw_ref, cb_ref, o_ref, wg_sc, wu_sc, tail_sc, conv_sc):
    halo = SUBLANES
    tm = conv_sc.shape[0]

    @pl.when(pl.program_id(1) == 0)
    def _():
        wg_sc[...] = wg_ref[...].astype(BF16)
        wu_sc[...] = wu_ref[...].astype(BF16)
        tail_sc[...] = jnp.zeros_like(tail_sc)

    h = h_ref[...]
    g = jnp.dot(h, wg_sc[...], preferred_element_type=F32)
    up = jnp.dot(h, wu_sc[...], preferred_element_type=F32)

    prev = tail_sc[...]
    tail_sc[...] = g[tm - halo:tm, :]

    def conv(cur, back1, back2):
        return (cb_ref[...] + cur * cw_ref[2:3, :] + back1 * cw_ref[1:2, :]
                + back2 * cw_ref[0:1, :])

    conv_sc[...] = conv(g, pltpu.roll(g, 1, axis=0), pltpu.roll(g, 2, axis=0))
    top = g[0:halo, :]
    r = lax.broadcasted_iota(jnp.int32, top.shape, 0)
    back1 = jnp.where(r < 1, pltpu.roll(prev, 1, axis=0), pltpu.roll(top, 1, axis=0))
    back2 = jnp.where(r < 2, pltpu.roll(prev, 2, axis=0), pltpu.roll(top, 2, axis=0))
    conv_sc[0:halo, :] = conv(top, back1, back2)
    o_ref[...] = (_silu(conv_sc[...]) * up).astype(o_ref.dtype)


def _gate_up(h, w_gu, cw, cb, tm=1024, tn=512):
    m, d = h.shape
    nj = D_FF // tn
    vmem = 2 * (tm * d * 2 + 2 * d * tn * 4 + tm * tn * 2) + 2 * d * tn * 2 \
        + 10 * tm * tn * 4
    return pl.pallas_call(
        _gate_up_kernel,
        out_shape=jax.ShapeDtypeStruct((m, D_FF), BF16),
        grid=(nj, m // tm),
        in_specs=[pl.BlockSpec((tm, d), lambda j, i: (i, 0)),
                  pl.BlockSpec((d, tn), lambda j, i: (0, j)),
                  pl.BlockSpec((d, tn), lambda j, i: (0, j + nj)),
                  pl.BlockSpec((FFN_CONV, tn), lambda j, i: (0, j)),
                  pl.BlockSpec((1, tn), lambda j, i: (0, j))],
        out_specs=pl.BlockSpec((tm, tn), lambda j, i: (i, j)),
        scratch_shapes=[pltpu.VMEM((d, tn), BF16), pltpu.VMEM((d, tn), BF16),
                        pltpu.VMEM((SUBLANES, tn), F32), pltpu.VMEM((tm, tn), F32)],
        compiler_params=_params(("arbitrary", "arbitrary"), vmem),
        name="ffn_gate_up",
    )(h, w_gu, w_gu, cw, cb)


def _down_kernel(a_ref, w_ref, x_ref, o_ref, w_sc):
    @pl.when(pl.program_id(1) == 0)
    def _():
        w_sc[...] = w_ref[...].astype(BF16)

    o_ref[...] = x_ref[...] + jnp.dot(a_ref[...], w_sc[...], preferred_element_type=F32)


def _down(act, w, x, tm=512, tn=512):
    m, d = x.shape
    kk = act.shape[1]
    vmem = 2 * (tm * kk * 2 + kk * tn * 4 + 2 * tm * tn * 4) + kk * tn * 2 + 2 * tm * tn * 4
    return pl.pallas_call(
        _down_kernel,
        out_shape=jax.ShapeDtypeStruct((m, d), F32),
        grid=(d // tn, m // tm),
        in_specs=[pl.BlockSpec((tm, kk), lambda j, i: (i, 0)),
                  pl.BlockSpec((kk, tn), lambda j, i: (0, j)),
                  pl.BlockSpec((tm, tn), lambda j, i: (i, j))],
        out_specs=pl.BlockSpec((tm, tn), lambda j, i: (i, j)),
        scratch_shapes=[pltpu.VMEM((kk, tn), BF16)],
        compiler_params=_params(("arbitrary", "arbitrary"), vmem),
        name="ffn_down",
    )(act, w, x)


def _ple_kernel(x_ref, p_ref, nw_ref, wg_ref, wp_ref, fw_ref, o_ref, wg_sc, wp_sc, *, final):
    @pl.when(pl.program_id(0) == 0)
    def _():
        wg_sc[...] = wg_ref[...].astype(BF16)
        wp_sc[...] = wp_ref[...].astype(BF16)

    x = x_ref[...]
    h = _rms(x, nw_ref[...]).astype(BF16)
    gate = jax.nn.sigmoid(jnp.dot(h, wg_sc[...], preferred_element_type=F32))
    proj = jnp.dot(p_ref[...].astype(BF16), wp_sc[...], preferred_element_type=F32)
    x = x + gate * proj
    if final:
        x = _rms(x, fw_ref[...])
    o_ref[...] = x


def _ple(x, p, nw, wg, wp, fw, final, tm=512):
    m, d = x.shape
    dp = p.shape[1]
    vmem = 2 * (2 * tm * d * 4 + tm * dp * 4) + (d + dp) * d * 6 + 6 * tm * d * 4
    resident = lambda shape: pl.BlockSpec(shape, lambda i: (0, 0), pipeline_mode=pl.Buffered(1))
    return pl.pallas_call(
        functools.partial(_ple_kernel, final=final),
        out_shape=jax.ShapeDtypeStruct((m, d), F32),
        grid=(m // tm,),
        in_specs=[pl.BlockSpec((tm, d), lambda i: (i, 0)),
                  pl.BlockSpec((tm, dp), lambda i: (i, 0)),
                  pl.BlockSpec((1, d), lambda i: (0, 0)),
                  resident((d, d)),
                  resident((dp, d)),
                  pl.BlockSpec((1, d), lambda i: (0, 0))],
        out_specs=pl.BlockSpec((tm, d), lambda i: (i, 0)),
        scratch_shapes=[pltpu.VMEM((d, d), BF16), pltpu.VMEM((dp, d), BF16)],
        compiler_params=_params(("arbitrary",), vmem),
        name="ple_gate",
    )(x, p, nw, wg, wp, fw)


def _layer(x, p, mix_norm_w, w_in, fox_forget_bias, ssd_conv_w, ssd_conv_b, ssd_dt_bias,
           ssd_A_log, ssd_D, ssd_norm_w, w_out, ffn_norm_w, w_gate_up, ffn_conv_w,
           ffn_conv_b, w_down, ple_norm_w, w_ple_gate, w_ple_proj, final_norm_w, final):
    row = lambda v: v.reshape(1, -1).astype(F32)

    w_inT = jnp.swapaxes(w_in, 0, 1).astype(F32)
    w_u2T = jnp.concatenate(
        [w_inT[_OFF_XS:_OFF_B], w_inT[_OFF_Z:_OFF_DT], w_inT[_OFF_B:_OFF_Z]], axis=0)
    w_smallT = jnp.concatenate(
        [w_inT[_OFF_F:_OFF_XS], w_inT[_OFF_DT:_IN_COLS],
         jnp.zeros((LANES - FOX_HEADS - SSD_HEADS, D_MODEL), F32)], axis=0)

    h_mix, small = _norm(x, row(mix_norm_w), w_smallT)
    qk = _proj(h_mix, w_inT, 2 * FOX_WIDTH, BF16, tm=1024, tn=512, name="in_proj_qk")
    vT = _proj(h_mix, w_inT, FOX_WIDTH, BF16, tm=1024, tn=512, name="in_proj_v",
               first_col=2 * FOX_WIDTH, transposed=True)
    u2 = _proj(h_mix, w_u2T, _U2_COLS, F32, tm=1024, tn=_U2_COLS // 2, name="in_proj_ssd")

    b_pad = jnp.zeros((1, LANES), F32).at[0, :FOX_HEADS].set(fox_forget_bias.astype(F32))
    c, aug = _fox_c(small, b_pad)
    cT = c[:, :FOX_HEADS].T
    y_fox = _fox_attn(qk, vT, aug, cT[:, None, :])

    dt_raw = small[:, FOX_HEADS:FOX_HEADS + SSD_HEADS]
    col = lambda v: v.reshape(-1, 1).astype(F32)
    y_ssd = _ssd(u2, dt_raw, dt_raw.T,
                 ssd_conv_w[:, :SSD_WIDTH].astype(F32), row(ssd_conv_b[:SSD_WIDTH]),
                 ssd_conv_w[:, SSD_WIDTH:].astype(F32), row(ssd_conv_b[SSD_WIDTH:]),
                 row(ssd_dt_bias), col(ssd_dt_bias), row(ssd_A_log), col(ssd_A_log),
                 row(jnp.repeat(ssd_D, HEAD_DIM)), row(ssd_norm_w))

    x, h_ffn = _out_proj(x, y_fox, y_ssd, w_out.astype(F32), row(ffn_norm_w))

    act = _gate_up(h_ffn, w_gate_up.astype(F32), ffn_conv_w.astype(F32), row(ffn_conv_b))
    x = _down(act, w_down.astype(F32), x)

    return _ple(x, p, row(ple_norm_w), w_ple_gate.astype(F32), w_ple_proj.astype(F32),
                row(final_norm_w), final)


def kernel(x, p, mix_norm_w, w_in, fox_forget_bias, ssd_conv_w, ssd_conv_b, ssd_dt_bias,
           ssd_A_log, ssd_D, ssd_norm_w, w_out, ffn_norm_w, w_gate_up, ffn_conv_w,
           ffn_conv_b, w_down, ple_norm_w, w_ple_gate, w_ple_proj, final_norm_w):
    bsz = x.shape[0]
    depth = p.shape[0]
    outs = []
    for b in range(bsz):
        xb = x[b]
        for i in range(depth):
            xb = _layer(xb, p[i, b], mix_norm_w[i], w_in[i], fox_forget_bias[i], ssd_conv_w[i],
                        ssd_conv_b[i], ssd_dt_bias[i], ssd_A_log[i], ssd_D[i], ssd_norm_w[i],
                        w_out[i], ffn_norm_w[i], w_gate_up[i], ffn_conv_w[i], ffn_conv_b[i],
                        w_down[i], ple_norm_w[i], w_ple_gate[i], w_ple_proj[i], final_norm_w,
                        final=(i == depth - 1))
        outs.append(xb)
    return outs[0][None] if bsz == 1 else jnp.stack(outs, axis=0)
```

```python
import functools

import numpy as np
import jax
import jax.numpy as jnp
from jax import lax
from jax.experimental import pallas as pl
from jax.experimental.pallas import tpu as pltpu

F32 = jnp.float32
BF16 = jnp.bfloat16

D_MODEL = 2048
HEAD_DIM = 64
FOX_WIDTH = 1024
FOX_HEADS = 16
SSD_WIDTH = 1024
SSD_HEADS = 16
SSD_GROUPS = 2
SSD_STATE = 128
SSD_CONV = 4
D_FF = 5632
FFN_CONV = 3
EPS = 1e-6
LOG2_E = 1.4426950408889634

_OFF_Q = 0
_OFF_F = 3 * FOX_WIDTH
_OFF_XS = _OFF_F + FOX_HEADS
_OFF_B = _OFF_XS + SSD_WIDTH
_OFF_C = _OFF_B + SSD_GROUPS * SSD_STATE
_OFF_Z = _OFF_C + SSD_GROUPS * SSD_STATE
_OFF_DT = _OFF_Z + SSD_WIDTH
_IN_COLS = _OFF_DT + SSD_HEADS

LANES = 128
SUBLANES = 8
VMEM_CAP = 60 * 1024 * 1024

_BC_WIDTH = 2 * SSD_GROUPS * SSD_STATE
_U2_COLS = 2 * SSD_WIDTH + _BC_WIDTH


def _params(sem, vmem_bytes, claim_all=False):
    limit = VMEM_CAP if claim_all else min(VMEM_CAP, vmem_bytes)
    return pltpu.CompilerParams(dimension_semantics=sem, vmem_limit_bytes=int(limit))


def _rms(xf, w):
    ms = jnp.mean(xf * xf, axis=-1, keepdims=True)
    return xf * lax.rsqrt(ms + EPS) * w


def _split3(a):
    hi = a.astype(BF16)
    r1 = a - hi.astype(F32)
    mid = r1.astype(BF16)
    lo = (r1 - mid.astype(F32)).astype(BF16)
    return hi, mid, lo


def _tri_left(tri, a):
    return sum(jnp.dot(tri, t, preferred_element_type=F32) for t in _split3(a))


def _tri_right(a, tri):
    return sum(jnp.dot(t, tri, preferred_element_type=F32) for t in _split3(a))


def _softplus(v):
    return jnp.maximum(v, 0.0) + jnp.log1p(jnp.exp(-jnp.abs(v)))


def _silu(v):
    return v / (1.0 + jnp.exp2(v * (-LOG2_E)))


def _norm_kernel(x_ref, nw_ref, wT_ref, h_ref, small_ref):
    h = _rms(x_ref[...], nw_ref[...]).astype(h_ref.dtype)
    h_ref[...] = h
    small_ref[...] = lax.dot_general(h, wT_ref[...].astype(BF16), (((1,), (1,)), ((), ())),
                                     preferred_element_type=F32)


def _norm(x, nw, w_smallT, tm=512):
    m, d = x.shape
    n_small = w_smallT.shape[0]
    vmem = 2 * (tm * d * 6 + n_small * d * 4 + tm * n_small * 4) + 4 * tm * d * 4
    return pl.pallas_call(
        _norm_kernel,
        out_shape=(jax.ShapeDtypeStruct((m, d), BF16),
                   jax.ShapeDtypeStruct((m, n_small), F32)),
        grid=(m // tm,),
        in_specs=[pl.BlockSpec((tm, d), lambda i: (i, 0)),
                  pl.BlockSpec((1, d), lambda i: (0, 0)),
                  pl.BlockSpec((n_small, d), lambda i: (0, 0))],
        out_specs=(pl.BlockSpec((tm, d), lambda i: (i, 0)),
                   pl.BlockSpec((tm, n_small), lambda i: (i, 0))),
        compiler_params=_params(("arbitrary",), vmem),
        name="mix_norm",
    )(x, nw, w_smallT)


def _proj_kernel(h_ref, wT_ref, o_ref, w_sc, *, transposed):
    @pl.when(pl.program_id(1) == 0)
    def _():
        w_sc[...] = wT_ref[...].astype(BF16)

    lhs, rhs = (w_sc[...], h_ref[...]) if transposed else (h_ref[...], w_sc[...])
    o_ref[...] = lax.dot_general(lhs, rhs, (((1,), (1,)), ((), ())),
                                 preferred_element_type=F32).astype(o_ref.dtype)


def _proj(h, wT, n_cols, out_dtype, tm, tn, name, first_col=0, transposed=False):
    m, k = h.shape
    ob = jnp.dtype(out_dtype).itemsize
    j0 = first_col // tn
    vmem = 2 * (tm * k * 2 + k * tn * 4 + tm * tn * ob) + k * tn * 2 + 2 * tm * tn * 4
    return pl.pallas_call(
        functools.partial(_proj_kernel, transposed=transposed),
        out_shape=jax.ShapeDtypeStruct((n_cols, m) if transposed else (m, n_cols), out_dtype),
        grid=(n_cols // tn, m // tm),
        in_specs=[pl.BlockSpec((tm, k), lambda j, i: (i, 0)),
                  pl.BlockSpec((tn, k), lambda j, i: (j0 + j, 0))],
        out_specs=(pl.BlockSpec((tn, tm), lambda j, i: (j, i)) if transposed
                   else pl.BlockSpec((tm, tn), lambda j, i: (i, j))),
        scratch_shapes=[pltpu.VMEM((tn, k), BF16)],
        compiler_params=_params(("arbitrary", "arbitrary"), vmem),
        name=name,
    )(h, wT)


_AUG_ONES = 6
_N_PAIRS = FOX_HEADS // 2


def _aug_selector():
    sel = np.zeros((LANES, _N_PAIRS * LANES), np.float32)
    for pr in range(_N_PAIRS):
        for e in range(2):
            for j in range(3):
                sel[j * FOX_HEADS + 2 * pr + e, pr * LANES + 3 * e + j] = -1.0
        sel[3 * FOX_HEADS, pr * LANES + _AUG_ONES:pr * LANES + _AUG_ONES + 3] = 1.0
    return jnp.asarray(sel, BF16)


def _fox_c_kernel(f_ref, b_ref, sel_ref, c_ref, aug_ref, carry_sc):
    @pl.when(pl.program_id(0) == 0)
    def _():
        carry_sc[...] = jnp.zeros_like(carry_sc)

    sub = _CUM_ROWS
    row = lax.broadcasted_iota(jnp.int32, (sub, sub), 0)
    col = lax.broadcasted_iota(jnp.int32, (sub, sub), 1)
    tril = (col <= row).astype(BF16)
    lane = lax.broadcasted_iota(jnp.int32, (sub, LANES), 1)
    h = FOX_HEADS
    carry = carry_sc[...]
    for r in range(f_ref.shape[0] // sub):
        rows = slice(r * sub, (r + 1) * sub)
        z = f_ref[rows, :] + b_ref[...]
        lf = jnp.minimum(z, 0.0) - jnp.log1p(jnp.exp(-jnp.abs(z)))
        cum = _tri_left(tril, lf) + carry
        c_ref[rows, :] = cum
        carry = cum[sub - 1:sub, :]
        hi, mid, lo = _split3(cum * LOG2_E)
        parts = jnp.where(
            lane < h, hi.astype(F32),
            jnp.where(lane < 2 * h, pltpu.roll(mid.astype(F32), h, axis=1),
                      jnp.where(lane < 3 * h, pltpu.roll(lo.astype(F32), 2 * h, axis=1),
                                jnp.where(lane == 3 * h, 1.0, 0.0))))
        aug_ref[rows, :] = jnp.dot(parts.astype(BF16), sel_ref[...],
                                   preferred_element_type=F32).astype(BF16)
    carry_sc[...] = carry


_CUM_ROWS = 256


def _fox_c(small, b_pad, t=4 * _CUM_ROWS):
    s = small.shape[0]
    aug_cols = _N_PAIRS * LANES
    return pl.pallas_call(
        _fox_c_kernel,
        out_shape=(jax.ShapeDtypeStruct((s, LANES), F32),
                   jax.ShapeDtypeStruct((s, aug_cols), BF16)),
        grid=(s // t,),
        in_specs=[pl.BlockSpec((t, LANES), lambda i: (i, 0)),
                  pl.BlockSpec((1, LANES), lambda i: (0, 0)),
                  pl.BlockSpec((LANES, aug_cols), lambda i: (0, 0))],
        out_specs=(pl.BlockSpec((t, LANES), lambda i: (i, 0)),
                   pl.BlockSpec((t, aug_cols), lambda i: (i, 0))),
        scratch_shapes=[pltpu.VMEM((1, LANES), F32)],
        compiler_params=_params(("arbitrary",), 16 * 1024 * 1024, claim_all=True),
        name="fox_cumlogf",
    )(small, b_pad, _aug_selector())


_V_ROWS = HEAD_DIM + 16


def _fox_attn_kernel(q_ref, k_ref, vT_ref, aug_ref, cq_ref, o_ref, vT_sc, acc_sc, s0_sc, s1_sc,
                     w_sc, m_sc, smax0_sc, smax1_sc, *, t):
    qi = pl.program_id(1)
    n_kv = k_ref.shape[0] // t
    n_heads = q_ref.shape[1] // HEAD_DIM

    @pl.when(qi == 0)
    def _():
        pad_row = lax.broadcasted_iota(jnp.int32, (_V_ROWS - HEAD_DIM, t), 0)
        ones_rows = jnp.where(pad_row == 0, 1.0, 0.0).astype(BF16)

        def place_v(c, carry):
            st = pl.multiple_of(c * t, t)
            for e in range(n_heads):
                vT_sc[e, 0:HEAD_DIM, pl.ds(st, t)] = vT_ref[e * HEAD_DIM:(e + 1) * HEAD_DIM,
                                                            pl.ds(st, t)]
                vT_sc[e, HEAD_DIM:_V_ROWS, pl.ds(st, t)] = ones_rows
            return carry
        lax.fori_loop(0, n_kv, place_v, 0)

    scale = HEAD_DIM ** -0.5 * LOG2_E
    qT = q_ref[...].astype(F32).T * scale
    row = lax.broadcasted_iota(jnp.int32, (LANES, t), 0)
    tk = s0_sc.shape[1]
    key_idx = lax.broadcasted_iota(jnp.int32, (tk, t), 0)
    qry_idx = lax.broadcasted_iota(jnp.int32, (tk, t), 1)

    ws = []
    for e in range(n_heads):
        pair, side = divmod(e, 2)
        qTe = jnp.where(row // HEAD_DIM == side, qT[pair * LANES:(pair + 1) * LANES, :], 0.0)
        cq_hi, cq_mid, cq_lo = (part.astype(F32)
                                for part in _split3(cq_ref[e] * LOG2_E))
        top = lax.broadcasted_iota(jnp.int32, (2 * SUBLANES, t), 0)
        aug_rows = jnp.where(
            top // 3 == side, 1.0,
            jnp.where(top == _AUG_ONES, cq_hi,
                      jnp.where(top == _AUG_ONES + 1, cq_mid,
                                jnp.where(top == _AUG_ONES + 2, cq_lo, 0.0))))
        unused = jnp.zeros((LANES - 2 * SUBLANES, t), F32)
        ws.append(jnp.concatenate([qTe, aug_rows, unused], axis=0).astype(BF16))
        acc_sc[e] = jnp.zeros((_V_ROWS, t), F32)

    def key_blocks(j):
        start = pl.multiple_of(j * tk, tk)
        return [jnp.concatenate([k_ref[pl.ds(start, tk), pr * LANES:(pr + 1) * LANES],
                                 aug_ref[pl.ds(start, tk), pr * LANES:(pr + 1) * LANES]], axis=1)
                for pr in range(n_heads // 2)]

    def scores_into(j, s_sc, smax_sc):
        keys = key_blocks(j)
        for e in range(n_heads):
            s = jnp.dot(keys[e // 2], ws[e], preferred_element_type=F32)
            s_sc[e] = s
            smax_sc[e] = jnp.max(s, axis=0, keepdims=True)

    def softmax_pv(j, s_sc, smax_sc, m_prev, masked):
        start = pl.multiple_of(j * tk, tk)
        if masked:
            visible = key_idx + start <= qry_idx + qi * t
        m_out, probs, alphas = [], [], []
        for e in range(n_heads):
            s = s_sc[e]
            if masked:
                s = jnp.where(visible, s, -jnp.inf)
                m_new = jnp.maximum(m_prev[e], jnp.max(s, axis=0, keepdims=True))
            else:
                m_new = jnp.maximum(m_prev[e], smax_sc[e])
            probs.append(jnp.exp2(s - m_new).astype(BF16))
            alphas.append(jnp.exp2(m_prev[e] - m_new))
            m_out.append(m_new)
        for e in range(n_heads):
            acc_sc[e] = alphas[e] * acc_sc[e] + jnp.dot(
                vT_sc[e, :, pl.ds(start, tk)], probs[e], preferred_element_type=F32)
        return tuple(m_out)

    def two_blocks(i, m):
        scores_into(2 * i + 1, s1_sc, smax1_sc)
        m = softmax_pv(2 * i, s0_sc, smax0_sc, m, False)
        scores_into(2 * i + 2, s0_sc, smax0_sc)
        return softmax_pv(2 * i + 1, s1_sc, smax1_sc, m, False)

    def unrolled(n):
        def trip(i, m):
            for r in range(n):
                m = two_blocks(n * i + r, m)
            return m
        return trip

    m_init = jnp.full((1, t), -1e30, F32)
    scores_into(0, s0_sc, smax0_sc)
    fours, rest = qi // 4, qi % 4
    m = lax.fori_loop(0, fours, unrolled(4), (m_init,) * n_heads)
    m = lax.fori_loop(2 * fours, 2 * fours + rest // 2, unrolled(2), m)
    m = lax.fori_loop(4 * fours + 2 * (rest // 2), qi, two_blocks, m)
    upper = slice(tk, t)
    last = pl.multiple_of((2 * qi + 1) * tk, tk)
    last_keys = key_blocks(2 * qi + 1)
    for e in range(n_heads):
        w_sc[e] = ws[e]
        s1_sc[e, :, 0:tk] = jnp.dot(last_keys[e // 2], w_sc[e, :, upper],
                                    preferred_element_type=F32)
    m = softmax_pv(2 * qi, s0_sc, smax0_sc, m, True)
    visible = (lax.broadcasted_iota(jnp.int32, (tk, tk), 0)
               <= lax.broadcasted_iota(jnp.int32, (tk, tk), 1))
    for e in range(n_heads):
        s = jnp.where(visible, s1_sc[e, :, 0:tk], -jnp.inf)
        m_sc[e] = m[e]
        m_upper = m_sc[e, :, upper]
        m_new = jnp.maximum(m_upper, jnp.max(s, axis=0, keepdims=True))
        p = jnp.exp2(s - m_new).astype(BF16)
        acc_sc[e, :, upper] = jnp.exp2(m_upper - m_new) * acc_sc[e, :, upper] + jnp.dot(
            vT_sc[e, :, pl.ds(last, tk)], p, preferred_element_type=F32)
    outs = []
    for e in range(n_heads):
        acc = acc_sc[e]
        outs.append(acc[0:HEAD_DIM, :] / acc[HEAD_DIM:HEAD_DIM + 1, :])
    o_ref[...] = jnp.concatenate(outs, axis=0).T.astype(o_ref.dtype)


def _fox_attn(qk, vT, aug, cq, t=512, heads=4):
    s = qk.shape[0]
    w = heads * HEAD_DIM
    kcol = FOX_WIDTH // w
    vmem = 2 * (2 * t * w * 2 + 3 * s * w * 2 + heads * t * 4) \
        + heads * _V_ROWS * (s * 2 + t * 4) + 5 * heads * t * t * 4
    return pl.pallas_call(
        functools.partial(_fox_attn_kernel, t=t),
        out_shape=jax.ShapeDtypeStruct((s, FOX_WIDTH), BF16),
        grid=(FOX_HEADS // heads, s // t),
        in_specs=[pl.BlockSpec((t, w), lambda g, qi: (qi, g)),
                  pl.BlockSpec((s, w), lambda g, qi: (0, kcol + g)),
                  pl.BlockSpec((w, s), lambda g, qi: (g, 0)),
                  pl.BlockSpec((s, w), lambda g, qi: (0, g)),
                  pl.BlockSpec((heads, 1, t), lambda g, qi: (g, 0, qi))],
        out_specs=pl.BlockSpec((t, w), lambda g, qi: (qi, g)),
        scratch_shapes=[pltpu.VMEM((heads, _V_ROWS, s), BF16),
                        pltpu.VMEM((heads, _V_ROWS, t), F32),
                        pltpu.VMEM((heads, t // 2, t), F32), pltpu.VMEM((heads, t // 2, t), F32),
                        pltpu.VMEM((heads, 2 * LANES, t), BF16), pltpu.VMEM((heads, 1, t), F32),
                        pltpu.VMEM((heads, 1, t), F32), pltpu.VMEM((heads, 1, t), F32)],
        compiler_params=_params(("arbitrary", "arbitrary"), vmem, claim_all=True),
        name="fox_attention",
    )(qk, qk, vT, aug, cq)


def _ssd_kernel(xs_ref, z_ref, bc_ref, dt_ref, dtT_ref, cwx_ref, cbx_ref, cwbc_ref, cbbc_ref,
                dtb_ref, dtbT_ref, alog_ref, alogT_ref, dskip_ref, nw_ref, y_ref,
                xbuf, bcbuf, xtail_sc, bctail_sc, state_sc, ybuf, *, L):
    halo = SUBLANES

    @pl.when(pl.program_id(0) == 0)
    def _():
        xtail_sc[...] = jnp.zeros_like(xtail_sc)
        bctail_sc[...] = jnp.zeros_like(bctail_sc)
        state_sc[...] = jnp.zeros_like(state_sc)

    def conv_silu(u_ref, tail_sc, buf, w_ref, b_ref):
        cur = u_ref[...]
        prev = tail_sc[...]
        tail_sc[...] = cur[L - halo:L, :]
        last = SSD_CONV - 1

        def taps(c, backs):
            acc = b_ref[...] + c * w_ref[last:last + 1, :]
            for k in range(1, SSD_CONV):
                acc = acc + backs[k - 1] * w_ref[last - k:last - k + 1, :]
            return acc

        buf[...] = taps(cur, [pltpu.roll(cur, k, axis=0) for k in range(1, SSD_CONV)])
        top = cur[0:halo, :]
        r = lax.broadcasted_iota(jnp.int32, top.shape, 0)
        buf[0:halo, :] = taps(top, [
            jnp.where(r < k, pltpu.roll(prev, k, axis=0), pltpu.roll(top, k, axis=0))
            for k in range(1, SSD_CONV)])
        buf[...] = _silu(buf[...])

    conv_silu(xs_ref, xtail_sc, xbuf, cwx_ref, cbx_ref)
    conv_silu(bc_ref, bctail_sc, bcbuf, cwbc_ref, cbbc_ref)

    dt = _softplus(dt_ref[...] + dtb_ref[...])
    dtT = _softplus(dtT_ref[...] + dtbT_ref[...])
    a = dt * (-jnp.exp(alog_ref[...]) * LOG2_E)
    aT = dtT * (-jnp.exp(alogT_ref[...]) * LOG2_E)

    row = lax.broadcasted_iota(jnp.int32, (L, L), 0)
    col = lax.broadcasted_iota(jnp.int32, (L, L), 1)
    causal = col <= row
    a_cum = _tri_left(causal.astype(BF16), a)
    a_cumT = _tri_right(aT, (row <= col).astype(BF16))
    a_last = a_cum[L - 1:L, :]
    a_lastT = a_cumT[:, L - 1:L]
    exp_acum = jnp.exp2(a_cum)
    w_endT = jnp.exp2(a_lastT - a_cumT) * dtT
    chunk_decay = jnp.exp2(a_last)
    src_T = a_cumT - jnp.log2(dtT)

    n = SSD_STATE
    heads_per_group = SSD_HEADS // SSD_GROUPS
    first_half = lax.broadcasted_iota(jnp.int32, (1, LANES), 1) < HEAD_DIM
    for g in range(SSD_GROUPS):
        bg = bcbuf[:, g * n:(g + 1) * n]
        cg = bcbuf[:, SSD_GROUPS * n + g * n:SSD_GROUPS * n + (g + 1) * n]
        bgT = bg.T
        cb = lax.dot_general(cg.astype(BF16), bg.astype(BF16), (((1,), (1,)), ((), ())),
                             preferred_element_type=F32)
        for pair in range(heads_per_group // 2):
            h0 = g * heads_per_group + 2 * pair
            cols = slice(h0 * HEAD_DIM, (h0 + 2) * HEAD_DIM)
            x_pair = xbuf[:, cols]
            x_pair_b = x_pair.astype(BF16)
            st = state_sc[:, cols]
            st_b = st.astype(BF16)
            ys, upds = [], []
            for h in (h0, h0 + 1):
                seg = a_cum[:, h:h + 1] - src_T[h:h + 1, :]
                mix = cb * jnp.exp2(jnp.where(causal, seg, -jnp.inf))
                y = jnp.dot(mix.astype(BF16), x_pair_b, preferred_element_type=F32)
                ys.append(y + jnp.dot((cg * exp_acum[:, h:h + 1]).astype(BF16), st_b,
                                      preferred_element_type=F32))
                upds.append(jnp.dot((bgT * w_endT[h:h + 1, :]).astype(BF16), x_pair_b,
                                    preferred_element_type=F32))
            ybuf[:, cols] = jnp.where(first_half, ys[0], ys[1]) + dskip_ref[:, cols] * x_pair
            keep = jnp.where(first_half, chunk_decay[:, h0:h0 + 1], chunk_decay[:, h0 + 1:h0 + 2])
            state_sc[:, cols] = keep * st + jnp.where(first_half, upds[0], upds[1])

    y = ybuf[...] * _silu(z_ref[...])
    y_ref[...] = _rms(y, nw_ref[...]).astype(y_ref.dtype)


def _ssd(u2, dt_raw, dt_rawT, cwx, cbx, cwbc, cbbc, dtb, dtbT, alog, alogT, dskip, nw, L=256):
    s = u2.shape[0]
    w = SSD_WIDTH
    full = lambda shape: pl.BlockSpec(shape, lambda i: (0,) * len(shape))
    vmem = 2 * (2 * L * w * 4 + L * _BC_WIDTH * 4 + L * w * 2) + (2 * L + 16) * w * 4 \
        + (L + 8) * _BC_WIDTH * 4 + SSD_STATE * w * 4 + 24 * L * L * 4 + 6 * L * w * 4
    return pl.pallas_call(
        functools.partial(_ssd_kernel, L=L),
        out_shape=jax.ShapeDtypeStruct((s, w), BF16),
        grid=(s // L,),
        in_specs=[pl.BlockSpec((L, w), lambda i: (i, 0)),
                  pl.BlockSpec((L, w), lambda i: (i, 1)),
                  pl.BlockSpec((L, _BC_WIDTH), lambda i: (i, 2 * w // _BC_WIDTH)),
                  pl.BlockSpec((L, SSD_HEADS), lambda i: (i, 0)),
                  pl.BlockSpec((SSD_HEADS, L), lambda i: (0, i)),
                  full((SSD_CONV, w)), full((1, w)),
                  full((SSD_CONV, _BC_WIDTH)), full((1, _BC_WIDTH)),
                  full((1, SSD_HEADS)), full((SSD_HEADS, 1)),
                  full((1, SSD_HEADS)), full((SSD_HEADS, 1)),
                  full((1, w)), full((1, w))],
        out_specs=pl.BlockSpec((L, w), lambda i: (i, 0)),
        scratch_shapes=[pltpu.VMEM((L, w), F32),
                        pltpu.VMEM((L, _BC_WIDTH), F32),
                        pltpu.VMEM((SUBLANES, w), F32),
                        pltpu.VMEM((SUBLANES, _BC_WIDTH), F32),
                        pltpu.VMEM((SSD_STATE, w), F32),
                        pltpu.VMEM((L, w), F32)],
        compiler_params=_params(("arbitrary",), vmem, claim_all=True),
        name="ssd_mixer",
    )(u2, u2, u2, dt_raw, dt_rawT, cwx, cbx, cwbc, cbbc, dtb, dtbT, alog, alogT, dskip, nw)


def _out_proj_kernel(x_ref, ya_ref, yb_ref, w_ref, nw_ref, o_ref, h_ref, w_sc):
    @pl.when(pl.program_id(0) == 0)
    def _():
        w_sc[...] = w_ref[...].astype(BF16)

    ka = ya_ref.shape[1]
    o = (x_ref[...]
         + jnp.dot(ya_ref[...], w_sc[0:ka, :], preferred_element_type=F32)
         + jnp.dot(yb_ref[...], w_sc[ka:, :], preferred_element_type=F32))
    o_ref[...] = o
    h_ref[...] = _rms(o, nw_ref[...]).astype(h_ref.dtype)


def _out_proj(x, ya, yb, w, nw, tm=512):
    m, d = x.shape
    ka, kb = ya.shape[1], yb.shape[1]
    vmem = 2 * (2 * tm * d * 4 + tm * d * 2 + tm * (ka + kb) * 2) + (ka + kb) * d * 6 \
        + 4 * tm * d * 4
    return pl.pallas_call(
        _out_proj_kernel,
        out_shape=(jax.ShapeDtypeStruct((m, d), F32), jax.ShapeDtypeStruct((m, d), BF16)),
        grid=(m // tm,),
        in_specs=[pl.BlockSpec((tm, d), lambda i: (i, 0)),
                  pl.BlockSpec((tm, ka), lambda i: (i, 0)),
                  pl.BlockSpec((tm, kb), lambda i: (i, 0)),
                  pl.BlockSpec((ka + kb, d), lambda i: (0, 0), pipeline_mode=pl.Buffered(1)),
                  pl.BlockSpec((1, d), lambda i: (0, 0))],
        out_specs=(pl.BlockSpec((tm, d), lambda i: (i, 0)),
                   pl.BlockSpec((tm, d), lambda i: (i, 0))),
        scratch_shapes=[pltpu.VMEM((ka + kb, d), BF16)],
        compiler_params=_params(("arbitrary",), vmem),
        name="mix_out_proj",
    )(x, ya, yb, w, nw)


def _gate_up_kernel(h_ref, wg_ref, wu_ref, cw_ref, cb_ref, o_ref, wg_sc, wu_sc, tail_sc, conv_sc):
    halo = SUBLANES
    tm = conv_sc.shape[0]

    @pl.when(pl.program_id(1) == 0)
    def _():
        wg_sc[...] = wg_ref[...].astype(BF16)
        wu_sc[...] = wu_ref[...].astype(BF16)
        tail_sc[...] = jnp.zeros_like(tail_sc)

    h = h_ref[...]
    g = jnp.dot(h, wg_sc[...], preferred_element_type=F32)
    up = jnp.dot(h, wu_sc[...], preferred_element_type=F32)

    prev = tail_sc[...]
    tail_sc[...] = g[tm - halo:tm, :]

    def conv(cur, back1, back2):
        return (cb_ref[...] + cur * cw_ref[2:3, :] + back1 * cw_ref[1:2, :]
                + back2 * cw_ref[0:1, :])

    conv_sc[...] = conv(g, pltpu.roll(g, 1, axis=0), pltpu.roll(g, 2, axis=0))
    top = g[0:halo, :]
    r = lax.broadcasted_iota(jnp.int32, top.shape, 0)
    back1 = jnp.where(r < 1, pltpu.roll(prev, 1, axis=0), pltpu.roll(top, 1, axis=0))
    back2 = jnp.where(r < 2, pltpu.roll(prev, 2, axis=0), pltpu.roll(top, 2, axis=0))
    conv_sc[0:halo, :] = conv(top, back1, back2)
    o_ref[...] = (_silu(conv_sc[...]) * up).astype(o_ref.dtype)


def _gate_up(h, w_gu, cw, cb, tm=1024, tn=512):
    m, d = h.shape
    nj = D_FF // tn
    vmem = 2 * (tm * d * 2 + 2 * d * tn * 4 + tm * tn * 2) + 2 * d * tn * 2 \
        + 10 * tm * tn * 4
    return pl.pallas_call(
        _gate_up_kernel,
        out_shape=jax.ShapeDtypeStruct((m, D_FF), BF16),
        grid=(nj, m // tm),
        in_specs=[pl.BlockSpec((tm, d), lambda j, i: (i, 0)),
                  pl.BlockSpec((d, tn), lambda j, i: (0, j)),
                  pl.BlockSpec((d, tn), lambda j, i: (0, j + nj)),
                  pl.BlockSpec((FFN_CONV, tn), lambda j, i: (0, j)),
                  pl.BlockSpec((1, tn), lambda j, i: (0, j))],
        out_specs=pl.BlockSpec((tm, tn), lambda j, i: (i, j)),
        scratch_shapes=[pltpu.VMEM((d, tn), BF16), pltpu.VMEM((d, tn), BF16),
                        pltpu.VMEM((SUBLANES, tn), F32), pltpu.VMEM((tm, tn), F32)],
        compiler_params=_params(("arbitrary", "arbitrary"), vmem),
        name="ffn_gate_up",
    )(h, w_gu, w_gu, cw, cb)


def _down_kernel(a_ref, w_ref, x_ref, o_ref, w_sc):
    @pl.when(pl.program_id(1) == 0)
    def _():
        w_sc[...] = w_ref[...].astype(BF16)

    o_ref[...] = x_ref[...] + jnp.dot(a_ref[...], w_sc[...], preferred_element_type=F32)


def _down(act, w, x, tm=512, tn=512):
    m, d = x.shape
    kk = act.shape[1]
    vmem = 2 * (tm * kk * 2 + kk * tn * 4 + 2 * tm * tn * 4) + kk * tn * 2 + 2 * tm * tn * 4
    return pl.pallas_call(
        _down_kernel,
        out_shape=jax.ShapeDtypeStruct((m, d), F32),
        grid=(d // tn, m // tm),
        in_specs=[pl.BlockSpec((tm, kk), lambda j, i: (i, 0)),
                  pl.BlockSpec((kk, tn), lambda j, i: (0, j)),
                  pl.BlockSpec((tm, tn), lambda j, i: (i, j))],
        out_specs=pl.BlockSpec((tm, tn), lambda j, i: (i, j)),
        scratch_shapes=[pltpu.VMEM((kk, tn), BF16)],
        compiler_params=_params(("arbitrary", "arbitrary"), vmem),
        name="ffn_down",
    )(act, w, x)


def _ple_kernel(x_ref, p_ref, nw_ref, wg_ref, wp_ref, fw_ref, o_ref, wg_sc, wp_sc, *, final):
    @pl.when(pl.program_id(0) == 0)
    def _():
        wg_sc[...] = wg_ref[...].astype(BF16)
        wp_sc[...] = wp_ref[...].astype(BF16)

    x = x_ref[...]
    h = _rms(x, nw_ref[...]).astype(BF16)
    gate = jax.nn.sigmoid(jnp.dot(h, wg_sc[...], preferred_element_type=F32))
    proj = jnp.dot(p_ref[...].astype(BF16), wp_sc[...], preferred_element_type=F32)
    x = x + gate * proj
    if final:
        x = _rms(x, fw_ref[...])
    o_ref[...] = x


def _ple(x, p, nw, wg, wp, fw, final, tm=512):
    m, d = x.shape
    dp = p.shape[1]
    vmem = 2 * (2 * tm * d * 4 + tm * dp * 4) + (d + dp) * d * 6 + 6 * tm * d * 4
    resident = lambda shape: pl.BlockSpec(shape, lambda i: (0, 0), pipeline_mode=pl.Buffered(1))
    return pl.pallas_call(
        functools.partial(_ple_kernel, final=final),
        out_shape=jax.ShapeDtypeStruct((m, d), F32),
        grid=(m // tm,),
        in_specs=[pl.BlockSpec((tm, d), lambda i: (i, 0)),
                  pl.BlockSpec((tm, dp), lambda i: (i, 0)),
                  pl.BlockSpec((1, d), lambda i: (0, 0)),
                  resident((d, d)),
                  resident((dp, d)),
                  pl.BlockSpec((1, d), lambda i: (0, 0))],
        out_specs=pl.BlockSpec((tm, d), lambda i: (i, 0)),
        scratch_shapes=[pltpu.VMEM((d, d), BF16), pltpu.VMEM((dp, d), BF16)],
        compiler_params=_params(("arbitrary",), vmem),
        name="ple_gate",
    )(x, p, nw, wg, wp, fw)


def _layer(x, p, mix_norm_w, w_in, fox_forget_bias, ssd_conv_w, ssd_conv_b, ssd_dt_bias,
           ssd_A_log, ssd_D, ssd_norm_w, w_out, ffn_norm_w, w_gate_up, ffn_conv_w,
           ffn_conv_b, w_down, ple_norm_w, w_ple_gate, w_ple_proj, final_norm_w, final):
    row = lambda v: v.reshape(1, -1).astype(F32)

    w_inT = jnp.swapaxes(w_in, 0, 1).astype(F32)
    w_u2T = jnp.concatenate(
        [w_inT[_OFF_XS:_OFF_B], w_inT[_OFF_Z:_OFF_DT], w_inT[_OFF_B:_OFF_Z]], axis=0)
    w_smallT = jnp.concatenate(
        [w_inT[_OFF_F:_OFF_XS], w_inT[_OFF_DT:_IN_COLS],
         jnp.zeros((LANES - FOX_HEADS - SSD_HEADS, D_MODEL), F32)], axis=0)

    h_mix, small = _norm(x, row(mix_norm_w), w_smallT)
    qk = _proj(h_mix, w_inT, 2 * FOX_WIDTH, BF16, tm=1024, tn=512, name="in_proj_qk")
    vT = _proj(h_mix, w_inT, FOX_WIDTH, BF16, tm=1024, tn=512, name="in_proj_v",
               first_col=2 * FOX_WIDTH, transposed=True)
    u2 = _proj(h_mix, w_u2T, _U2_COLS, F32, tm=1024, tn=_U2_COLS // 2, name="in_proj_ssd")

    b_pad = jnp.zeros((1, LANES), F32).at[0, :FOX_HEADS].set(fox_forget_bias.astype(F32))
    c, aug = _fox_c(small, b_pad)
    cT = c[:, :FOX_HEADS].T
    y_fox = _fox_attn(qk, vT, aug, cT[:, None, :])

    dt_raw = small[:, FOX_HEADS:FOX_HEADS + SSD_HEADS]
    col = lambda v: v.reshape(-1, 1).astype(F32)
    y_ssd = _ssd(u2, dt_raw, dt_raw.T,
                 ssd_conv_w[:, :SSD_WIDTH].astype(F32), row(ssd_conv_b[:SSD_WIDTH]),
                 ssd_conv_w[:, SSD_WIDTH:].astype(F32), row(ssd_conv_b[SSD_WIDTH:]),
                 row(ssd_dt_bias), col(ssd_dt_bias), row(ssd_A_log), col(ssd_A_log),
                 row(jnp.repeat(ssd_D, HEAD_DIM)), row(ssd_norm_w))

    x, h_ffn = _out_proj(x, y_fox, y_ssd, w_out.astype(F32), row(ffn_norm_w))

    act = _gate_up(h_ffn, w_gate_up.astype(F32), ffn_conv_w.astype(F32), row(ffn_conv_b))
    x = _down(act, w_down.astype(F32), x)

    return _ple(x, p, row(ple_norm_w), w_ple_gate.astype(F32), w_ple_proj.astype(F32),
                row(final_norm_w), final)


def kernel(x, p, mix_norm_w, w_in, fox_forget_bias, ssd_conv_w, ssd_conv_b, ssd_dt_bias,
           ssd_A_log, ssd_D, ssd_norm_w, w_out, ffn_norm_w, w_gate_up, ffn_conv_w,
           ffn_conv_b, w_down, ple_norm_w, w_ple_gate, w_ple_proj, final_norm_w):
    bsz = x.shape[0]
    depth = p.shape[0]
    outs = []
    for b in range(bsz):
        xb = x[b]
        for i in range(depth):
            xb = _layer(xb, p[i, b], mix_norm_w[i], w_in[i], fox_forget_bias[i], ssd_conv_w[i],
                        ssd_conv_b[i], ssd_dt_bias[i], ssd_A_log[i], ssd_D[i], ssd_norm_w[i],
                        w_out[i], ffn_norm_w[i], w_gate_up[i], ffn_conv_w[i], ffn_conv_b[i],
                        w_down[i], ple_norm_w[i], w_ple_gate[i], w_ple_proj[i], final_norm_w,
                        final=(i == depth - 1))
        outs.append(xb)
    return outs[0][None] if bsz == 1 else jnp.stack(outs, axis=0)
```
